```python
import jax
import jax.numpy as jnp
from jax import lax
import numpy as np

D_MODEL = 1024
BATCH = 8
SEQ = 4096
DEPTH = 4

GRID_W = 64
CTX_LEN = 256
N_HEADS = 4
D_INNER = 2 * D_MODEL
HEAD_V = D_INNER // N_HEADS
MLSTM_QK = D_MODEL
MLSTM_HEAD_QK = MLSTM_QK // N_HEADS
CONV_K = 3
GLA_KEY = D_MODEL // 2
GLA_HEAD_K = GLA_KEY // N_HEADS
GLA_RANK = 16
GLA_TAU = 16.0
RET_KEY = D_MODEL
RET_HEAD_K = RET_KEY // N_HEADS
ROPE_BASE = 10000.0
CHUNK = 64
NORM_EPS = 1e-6
NEG_INF = -1e30
MIXERS = ('mlstm', 'gla', 'retention')
MLSTM_IN = 2 * MLSTM_QK + 2 * D_INNER + 4 * N_HEADS
GLA_IN = 2 * GLA_KEY + 2 * D_INNER + 2 * GLA_RANK
RET_IN = 2 * RET_KEY + 2 * D_INNER

kernel_name = 'hybrid_mlstm_gla_retention_prefix_dit'


def rmsnorm(x, w):
    xf = x.astype(jnp.float32)
    y = xf * lax.rsqrt(jnp.mean(xf * xf, axis=-1, keepdims=True) + NORM_EPS)
    return (y * w.astype(jnp.float32)).astype(x.dtype)


def head_norm(y, w, center):
    if center:
        y = y - jnp.mean(y, axis=-1, keepdims=True)
    y = y * lax.rsqrt(jnp.mean(y * y, axis=-1, keepdims=True) + NORM_EPS)
    return y.reshape(y.shape[:2] + (-1,)) * w.astype(jnp.float32)


def flip_parts(t, n_ctx):
    return jnp.concatenate([jnp.flip(t[:, :n_ctx], axis=1), jnp.flip(t[:, n_ctx:], axis=1)], axis=1)


def to_chunks(t):
    b, l = t.shape[:2]
    return jnp.moveaxis(t.reshape((b, l // CHUNK, CHUNK) + t.shape[2:]), 1, 0)


def from_chunks(t):
    t = jnp.moveaxis(t, 0, 1)
    return t.reshape((t.shape[0], -1) + t.shape[3:])


def mlstm_scan(q, k, v, log_i, log_f):
    b_, _, h_, dk = q.shape
    dv = v.shape[-1]
    tri = jnp.tril(jnp.ones((CHUNK, CHUNK), dtype=bool))

    def step(carry, inp):
        c_st, n_st, m_st = carry
        qc, kc, vc, ic, fc = inp
        ic = ic.transpose(0, 2, 1)
        b = jnp.cumsum(fc.transpose(0, 2, 1), axis=-1)
        d = jnp.where(tri, b[..., :, None] - b[..., None, :] + ic[..., None, :], NEG_INF)
        inter = b + m_st[..., None]
        m_row = jnp.maximum(inter, jnp.max(d, axis=-1))
        s = jnp.einsum('bjhd,blhd->bhjl', qc, kc) * jnp.exp(d - m_row[..., None])
        w_inter = jnp.exp(inter - m_row)
        numer = (jnp.einsum('bhjl,blhv->bjhv', s, vc)
                 + jnp.einsum('bjhd,bhdv->bjhv', qc, c_st) * w_inter.transpose(0, 2, 1)[..., None])
        denom = jnp.sum(s, axis=-1) + jnp.einsum('bjhd,bhd->bhj', qc, n_st) * w_inter
        floor = jnp.maximum(jnp.abs(denom), jnp.exp(-m_row))
        h = numer / floor.transpose(0, 2, 1)[..., None]
        b_end = b[..., -1]
        g = b_end[..., None] - b + ic
        m_new = jnp.maximum(b_end + m_st, jnp.max(g, axis=-1))
        decay = jnp.exp(b_end + m_st - m_new)
        wk = kc * jnp.exp(g - m_new[..., None]).transpose(0, 2, 1)[..., None]
        c_new = decay[..., None, None] * c_st + jnp.einsum('blhd,blhv->bhdv', wk, vc)
        n_new = decay[..., None] * n_st + jnp.sum(wk, axis=1)
        return (c_new, n_new, m_new), h

    init = (jnp.zeros((b_, h_, dk, dv), jnp.float32),
            jnp.zeros((b_, h_, dk), jnp.float32),
            jnp.zeros((b_, h_), jnp.float32))
    _, hs = lax.scan(step, init, (to_chunks(q), to_chunks(k), to_chunks(v), to_chunks(log_i), to_chunks(log_f)))
    return from_chunks(hs)


def decay_scan(q, k, v, log_a):
    b_, _, h_, dk = q.shape
    dv = v.shape[-1]
    per_channel = log_a.shape[-1] > 1
    tri = jnp.tril(jnp.ones((CHUNK, CHUNK), dtype=bool))

    def step(s_st, inp):
        qc, kc, vc, ac = inp
        b = jnp.cumsum(ac, axis=1)
        if per_channel:
            rel = jnp.where(tri[None, :, :, None, None], b[:, :, None] - b[:, None], NEG_INF)
            s = jnp.einsum('bjhd,blhd,bjlhd->bhjl', qc, kc, jnp.exp(rel))
        else:
            rel = jnp.where(tri[None, :, :, None], b[:, :, None, :, 0] - b[:, None, :, :, 0], NEG_INF)
            s = jnp.einsum('bjhd,blhd->bhjl', qc, kc) * jnp.exp(rel).transpose(0, 3, 1, 2)
        o = jnp.einsum('bhjl,blhv->bjhv', s, vc) + jnp.einsum('bjhd,bhdv->bjhv', qc * jnp.exp(b), s_st)
        b_end = b[:, -1:]
        s_new = (jnp.exp(b_end[:, 0])[..., None] * s_st
                 + jnp.einsum('blhd,blhv->bhdv', kc * jnp.exp(b_end - b), vc))
        return s_new, o

    init = jnp.zeros((b_, h_, dk, dv), jnp.float32)
    _, os_ = lax.scan(step, init, (to_chunks(q), to_chunks(k), to_chunks(v), to_chunks(log_a)))
    return from_chunks(os_)


def bidirectional(scan_fn, n_ctx, shared, fw, bw):
    out_fw = scan_fn(*shared, *fw)
    out_bw = scan_fn(*[flip_parts(t, n_ctx) for t in shared], *[flip_parts(t, n_ctx) for t in bw])
    return out_fw + flip_parts(out_bw, n_ctx)


def axial_rope(t):
    s, dh = t.shape[1], t.shape[-1]
    quarter = dh // 4
    pos = jnp.arange(s, dtype=jnp.int32)
    row = (pos // GRID_W).astype(jnp.float32)
    col = (pos % GRID_W).astype(jnp.float32)
    inv_freq = ROPE_BASE ** (-jnp.arange(quarter, dtype=jnp.float32) / quarter)

    def rot(u, p):
        ang = p[:, None] * inv_freq[None, :]
        cos = jnp.cos(ang)[None, :, None, :]
        sin = jnp.sin(ang)[None, :, None, :]
        u1, u2 = u[..., :quarter], u[..., quarter:]
        return jnp.concatenate([u1 * cos - u2 * sin, u1 * sin + u2 * cos], axis=-1)

    return jnp.concatenate([rot(t[..., :dh // 2], row), rot(t[..., dh // 2:], col)], axis=-1)


def grid_conv(u_ctx, u_lat, conv_w, conv_b):
    b, s, ch = u_lat.shape
    n_ctx = u_ctx.shape[1]
    rows = s // GRID_W
    lat = lax.conv_general_dilated(u_lat.reshape(b, rows, GRID_W, ch), conv_w[:, :, None, :],
                                   window_strides=(1, 1), padding='SAME',
                                   dimension_numbers=('NHWC', 'HWIO', 'NHWC'),
                                   feature_group_count=ch).reshape(b, s, ch)
    w1 = conv_w[CONV_K // 2]
    pad = jnp.pad(u_ctx, ((0, 0), (CONV_K // 2, CONV_K // 2), (0, 0)))
    ctx_out = sum(pad[:, i:i + n_ctx] * w1[i] for i in range(CONV_K))
    return jnp.concatenate([ctx_out, lat], axis=1) + conv_b


def mlstm_mixer(u, n_ctx, in_w, conv_w, conv_b, gate_b, head_norm_w):
    b, l, _ = u.shape
    p = u @ in_w
    qk, v, z, gates = jnp.split(p, [2 * MLSTM_QK, 2 * MLSTM_QK + D_INNER, 2 * MLSTM_QK + 2 * D_INNER], axis=-1)
    qk = jax.nn.silu(grid_conv(qk[:, :n_ctx], qk[:, n_ctx:], conv_w, conv_b))
    q, k = jnp.split(qk, 2, axis=-1)
    q = q.reshape(b, l, N_HEADS, MLSTM_HEAD_QK).astype(jnp.float32)
    k = k.reshape(b, l, N_HEADS, MLSTM_HEAD_QK).astype(jnp.float32) * (MLSTM_HEAD_QK ** -0.5)
    v = v.reshape(b, l, N_HEADS, HEAD_V).astype(jnp.float32)
    g = gates.astype(jnp.float32) + gate_b.astype(jnp.float32)
    i_fw, f_fw, i_bw, f_bw = jnp.split(g, 4, axis=-1)
    h = bidirectional(mlstm_scan, n_ctx, (q, k, v),
                      (i_fw, jax.nn.log_sigmoid(f_fw)), (i_bw, jax.nn.log_sigmoid(f_bw)))
    h = head_norm(h, head_norm_w, center=True)
    return (h * jax.nn.silu(z.astype(jnp.float32))).astype(u.dtype)


def gla_mixer(u, n_ctx, in_w, gk_w2, gk_b, head_norm_w):
    b, l, _ = u.shape
    p = u @ in_w
    q, k, v, z, lr = jnp.split(p, [GLA_KEY, 2 * GLA_KEY, 2 * GLA_KEY + D_INNER, 2 * GLA_KEY + 2 * D_INNER], axis=-1)
    q = q.reshape(b, l, N_HEADS, GLA_HEAD_K).astype(jnp.float32) * (GLA_HEAD_K ** -0.5)
    k = k.reshape(b, l, N_HEADS, GLA_HEAD_K).astype(jnp.float32)
    v = v.reshape(b, l, N_HEADS, HEAD_V).astype(jnp.float32)
    lr_fw, lr_bw = jnp.split(lr, 2, axis=-1)

    def log_alpha(code, w2, bias):
        pre = jnp.einsum('blr,rk->blk', code, w2).astype(jnp.float32) + bias.astype(jnp.float32)
        return (jax.nn.log_sigmoid(pre) / GLA_TAU).reshape(b, l, N_HEADS, GLA_HEAD_K)

    o = bidirectional(decay_scan, n_ctx, (q, k, v),
                      (log_alpha(lr_fw, gk_w2[0], gk_b[0]),), (log_alpha(lr_bw, gk_w2[1], gk_b[1]),))
    o = head_norm(o, head_norm_w, center=False)
    return (o * jax.nn.silu(z.astype(jnp.float32))).astype(u.dtype)


def retention_mixer(u, n_ctx, in_w, decay_logit, head_norm_w):
    b, l, _ = u.shape
    p = u @ in_w
    q, k, v, z = jnp.split(p, [RET_KEY, 2 * RET_KEY, 2 * RET_KEY + D_INNER], axis=-1)
    q = q.reshape(b, l, N_HEADS, RET_HEAD_K).astype(jnp.float32)
    k = k.reshape(b, l, N_HEADS, RET_HEAD_K).astype(jnp.float32)
    q = jnp.concatenate([q[:, :n_ctx], axial_rope(q[:, n_ctx:])], axis=1)
    k = jnp.concatenate([k[:, :n_ctx], axial_rope(k[:, n_ctx:])], axis=1) * (RET_HEAD_K ** -0.5)
    v = v.reshape(b, l, N_HEADS, HEAD_V).astype(jnp.float32)
    log_gamma = jax.nn.log_sigmoid(decay_logit.astype(jnp.float32))
    la_fw = jnp.broadcast_to(log_gamma[0][None, None, :, None], (b, l, N_HEADS, 1))
    la_bw = jnp.broadcast_to(log_gamma[1][None, None, :, None], (b, l, N_HEADS, 1))
    o = bidirectional(decay_scan, n_ctx, (q, k, v), (la_fw,), (la_bw,))
    o = head_norm(o, head_norm_w, center=True)
    return (o * jax.nn.silu(z.astype(jnp.float32))).astype(u.dtype)


def hybrid_layer(kind, ctx_h, lat_h, c, c_ctx, norm_w, ada_w, ada_b, out_w, mixer_params, last):
    n_ctx = ctx_h.shape[1]
    ada_lat = jax.nn.silu(c) @ ada_w + ada_b
    ada_ctx = jax.nn.silu(c_ctx) @ ada_w + ada_b
    sh_l, sc_l, g_l = jnp.split(ada_lat[:, None, :], 3, axis=-1)
    sh_c, sc_c, g_c = jnp.split(ada_ctx, 3, axis=-1)
    u = jnp.concatenate([rmsnorm(ctx_h, norm_w) * (1 + sc_c) + sh_c,
                         rmsnorm(lat_h, norm_w) * (1 + sc_l) + sh_l], axis=1)
    if kind == 'mlstm':
        y = mlstm_mixer(u, n_ctx, *mixer_params)
    elif kind == 'gla':
        y = gla_mixer(u, n_ctx, *mixer_params)
    else:
        y = retention_mixer(u, n_ctx, *mixer_params)
    lat_h = lat_h + g_l * (y[:, n_ctx:] @ out_w)
    if not last:
        ctx_h = ctx_h + g_c * (y[:, :n_ctx] @ out_w)
    return ctx_h, lat_h


def setup_inputs(seed: int = 0) -> dict:
    key = jax.random.key(seed)
    keys = iter(jax.random.split(key, 64))

    def rnd(shape, std):
        return std * jax.random.normal(next(keys), shape, jnp.float32)

    inputs = {
        'x': rnd((BATCH, SEQ, D_MODEL), 1.0),
        'c': rnd((BATCH, D_MODEL), 1.0),
        'ctx': rnd((BATCH, CTX_LEN, D_MODEL), 1.0),
        'c_ctx': rnd((D_MODEL,), 1.0),
    }
    for i in range(DEPTH):
        kind = MIXERS[i % len(MIXERS)]
        p = 'l%d_' % i
        inputs[p + 'norm_w'] = 1.0 + rnd((D_MODEL,), 0.1)
        inputs[p + 'ada_w'] = rnd((D_MODEL, 3 * D_MODEL), 0.5 * D_MODEL ** -0.5)
        inputs[p + 'ada_b'] = rnd((3 * D_MODEL,), 0.02)
        if kind == 'mlstm':
            inputs[p + 'in_w'] = rnd((D_MODEL, MLSTM_IN), D_MODEL ** -0.5)
            inputs[p + 'conv_w'] = rnd((CONV_K, CONV_K, 2 * MLSTM_QK), 1.0 / CONV_K)
            inputs[p + 'conv_b'] = rnd((2 * MLSTM_QK,), 0.02)
            f_bias = jnp.linspace(3.0, 6.0, N_HEADS, dtype=jnp.float32)
            inputs[p + 'gate_b'] = jnp.concatenate([rnd((N_HEADS,), 0.1), f_bias + rnd((N_HEADS,), 0.1),
                                                    rnd((N_HEADS,), 0.1), f_bias + rnd((N_HEADS,), 0.1)])
        elif kind == 'gla':
            inputs[p + 'in_w'] = rnd((D_MODEL, GLA_IN), D_MODEL ** -0.5)
            inputs[p + 'gk_w2'] = rnd((2, GLA_RANK, GLA_KEY), GLA_RANK ** -0.5)
            inputs[p + 'gk_b'] = rnd((2, GLA_KEY), 0.5)
        else:
            inputs[p + 'in_w'] = rnd((D_MODEL, RET_IN), D_MODEL ** -0.5)
            p_decay = 2.0 ** (-5.0 - jnp.arange(N_HEADS, dtype=jnp.float32))
            inputs[p + 'decay_logit'] = (jnp.log1p(-p_decay) - jnp.log(p_decay))[None, :] + rnd((2, N_HEADS), 0.1)
        inputs[p + 'head_norm_w'] = 1.0 + rnd((D_INNER,), 0.1)
        inputs[p + 'out_w'] = rnd((D_INNER, D_MODEL), D_INNER ** -0.5)
    inputs['final_norm_w'] = 1.0 + rnd((D_MODEL,), 0.1)
    return inputs


def reference(x, c, ctx, c_ctx,
              l0_norm_w, l0_ada_w, l0_ada_b, l0_in_w, l0_conv_w, l0_conv_b, l0_gate_b, l0_head_norm_w, l0_out_w,
              l1_norm_w, l1_ada_w, l1_ada_b, l1_in_w, l1_gk_w2, l1_gk_b, l1_head_norm_w, l1_out_w,
              l2_norm_w, l2_ada_w, l2_ada_b, l2_in_w, l2_decay_logit, l2_head_norm_w, l2_out_w,
              l3_norm_w, l3_ada_w, l3_ada_b, l3_in_w, l3_conv_w, l3_conv_b, l3_gate_b, l3_head_norm_w, l3_out_w,
              final_norm_w):
    layers = (
        ('mlstm', l0_norm_w, l0_ada_w, l0_ada_b, l0_out_w,
         (l0_in_w, l0_conv_w, l0_conv_b, l0_gate_b, l0_head_norm_w)),
        ('gla', l1_norm_w, l1_ada_w, l1_ada_b, l1_out_w,
         (l1_in_w, l1_gk_w2, l1_gk_b, l1_head_norm_w)),
        ('retention', l2_norm_w, l2_ada_w, l2_ada_b, l2_out_w,
         (l2_in_w, l2_decay_logit, l2_head_norm_w)),
        ('mlstm', l3_norm_w, l3_ada_w, l3_ada_b, l3_out_w,
         (l3_in_w, l3_conv_w, l3_conv_b, l3_gate_b, l3_head_norm_w)),
    )
    ctx_h, lat_h = ctx, x
    for i in range(DEPTH):
        kind, norm_w, ada_w, ada_b, out_w, mixer_params = layers[i]
        ctx_h, lat_h = hybrid_layer(kind, ctx_h, lat_h, c, c_ctx, norm_w, ada_w, ada_b, out_w,
                                    mixer_params, last=(i == DEPTH - 1))
    return rmsnorm(lat_h, final_norm_w)
```

```python
import functools

import jax
import jax.numpy as jnp
from jax import lax
from jax.experimental import pallas as pl
from jax.experimental.pallas import tpu as pltpu

N_HEADS = 4
GRID_W = 64
GLA_RANK = 16
GLA_TAU = 16.0
ROPE_BASE = 10000.0
NORM_EPS = 1e-6
NEG_INF = -1e30

TOKEN_TILE = 256
SCAN_CHUNK = 256
GLA_CHUNK = 128
GLA_SUB = 16
GLA_EXP_CLAMP = 80.0
LANES = 128
VMEM_LIMIT = 56 * 1024 * 1024

F32 = jnp.float32
BF16 = jnp.bfloat16

_NT = (((1,), (1,)), ((), ()))
_TN = (((0,), (0,)), ((), ()))


def _silu(x):
    return x * (1.0 / (1.0 + jnp.exp(-x)))


def _log_sigmoid(x):
    return jnp.minimum(x, 0.0) - jnp.log1p(jnp.exp(-jnp.abs(x)))


def _dot(a, b):
    return jnp.dot(a, b, preferred_element_type=F32)


def _dot_nt(a, b):
    return lax.dot_general(a, b, _NT, preferred_element_type=F32)


def _dot_tn(a, b):
    return lax.dot_general(a, b, _TN, preferred_element_type=F32)


def _split3(a):
    hi = a.astype(BF16)
    r1 = a - hi.astype(F32)
    mid = r1.astype(BF16)
    lo = (r1 - mid.astype(F32)).astype(BF16)
    return hi, mid, lo


def _dot01_left(m01, a):
    hi, mid, lo = _split3(a)
    return _dot(m01, lo) + _dot(m01, mid) + _dot(m01, hi)


def _dot01_right(a, m01):
    hi, mid, lo = _split3(a)
    return _dot(lo, m01) + _dot(mid, m01) + _dot(hi, m01)


def _tri_masks(n):
    row = lax.broadcasted_iota(jnp.int32, (n, n), 0)
    col = lax.broadcasted_iota(jnp.int32, (n, n), 1)
    return col <= row, col >= row


def _bw_chunk(step, n_ctx_chunks, n_chunks):
    return jnp.where(step < n_ctx_chunks, n_ctx_chunks - 1 - step, n_chunks - 1 - (step - n_ctx_chunks))


def _ada_body(c_ref, w_ref, b_ref, o_ref):
    s = _silu(c_ref[...])
    o_ref[...] = jnp.dot(s, w_ref[...], preferred_element_type=F32,
                         precision=lax.Precision.HIGHEST) + b_ref[...]


def _ada(cc, w, b):
    rows, d = cc.shape
    n = w.shape[1]
    tn = 512
    return pl.pallas_call(
        _ada_body,
        grid=(n // tn,),
        in_specs=[pl.BlockSpec((rows, d), lambda j: (0, 0)),
                  pl.BlockSpec((d, tn), lambda j: (0, j)),
                  pl.BlockSpec((1, tn), lambda j: (0, j))],
        out_specs=pl.BlockSpec((rows, tn), lambda j: (0, j)),
        out_shape=jax.ShapeDtypeStruct((rows, n), F32),
        name="ada",
    )(cc, w, b.reshape(1, n))


def _modulated(h_ref, mod_ref, nw_ref):
    x = h_ref[0]
    y = x * lax.rsqrt(jnp.mean(x * x, axis=-1, keepdims=True) + NORM_EPS) * nw_ref[...]
    m = mod_ref[0, 0]
    return (y * (1.0 + m[1:2]) + m[0:1]).astype(BF16)


def _proj_mlstm_body(h_ref, mod_ref, nw_ref, wqk_ref, wv_ref, wz_ref, wg_ref, gb_ref,
                     qk_ref, v_ref, z_ref, g_ref):
    u = _modulated(h_ref, mod_ref, nw_ref)
    qk_ref[0] = _dot(u, wqk_ref[...])
    v_ref[0] = _dot(u, wv_ref[...]).astype(BF16)
    z_ref[0] = _dot(u, wz_ref[...]).astype(BF16)
    g = _dot(u, wg_ref[...]) + gb_ref[...]
    lane = lax.broadcasted_iota(jnp.int32, g.shape, 1)
    is_forget = (lane // N_HEADS) % 2 == 1
    g_ref[0] = jnp.where(is_forget, _log_sigmoid(g), g)


def _proj_gla_body(h_ref, mod_ref, nw_ref, wq_ref, wk_ref, wv_ref, wz_ref, wlr_ref, w2_ref, b2_ref,
                   q_ref, k_ref, v_ref, z_ref, la_ref, *, q_scale):
    u = _modulated(h_ref, mod_ref, nw_ref)
    q_ref[0] = (_dot(u, wq_ref[...]) * q_scale).astype(BF16)
    k_ref[0] = _dot(u, wk_ref[...]).astype(BF16)
    v_ref[0] = _dot(u, wv_ref[...]).astype(BF16)
    z_ref[0] = _dot(u, wz_ref[...]).astype(BF16)
    code = _dot(u, wlr_ref[...]).astype(BF16)
    pre = _dot(code, w2_ref[...]) + b2_ref[...]
    la_ref[0] = _log_sigmoid(pre) * (1.0 / GLA_TAU)


def _rope(t, cos, sin):
    outs = []
    for g in range(t.shape[1] // LANES):
        tg = t[:, g * LANES:(g + 1) * LANES]
        cg = cos[:, (g % 2) * LANES:(g % 2 + 1) * LANES]
        sg = sin[:, (g % 2) * LANES:(g % 2 + 1) * LANES]
        outs.append(tg * cg + pltpu.roll(tg, LANES // 2, 1) * sg)
    return jnp.concatenate(outs, axis=1)


def _proj_ret_body(h_ref, mod_ref, nw_ref, wq_ref, wk_ref, wv_ref, wz_ref, cos_ref, sin_ref,
                   q_ref, k_ref, v_ref, z_ref, *, k_scale):
    u = _modulated(h_ref, mod_ref, nw_ref)
    cos = cos_ref[...]
    sin = sin_ref[...]
    q_ref[0] = _rope(_dot(u, wq_ref[...]), cos, sin).astype(BF16)
    k_ref[0] = (_rope(_dot(u, wk_ref[...]), cos, sin) * k_scale).astype(BF16)
    v_ref[0] = _dot(u, wv_ref[...]).astype(BF16)
    z_ref[0] = _dot(u, wz_ref[...]).astype(BF16)


def _proj_call(body, h, modsel, norm_w, consts, row_tables, out_dims, name):
    b, l, d = h.shape
    nt = l // TOKEN_TILE
    in_specs = [pl.BlockSpec((1, TOKEN_TILE, d), lambda i, t: (i, t, 0)),
                pl.BlockSpec((1, 1, 3, d), lambda i, t: (i, jnp.minimum(t, 1), 0, 0)),
                pl.BlockSpec((1, d), lambda i, t: (0, 0))]
    for a in consts:
        in_specs.append(pl.BlockSpec(a.shape, lambda i, t, nd=a.ndim: (0,) * nd))
    for a in row_tables:
        in_specs.append(pl.BlockSpec((TOKEN_TILE, a.shape[1]), lambda i, t: (t, 0)))
    out_specs = [pl.BlockSpec((1, TOKEN_TILE, n), lambda i, t: (i, t, 0)) for n, _ in out_dims]
    out_shape = [jax.ShapeDtypeStruct((b, l, n), dt) for n, dt in out_dims]
    return pl.pallas_call(
        body,
        grid=(b, nt),
        in_specs=in_specs,
        out_specs=out_specs,
        out_shape=out_shape,
        compiler_params=pltpu.CompilerParams(
            dimension_semantics=("parallel", "parallel"), vmem_limit_bytes=VMEM_LIMIT),
        name=name,
    )(h, modsel, norm_w.reshape(1, d), *consts, *row_tables)


def _conv_body(x_ref, w_ref, b_ref, s_ref, o_ref, *, n_ctx, n_rows):
    w = w_ref[...]
    bias = b_ref[...]
    scale = s_ref[...]

    def finish(acc):
        return (_silu(acc + bias) * scale).astype(BF16)

    def shifted(x, n):
        rid = lax.broadcasted_iota(jnp.int32, (n, 1), 0)
        left = jnp.where(rid == 0, 0.0, pltpu.roll(x, 1, 0))
        right = jnp.where(rid == n - 1, 0.0, pltpu.roll(x, n - 1, 0))
        return left, right

    xc = x_ref[0, 0:n_ctx, :]
    lc, rc = shifted(xc, n_ctx)
    o_ref[0, 0:n_ctx, :] = finish(lc * w[3:4] + xc * w[4:5] + rc * w[5:6])

    def row_step(r, carry):
        base = n_ctx + r * GRID_W
        acc = None
        for kh in range(3):
            start = base + (kh - 1) * GRID_W
            inside = jnp.logical_and(r + kh - 1 >= 0, r + kh - 1 < n_rows)
            start = jnp.clip(start, n_ctx, n_ctx + (n_rows - 1) * GRID_W)
            x = x_ref[0, pl.ds(pl.multiple_of(start, GRID_W), GRID_W), :]
            x = jnp.where(inside, x, 0.0)
            lx, rx = shifted(x, GRID_W)
            term = lx * w[3 * kh:3 * kh + 1] + x * w[3 * kh + 1:3 * kh + 2] + rx * w[3 * kh + 2:3 * kh + 3]
            acc = term if acc is None else acc + term
        o_ref[0, pl.ds(pl.multiple_of(base, GRID_W), GRID_W), :] = finish(acc)
        return carry

    lax.fori_loop(0, n_rows, row_step, 0)


def _conv(qk_raw, conv_w, conv_b, post_scale, n_ctx):
    b, l, c = qk_raw.shape
    tc = 256
    n_rows = (l - n_ctx) // GRID_W
    return pl.pallas_call(
        functools.partial(_conv_body, n_ctx=n_ctx, n_rows=n_rows),
        grid=(b, c // tc),
        in_specs=[pl.BlockSpec((1, l, tc), lambda i, j: (i, 0, j)),
                  pl.BlockSpec((9, tc), lambda i, j: (0, j)),
                  pl.BlockSpec((1, tc), lambda i, j: (0, j)),
                  pl.BlockSpec((1, tc), lambda i, j: (0, j))],
        out_specs=pl.BlockSpec((1, l, tc), lambda i, j: (i, 0, j)),
        out_shape=jax.ShapeDtypeStruct((b, l, c), BF16),
        compiler_params=pltpu.CompilerParams(
            dimension_semantics=("parallel", "parallel"), vmem_limit_bytes=VMEM_LIMIT),
        name="mlstm_conv",
    )(qk_raw, conv_w.reshape(9, c), conv_b.reshape(1, c), post_scale)


def _mlstm_scan_body(q_ref, k_ref, v_ref, g_ref, o_ref, st_ref, *, n_ctx_chunks, n_chunks):
    c = SCAN_CHUNK
    dv = v_ref.shape[2]
    o_ref[...] = jnp.zeros(o_ref.shape, F32)
    st_ref[...] = jnp.zeros(st_ref.shape, F32)
    mask_fw, mask_bw = _tri_masks(c)
    w_fw = mask_bw.astype(BF16)
    w_bw = mask_fw.astype(BF16)
    ones_col = (lax.broadcasted_iota(jnp.int32, (c, LANES), 1) == 0).astype(BF16)
    pad_rows = jnp.zeros((LANES - 8, c), F32)

    def chunk(dirn, idx, m_prev):
        mask = mask_bw if dirn else mask_fw
        start = pl.multiple_of(idx * c, c)
        q = q_ref[0, pl.ds(start, c), :]
        k = k_ref[0, pl.ds(start, c), :]
        v_aug = jnp.concatenate([v_ref[0, pl.ds(start, c), :], ones_col], axis=1)
        g = g_ref[0, 0, idx]
        i_row = g[2 * dirn:2 * dirn + 1]
        a_row = g[2 * dirn + 1:2 * dirn + 2]
        cum = _dot01_right(g, w_bw if dirn else w_fw)
        b_row = cum[2 * dirn + 1:2 * dirn + 2]
        cols = jnp.concatenate([i_row, b_row, jnp.zeros((6, c), F32), pad_rows], axis=0).T
        i_col = cols[:, 0:1]
        b_col = cols[:, 1:2]

        d = jnp.where(mask, b_col - b_row + i_row, NEG_INF)
        inter = b_col + m_prev
        m_row = jnp.maximum(inter, jnp.max(d, axis=1, keepdims=True))
        s = _dot_nt(q, k) * jnp.exp(d - m_row)
        w_inter = jnp.exp(inter - m_row)
        st = st_ref[dirn]
        num = _dot(s.astype(BF16), v_aug) + _dot(q, st.astype(BF16)) * w_inter
        den = num[:, dv:dv + 1]
        floor = jnp.maximum(jnp.abs(den), jnp.exp(-m_row))
        o_ref[0, pl.ds(start, c), :] += num[:, :dv] * (1.0 / floor)

        b_end = jnp.sum(a_row, axis=1, keepdims=True)
        g_col = b_end - b_col + i_col
        m_new = jnp.maximum(b_end + m_prev, jnp.max(g_col, axis=0, keepdims=True))
        decay = jnp.exp(b_end + m_prev - m_new)
        wk = (k.astype(F32) * jnp.exp(g_col - m_new)).astype(BF16)
        st_ref[dirn] = decay * st + _dot_tn(wk, v_aug)
        return m_new

    def step(s, carry):
        m_fw, m_bw = carry
        m_fw = chunk(0, s, m_fw)
        m_bw = chunk(1, _bw_chunk(s, n_ctx_chunks, n_chunks), m_bw)
        return m_fw, m_bw

    zero = jnp.zeros((1, 1), F32)
    lax.fori_loop(0, n_chunks, step, (zero, zero))


def _mlstm_scan(qk, v, g_rows, n_ctx):
    b, l, _ = qk.shape
    dk = qk.shape[2] // (2 * N_HEADS)
    dv = v.shape[2] // N_HEADS
    n_chunks = l // SCAN_CHUNK
    return pl.pallas_call(
        functools.partial(_mlstm_scan_body, n_ctx_chunks=n_ctx // SCAN_CHUNK, n_chunks=n_chunks),
        grid=(b, N_HEADS),
        in_specs=[pl.BlockSpec((1, l, dk), lambda i, h: (i, 0, h)),
                  pl.BlockSpec((1, l, dk), lambda i, h: (i, 0, N_HEADS + h)),
                  pl.BlockSpec((1, l, dv), lambda i, h: (i, 0, h)),
                  pl.BlockSpec((1, 1, n_chunks, 8, SCAN_CHUNK), lambda i, h: (i, h, 0, 0, 0))],
        out_specs=pl.BlockSpec((1, l, dv), lambda i, h: (i, 0, h)),
        out_shape=jax.ShapeDtypeStruct((b, l, N_HEADS * dv), F32),
        scratch_shapes=[pltpu.VMEM((2, dk, dv + LANES), F32)],
        compiler_params=pltpu.CompilerParams(
            dimension_semantics=("parallel", "parallel"), vmem_limit_bytes=VMEM_LIMIT),
        name="mlstm_scan",
    )(qk, qk, v, g_rows)


def _ret_scan_body(q_ref, k_ref, v_ref, dl_ref, o_ref, st_ref, dm_ref, *, n_ctx_chunks, n_chunks):
    c = SCAN_CHUNK
    o_ref[...] = jnp.zeros(o_ref.shape, F32)
    st_ref[...] = jnp.zeros(st_ref.shape, F32)
    masks = _tri_masks(c)
    row = lax.broadcasted_iota(jnp.int32, (c, c), 0)
    col = lax.broadcasted_iota(jnp.int32, (c, c), 1)
    dist = jnp.abs(row - col).astype(F32)
    rid = lax.broadcasted_iota(jnp.int32, (c, 1), 0).astype(F32)
    log_gamma = []
    for dirn in range(2):
        lg = _log_sigmoid(dl_ref[0, dirn])[0:1, 0:1]
        log_gamma.append(lg)
        dm_ref[dirn] = jnp.where(masks[dirn], jnp.exp(lg * dist), 0.0)

    def chunk(dirn, idx):
        lg = log_gamma[dirn]
        pos = (c - rid) if dirn else (rid + 1.0)
        start = pl.multiple_of(idx * c, c)
        q = q_ref[0, pl.ds(start, c), :]
        k = k_ref[0, pl.ds(start, c), :]
        v = v_ref[0, pl.ds(start, c), :]
        st = st_ref[dirn]
        s = _dot_nt(q, k) * dm_ref[dirn]
        o = _dot(s.astype(BF16), v) + _dot(q, st.astype(BF16)) * jnp.exp(lg * pos)
        o_ref[0, pl.ds(start, c), :] += o
        kd = (k.astype(F32) * jnp.exp(lg * (c - pos))).astype(BF16)
        st_ref[dirn] = jnp.exp(lg * c) * st + _dot_tn(kd, v)

    def step(s, carry):
        chunk(0, s)
        chunk(1, _bw_chunk(s, n_ctx_chunks, n_chunks))
        return carry

    lax.fori_loop(0, n_chunks, step, 0)


def _ret_scan(q, k, v, dl, n_ctx):
    b, l, _ = q.shape
    dk = q.shape[2] // N_HEADS
    dv = v.shape[2] // N_HEADS
    n_chunks = l // SCAN_CHUNK
    return pl.pallas_call(
        functools.partial(_ret_scan_body, n_ctx_chunks=n_ctx // SCAN_CHUNK, n_chunks=n_chunks),
        grid=(b, N_HEADS),
        in_specs=[pl.BlockSpec((1, l, dk), lambda i, h: (i, 0, h)),
                  pl.BlockSpec((1, l, dk), lambda i, h: (i, 0, h)),
                  pl.BlockSpec((1, l, dv), lambda i, h: (i, 0, h)),
                  pl.BlockSpec((1, 2, 8, LANES), lambda i, h: (h, 0, 0, 0))],
        out_specs=pl.BlockSpec((1, l, dv), lambda i, h: (i, 0, h)),
        out_shape=jax.ShapeDtypeStruct((b, l, N_HEADS * dv), F32),
        scratch_shapes=[pltpu.VMEM((2, dk, dv), F32),
                        pltpu.VMEM((2, SCAN_CHUNK, SCAN_CHUNK), F32)],
        compiler_params=pltpu.CompilerParams(
            dimension_semantics=("parallel", "parallel"), vmem_limit_bytes=VMEM_LIMIT),
        name="ret_scan",
    )(q, k, v, dl)


def _gla_scan_body(q_ref, k_ref, v_ref, la_fw_ref, la_bw_ref, o_ref, st_ref, *, n_ctx_chunks, n_chunks):
    c = GLA_CHUNK
    t = GLA_SUB
    n_sub = c // t
    dk = q_ref.shape[2]
    dv = v_ref.shape[2]
    o_ref[...] = jnp.zeros(o_ref.shape, F32)
    st_ref[...] = jnp.zeros(st_ref.shape, F32)
    masks = _tri_masks(c)
    cum_mats = (masks[0].astype(BF16), masks[1].astype(BF16))
    rid = lax.broadcasted_iota(jnp.int32, (c, 1), 0)
    la_refs = (la_fw_ref, la_bw_ref)

    def chunk(dirn, idx):
        start = pl.multiple_of(idx * c, c)
        qf = q_ref[0, pl.ds(start, c), :].astype(F32)
        kf = k_ref[0, pl.ds(start, c), :].astype(F32)
        v = v_ref[0, pl.ds(start, c), :]
        a = la_refs[dirn][0, pl.ds(start, c), :]
        b = _dot01_left(cum_mats[dirn], a)
        refs = []
        for sb in range(n_sub):
            r0 = sb * t + (t - 1 if dirn else 0)
            refs.append(b[r0:r0 + 1] - a[r0:r0 + 1])
        own = refs[0]
        for sb in range(1, n_sub):
            own = jnp.where(rid >= sb * t, refs[sb], own)
        k_rel = kf * jnp.exp(jnp.minimum(own - b, GLA_EXP_CLAMP))
        q_parts, k_parts = [], []
        for sb in range(n_sub):
            in_block = jnp.logical_and(rid >= sb * t, rid < (sb + 1) * t)
            reached = (rid < (sb + 1) * t) if dirn else (rid >= sb * t)
            q_parts.append(jnp.where(reached, qf * jnp.exp(jnp.minimum(b - refs[sb], 0.0)), 0.0).astype(BF16))
            k_parts.append(jnp.where(in_block, k_rel, 0.0).astype(BF16))
        s = _dot_nt(jnp.concatenate(q_parts, axis=1), jnp.concatenate(k_parts, axis=1))
        s = jnp.where(masks[dirn], s, 0.0)
        st = st_ref[dirn]
        q_inter = q_parts[n_sub - 1 if dirn else 0]
        o_ref[0, pl.ds(start, c), :] += _dot(s.astype(BF16), v) + _dot(q_inter, st.astype(BF16))

        r_end = 0 if dirn else c - 1
        b_end = b[r_end:r_end + 1]
        kd = (kf * jnp.exp(b_end - b)).astype(BF16)
        decay_cols = jnp.broadcast_to(jnp.exp(b_end), (dk, dk)).T
        decay = jnp.concatenate([decay_cols] * (dv // dk), axis=1)
        st_ref[dirn] = decay * st + _dot_tn(kd, v)

    def step(s, carry):
        chunk(0, s)
        chunk(1, _bw_chunk(s, n_ctx_chunks, n_chunks))
        return carry

    lax.fori_loop(0, n_chunks, step, 0)


def _gla_scan(q, k, v, la, n_ctx):
    b, l, _ = q.shape
    dk = q.shape[2] // N_HEADS
    dv = v.shape[2] // N_HEADS
    n_chunks = l // GLA_CHUNK
    assert dk == LANES and dv % dk == 0
    return pl.pallas_call(
        functools.partial(_gla_scan_body, n_ctx_chunks=n_ctx // GLA_CHUNK, n_chunks=n_chunks),
        grid=(b, N_HEADS),
        in_specs=[pl.BlockSpec((1, l, dk), lambda i, h: (i, 0, h)),
                  pl.BlockSpec((1, l, dk), lambda i, h: (i, 0, h)),
                  pl.BlockSpec((1, l, dv), lambda i, h: (i, 0, h)),
                  pl.BlockSpec((1, l, dk), lambda i, h: (i, 0, h)),
                  pl.BlockSpec((1, l, dk), lambda i, h: (i, 0, N_HEADS + h))],
        out_specs=pl.BlockSpec((1, l, dv), lambda i, h: (i, 0, h)),
        out_shape=jax.ShapeDtypeStruct((b, l, N_HEADS * dv), F32),
        scratch_shapes=[pltpu.VMEM((2, dk, dv), F32)],
        compiler_params=pltpu.CompilerParams(
            dimension_semantics=("parallel", "parallel"), vmem_limit_bytes=VMEM_LIMIT),
        name="gla_scan",
    )(q, k, v, la, la)


def _out_body(o_ref, z_ref, hw_ref, ow_ref, h_ref, mod_ref, *rest, center, final):
    if final:
        fw_ref, out_ref = rest
    else:
        (out_ref,) = rest
    y = o_ref[0]
    dv = y.shape[1] // N_HEADS
    parts = []
    for hh in range(N_HEADS):
        yh = y[:, hh * dv:(hh + 1) * dv]
        if center:
            yh = yh - jnp.mean(yh, axis=-1, keepdims=True)
        parts.append(yh * lax.rsqrt(jnp.mean(yh * yh, axis=-1, keepdims=True) + NORM_EPS))
    yn = jnp.concatenate(parts, axis=1) * hw_ref[...]
    a = (yn * _silu(z_ref[0].astype(F32))).astype(BF16)
    hn = h_ref[0] + mod_ref[0, 0][2:3] * _dot(a, ow_ref[...])
    if final:
        hn = hn * lax.rsqrt(jnp.mean(hn * hn, axis=-1, keepdims=True) + NORM_EPS) * fw_ref[...]
    out_ref[0] = hn


def _out_call(o, z, head_norm_w, out_w, h, modsel, final_w, n_ctx, center):
    b, l, d = h.shape
    di = o.shape[2]
    final = final_w is not None
    skip = n_ctx // TOKEN_TILE if final else 0
    nt = l // TOKEN_TILE - skip
    tok = lambda i, t: (i, t + skip, 0)
    in_specs = [pl.BlockSpec((1, TOKEN_TILE, di), tok),
                pl.BlockSpec((1, TOKEN_TILE, di), tok),
                pl.BlockSpec((1, di), lambda i, t: (0, 0)),
                pl.BlockSpec((di, d), lambda i, t: (0, 0)),
                pl.BlockSpec((1, TOKEN_TILE, d), tok),
                pl.BlockSpec((1, 1, 3, d), lambda i, t: (i, jnp.minimum(t + skip, 1), 0, 0))]
    args = [o, z, head_norm_w.reshape(1, di), out_w, h, modsel]
    if final:
        in_specs.append(pl.BlockSpec((1, d), lambda i, t: (0, 0)))
        args.append(final_w.reshape(1, d))
    return pl.pallas_call(
        functools.partial(_out_body, center=center, final=final),
        grid=(b, nt),
        in_specs=in_specs,
        out_specs=pl.BlockSpec((1, TOKEN_TILE, d), lambda i, t: (i, t, 0)),
        out_shape=jax.ShapeDtypeStruct((b, nt * TOKEN_TILE, d), F32),
        compiler_params=pltpu.CompilerParams(
            dimension_semantics=("parallel", "parallel"), vmem_limit_bytes=VMEM_LIMIT),
        name="out_proj_final" if final else "out_proj",
    )(*args)


def _pad_cols(w, n):
    return jnp.pad(w, ((0, 0), (0, n - w.shape[1])))


def _mlstm_layer(h, modsel, norm_w, in_w, conv_w, conv_b, gate_b, n_ctx):
    b, l, d = h.shape
    di = 2 * d
    qk_w = 2 * d
    wqk = in_w[:, :qk_w].astype(BF16)
    wv = in_w[:, qk_w:qk_w + di].astype(BF16)
    wz = in_w[:, qk_w + di:qk_w + 2 * di].astype(BF16)
    wg = _pad_cols(in_w[:, qk_w + 2 * di:], LANES).astype(BF16)
    gb = _pad_cols(gate_b.reshape(1, -1), LANES)
    qk_raw, v, z, g = _proj_call(
        _proj_mlstm_body, h, modsel, norm_w, [wqk, wv, wz, wg, gb], [],
        [(qk_w, F32), (di, BF16), (di, BF16), (LANES, F32)], "proj_mlstm")
    head_qk = qk_w // (2 * N_HEADS)
    post_scale = jnp.concatenate([jnp.ones((1, qk_w // 2), F32),
                                  jnp.full((1, qk_w // 2), head_qk ** -0.5, F32)], axis=1)
    qk = _conv(qk_raw, conv_w, conv_b, post_scale, n_ctx)
    n_chunks = l // SCAN_CHUNK
    g_rows = g[:, :, :4 * N_HEADS].reshape(b, n_chunks, SCAN_CHUNK, 4, N_HEADS).transpose(0, 4, 1, 3, 2)
    g_rows = jnp.pad(g_rows, ((0, 0), (0, 0), (0, 0), (0, 4), (0, 0)))
    return _mlstm_scan(qk, v, g_rows, n_ctx), z


def _gla_layer(h, modsel, norm_w, in_w, gk_w2, gk_b, n_ctx):
    b, l, d = h.shape
    di = 2 * d
    key = gk_w2.shape[2]
    wq = in_w[:, :key].astype(BF16)
    wk = in_w[:, key:2 * key].astype(BF16)
    wv = in_w[:, 2 * key:2 * key + di].astype(BF16)
    wz = in_w[:, 2 * key + di:2 * key + 2 * di].astype(BF16)
    wlr = _pad_cols(in_w[:, 2 * key + 2 * di:], LANES).astype(BF16)
    w2 = jnp.zeros((LANES, 2 * key), F32)
    w2 = w2.at[:GLA_RANK, :key].set(gk_w2[0]).at[GLA_RANK:2 * GLA_RANK, key:].set(gk_w2[1]).astype(BF16)
    b2 = gk_b.reshape(1, 2 * key)
    q, k, v, z, la = _proj_call(
        functools.partial(_proj_gla_body, q_scale=(key // N_HEADS) ** -0.5),
        h, modsel, norm_w, [wq, wk, wv, wz, wlr, w2, b2], [],
        [(key, BF16), (key, BF16), (di, BF16), (di, BF16), (2 * key, F32)], "proj_gla")
    return _gla_scan(q, k, v, la, n_ctx), z


def _rope_tables(n_ctx, n_lat, head_k):
    quarter = head_k // 4
    pos = jnp.arange(n_lat, dtype=jnp.int32)
    inv_freq = ROPE_BASE ** (-jnp.arange(quarter, dtype=F32) / quarter)
    cos_parts, sin_parts = [], []
    for p in ((pos // GRID_W).astype(F32), (pos % GRID_W).astype(F32)):
        ang = p[:, None] * inv_freq[None, :]
        cos_parts += [jnp.cos(ang), jnp.cos(ang)]
        sin_parts += [-jnp.sin(ang), jnp.sin(ang)]
    cos = jnp.concatenate(cos_parts, axis=1)
    sin = jnp.concatenate(sin_parts, axis=1)
    cos = jnp.concatenate([jnp.ones((n_ctx, head_k), F32), cos], axis=0)
    sin = jnp.concatenate([jnp.zeros((n_ctx, head_k), F32), sin], axis=0)
    return cos, sin


def _ret_layer(h, modsel, norm_w, in_w, decay_logit, n_ctx):
    b, l, d = h.shape
    di = 2 * d
    key = (in_w.shape[1] - 2 * di) // 2
    head_k = key // N_HEADS
    wq = in_w[:, :key].astype(BF16)
    wk = in_w[:, key:2 * key].astype(BF16)
    wv = in_w[:, 2 * key:2 * key + di].astype(BF16)
    wz = in_w[:, 2 * key + di:].astype(BF16)
    cos, sin = _rope_tables(n_ctx, l - n_ctx, head_k)
    q, k, v, z = _proj_call(
        functools.partial(_proj_ret_body, k_scale=head_k ** -0.5),
        h, modsel, norm_w, [wq, wk, wv, wz], [cos, sin],
        [(key, BF16), (key, BF16), (di, BF16), (di, BF16)], "proj_ret")
    dl = jnp.broadcast_to(decay_logit.astype(F32).T[:, :, None, None], (N_HEADS, 2, 8, LANES))
    return _ret_scan(q, k, v, dl, n_ctx), z


def kernel(x, c, ctx, c_ctx, l0_norm_w, l0_ada_w, l0_ada_b, l0_in_w, l0_conv_w, l0_conv_b, l0_gate_b, l0_head_norm_w, l0_out_w, l1_norm_w, l1_ada_w, l1_ada_b, l1_in_w, l1_gk_w2, l1_gk_b, l1_head_norm_w, l1_out_w, l2_norm_w, l2_ada_w, l2_ada_b, l2_in_w, l2_decay_logit, l2_head_norm_w, l2_out_w, l3_norm_w, l3_ada_w, l3_ada_b, l3_in_w, l3_conv_w, l3_conv_b, l3_gate_b, l3_head_norm_w, l3_out_w, final_norm_w):
    layers = (
        ("mlstm", l0_norm_w, l0_ada_w, l0_ada_b, l0_out_w, l0_head_norm_w, (l0_in_w, l0_conv_w, l0_conv_b, l0_gate_b)),
        ("gla", l1_norm_w, l1_ada_w, l1_ada_b, l1_out_w, l1_head_norm_w, (l1_in_w, l1_gk_w2, l1_gk_b)),
        ("retention", l2_norm_w, l2_ada_w, l2_ada_b, l2_out_w, l2_head_norm_w, (l2_in_w, l2_decay_logit)),
        ("mlstm", l3_norm_w, l3_ada_w, l3_ada_b, l3_out_w, l3_head_norm_w, (l3_in_w, l3_conv_w, l3_conv_b, l3_gate_b)),
    )
    b, _, d = x.shape
    n_ctx = ctx.shape[1]
    assert n_ctx % TOKEN_TILE == 0 and n_ctx % SCAN_CHUNK == 0 and x.shape[1] % SCAN_CHUNK == 0
    h = jnp.concatenate([ctx, x], axis=1)
    cc = jnp.concatenate([c, c_ctx[None, :], jnp.zeros((7, d), F32)], axis=0)
    for li, (kind, norm_w, ada_w, ada_b, out_w, head_norm_w, params) in enumerate(layers):
        last = li == len(layers) - 1
        mod = _ada(cc, ada_w, ada_b)
        mod_lat = mod[:b].reshape(b, 1, 3, d)
        mod_ctx = jnp.broadcast_to(mod[b].reshape(1, 1, 3, d), (b, 1, 3, d))
        modsel = jnp.concatenate([mod_ctx, mod_lat], axis=1)
        if kind == "mlstm":
            o, z = _mlstm_layer(h, modsel, norm_w, *params, n_ctx)
        elif kind == "gla":
            o, z = _gla_layer(h, modsel, norm_w, *params, n_ctx)
        else:
            o, z = _ret_layer(h, modsel, norm_w, *params, n_ctx)
        h = _out_call(o, z, head_norm_w, out_w.astype(BF16), h, modsel,
                      final_norm_w if last else None, n_ctx, center=(kind != "gla"))
    return h
```

```python
import functools

import jax
import jax.numpy as jnp
from jax import lax
from jax.experimental import pallas as pl
from jax.experimental.pallas import tpu as pltpu

N_HEADS = 4
GRID_W = 64
GLA_RANK = 16
GLA_TAU = 16.0
ROPE_BASE = 10000.0
NORM_EPS = 1e-6
NEG_INF = -1e30

TOKEN_TILE = 256
SCAN_CHUNK = 256
GLA_CHUNK = 128
GLA_SUB = 16
GLA_EXP_CLAMP = 80.0
LANES = 128
VMEM_LIMIT = 56 * 1024 * 1024

F32 = jnp.float32
BF16 = jnp.bfloat16


def _silu(x):
    return x * (1.0 / (1.0 + jnp.exp(-x)))


def _log_sigmoid(x):
    return jnp.minimum(x, 0.0) - jnp.log1p(jnp.exp(-jnp.abs(x)))


def _dot(a, b):
    return jnp.dot(a, b, preferred_element_type=F32)


def _split3(a):
    hi = a.astype(BF16)
    r1 = a - hi.astype(F32)
    mid = r1.astype(BF16)
    lo = (r1 - mid.astype(F32)).astype(BF16)
    return hi, mid, lo


def _dot01_left(m01, a):
    hi, mid, lo = _split3(a)
    return _dot(m01, lo) + _dot(m01, mid) + _dot(m01, hi)


def _dot01_right(a, m01):
    hi, mid, lo = _split3(a)
    return _dot(lo, m01) + _dot(mid, m01) + _dot(hi, m01)


def _tri_masks(n):
    row = lax.broadcasted_iota(jnp.int32, (n, n), 0)
    col = lax.broadcasted_iota(jnp.int32, (n, n), 1)
    return col <= row, col >= row


def _bw_chunk(step, n_ctx_chunks, n_chunks):
    return jnp.where(step < n_ctx_chunks, n_ctx_chunks - 1 - step, n_chunks - 1 - (step - n_ctx_chunks))


def _ada_body(c_ref, w_ref, b_ref, o_ref):
    s = _silu(c_ref[...])
    o_ref[...] = jnp.dot(s, w_ref[...], preferred_element_type=F32,
                         precision=lax.Precision.HIGHEST) + b_ref[...]


def _ada(cc, w, b):
    rows, d = cc.shape
    n = w.shape[1]
    tn = 512
    return pl.pallas_call(
        _ada_body,
        grid=(n // tn,),
        in_specs=[pl.BlockSpec((rows, d), lambda j: (0, 0)),
                  pl.BlockSpec((d, tn), lambda j: (0, j)),
                  pl.BlockSpec((1, tn), lambda j: (0, j))],
        out_specs=pl.BlockSpec((rows, tn), lambda j: (0, j)),
        out_shape=jax.ShapeDtypeStruct((rows, n), F32),
        name="ada",
    )(cc, w, b.reshape(1, n))


def _modulated(h_ref, mod_ref, nw_ref):
    x = h_ref[0]
    y = x * lax.rsqrt(jnp.mean(x * x, axis=-1, keepdims=True) + NORM_EPS) * nw_ref[...]
    m = mod_ref[0, 0]
    return (y * (1.0 + m[1:2]) + m[0:1]).astype(BF16)


def _store_chunked_t(ref, x, chunk):
    xt = x.T
    for j in range(x.shape[0] // chunk):
        ref[0, j] = xt[:, j * chunk:(j + 1) * chunk].astype(ref.dtype)


def _proj_mlstm_body(h_ref, mod_ref, nw_ref, wqk_ref, wv_ref, wz_ref, wg_ref, gb_ref,
                     qk_ref, v_ref, z_ref, g_ref):
    u = _modulated(h_ref, mod_ref, nw_ref)
    qk_ref[0] = _dot(u, wqk_ref[...])
    v_ref[0] = _dot(u, wv_ref[...]).astype(BF16)
    z_ref[0] = _dot(u, wz_ref[...]).astype(BF16)
    g = _dot(u, wg_ref[...]) + gb_ref[...]
    lane = lax.broadcasted_iota(jnp.int32, g.shape, 1)
    is_forget = (lane // N_HEADS) % 2 == 1
    g_ref[0] = jnp.where(is_forget, _log_sigmoid(g), g)


def _proj_gla_body(h_ref, mod_ref, nw_ref, wq_ref, wk_ref, wv_ref, wz_ref, wlr_ref, w2_ref, b2_ref,
                   q_ref, kt_ref, v_ref, z_ref, la_ref, lat_ref, *, q_scale):
    u = _modulated(h_ref, mod_ref, nw_ref)
    q_ref[0] = (_dot(u, wq_ref[...]) * q_scale).astype(BF16)
    _store_chunked_t(kt_ref, _dot(u, wk_ref[...]), GLA_CHUNK)
    v_ref[0] = _dot(u, wv_ref[...]).astype(BF16)
    z_ref[0] = _dot(u, wz_ref[...]).astype(BF16)
    code = _dot(u, wlr_ref[...]).astype(BF16)
    pre = _dot(code, w2_ref[...]) + b2_ref[...]
    la = _log_sigmoid(pre) * (1.0 / GLA_TAU)
    la_ref[0] = la
    _store_chunked_t(lat_ref, la, GLA_CHUNK)


def _rope(t, cos, sin):
    outs = []
    for g in range(t.shape[1] // LANES):
        tg = t[:, g * LANES:(g + 1) * LANES]
        cg = cos[:, (g % 2) * LANES:(g % 2 + 1) * LANES]
        sg = sin[:, (g % 2) * LANES:(g % 2 + 1) * LANES]
        outs.append(tg * cg + pltpu.roll(tg, LANES // 2, 1) * sg)
    return jnp.concatenate(outs, axis=1)


def _proj_ret_body(h_ref, mod_ref, nw_ref, wq_ref, wk_ref, wv_ref, wz_ref, cos_ref, sin_ref,
                   q_ref, kt_ref, v_ref, z_ref, *, k_scale):
    u = _modulated(h_ref, mod_ref, nw_ref)
    cos = cos_ref[...]
    sin = sin_ref[...]
    q_ref[0] = _rope(_dot(u, wq_ref[...]), cos, sin).astype(BF16)
    _store_chunked_t(kt_ref, _rope(_dot(u, wk_ref[...]), cos, sin) * k_scale, SCAN_CHUNK)
    v_ref[0] = _dot(u, wv_ref[...]).astype(BF16)
    z_ref[0] = _dot(u, wz_ref[...]).astype(BF16)


def _proj_call(body, h, modsel, norm_w, consts, row_tables, outs, name):
    b, l, d = h.shape
    nt = l // TOKEN_TILE
    in_specs = [pl.BlockSpec((1, TOKEN_TILE, d), lambda i, t: (i, t, 0)),
                pl.BlockSpec((1, 1, 3, d), lambda i, t: (i, jnp.minimum(t, 1), 0, 0)),
                pl.BlockSpec((1, d), lambda i, t: (0, 0))]
    for a in consts:
        in_specs.append(pl.BlockSpec(a.shape, lambda i, t, nd=a.ndim: (0,) * nd))
    for a in row_tables:
        in_specs.append(pl.BlockSpec((TOKEN_TILE, a.shape[1]), lambda i, t: (t, 0)))
    out_specs, out_shape = [], []
    for n, dt, chunk in outs:
        if chunk is None:
            out_specs.append(pl.BlockSpec((1, TOKEN_TILE, n), lambda i, t: (i, t, 0)))
            out_shape.append(jax.ShapeDtypeStruct((b, l, n), dt))
        else:
            out_specs.append(pl.BlockSpec((1, TOKEN_TILE // chunk, n, chunk), lambda i, t: (i, t, 0, 0)))
            out_shape.append(jax.ShapeDtypeStruct((b, l // chunk, n, chunk), dt))
    return pl.pallas_call(
        body,
        grid=(b, nt),
        in_specs=in_specs,
        out_specs=out_specs,
        out_shape=out_shape,
        compiler_params=pltpu.CompilerParams(
            dimension_semantics=("parallel", "parallel"), vmem_limit_bytes=VMEM_LIMIT),
        name=name,
    )(h, modsel, norm_w.reshape(1, d), *consts, *row_tables)


def _conv_body(x_ref, w_ref, b_ref, o_ref, *, n_ctx, n_blocks, scale, transposed):
    blk = SCAN_CHUNK
    l = x_ref.shape[1]
    w = w_ref[...]
    bias = b_ref[...]
    rid = lax.broadcasted_iota(jnp.int32, (blk, 1), 0)

    def emit(chunk, start, acc):
        y = _silu(acc + bias) * scale
        if transposed:
            o_ref[0, chunk] = y.T.astype(BF16)
        else:
            o_ref[0, pl.ds(start, blk), :] = y.astype(BF16)

    def combine(a, bm, cc, first, last):
        return (jnp.where(first, 0.0, pltpu.roll(a, 1, 0)) + bm
                + jnp.where(last, 0.0, pltpu.roll(cc, blk - 1, 0)))

    xc = x_ref[0, 0:blk, :]
    emit(0, 0, combine(xc * w[3:4], xc * w[4:5], xc * w[5:6], rid == 0, rid == blk - 1))

    col = rid % GRID_W

    def lat_block(it, carry):
        base = pl.multiple_of(n_ctx + it * blk, blk)
        mid = x_ref[0, pl.ds(base, blk), :]
        up_edge = x_ref[0, pl.ds(pl.multiple_of(base - GRID_W, GRID_W), GRID_W), :]
        dn_start = pl.multiple_of(jnp.minimum(base + blk, l - GRID_W), GRID_W)
        dn_edge = x_ref[0, pl.ds(dn_start, GRID_W), :]
        up_edge = jnp.where(it > 0, up_edge, 0.0)
        dn_edge = jnp.where(it < n_blocks - 1, dn_edge, 0.0)
        up = jnp.concatenate([up_edge, mid[:blk - GRID_W]], axis=0)
        down = jnp.concatenate([mid[GRID_W:], dn_edge], axis=0)
        a = up * w[0:1] + mid * w[3:4] + down * w[6:7]
        bm = up * w[1:2] + mid * w[4:5] + down * w[7:8]
        cc = up * w[2:3] + mid * w[5:6] + down * w[8:9]
        emit(n_ctx // blk + it, base, combine(a, bm, cc, col == 0, col == GRID_W - 1))
        return carry

    lax.fori_loop(0, n_blocks, lat_block, 0)


def _conv(qk_raw, conv_w, conv_b, n_ctx, col_offset, width, scale, transposed):
    b, l, c = qk_raw.shape
    tc = 256
    blk = SCAN_CHUNK
    assert n_ctx == blk and (l - n_ctx) % blk == 0 and blk % GRID_W == 0
    off = col_offset // tc
    if transposed:
        out_spec = pl.BlockSpec((1, l // blk, tc, blk), lambda i, j: (i, 0, j, 0))
        out_shape = jax.ShapeDtypeStruct((b, l // blk, width, blk), BF16)
    else:
        out_spec = pl.BlockSpec((1, l, tc), lambda i, j: (i, 0, j))
        out_shape = jax.ShapeDtypeStruct((b, l, width), BF16)
    return pl.pallas_call(
        functools.partial(_conv_body, n_ctx=n_ctx, n_blocks=(l - n_ctx) // blk, scale=scale,
                          transposed=transposed),
        grid=(b, width // tc),
        in_specs=[pl.BlockSpec((1, l, tc), lambda i, j: (i, 0, j + off)),
                  pl.BlockSpec((9, tc), lambda i, j: (0, j + off)),
                  pl.BlockSpec((1, tc), lambda i, j: (0, j + off))],
        out_specs=out_spec,
        out_shape=out_shape,
        compiler_params=pltpu.CompilerParams(
            dimension_semantics=("parallel", "parallel"), vmem_limit_bytes=VMEM_LIMIT),
        name="mlstm_conv_kt" if transposed else "mlstm_conv_q",
    )(qk_raw, conv_w.reshape(9, c), conv_b.reshape(1, c))


def _prefix_max(x, lane, backward):
    ax = x.ndim - 1
    n = x.shape[ax]
    sh = 1
    while sh < n:
        if backward:
            x = jnp.maximum(x, jnp.where(lane < n - sh, pltpu.roll(x, n - sh, ax), NEG_INF))
        else:
            x = jnp.maximum(x, jnp.where(lane >= sh, pltpu.roll(x, sh, ax), NEG_INF))
        sh *= 2
    return x


def _mlstm_scan_body(q_ref, kt_ref, v_ref, g_ref, o_ref, st_ref, n_ref, row_ref, *, n_ctx_chunks, n_chunks):
    c = SCAN_CHUNK
    dv = v_ref.shape[2]
    o_ref[...] = jnp.zeros(o_ref.shape, F32)
    st_ref[...] = jnp.zeros(st_ref.shape, F32)
    n_ref[...] = jnp.zeros(n_ref.shape, F32)
    masks = _tri_masks(c)

    g3 = g_ref[0, 0]
    g2 = g3.reshape(n_chunks * 8, c)
    lane = lax.broadcasted_iota(jnp.int32, (1, 1, c), 2)
    token_rows, chunk_rows = [], []
    for dirn in range(2):
        cum = _dot01_right(g2, masks[1 - dirn].astype(BF16)).reshape(n_chunks, 8, c)
        i_r = g3[:, 2 * dirn:2 * dirn + 1]
        a_r = g3[:, 2 * dirn + 1:2 * dirn + 2]
        b_r = cum[:, 2 * dirn + 1:2 * dirn + 2]
        u = i_r - b_r
        b_end = jnp.sum(a_r, axis=2, keepdims=True)
        g_max = jnp.max(b_end + u, axis=2, keepdims=True)
        cm = _prefix_max(u, lane, bool(dirn))
        m = jnp.zeros((1, 1, 1), F32)
        m_prev_l, m_new_l = [None] * n_chunks, [None] * n_chunks
        for s in range(n_chunks):
            idx = s
            if dirn:
                idx = n_ctx_chunks - 1 - s if s < n_ctx_chunks else n_chunks - 1 - (s - n_ctx_chunks)
            m_prev_l[idx] = m
            m = jnp.maximum(b_end[idx:idx + 1] + m, g_max[idx:idx + 1])
            m_new_l[idx] = m
        m_prev = jnp.concatenate(m_prev_l, axis=0)
        m_new = jnp.concatenate(m_new_l, axis=0)
        mm = jnp.maximum(m_prev, cm)
        token_rows += [u, jnp.exp(b_end + u - m_new), mm, b_r + mm]
        chunk_rows += [jnp.broadcast_to(jnp.exp(b_end + m_prev - m_new), (n_chunks, 1, c)),
                       jnp.broadcast_to(m_prev, (n_chunks, 1, c))]
    row_ref[...] = jnp.concatenate(token_rows + chunk_rows + [jnp.zeros((n_chunks, 4, c), F32)], axis=1)

    def lane_bcast_col(row):
        return jnp.broadcast_to(row, (LANES, c)).T

    def chunk(dirn, idx):
        start = pl.multiple_of(idx * c, c)
        q = q_ref[0, pl.ds(start, c), :]
        kt = kt_ref[0, idx]
        v = v_ref[0, pl.ds(start, c), :]
        rows = row_ref[idx]
        u_row = rows[4 * dirn:4 * dirn + 1]
        wk_row = rows[4 * dirn + 1:4 * dirn + 2]
        mm_b = lane_bcast_col(rows[4 * dirn + 2:4 * dirn + 3])
        bmm_b = lane_bcast_col(rows[4 * dirn + 3:4 * dirn + 4])
        decay = rows[8 + 2 * dirn:9 + 2 * dirn, 0:1]
        m_prev = rows[9 + 2 * dirn:10 + 2 * dirn, 0:1]

        p = jnp.exp(jnp.where(masks[dirn], u_row - jnp.concatenate([mm_b] * (c // LANES), axis=1), NEG_INF))
        sc = _dot(q, kt) * p
        w_inter = jnp.exp(m_prev - mm_b)
        st = st_ref[dirn]
        nrm = n_ref[dirn]
        num = _dot(sc.astype(BF16), v) + _dot(q, st.astype(BF16)) * jnp.concatenate([w_inter] * (dv // LANES), axis=1)
        den = jnp.sum(sc, axis=1, keepdims=True) + _dot(q, nrm.astype(BF16)) * w_inter
        rcp = 1.0 / jnp.maximum(jnp.abs(den), jnp.exp(-bmm_b))
        o_ref[0, pl.ds(start, c), :] += num * jnp.concatenate([rcp] * (dv // LANES), axis=1)

        wkt = kt.astype(F32) * wk_row
        st_ref[dirn] = decay * st + _dot(wkt.astype(BF16), v)
        n_ref[dirn] = decay * nrm + jnp.sum(wkt, axis=1, keepdims=True)

    def step(s, carry):
        chunk(0, s)
        chunk(1, _bw_chunk(s, n_ctx_chunks, n_chunks))
        return carry

    lax.fori_loop(0, n_chunks, step, 0)


def _mlstm_scan(q, kt, v, g_rows, n_ctx):
    b, l, _ = q.shape
    dk = q.shape[2] // N_HEADS
    dv = v.shape[2] // N_HEADS
    n_chunks = l // SCAN_CHUNK
    return pl.pallas_call(
        functools.partial(_mlstm_scan_body, n_ctx_chunks=n_ctx // SCAN_CHUNK, n_chunks=n_chunks),
        grid=(b, N_HEADS),
        in_specs=[pl.BlockSpec((1, l, dk), lambda i, h: (i, 0, h)),
                  pl.BlockSpec((1, n_chunks, dk, SCAN_CHUNK), lambda i, h: (i, 0, h, 0)),
                  pl.BlockSpec((1, l, dv), lambda i, h: (i, 0, h)),
                  pl.BlockSpec((1, 1, n_chunks, 8, SCAN_CHUNK), lambda i, h: (i, h, 0, 0, 0))],
        out_specs=pl.BlockSpec((1, l, dv), lambda i, h: (i, 0, h)),
        out_shape=jax.ShapeDtypeStruct((b, l, N_HEADS * dv), F32),
        scratch_shapes=[pltpu.VMEM((2, dk, dv), F32),
                        pltpu.VMEM((2, dk, LANES), F32),
                        pltpu.VMEM((n_chunks, 16, SCAN_CHUNK), F32)],
        compiler_params=pltpu.CompilerParams(
            dimension_semantics=("parallel", "parallel"), vmem_limit_bytes=VMEM_LIMIT),
        name="mlstm_scan",
    )(q, kt, v, g_rows)


def _ret_scan_body(q_ref, kt_ref, v_ref, dl_ref, o_ref, st_ref, dm_ref, *, n_ctx_chunks, n_chunks):
    c = SCAN_CHUNK
    o_ref[...] = jnp.zeros(o_ref.shape, F32)
    st_ref[...] = jnp.zeros(st_ref.shape, F32)
    masks = _tri_masks(c)
    row = lax.broadcasted_iota(jnp.int32, (c, c), 0)
    col = lax.broadcasted_iota(jnp.int32, (c, c), 1)
    dist = jnp.abs(row - col).astype(F32)
    rid = lax.broadcasted_iota(jnp.int32, (c, 1), 0).astype(F32)
    lid = lax.broadcasted_iota(jnp.int32, (1, c), 1).astype(F32)
    log_gamma = []
    for dirn in range(2):
        lg = _log_sigmoid(dl_ref[0, dirn])[0:1, 0:1]
        log_gamma.append(lg)
        dm_ref[dirn] = jnp.where(masks[dirn], jnp.exp(lg * dist), 0.0)

    def chunk(dirn, idx):
        lg = log_gamma[dirn]
        pos_col = (c - rid) if dirn else (rid + 1.0)
        pos_row = (c - lid) if dirn else (lid + 1.0)
        start = pl.multiple_of(idx * c, c)
        q = q_ref[0, pl.ds(start, c), :]
        kt = kt_ref[0, idx]
        v = v_ref[0, pl.ds(start, c), :]
        st = st_ref[dirn]
        s = _dot(q, kt) * dm_ref[dirn]
        o_ref[0, pl.ds(start, c), :] += _dot(s.astype(BF16), v) + _dot(q, st.astype(BF16)) * jnp.exp(lg * pos_col)
        kdt = (kt.astype(F32) * jnp.exp(lg * (c - pos_row))).astype(BF16)
        st_ref[dirn] = jnp.exp(lg * c) * st + _dot(kdt, v)

    def step(s, carry):
        chunk(0, s)
        chunk(1, _bw_chunk(s, n_ctx_chunks, n_chunks))
        return carry

    lax.fori_loop(0, n_chunks, step, 0)


def _ret_scan(q, kt, v, dl, n_ctx):
    b, l, _ = q.shape
    dk = q.shape[2] // N_HEADS
    dv = v.shape[2] // N_HEADS
    n_chunks = l // SCAN_CHUNK
    return pl.pallas_call(
        functools.partial(_ret_scan_body, n_ctx_chunks=n_ctx // SCAN_CHUNK, n_chunks=n_chunks),
        grid=(b, N_HEADS),
        in_specs=[pl.BlockSpec((1, l, dk), lambda i, h: (i, 0, h)),
                  pl.BlockSpec((1, n_chunks, dk, SCAN_CHUNK), lambda i, h: (i, 0, h, 0)),
                  pl.BlockSpec((1, l, dv), lambda i, h: (i, 0, h)),
                  pl.BlockSpec((1, 2, 8, LANES), lambda i, h: (h, 0, 0, 0))],
        out_specs=pl.BlockSpec((1, l, dv), lambda i, h: (i, 0, h)),
        out_shape=jax.ShapeDtypeStruct((b, l, N_HEADS * dv), F32),
        scratch_shapes=[pltpu.VMEM((2, dk, dv), F32),
                        pltpu.VMEM((2, SCAN_CHUNK, SCAN_CHUNK), F32)],
        compiler_params=pltpu.CompilerParams(
            dimension_semantics=("parallel", "parallel"), vmem_limit_bytes=VMEM_LIMIT),
        name="ret_scan",
    )(q, kt, v, dl)


def _gla_scan_body(q_ref, kt_ref, v_ref, la_fw_ref, la_bw_ref, lat_fw_ref, lat_bw_ref, o_ref, st_ref,
                   *, n_ctx_chunks, n_chunks):
    c = GLA_CHUNK
    t = GLA_SUB
    n_sub = c // t
    o_ref[...] = jnp.zeros(o_ref.shape, F32)
    st_ref[...] = jnp.zeros(st_ref.shape, F32)
    masks = _tri_masks(c)
    masks01 = (masks[0].astype(BF16), masks[1].astype(BF16))
    rid = lax.broadcasted_iota(jnp.int32, (c, 1), 0)
    lid = lax.broadcasted_iota(jnp.int32, (1, c), 1)
    la_refs = (la_fw_ref, la_bw_ref)
    lat_refs = (lat_fw_ref, lat_bw_ref)

    def chunk(dirn, idx):
        start = pl.multiple_of(idx * c, c)
        qf = q_ref[0, pl.ds(start, c), :].astype(F32)
        ktf = kt_ref[0, idx].astype(F32)
        v = v_ref[0, pl.ds(start, c), :]
        a = la_refs[dirn][0, pl.ds(start, c), :]
        at = lat_refs[dirn][0, idx]
        b = _dot01_left(masks01[dirn], a)
        bt = _dot01_right(at, masks01[1 - dirn])
        refs_row, refs_col = [], []
        for sb in range(n_sub):
            r0 = sb * t + (t - 1 if dirn else 0)
            refs_row.append(b[r0:r0 + 1] - a[r0:r0 + 1])
            refs_col.append(bt[:, r0:r0 + 1] - at[:, r0:r0 + 1])
        own_t = refs_col[0]
        for sb in range(1, n_sub):
            own_t = jnp.where(lid >= sb * t, refs_col[sb], own_t)
        k_rel_t = ktf * jnp.exp(jnp.minimum(own_t - bt, GLA_EXP_CLAMP))
        q_parts, k_parts = [], []
        for sb in range(n_sub):
            in_block = jnp.logical_and(lid >= sb * t, lid < (sb + 1) * t)
            reached = (rid < (sb + 1) * t) if dirn else (rid >= sb * t)
            q_parts.append(jnp.where(reached, qf * jnp.exp(jnp.minimum(b - refs_row[sb], 0.0)), 0.0).astype(BF16))
            k_parts.append(jnp.where(in_block, k_rel_t, 0.0).astype(BF16))
        s = _dot(jnp.concatenate(q_parts, axis=1), jnp.concatenate(k_parts, axis=0))
        s = jnp.where(masks[dirn], s, 0.0)
        st = st_ref[dirn]
        q_inter = q_parts[n_sub - 1 if dirn else 0]
        o_ref[0, pl.ds(start, c), :] += _dot(s.astype(BF16), v) + _dot(q_inter, st.astype(BF16))

        r_end = 0 if dirn else c - 1
        b_end = bt[:, r_end:r_end + 1]
        kdt = (ktf * jnp.exp(b_end - bt)).astype(BF16)
        st_ref[dirn] = jnp.exp(b_end) * st + _dot(kdt, v)

    def step(s, carry):
        chunk(0, s)
        chunk(1, _bw_chunk(s, n_ctx_chunks, n_chunks))
        return carry

    lax.fori_loop(0, n_chunks, step, 0)


def _gla_scan(q, kt, v, la, lat, n_ctx):
    b, l, _ = q.shape
    dk = q.shape[2] // N_HEADS
    dv = v.shape[2] // N_HEADS
    n_chunks = l // GLA_CHUNK
    tok = lambda i, h: (i, 0, h)
    tok_bw = lambda i, h: (i, 0, N_HEADS + h)
    chm = lambda i, h: (i, 0, h, 0)
    chm_bw = lambda i, h: (i, 0, N_HEADS + h, 0)
    return pl.pallas_call(
        functools.partial(_gla_scan_body, n_ctx_chunks=n_ctx // GLA_CHUNK, n_chunks=n_chunks),
        grid=(b, N_HEADS),
        in_specs=[pl.BlockSpec((1, l, dk), tok),
                  pl.BlockSpec((1, n_chunks, dk, GLA_CHUNK), chm),
                  pl.BlockSpec((1, l, dv), tok),
                  pl.BlockSpec((1, l, dk), tok),
                  pl.BlockSpec((1, l, dk), tok_bw),
                  pl.BlockSpec((1, n_chunks, dk, GLA_CHUNK), chm),
                  pl.BlockSpec((1, n_chunks, dk, GLA_CHUNK), chm_bw)],
        out_specs=pl.BlockSpec((1, l, dv), tok),
        out_shape=jax.ShapeDtypeStruct((b, l, N_HEADS * dv), F32),
        scratch_shapes=[pltpu.VMEM((2, dk, dv), F32)],
        compiler_params=pltpu.CompilerParams(
            dimension_semantics=("parallel", "parallel"), vmem_limit_bytes=VMEM_LIMIT),
        name="gla_scan",
    )(q, kt, v, la, la, lat, lat)


def _out_body(o_ref, z_ref, hw_ref, ow_ref, h_ref, mod_ref, *rest, center, final):
    if final:
        fw_ref, out_ref = rest
    else:
        (out_ref,) = rest
    y = o_ref[0]
    dv = y.shape[1] // N_HEADS
    parts = []
    for hh in range(N_HEADS):
        yh = y[:, hh * dv:(hh + 1) * dv]
        if center:
            yh = yh - jnp.mean(yh, axis=-1, keepdims=True)
        parts.append(yh * lax.rsqrt(jnp.mean(yh * yh, axis=-1, keepdims=True) + NORM_EPS))
    yn = jnp.concatenate(parts, axis=1) * hw_ref[...]
    a = (yn * _silu(z_ref[0].astype(F32))).astype(BF16)
    hn = h_ref[0] + mod_ref[0, 0][2:3] * _dot(a, ow_ref[...])
    if final:
        hn = hn * lax.rsqrt(jnp.mean(hn * hn, axis=-1, keepdims=True) + NORM_EPS) * fw_ref[...]
    out_ref[0] = hn


def _out_call(o, z, head_norm_w, out_w, h, modsel, final_w, n_ctx, center):
    b, l, d = h.shape
    di = o.shape[2]
    final = final_w is not None
    skip = n_ctx // TOKEN_TILE if final else 0
    nt = l // TOKEN_TILE - skip
    tok = lambda i, t: (i, t + skip, 0)
    in_specs = [pl.BlockSpec((1, TOKEN_TILE, di), tok),
                pl.BlockSpec((1, TOKEN_TILE, di), tok),
                pl.BlockSpec((1, di), lambda i, t: (0, 0)),
                pl.BlockSpec((di, d), lambda i, t: (0, 0)),
                pl.BlockSpec((1, TOKEN_TILE, d), tok),
                pl.BlockSpec((1, 1, 3, d), lambda i, t: (i, jnp.minimum(t + skip, 1), 0, 0))]
    args = [o, z, head_norm_w.reshape(1, di), out_w, h, modsel]
    if final:
        in_specs.append(pl.BlockSpec((1, d), lambda i, t: (0, 0)))
        args.append(final_w.reshape(1, d))
    return pl.pallas_call(
        functools.partial(_out_body, center=center, final=final),
        grid=(b, nt),
        in_specs=in_specs,
        out_specs=pl.BlockSpec((1, TOKEN_TILE, d), lambda i, t: (i, t, 0)),
        out_shape=jax.ShapeDtypeStruct((b, nt * TOKEN_TILE, d), F32),
        compiler_params=pltpu.CompilerParams(
            dimension_semantics=("parallel", "parallel"), vmem_limit_bytes=VMEM_LIMIT),
        name="out_proj_final" if final else "out_proj",
    )(*args)


def _pad_cols(w, n):
    return jnp.pad(w, ((0, 0), (0, n - w.shape[1])))


def _mlstm_layer(h, modsel, norm_w, in_w, conv_w, conv_b, gate_b, n_ctx):
    b, l, d = h.shape
    di = 2 * d
    qk_w = 2 * d
    wqk = in_w[:, :qk_w].astype(BF16)
    wv = in_w[:, qk_w:qk_w + di].astype(BF16)
    wz = in_w[:, qk_w + di:qk_w + 2 * di].astype(BF16)
    wg = _pad_cols(in_w[:, qk_w + 2 * di:], LANES).astype(BF16)
    gb = _pad_cols(gate_b.reshape(1, -1), LANES)
    qk_raw, v, z, g = _proj_call(
        _proj_mlstm_body, h, modsel, norm_w, [wqk, wv, wz, wg, gb], [],
        [(qk_w, F32, None), (di, BF16, None), (di, BF16, None), (LANES, F32, None)], "proj_mlstm")
    head_qk = qk_w // (2 * N_HEADS)
    q = _conv(qk_raw, conv_w, conv_b, n_ctx, 0, qk_w // 2, 1.0, False)
    kt = _conv(qk_raw, conv_w, conv_b, n_ctx, qk_w // 2, qk_w // 2, head_qk ** -0.5, True)
    n_chunks = l // SCAN_CHUNK
    g_rows = g[:, :, :4 * N_HEADS].reshape(b, n_chunks, SCAN_CHUNK, 4, N_HEADS).transpose(0, 4, 1, 3, 2)
    g_rows = jnp.pad(g_rows, ((0, 0), (0, 0), (0, 0), (0, 4), (0, 0)))
    return _mlstm_scan(q, kt, v, g_rows, n_ctx), z


def _gla_layer(h, modsel, norm_w, in_w, gk_w2, gk_b, n_ctx):
    b, l, d = h.shape
    di = 2 * d
    key = gk_w2.shape[2]
    assert key // N_HEADS == LANES
    wq = in_w[:, :key].astype(BF16)
    wk = in_w[:, key:2 * key].astype(BF16)
    wv = in_w[:, 2 * key:2 * key + di].astype(BF16)
    wz = in_w[:, 2 * key + di:2 * key + 2 * di].astype(BF16)
    wlr = _pad_cols(in_w[:, 2 * key + 2 * di:], LANES).astype(BF16)
    w2 = jnp.zeros((LANES, 2 * key), F32)
    w2 = w2.at[:GLA_RANK, :key].set(gk_w2[0]).at[GLA_RANK:2 * GLA_RANK, key:].set(gk_w2[1]).astype(BF16)
    b2 = gk_b.reshape(1, 2 * key)
    q, kt, v, z, la, lat = _proj_call(
        functools.partial(_proj_gla_body, q_scale=(key // N_HEADS) ** -0.5),
        h, modsel, norm_w, [wq, wk, wv, wz, wlr, w2, b2], [],
        [(key, BF16, None), (key, BF16, GLA_CHUNK), (di, BF16, None), (di, BF16, None),
         (2 * key, F32, None), (2 * key, F32, GLA_CHUNK)], "proj_gla")
    return _gla_scan(q, kt, v, la, lat, n_ctx), z


def _rope_tables(n_ctx, n_lat, head_k):
    quarter = head_k // 4
    pos = jnp.arange(n_lat, dtype=jnp.int32)
    inv_freq = ROPE_BASE ** (-jnp.arange(quarter, dtype=F32) / quarter)
    cos_parts, sin_parts = [], []
    for p in ((pos // GRID_W).astype(F32), (pos % GRID_W).astype(F32)):
        ang = p[:, None] * inv_freq[None, :]
        cos_parts += [jnp.cos(ang), jnp.cos(ang)]
        sin_parts += [-jnp.sin(ang), jnp.sin(ang)]
    cos = jnp.concatenate(cos_parts, axis=1)
    sin = jnp.concatenate(sin_parts, axis=1)
    cos = jnp.concatenate([jnp.ones((n_ctx, head_k), F32), cos], axis=0)
    sin = jnp.concatenate([jnp.zeros((n_ctx, head_k), F32), sin], axis=0)
    return cos, sin


def _ret_layer(h, modsel, norm_w, in_w, decay_logit, n_ctx):
    b, l, d = h.shape
    di = 2 * d
    key = (in_w.shape[1] - 2 * di) // 2
    head_k = key // N_HEADS
    wq = in_w[:, :key].astype(BF16)
    wk = in_w[:, key:2 * key].astype(BF16)
    wv = in_w[:, 2 * key:2 * key + di].astype(BF16)
    wz = in_w[:, 2 * key + di:].astype(BF16)
    cos, sin = _rope_tables(n_ctx, l - n_ctx, head_k)
    q, kt, v, z = _proj_call(
        functools.partial(_proj_ret_body, k_scale=head_k ** -0.5),
        h, modsel, norm_w, [wq, wk, wv, wz], [cos, sin],
        [(key, BF16, None), (key, BF16, SCAN_CHUNK), (di, BF16, None), (di, BF16, None)], "proj_ret")
    dl = jnp.broadcast_to(decay_logit.astype(F32).T[:, :, None, None], (N_HEADS, 2, 8, LANES))
    return _ret_scan(q, kt, v, dl, n_ctx), z


def kernel(x, c, ctx, c_ctx, l0_norm_w, l0_ada_w, l0_ada_b, l0_in_w, l0_conv_w, l0_conv_b, l0_gate_b, l0_head_norm_w, l0_out_w, l1_norm_w, l1_ada_w, l1_ada_b, l1_in_w, l1_gk_w2, l1_gk_b, l1_head_norm_w, l1_out_w, l2_norm_w, l2_ada_w, l2_ada_b, l2_in_w, l2_decay_logit, l2_head_norm_w, l2_out_w, l3_norm_w, l3_ada_w, l3_ada_b, l3_in_w, l3_conv_w, l3_conv_b, l3_gate_b, l3_head_norm_w, l3_out_w, final_norm_w):
    layers = (
        ("mlstm", l0_norm_w, l0_ada_w, l0_ada_b, l0_out_w, l0_head_norm_w, (l0_in_w, l0_conv_w, l0_conv_b, l0_gate_b)),
        ("gla", l1_norm_w, l1_ada_w, l1_ada_b, l1_out_w, l1_head_norm_w, (l1_in_w, l1_gk_w2, l1_gk_b)),
        ("retention", l2_norm_w, l2_ada_w, l2_ada_b, l2_out_w, l2_head_norm_w, (l2_in_w, l2_decay_logit)),
        ("mlstm", l3_norm_w, l3_ada_w, l3_ada_b, l3_out_w, l3_head_norm_w, (l3_in_w, l3_conv_w, l3_conv_b, l3_gate_b)),
    )
    b, _, d = x.shape
    n_ctx = ctx.shape[1]
    assert n_ctx % TOKEN_TILE == 0 and n_ctx % SCAN_CHUNK == 0 and x.shape[1] % SCAN_CHUNK == 0
    assert TOKEN_TILE == SCAN_CHUNK and TOKEN_TILE % GLA_CHUNK == 0
    h = jnp.concatenate([ctx, x], axis=1)
    cc = jnp.concatenate([c, c_ctx[None, :], jnp.zeros((7, d), F32)], axis=0)
    for li, (kind, norm_w, ada_w, ada_b, out_w, head_norm_w, params) in enumerate(layers):
        last = li == len(layers) - 1
        mod = _ada(cc, ada_w, ada_b)
        mod_lat = mod[:b].reshape(b, 1, 3, d)
        mod_ctx = jnp.broadcast_to(mod[b].reshape(1, 1, 3, d), (b, 1, 3, d))
        modsel = jnp.concatenate([mod_ctx, mod_lat], axis=1)
        if kind == "mlstm":
            o, z = _mlstm_layer(h, modsel, norm_w, *params, n_ctx)
        elif kind == "gla":
            o, z = _gla_layer(h, modsel, norm_w, *params, n_ctx)
        else:
            o, z = _ret_layer(h, modsel, norm_w, *params, n_ctx)
        h = _out_call(o, z, head_norm_w, out_w.astype(BF16), h, modsel,
                      final_norm_w if last else None, n_ctx, center=(kind != "gla"))
    return h
```

```python
import functools

import jax
import jax.numpy as jnp
from jax import lax
from jax.experimental import pallas as pl
from jax.experimental.pallas import tpu as pltpu

N_HEADS = 4
GRID_W = 64
GLA_RANK = 16
GLA_TAU = 16.0
ROPE_BASE = 10000.0
NORM_EPS = 1e-6
NEG_INF = -1e30

TOKEN_TILE = 256
SCAN_CHUNK = 256
GLA_CHUNK = 128
GLA_SUB = 16
GLA_EXP_CLAMP = 80.0
GLA_INTRA_CHUNKS = 2
LANES = 128
VMEM_LIMIT = 56 * 1024 * 1024

F32 = jnp.float32
BF16 = jnp.bfloat16


def _silu(x):
    return x * (1.0 / (1.0 + jnp.exp(-x)))


def _log_sigmoid(x):
    return jnp.minimum(x, 0.0) - jnp.log1p(jnp.exp(-jnp.abs(x)))


def _dot(a, b):
    return jnp.dot(a, b, preferred_element_type=F32)


def _split3(a):
    hi = a.astype(BF16)
    r1 = a - hi.astype(F32)
    mid = r1.astype(BF16)
    lo = (r1 - mid.astype(F32)).astype(BF16)
    return hi, mid, lo


def _dot01_left(m01, a):
    hi, mid, lo = _split3(a)
    return _dot(m01, lo) + _dot(m01, mid) + _dot(m01, hi)


def _dot01_right(a, m01):
    hi, mid, lo = _split3(a)
    return _dot(lo, m01) + _dot(mid, m01) + _dot(hi, m01)


def _tri_masks(n):
    row = lax.broadcasted_iota(jnp.int32, (n, n), 0)
    col = lax.broadcasted_iota(jnp.int32, (n, n), 1)
    return col <= row, col >= row


def _bw_chunk(step, n_ctx_chunks, n_chunks):
    return jnp.where(step < n_ctx_chunks, n_ctx_chunks - 1 - step, n_chunks - 1 - (step - n_ctx_chunks))


def _ada_body(c_ref, w_ref, b_ref, o_ref):
    s = _silu(c_ref[...])
    o_ref[...] = jnp.dot(s, w_ref[...], preferred_element_type=F32,
                         precision=lax.Precision.HIGHEST) + b_ref[...]


def _ada(cc, w, b):
    rows, d = cc.shape
    n = w.shape[1]
    tn = 512
    return pl.pallas_call(
        _ada_body,
        grid=(n // tn,),
        in_specs=[pl.BlockSpec((rows, d), lambda j: (0, 0)),
                  pl.BlockSpec((d, tn), lambda j: (0, j)),
                  pl.BlockSpec((1, tn), lambda j: (0, j))],
        out_specs=pl.BlockSpec((rows, tn), lambda j: (0, j)),
        out_shape=jax.ShapeDtypeStruct((rows, n), F32),
        name="ada",
    )(cc, w, b.reshape(1, n))


def _modulated(h_ref, mod_ref, nw_ref):
    x = h_ref[0]
    y = x * lax.rsqrt(jnp.mean(x * x, axis=-1, keepdims=True) + NORM_EPS) * nw_ref[...]
    m = mod_ref[0, 0]
    return (y * (1.0 + m[1:2]) + m[0:1]).astype(BF16)


def _store_chunked_t(ref, x, chunk):
    xt = x.T
    for j in range(x.shape[0] // chunk):
        ref[0, j] = xt[:, j * chunk:(j + 1) * chunk].astype(ref.dtype)


def _proj_mlstm_body(h_ref, mod_ref, nw_ref, wqk_ref, wv_ref, wz_ref, wg_ref, gb_ref,
                     qk_ref, v_ref, z_ref, g_ref):
    u = _modulated(h_ref, mod_ref, nw_ref)
    qk_ref[0] = _dot(u, wqk_ref[...])
    v_ref[0] = _dot(u, wv_ref[...]).astype(BF16)
    z_ref[0] = _dot(u, wz_ref[...]).astype(BF16)
    g = _dot(u, wg_ref[...]) + gb_ref[...]
    lane = lax.broadcasted_iota(jnp.int32, g.shape, 1)
    is_forget = (lane // N_HEADS) % 2 == 1
    g_ref[0] = jnp.where(is_forget, _log_sigmoid(g), g)


def _proj_gla_body(h_ref, mod_ref, nw_ref, wq_ref, wk_ref, wv_ref, wz_ref, wlr_ref,
                   q_ref, kt_ref, v_ref, z_ref, code_ref, *, q_scale):
    u = _modulated(h_ref, mod_ref, nw_ref)
    q_ref[0] = (_dot(u, wq_ref[...]) * q_scale).astype(BF16)
    _store_chunked_t(kt_ref, _dot(u, wk_ref[...]), GLA_CHUNK)
    v_ref[0] = _dot(u, wv_ref[...]).astype(BF16)
    z_ref[0] = _dot(u, wz_ref[...]).astype(BF16)
    code_ref[0] = _dot(u, wlr_ref[...]).astype(BF16)


def _rope(t, cos, sin):
    outs = []
    for g in range(t.shape[1] // LANES):
        tg = t[:, g * LANES:(g + 1) * LANES]
        cg = cos[:, (g % 2) * LANES:(g % 2 + 1) * LANES]
        sg = sin[:, (g % 2) * LANES:(g % 2 + 1) * LANES]
        outs.append(tg * cg + pltpu.roll(tg, LANES // 2, 1) * sg)
    return jnp.concatenate(outs, axis=1)


def _proj_ret_body(h_ref, mod_ref, nw_ref, wq_ref, wk_ref, wv_ref, wz_ref, cos_ref, sin_ref,
                   q_ref, kt_ref, v_ref, z_ref, *, k_scale):
    u = _modulated(h_ref, mod_ref, nw_ref)
    cos = cos_ref[...]
    sin = sin_ref[...]
    q_ref[0] = _rope(_dot(u, wq_ref[...]), cos, sin).astype(BF16)
    _store_chunked_t(kt_ref, _rope(_dot(u, wk_ref[...]), cos, sin) * k_scale, SCAN_CHUNK)
    v_ref[0] = _dot(u, wv_ref[...]).astype(BF16)
    z_ref[0] = _dot(u, wz_ref[...]).astype(BF16)


def _proj_call(body, h, modsel, norm_w, consts, row_tables, outs, name):
    b, l, d = h.shape
    nt = l // TOKEN_TILE
    in_specs = [pl.BlockSpec((1, TOKEN_TILE, d), lambda i, t: (i, t, 0)),
                pl.BlockSpec((1, 1, 3, d), lambda i, t: (i, jnp.minimum(t, 1), 0, 0)),
                pl.BlockSpec((1, d), lambda i, t: (0, 0))]
    for a in consts:
        in_specs.append(pl.BlockSpec(a.shape, lambda i, t, nd=a.ndim: (0,) * nd))
    for a in row_tables:
        in_specs.append(pl.BlockSpec((TOKEN_TILE, a.shape[1]), lambda i, t: (t, 0)))
    out_specs, out_shape = [], []
    for n, dt, chunk in outs:
        if chunk is None:
            out_specs.append(pl.BlockSpec((1, TOKEN_TILE, n), lambda i, t: (i, t, 0)))
            out_shape.append(jax.ShapeDtypeStruct((b, l, n), dt))
        else:
            out_specs.append(pl.BlockSpec((1, TOKEN_TILE // chunk, n, chunk), lambda i, t: (i, t, 0, 0)))
            out_shape.append(jax.ShapeDtypeStruct((b, l // chunk, n, chunk), dt))
    return pl.pallas_call(
        body,
        grid=(b, nt),
        in_specs=in_specs,
        out_specs=out_specs,
        out_shape=out_shape,
        compiler_params=pltpu.CompilerParams(
            dimension_semantics=("parallel", "parallel"), vmem_limit_bytes=VMEM_LIMIT),
        name=name,
    )(h, modsel, norm_w.reshape(1, d), *consts, *row_tables)


def _conv_body(x_ref, w_ref, b_ref, o_ref, *, n_ctx, n_blocks, scale, transposed):
    blk = SCAN_CHUNK
    l = x_ref.shape[1]
    w = w_ref[...]
    bias = b_ref[...]
    rid = lax.broadcasted_iota(jnp.int32, (blk, 1), 0)

    def emit(chunk, start, acc):
        y = _silu(acc + bias) * scale
        if transposed:
            o_ref[0, chunk] = y.T.astype(BF16)
        else:
            o_ref[0, pl.ds(start, blk), :] = y.astype(BF16)

    def combine(a, bm, cc, first, last):
        return (jnp.where(first, 0.0, pltpu.roll(a, 1, 0)) + bm
                + jnp.where(last, 0.0, pltpu.roll(cc, blk - 1, 0)))

    xc = x_ref[0, 0:blk, :]
    emit(0, 0, combine(xc * w[3:4], xc * w[4:5], xc * w[5:6], rid == 0, rid == blk - 1))

    col = rid % GRID_W

    def lat_block(it, carry):
        base = pl.multiple_of(n_ctx + it * blk, blk)
        mid = x_ref[0, pl.ds(base, blk), :]
        up_edge = x_ref[0, pl.ds(pl.multiple_of(base - GRID_W, GRID_W), GRID_W), :]
        dn_start = pl.multiple_of(jnp.minimum(base + blk, l - GRID_W), GRID_W)
        dn_edge = x_ref[0, pl.ds(dn_start, GRID_W), :]
        up_edge = jnp.where(it > 0, up_edge, 0.0)
        dn_edge = jnp.where(it < n_blocks - 1, dn_edge, 0.0)
        up = jnp.concatenate([up_edge, mid[:blk - GRID_W]], axis=0)
        down = jnp.concatenate([mid[GRID_W:], dn_edge], axis=0)
        a = up * w[0:1] + mid * w[3:4] + down * w[6:7]
        bm = up * w[1:2] + mid * w[4:5] + down * w[7:8]
        cc = up * w[2:3] + mid * w[5:6] + down * w[8:9]
        emit(n_ctx // blk + it, base, combine(a, bm, cc, col == 0, col == GRID_W - 1))
        return carry

    lax.fori_loop(0, n_blocks, lat_block, 0)


def _conv(qk_raw, conv_w, conv_b, n_ctx, col_offset, width, scale, transposed):
    b, l, c = qk_raw.shape
    tc = 256
    blk = SCAN_CHUNK
    assert n_ctx == blk and (l - n_ctx) % blk == 0 and blk % GRID_W == 0
    off = col_offset // tc
    if transposed:
        out_spec = pl.BlockSpec((1, l // blk, tc, blk), lambda i, j: (i, 0, j, 0))
        out_shape = jax.ShapeDtypeStruct((b, l // blk, width, blk), BF16)
    else:
        out_spec = pl.BlockSpec((1, l, tc), lambda i, j: (i, 0, j))
        out_shape = jax.ShapeDtypeStruct((b, l, width), BF16)
    return pl.pallas_call(
        functools.partial(_conv_body, n_ctx=n_ctx, n_blocks=(l - n_ctx) // blk, scale=scale,
                          transposed=transposed),
        grid=(b, width // tc),
        in_specs=[pl.BlockSpec((1, l, tc), lambda i, j: (i, 0, j + off)),
                  pl.BlockSpec((9, tc), lambda i, j: (0, j + off)),
                  pl.BlockSpec((1, tc), lambda i, j: (0, j + off))],
        out_specs=out_spec,
        out_shape=out_shape,
        compiler_params=pltpu.CompilerParams(
            dimension_semantics=("parallel", "parallel"), vmem_limit_bytes=VMEM_LIMIT),
        name="mlstm_conv_kt" if transposed else "mlstm_conv_q",
    )(qk_raw, conv_w.reshape(9, c), conv_b.reshape(1, c))


def _prefix_max(x, lane, backward):
    ax = x.ndim - 1
    n = x.shape[ax]
    sh = 1
    while sh < n:
        if backward:
            x = jnp.maximum(x, jnp.where(lane < n - sh, pltpu.roll(x, n - sh, ax), NEG_INF))
        else:
            x = jnp.maximum(x, jnp.where(lane >= sh, pltpu.roll(x, sh, ax), NEG_INF))
        sh *= 2
    return x


def _mlstm_scan_body(q_ref, kt_ref, v_ref, g_ref, o_ref, st_ref, n_ref, row_ref, *, n_ctx_chunks, n_chunks):
    c = SCAN_CHUNK
    dv = v_ref.shape[2]
    o_ref[...] = jnp.zeros(o_ref.shape, F32)
    st_ref[...] = jnp.zeros(st_ref.shape, F32)
    n_ref[...] = jnp.zeros(n_ref.shape, F32)
    masks = _tri_masks(c)

    g3 = g_ref[0, 0]
    g2 = g3.reshape(n_chunks * 8, c)
    lane = lax.broadcasted_iota(jnp.int32, (1, 1, c), 2)
    token_rows, chunk_rows = [], []
    for dirn in range(2):
        cum = _dot01_right(g2, masks[1 - dirn].astype(BF16)).reshape(n_chunks, 8, c)
        i_r = g3[:, 2 * dirn:2 * dirn + 1]
        a_r = g3[:, 2 * dirn + 1:2 * dirn + 2]
        b_r = cum[:, 2 * dirn + 1:2 * dirn + 2]
        u = i_r - b_r
        b_end = jnp.sum(a_r, axis=2, keepdims=True)
        g_max = jnp.max(b_end + u, axis=2, keepdims=True)
        cm = _prefix_max(u, lane, bool(dirn))
        m = jnp.zeros((1, 1, 1), F32)
        m_prev_l, m_new_l = [None] * n_chunks, [None] * n_chunks
        for s in range(n_chunks):
            idx = s
            if dirn:
                idx = n_ctx_chunks - 1 - s if s < n_ctx_chunks else n_chunks - 1 - (s - n_ctx_chunks)
            m_prev_l[idx] = m
            m = jnp.maximum(b_end[idx:idx + 1] + m, g_max[idx:idx + 1])
            m_new_l[idx] = m
        m_prev = jnp.concatenate(m_prev_l, axis=0)
        m_new = jnp.concatenate(m_new_l, axis=0)
        mm = jnp.maximum(m_prev, cm)
        token_rows += [u, jnp.exp(b_end + u - m_new), mm, b_r + mm]
        chunk_rows += [jnp.broadcast_to(jnp.exp(b_end + m_prev - m_new), (n_chunks, 1, c)),
                       jnp.broadcast_to(m_prev, (n_chunks, 1, c))]
    row_ref[...] = jnp.concatenate(token_rows + chunk_rows + [jnp.zeros((n_chunks, 4, c), F32)], axis=1)

    def lane_bcast_col(row):
        return jnp.broadcast_to(row, (LANES, c)).T

    def step(s, carry):
        dirs = (0, 1)
        idxs = (s, _bw_chunk(s, n_ctx_chunks, n_chunks))
        starts = [pl.multiple_of(idx * c, c) for idx in idxs]
        q = [q_ref[0, pl.ds(st, c), :] for st in starts]
        kt = [kt_ref[0, idx] for idx in idxs]
        v = [v_ref[0, pl.ds(st, c), :] for st in starts]
        rows = [row_ref[idx] for idx in idxs]
        state = [st_ref[d] for d in dirs]
        nrm = [n_ref[d] for d in dirs]
        qk = [_dot(q[d], kt[d]) for d in dirs]
        mm_b = [lane_bcast_col(rows[d][4 * d + 2:4 * d + 3]) for d in dirs]
        bmm_b = [lane_bcast_col(rows[d][4 * d + 3:4 * d + 4]) for d in dirs]
        m_prev = [rows[d][9 + 2 * d:10 + 2 * d, 0:1] for d in dirs]
        decay = [rows[d][8 + 2 * d:9 + 2 * d, 0:1] for d in dirs]
        sc = [qk[d] * jnp.exp(jnp.where(masks[d], rows[d][4 * d:4 * d + 1]
                                        - jnp.concatenate([mm_b[d]] * (c // LANES), axis=1), NEG_INF))
              for d in dirs]
        w_inter = [jnp.exp(m_prev[d] - mm_b[d]) for d in dirs]
        intra = [_dot(sc[d].astype(BF16), v[d]) for d in dirs]
        inter = [_dot(q[d], state[d].astype(BF16)) for d in dirs]
        qn = [_dot(q[d], nrm[d].astype(BF16)) for d in dirs]
        wkt = [kt[d].astype(F32) * rows[d][4 * d + 1:4 * d + 2] for d in dirs]
        upd = [_dot(wkt[d].astype(BF16), v[d]) for d in dirs]
        for d in dirs:
            num = intra[d] + inter[d] * jnp.concatenate([w_inter[d]] * (dv // LANES), axis=1)
            den = jnp.sum(sc[d], axis=1, keepdims=True) + qn[d] * w_inter[d]
            rcp = 1.0 / jnp.maximum(jnp.abs(den), jnp.exp(-bmm_b[d]))
            o_ref[0, pl.ds(starts[d], c), :] += num * jnp.concatenate([rcp] * (dv // LANES), axis=1)
        for d in dirs:
            st_ref[d] = decay[d] * state[d] + upd[d]
            n_ref[d] = decay[d] * nrm[d] + jnp.sum(wkt[d], axis=1, keepdims=True)
        return carry

    lax.fori_loop(0, n_chunks, step, 0)


def _mlstm_scan(q, kt, v, g_rows, n_ctx):
    b, l, _ = q.shape
    dk = q.shape[2] // N_HEADS
    dv = v.shape[2] // N_HEADS
    n_chunks = l // SCAN_CHUNK
    return pl.pallas_call(
        functools.partial(_mlstm_scan_body, n_ctx_chunks=n_ctx // SCAN_CHUNK, n_chunks=n_chunks),
        grid=(b, N_HEADS),
        in_specs=[pl.BlockSpec((1, l, dk), lambda i, h: (i, 0, h)),
                  pl.BlockSpec((1, n_chunks, dk, SCAN_CHUNK), lambda i, h: (i, 0, h, 0)),
                  pl.BlockSpec((1, l, dv), lambda i, h: (i, 0, h)),
                  pl.BlockSpec((1, 1, n_chunks, 8, SCAN_CHUNK), lambda i, h: (i, h, 0, 0, 0))],
        out_specs=pl.BlockSpec((1, l, dv), lambda i, h: (i, 0, h)),
        out_shape=jax.ShapeDtypeStruct((b, l, N_HEADS * dv), F32),
        scratch_shapes=[pltpu.VMEM((2, dk, dv), F32),
                        pltpu.VMEM((2, dk, LANES), F32),
                        pltpu.VMEM((n_chunks, 16, SCAN_CHUNK), F32)],
        compiler_params=pltpu.CompilerParams(
            dimension_semantics=("parallel", "parallel"), vmem_limit_bytes=VMEM_LIMIT),
        name="mlstm_scan",
    )(q, kt, v, g_rows)


def _ret_scan_body(q_ref, kt_ref, v_ref, dl_ref, o_ref, st_ref, dm_ref, *, n_ctx_chunks, n_chunks):
    c = SCAN_CHUNK
    o_ref[...] = jnp.zeros(o_ref.shape, F32)
    st_ref[...] = jnp.zeros(st_ref.shape, F32)
    masks = _tri_masks(c)
    row = lax.broadcasted_iota(jnp.int32, (c, c), 0)
    col = lax.broadcasted_iota(jnp.int32, (c, c), 1)
    dist = jnp.abs(row - col).astype(F32)
    rid = lax.broadcasted_iota(jnp.int32, (c, 1), 0).astype(F32)
    lid = lax.broadcasted_iota(jnp.int32, (1, c), 1).astype(F32)
    log_gamma = []
    for dirn in range(2):
        lg = _log_sigmoid(dl_ref[0, dirn])[0:1, 0:1]
        log_gamma.append(lg)
        dm_ref[dirn] = jnp.where(masks[dirn], jnp.exp(lg * dist), 0.0)

    pos_col = (rid + 1.0, c - rid)
    pos_row = (lid + 1.0, c - lid)

    def step(s, carry):
        dirs = (0, 1)
        idxs = (s, _bw_chunk(s, n_ctx_chunks, n_chunks))
        starts = [pl.multiple_of(idx * c, c) for idx in idxs]
        q = [q_ref[0, pl.ds(st, c), :] for st in starts]
        kt = [kt_ref[0, idx] for idx in idxs]
        v = [v_ref[0, pl.ds(st, c), :] for st in starts]
        state = [st_ref[d] for d in dirs]
        sc = [(_dot(q[d], kt[d]) * dm_ref[d]).astype(BF16) for d in dirs]
        intra = [_dot(sc[d], v[d]) for d in dirs]
        inter = [_dot(q[d], state[d].astype(BF16)) for d in dirs]
        kdt = [(kt[d].astype(F32) * jnp.exp(log_gamma[d] * (c - pos_row[d]))).astype(BF16) for d in dirs]
        upd = [_dot(kdt[d], v[d]) for d in dirs]
        for d in dirs:
            o_ref[0, pl.ds(starts[d], c), :] += intra[d] + inter[d] * jnp.exp(log_gamma[d] * pos_col[d])
        for d in dirs:
            st_ref[d] = jnp.exp(log_gamma[d] * c) * state[d] + upd[d]
        return carry

    lax.fori_loop(0, n_chunks, step, 0, unroll=2)


def _ret_scan(q, kt, v, dl, n_ctx):
    b, l, _ = q.shape
    dk = q.shape[2] // N_HEADS
    dv = v.shape[2] // N_HEADS
    n_chunks = l // SCAN_CHUNK
    return pl.pallas_call(
        functools.partial(_ret_scan_body, n_ctx_chunks=n_ctx // SCAN_CHUNK, n_chunks=n_chunks),
        grid=(b, N_HEADS),
        in_specs=[pl.BlockSpec((1, l, dk), lambda i, h: (i, 0, h)),
                  pl.BlockSpec((1, n_chunks, dk, SCAN_CHUNK), lambda i, h: (i, 0, h, 0)),
                  pl.BlockSpec((1, l, dv), lambda i, h: (i, 0, h)),
                  pl.BlockSpec((1, 2, 8, LANES), lambda i, h: (h, 0, 0, 0))],
        out_specs=pl.BlockSpec((1, l, dv), lambda i, h: (i, 0, h)),
        out_shape=jax.ShapeDtypeStruct((b, l, N_HEADS * dv), F32),
        scratch_shapes=[pltpu.VMEM((2, dk, dv), F32),
                        pltpu.VMEM((2, SCAN_CHUNK, SCAN_CHUNK), F32)],
        compiler_params=pltpu.CompilerParams(
            dimension_semantics=("parallel", "parallel"), vmem_limit_bytes=VMEM_LIMIT),
        name="ret_scan",
    )(q, kt, v, dl)


def _gla_scan_body(q_ref, kt_ref, v_ref, code_ref, w2_ref, b2_ref, o_ref,
                   st_ref, s_ref, qi_ref, kdt_ref, e_ref, *, n_ctx_chunks, n_chunks):
    c = GLA_CHUNK
    t = GLA_SUB
    n_sub = c // t
    dk = q_ref.shape[2]
    dv = v_ref.shape[2]
    o_ref[...] = jnp.zeros(o_ref.shape, F32)
    st_ref[...] = jnp.zeros(st_ref.shape, F32)
    masks = _tri_masks(c)
    masks01 = (masks[0].astype(BF16), masks[1].astype(BF16))
    lid = lax.broadcasted_iota(jnp.int32, (1, c), 1)

    def intra(it, carry):
        chains = [(it * GLA_INTRA_CHUNKS + j, dirn) for j in range(GLA_INTRA_CHUNKS) for dirn in range(2)]
        starts = [pl.multiple_of(idx * c, c) for idx, _ in chains]
        qf = [q_ref[0, pl.ds(st, c), :].astype(F32) for st in starts]
        ktf = [kt_ref[0, idx].astype(F32) for idx, _ in chains]
        code = [code_ref[0, pl.ds(st, c), :] for st in starts]
        a = [_log_sigmoid(_dot(code[k], w2_ref[d]) + b2_ref[d]) * (1.0 / GLA_TAU) for k, (_, d) in enumerate(chains)]
        parts = [_split3(x) for x in a]
        b = [None] * len(chains)
        for p in (2, 1, 0):
            for k, (_, d) in enumerate(chains):
                term = _dot(masks01[d], parts[k][p])
                b[k] = term if b[k] is None else b[k] + term
        qp, kp, kdt, e_end = [], [], [], []
        for k, (_, dirn) in enumerate(chains):
            refs = []
            for sb in range(n_sub):
                r0 = sb * t + (t - 1 if dirn else 0)
                refs.append(b[k][r0:r0 + 1] - a[k][r0:r0 + 1])
            own = jnp.concatenate([jnp.broadcast_to(r, (t, dk)) for r in refs], axis=0)
            b_rel = b[k] - own
            q_own = qf[k] * jnp.exp(b_rel)
            k_rel_t = (ktf[k] * jnp.exp(jnp.minimum(-(b_rel.T), GLA_EXP_CLAMP))).astype(BF16)
            r_end = 0 if dirn else c - 1
            b_end = b[k][r_end:r_end + 1]
            kdt.append((ktf[k] * jnp.exp((b_end - b[k]).T)).astype(BF16))
            e_end.append(jnp.broadcast_to(jnp.exp(b_end), (8, dk)))
            col_blocks = []
            for sb in range(n_sub):
                pieces = []
                for j in range(n_sub):
                    if (j <= sb) if dirn else (j >= sb):
                        qj = q_own[j * t:(j + 1) * t]
                        pieces.append(qj if j == sb else qj * jnp.exp(refs[j] - refs[sb]))
                    else:
                        pieces.append(jnp.zeros((t, dk), F32))
                col_blocks.append(jnp.concatenate(pieces, axis=0).astype(BF16))
            qp.append(col_blocks)
            kp.append(jnp.concatenate(
                [jnp.where(jnp.logical_and(lid >= sb * t, lid < (sb + 1) * t), k_rel_t, 0.0)
                 for sb in range(n_sub)], axis=0))
        s = [_dot(jnp.concatenate(qp[k], axis=1), kp[k]) for k in range(len(chains))]
        for k, (idx, dirn) in enumerate(chains):
            s_ref[dirn, pl.ds(starts[k], c), :] = jnp.where(masks[dirn], s[k], 0.0).astype(BF16)
            qi_ref[dirn, pl.ds(starts[k], c), :] = qp[k][n_sub - 1 if dirn else 0]
            kdt_ref[dirn, idx] = kdt[k]
            e_ref[dirn, idx] = e_end[k]
        return carry

    lax.fori_loop(0, n_chunks // GLA_INTRA_CHUNKS, intra, 0)

    def step(s, carry):
        dirs = (0, 1)
        idxs = (s, _bw_chunk(s, n_ctx_chunks, n_chunks))
        starts = [pl.multiple_of(idx * c, c) for idx in idxs]
        v = [v_ref[0, pl.ds(st, c), :] for st in starts]
        state = [st_ref[d] for d in dirs]
        intra = [_dot(s_ref[d, pl.ds(starts[d], c), :], v[d]) for d in dirs]
        inter = [_dot(qi_ref[d, pl.ds(starts[d], c), :], state[d].astype(BF16)) for d in dirs]
        upd = [_dot(kdt_ref[d, idxs[d]], v[d]) for d in dirs]
        e_cols = [jnp.concatenate([e_ref[d, idxs[d]]] * (dk // 8), axis=0).T for d in dirs]
        for d in dirs:
            o_ref[0, pl.ds(starts[d], c), :] += intra[d] + inter[d]
        for d in dirs:
            st_ref[d] = jnp.concatenate([e_cols[d]] * (dv // dk), axis=1) * state[d] + upd[d]
        return carry

    lax.fori_loop(0, n_chunks, step, 0, unroll=2)


def _gla_scan(q, kt, v, code, w2, b2, n_ctx):
    b, l, _ = q.shape
    dk = q.shape[2] // N_HEADS
    dv = v.shape[2] // N_HEADS
    n_chunks = l // GLA_CHUNK
    tok = lambda i, h: (i, 0, h)
    return pl.pallas_call(
        functools.partial(_gla_scan_body, n_ctx_chunks=n_ctx // GLA_CHUNK, n_chunks=n_chunks),
        grid=(b, N_HEADS),
        in_specs=[pl.BlockSpec((1, l, dk), tok),
                  pl.BlockSpec((1, n_chunks, dk, GLA_CHUNK), lambda i, h: (i, 0, h, 0)),
                  pl.BlockSpec((1, l, dv), tok),
                  pl.BlockSpec((1, l, LANES), lambda i, h: (i, 0, 0)),
                  pl.BlockSpec((2, LANES, dk), lambda i, h: (0, 0, h)),
                  pl.BlockSpec((2, 1, dk), lambda i, h: (0, 0, h))],
        out_specs=pl.BlockSpec((1, l, dv), tok),
        out_shape=jax.ShapeDtypeStruct((b, l, N_HEADS * dv), F32),
        scratch_shapes=[pltpu.VMEM((2, dk, dv), F32),
                        pltpu.VMEM((2, l, GLA_CHUNK), BF16),
                        pltpu.VMEM((2, l, dk), BF16),
                        pltpu.VMEM((2, n_chunks, dk, GLA_CHUNK), BF16),
                        pltpu.VMEM((2, n_chunks, 8, dk), F32)],
        compiler_params=pltpu.CompilerParams(
            dimension_semantics=("parallel", "parallel"), vmem_limit_bytes=VMEM_LIMIT),
        name="gla_scan",
    )(q, kt, v, code, w2, b2)


def _out_body(o_ref, z_ref, hw_ref, ow_ref, h_ref, mod_ref, *rest, center, final):
    if final:
        fw_ref, out_ref = rest
    else:
        (out_ref,) = rest
    y = o_ref[0]
    dv = y.shape[1] // N_HEADS
    parts = []
    for hh in range(N_HEADS):
        yh = y[:, hh * dv:(hh + 1) * dv]
        if center:
            yh = yh - jnp.mean(yh, axis=-1, keepdims=True)
        parts.append(yh * lax.rsqrt(jnp.mean(yh * yh, axis=-1, keepdims=True) + NORM_EPS))
    yn = jnp.concatenate(parts, axis=1) * hw_ref[...]
    a = (yn * _silu(z_ref[0].astype(F32))).astype(BF16)
    hn = h_ref[0] + mod_ref[0, 0][2:3] * _dot(a, ow_ref[...])
    if final:
        hn = hn * lax.rsqrt(jnp.mean(hn * hn, axis=-1, keepdims=True) + NORM_EPS) * fw_ref[...]
    out_ref[0] = hn


def _out_call(o, z, head_norm_w, out_w, h, modsel, final_w, n_ctx, center):
    b, l, d = h.shape
    di = o.shape[2]
    final = final_w is not None
    skip = n_ctx // TOKEN_TILE if final else 0
    nt = l // TOKEN_TILE - skip
    tok = lambda i, t: (i, t + skip, 0)
    in_specs = [pl.BlockSpec((1, TOKEN_TILE, di), tok),
                pl.BlockSpec((1, TOKEN_TILE, di), tok),
                pl.BlockSpec((1, di), lambda i, t: (0, 0)),
                pl.BlockSpec((di, d), lambda i, t: (0, 0)),
                pl.BlockSpec((1, TOKEN_TILE, d), tok),
                pl.BlockSpec((1, 1, 3, d), lambda i, t: (i, jnp.minimum(t + skip, 1), 0, 0))]
    args = [o, z, head_norm_w.reshape(1, di), out_w, h, modsel]
    if final:
        in_specs.append(pl.BlockSpec((1, d), lambda i, t: (0, 0)))
        args.append(final_w.reshape(1, d))
    return pl.pallas_call(
        functools.partial(_out_body, center=center, final=final),
        grid=(b, nt),
        in_specs=in_specs,
        out_specs=pl.BlockSpec((1, TOKEN_TILE, d), lambda i, t: (i, t, 0)),
        out_shape=jax.ShapeDtypeStruct((b, nt * TOKEN_TILE, d), F32),
        compiler_params=pltpu.CompilerParams(
            dimension_semantics=("parallel", "parallel"), vmem_limit_bytes=VMEM_LIMIT),
        name="out_proj_final" if final else "out_proj",
    )(*args)


def _pad_cols(w, n):
    return jnp.pad(w, ((0, 0), (0, n - w.shape[1])))


def _mlstm_layer(h, modsel, norm_w, in_w, conv_w, conv_b, gate_b, n_ctx):
    b, l, d = h.shape
    di = 2 * d
    qk_w = 2 * d
    wqk = in_w[:, :qk_w].astype(BF16)
    wv = in_w[:, qk_w:qk_w + di].astype(BF16)
    wz = in_w[:, qk_w + di:qk_w + 2 * di].astype(BF16)
    wg = _pad_cols(in_w[:, qk_w + 2 * di:], LANES).astype(BF16)
    gb = _pad_cols(gate_b.reshape(1, -1), LANES)
    qk_raw, v, z, g = _proj_call(
        _proj_mlstm_body, h, modsel, norm_w, [wqk, wv, wz, wg, gb], [],
        [(qk_w, F32, None), (di, BF16, None), (di, BF16, None), (LANES, F32, None)], "proj_mlstm")
    head_qk = qk_w // (2 * N_HEADS)
    q = _conv(qk_raw, conv_w, conv_b, n_ctx, 0, qk_w // 2, 1.0, False)
    kt = _conv(qk_raw, conv_w, conv_b, n_ctx, qk_w // 2, qk_w // 2, head_qk ** -0.5, True)
    n_chunks = l // SCAN_CHUNK
    g_rows = g[:, :, :4 * N_HEADS].reshape(b, n_chunks, SCAN_CHUNK, 4, N_HEADS).transpose(0, 4, 1, 3, 2)
    g_rows = jnp.pad(g_rows, ((0, 0), (0, 0), (0, 0), (0, 4), (0, 0)))
    return _mlstm_scan(q, kt, v, g_rows, n_ctx), z


def _gla_layer(h, modsel, norm_w, in_w, gk_w2, gk_b, n_ctx):
    b, l, d = h.shape
    di = 2 * d
    key = gk_w2.shape[2]
    assert key // N_HEADS == LANES
    wq = in_w[:, :key].astype(BF16)
    wk = in_w[:, key:2 * key].astype(BF16)
    wv = in_w[:, 2 * key:2 * key + di].astype(BF16)
    wz = in_w[:, 2 * key + di:2 * key + 2 * di].astype(BF16)
    wlr = _pad_cols(in_w[:, 2 * key + 2 * di:], LANES).astype(BF16)
    q, kt, v, z, code = _proj_call(
        functools.partial(_proj_gla_body, q_scale=(key // N_HEADS) ** -0.5),
        h, modsel, norm_w, [wq, wk, wv, wz, wlr], [],
        [(key, BF16, None), (key, BF16, GLA_CHUNK), (di, BF16, None), (di, BF16, None),
         (LANES, BF16, None)], "proj_gla")
    w2 = jnp.zeros((2, LANES, key), F32)
    w2 = w2.at[0, :GLA_RANK].set(gk_w2[0]).at[1, GLA_RANK:2 * GLA_RANK].set(gk_w2[1]).astype(BF16)
    return _gla_scan(q, kt, v, code, w2, gk_b.reshape(2, 1, key), n_ctx), z


def _rope_tables(n_ctx, n_lat, head_k):
    quarter = head_k // 4
    pos = jnp.arange(n_lat, dtype=jnp.int32)
    inv_freq = ROPE_BASE ** (-jnp.arange(quarter, dtype=F32) / quarter)
    cos_parts, sin_parts = [], []
    for p in ((pos // GRID_W).astype(F32), (pos % GRID_W).astype(F32)):
        ang = p[:, None] * inv_freq[None, :]
        cos_parts += [jnp.cos(ang), jnp.cos(ang)]
        sin_parts += [-jnp.sin(ang), jnp.sin(ang)]
    cos = jnp.concatenate(cos_parts, axis=1)
    sin = jnp.concatenate(sin_parts, axis=1)
    cos = jnp.concatenate([jnp.ones((n_ctx, head_k), F32), cos], axis=0)
    sin = jnp.concatenate([jnp.zeros((n_ctx, head_k), F32), sin], axis=0)
    return cos, sin


def _ret_layer(h, modsel, norm_w, in_w, decay_logit, n_ctx):
    b, l, d = h.shape
    di = 2 * d
    key = (in_w.shape[1] - 2 * di) // 2
    head_k = key // N_HEADS
    wq = in_w[:, :key].astype(BF16)
    wk = in_w[:, key:2 * key].astype(BF16)
    wv = in_w[:, 2 * key:2 * key + di].astype(BF16)
    wz = in_w[:, 2 * key + di:].astype(BF16)
    cos, sin = _rope_tables(n_ctx, l - n_ctx, head_k)
    q, kt, v, z = _proj_call(
        functools.partial(_proj_ret_body, k_scale=head_k ** -0.5),
        h, modsel, norm_w, [wq, wk, wv, wz], [cos, sin],
        [(key, BF16, None), (key, BF16, SCAN_CHUNK), (di, BF16, None), (di, BF16, None)], "proj_ret")
    dl = jnp.broadcast_to(decay_logit.astype(F32).T[:, :, None, None], (N_HEADS, 2, 8, LANES))
    return _ret_scan(q, kt, v, dl, n_ctx), z


def kernel(x, c, ctx, c_ctx, l0_norm_w, l0_ada_w, l0_ada_b, l0_in_w, l0_conv_w, l0_conv_b, l0_gate_b, l0_head_norm_w, l0_out_w, l1_norm_w, l1_ada_w, l1_ada_b, l1_in_w, l1_gk_w2, l1_gk_b, l1_head_norm_w, l1_out_w, l2_norm_w, l2_ada_w, l2_ada_b, l2_in_w, l2_decay_logit, l2_head_norm_w, l2_out_w, l3_norm_w, l3_ada_w, l3_ada_b, l3_in_w, l3_conv_w, l3_conv_b, l3_gate_b, l3_head_norm_w, l3_out_w, final_norm_w):
    layers = (
        ("mlstm", l0_norm_w, l0_ada_w, l0_ada_b, l0_out_w, l0_head_norm_w, (l0_in_w, l0_conv_w, l0_conv_b, l0_gate_b)),
        ("gla", l1_norm_w, l1_ada_w, l1_ada_b, l1_out_w, l1_head_norm_w, (l1_in_w, l1_gk_w2, l1_gk_b)),
        ("retention", l2_norm_w, l2_ada_w, l2_ada_b, l2_out_w, l2_head_norm_w, (l2_in_w, l2_decay_logit)),
        ("mlstm", l3_norm_w, l3_ada_w, l3_ada_b, l3_out_w, l3_head_norm_w, (l3_in_w, l3_conv_w, l3_conv_b, l3_gate_b)),
    )
    b, _, d = x.shape
    n_ctx = ctx.shape[1]
    assert n_ctx % TOKEN_TILE == 0 and n_ctx % SCAN_CHUNK == 0 and x.shape[1] % SCAN_CHUNK == 0
    assert TOKEN_TILE == SCAN_CHUNK and TOKEN_TILE % GLA_CHUNK == 0
    h = jnp.concatenate([ctx, x], axis=1)
    cc = jnp.concatenate([c, c_ctx[None, :], jnp.zeros((7, d), F32)], axis=0)
    for li, (kind, norm_w, ada_w, ada_b, out_w, head_norm_w, params) in enumerate(layers):
        last = li == len(layers) - 1
        mod = _ada(cc, ada_w, ada_b)
        mod_lat = mod[:b].reshape(b, 1, 3, d)
        mod_ctx = jnp.broadcast_to(mod[b].reshape(1, 1, 3, d), (b, 1, 3, d))
        modsel = jnp.concatenate([mod_ctx, mod_lat], axis=1)
        if kind == "mlstm":
            o, z = _mlstm_layer(h, modsel, norm_w, *params, n_ctx)
        elif kind == "gla":
            o, z = _gla_layer(h, modsel, norm_w, *params, n_ctx)
        else:
            o, z = _ret_layer(h, modsel, norm_w, *params, n_ctx)
        h = _out_call(o, z, head_norm_w, out_w.astype(BF16), h, modsel,
                      final_norm_w if last else None, n_ctx, center=(kind != "gla"))
    return h
```

```python
import functools

import jax
import jax.numpy as jnp
from jax import lax
from jax.experimental import pallas as pl
from jax.experimental.pallas import tpu as pltpu

N_HEADS = 4
GRID_W = 64
GLA_RANK = 16
GLA_TAU = 16.0
ROPE_BASE = 10000.0
NORM_EPS = 1e-6
NEG_INF = -1e30

TOKEN_TILE = 256
SCAN_CHUNK = 256
GLA_CHUNK = 128
GLA_SUB = 16
GLA_EXP_CLAMP = 80.0
GLA_INTRA_CHUNKS = 2
LANES = 128
VMEM_LIMIT = 56 * 1024 * 1024

F32 = jnp.float32
BF16 = jnp.bfloat16


def _silu(x):
    half = 0.5 * x
    return half + half * jnp.tanh(half)


def _log_sigmoid(x):
    return jnp.minimum(x, 0.0) - jnp.log1p(jnp.exp(-jnp.abs(x)))


def _dot(a, b):
    return jnp.dot(a, b, preferred_element_type=F32)


def _split3(a):
    hi = a.astype(BF16)
    r1 = a - hi.astype(F32)
    mid = r1.astype(BF16)
    lo = (r1 - mid.astype(F32)).astype(BF16)
    return hi, mid, lo


def _dot01_left(m01, a):
    hi, mid, lo = _split3(a)
    return _dot(m01, lo) + _dot(m01, mid) + _dot(m01, hi)


def _dot01_right(a, m01):
    hi, mid, lo = _split3(a)
    return _dot(lo, m01) + _dot(mid, m01) + _dot(hi, m01)


def _tri_masks(n):
    row = lax.broadcasted_iota(jnp.int32, (n, n), 0)
    col = lax.broadcasted_iota(jnp.int32, (n, n), 1)
    return col <= row, col >= row


def _bw_chunk(step, n_ctx_chunks, n_chunks):
    return jnp.where(step < n_ctx_chunks, n_ctx_chunks - 1 - step, n_chunks - 1 - (step - n_ctx_chunks))


def _first_touch_groups(n_ctx_chunks, n_chunks):
    def bw_idx(s):
        return n_ctx_chunks - 1 - s if s < n_ctx_chunks else n_chunks - 1 - (s - n_ctx_chunks)
    bw_step = {bw_idx(s): s for s in range(n_chunks)}
    groups = []
    for s in range(n_chunks):
        flags = (s <= bw_step[s], s < bw_idx(s))
        if groups and groups[-1][2] == flags:
            groups[-1][1] = s + 1
        else:
            groups.append([s, s + 1, flags])
    return [tuple(g) for g in groups]


def _scan_steps(step, n_ctx_chunks, n_chunks, unroll):
    for start, stop, first in _first_touch_groups(n_ctx_chunks, n_chunks):
        lax.fori_loop(start, stop, lambda s, carry, first=first: step(s, first) or carry, 0,
                      unroll=unroll if (stop - start) % unroll == 0 else 1)


def _emit(o_ref, start, n, value, first):
    if first:
        o_ref[0, pl.ds(start, n), :] = value
    else:
        o_ref[0, pl.ds(start, n), :] += value


def _ada_body(c_ref, w_ref, b_ref, o_ref):
    s = _silu(c_ref[...])
    o_ref[...] = jnp.dot(s, w_ref[...], preferred_element_type=F32,
                         precision=lax.Precision.HIGHEST) + b_ref[...]


def _ada(cc, w, b):
    rows, d = cc.shape
    n = w.shape[1]
    tn = 512
    return pl.pallas_call(
        _ada_body,
        grid=(n // tn,),
        in_specs=[pl.BlockSpec((rows, d), lambda j: (0, 0)),
                  pl.BlockSpec((d, tn), lambda j: (0, j)),
                  pl.BlockSpec((1, tn), lambda j: (0, j))],
        out_specs=pl.BlockSpec((rows, tn), lambda j: (0, j)),
        out_shape=jax.ShapeDtypeStruct((rows, n), F32),
        name="ada",
    )(cc, w, b.reshape(1, n))


def _modulated(h_ref, mod_ref, nw_ref):
    x = h_ref[0]
    y = x * lax.rsqrt(jnp.mean(x * x, axis=-1, keepdims=True) + NORM_EPS) * nw_ref[...]
    m = mod_ref[0, 0]
    return (y * (1.0 + m[1:2]) + m[0:1]).astype(BF16)


def _store_chunked_t(ref, x, chunk):
    xt = x.T
    for j in range(x.shape[0] // chunk):
        ref[0, j] = xt[:, j * chunk:(j + 1) * chunk].astype(ref.dtype)


def _proj_mlstm_body(h_ref, mod_ref, nw_ref, wqk_ref, wv_ref, wz_ref, wg_ref, gb_ref,
                     qk_ref, v_ref, z_ref, g_ref):
    u = _modulated(h_ref, mod_ref, nw_ref)
    qk_ref[0] = _dot(u, wqk_ref[...])
    v_ref[0] = _dot(u, wv_ref[...]).astype(BF16)
    z_ref[0] = _dot(u, wz_ref[...]).astype(BF16)
    g = _dot(u, wg_ref[...]) + gb_ref[...]
    lane = lax.broadcasted_iota(jnp.int32, g.shape, 1)
    is_forget = (lane // N_HEADS) % 2 == 1
    g_ref[0] = jnp.where(is_forget, _log_sigmoid(g), g)


def _proj_gla_body(h_ref, mod_ref, nw_ref, wq_ref, wk_ref, wv_ref, wz_ref, wlr_ref,
                   q_ref, kt_ref, v_ref, z_ref, code_ref, *, q_scale):
    u = _modulated(h_ref, mod_ref, nw_ref)
    q_ref[0] = (_dot(u, wq_ref[...]) * q_scale).astype(BF16)
    _store_chunked_t(kt_ref, _dot(u, wk_ref[...]), GLA_CHUNK)
    v_ref[0] = _dot(u, wv_ref[...]).astype(BF16)
    z_ref[0] = _dot(u, wz_ref[...]).astype(BF16)
    code_ref[0] = _dot(u, wlr_ref[...]).astype(BF16)


def _rope(t, cos, sin):
    outs = []
    for g in range(t.shape[1] // LANES):
        tg = t[:, g * LANES:(g + 1) * LANES]
        cg = cos[:, (g % 2) * LANES:(g % 2 + 1) * LANES]
        sg = sin[:, (g % 2) * LANES:(g % 2 + 1) * LANES]
        outs.append(tg * cg + pltpu.roll(tg, LANES // 2, 1) * sg)
    return jnp.concatenate(outs, axis=1)


def _proj_ret_body(h_ref, mod_ref, nw_ref, wq_ref, wk_ref, wv_ref, wz_ref, cos_ref, sin_ref,
                   q_ref, kt_ref, v_ref, z_ref, *, k_scale):
    u = _modulated(h_ref, mod_ref, nw_ref)
    cos = cos_ref[...]
    sin = sin_ref[...]
    q_ref[0] = _rope(_dot(u, wq_ref[...]), cos, sin).astype(BF16)
    _store_chunked_t(kt_ref, _rope(_dot(u, wk_ref[...]), cos, sin) * k_scale, SCAN_CHUNK)
    v_ref[0] = _dot(u, wv_ref[...]).astype(BF16)
    z_ref[0] = _dot(u, wz_ref[...]).astype(BF16)


def _proj_call(body, h, modsel, norm_w, consts, row_tables, outs, name):
    b, l, d = h.shape
    nt = l // TOKEN_TILE
    in_specs = [pl.BlockSpec((1, TOKEN_TILE, d), lambda i, t: (i, t, 0)),
                pl.BlockSpec((1, 1, 3, d), lambda i, t: (i, jnp.minimum(t, 1), 0, 0)),
                pl.BlockSpec((1, d), lambda i, t: (0, 0))]
    for a in consts:
        in_specs.append(pl.BlockSpec(a.shape, lambda i, t, nd=a.ndim: (0,) * nd))
    for a in row_tables:
        in_specs.append(pl.BlockSpec((TOKEN_TILE, a.shape[1]), lambda i, t: (t, 0)))
    out_specs, out_shape = [], []
    for n, dt, chunk in outs:
        if chunk is None:
            out_specs.append(pl.BlockSpec((1, TOKEN_TILE, n), lambda i, t: (i, t, 0)))
            out_shape.append(jax.ShapeDtypeStruct((b, l, n), dt))
        else:
            out_specs.append(pl.BlockSpec((1, TOKEN_TILE // chunk, n, chunk), lambda i, t: (i, t, 0, 0)))
            out_shape.append(jax.ShapeDtypeStruct((b, l // chunk, n, chunk), dt))
    return pl.pallas_call(
        body,
        grid=(b, nt),
        in_specs=in_specs,
        out_specs=out_specs,
        out_shape=out_shape,
        compiler_params=pltpu.CompilerParams(
            dimension_semantics=("parallel", "parallel"), vmem_limit_bytes=VMEM_LIMIT),
        name=name,
    )(h, modsel, norm_w.reshape(1, d), *consts, *row_tables)


def _conv_body(x_ref, w_ref, b_ref, o_ref, *, n_ctx, n_blocks, scale, transposed):
    blk = SCAN_CHUNK
    l = x_ref.shape[1]
    w = w_ref[...]
    bias = b_ref[...]
    rid = lax.broadcasted_iota(jnp.int32, (blk, 1), 0)

    def emit(chunk, start, acc):
        y = _silu(acc + bias) * scale
        if transposed:
            o_ref[0, chunk] = y.T.astype(BF16)
        else:
            o_ref[0, pl.ds(start, blk), :] = y.astype(BF16)

    def combine(a, bm, cc, first, last):
        return (jnp.where(first, 0.0, pltpu.roll(a, 1, 0)) + bm
                + jnp.where(last, 0.0, pltpu.roll(cc, blk - 1, 0)))

    xc = x_ref[0, 0:blk, :]
    emit(0, 0, combine(xc * w[3:4], xc * w[4:5], xc * w[5:6], rid == 0, rid == blk - 1))

    col = rid % GRID_W

    def lat_block(it, carry):
        base = pl.multiple_of(n_ctx + it * blk, blk)
        mid = x_ref[0, pl.ds(base, blk), :]
        up_edge = x_ref[0, pl.ds(pl.multiple_of(base - GRID_W, GRID_W), GRID_W), :]
        dn_start = pl.multiple_of(jnp.minimum(base + blk, l - GRID_W), GRID_W)
        dn_edge = x_ref[0, pl.ds(dn_start, GRID_W), :]
        up_edge = jnp.where(it > 0, up_edge, 0.0)
        dn_edge = jnp.where(it < n_blocks - 1, dn_edge, 0.0)
        up = jnp.concatenate([up_edge, mid[:blk - GRID_W]], axis=0)
        down = jnp.concatenate([mid[GRID_W:], dn_edge], axis=0)
        a = up * w[0:1] + mid * w[3:4] + down * w[6:7]
        bm = up * w[1:2] + mid * w[4:5] + down * w[7:8]
        cc = up * w[2:3] + mid * w[5:6] + down * w[8:9]
        emit(n_ctx // blk + it, base, combine(a, bm, cc, col == 0, col == GRID_W - 1))
        return carry

    lax.fori_loop(0, n_blocks, lat_block, 0)


def _conv(qk_raw, conv_w, conv_b, n_ctx, col_offset, width, scale, transposed):
    b, l, c = qk_raw.shape
    tc = 256
    blk = SCAN_CHUNK
    assert n_ctx == blk and (l - n_ctx) % blk == 0 and blk % GRID_W == 0
    off = col_offset // tc
    if transposed:
        out_spec = pl.BlockSpec((1, l // blk, tc, blk), lambda i, j: (i, 0, j, 0))
        out_shape = jax.ShapeDtypeStruct((b, l // blk, width, blk), BF16)
    else:
        out_spec = pl.BlockSpec((1, l, tc), lambda i, j: (i, 0, j))
        out_shape = jax.ShapeDtypeStruct((b, l, width), BF16)
    return pl.pallas_call(
        functools.partial(_conv_body, n_ctx=n_ctx, n_blocks=(l - n_ctx) // blk, scale=scale,
                          transposed=transposed),
        grid=(b, width // tc),
        in_specs=[pl.BlockSpec((1, l, tc), lambda i, j: (i, 0, j + off)),
                  pl.BlockSpec((9, tc), lambda i, j: (0, j + off)),
                  pl.BlockSpec((1, tc), lambda i, j: (0, j + off))],
        out_specs=out_spec,
        out_shape=out_shape,
        compiler_params=pltpu.CompilerParams(
            dimension_semantics=("parallel", "parallel"), vmem_limit_bytes=VMEM_LIMIT),
        name="mlstm_conv_kt" if transposed else "mlstm_conv_q",
    )(qk_raw, conv_w.reshape(9, c), conv_b.reshape(1, c))


def _prefix_max(x, lane, backward):
    ax = x.ndim - 1
    n = x.shape[ax]
    sh = 1
    while sh < n:
        if backward:
            x = jnp.maximum(x, jnp.where(lane < n - sh, pltpu.roll(x, n - sh, ax), NEG_INF))
        else:
            x = jnp.maximum(x, jnp.where(lane >= sh, pltpu.roll(x, sh, ax), NEG_INF))
        sh *= 2
    return x


def _mlstm_scan_body(q_ref, kt_ref, v_ref, g_ref, o_ref, st_ref, n_ref, row_ref, *, n_ctx_chunks, n_chunks):
    c = SCAN_CHUNK
    dv = v_ref.shape[2]
    st_ref[...] = jnp.zeros(st_ref.shape, F32)
    n_ref[...] = jnp.zeros(n_ref.shape, F32)
    masks = _tri_masks(c)

    g3 = g_ref[0, 0]
    g2 = g3.reshape(n_chunks * 8, c)
    lane = lax.broadcasted_iota(jnp.int32, (1, 1, c), 2)
    token_rows, chunk_rows = [], []
    for dirn in range(2):
        cum = _dot01_right(g2, masks[1 - dirn].astype(BF16)).reshape(n_chunks, 8, c)
        i_r = g3[:, 2 * dirn:2 * dirn + 1]
        a_r = g3[:, 2 * dirn + 1:2 * dirn + 2]
        b_r = cum[:, 2 * dirn + 1:2 * dirn + 2]
        u = i_r - b_r
        b_end = jnp.sum(a_r, axis=2, keepdims=True)
        g_max = jnp.max(b_end + u, axis=2, keepdims=True)
        cm = _prefix_max(u, lane, bool(dirn))
        m = jnp.zeros((1, 1, 1), F32)
        m_prev_l, m_new_l = [None] * n_chunks, [None] * n_chunks
        for s in range(n_chunks):
            idx = s
            if dirn:
                idx = n_ctx_chunks - 1 - s if s < n_ctx_chunks else n_chunks - 1 - (s - n_ctx_chunks)
            m_prev_l[idx] = m
            m = jnp.maximum(b_end[idx:idx + 1] + m, g_max[idx:idx + 1])
            m_new_l[idx] = m
        m_prev = jnp.concatenate(m_prev_l, axis=0)
        m_new = jnp.concatenate(m_new_l, axis=0)
        mm = jnp.maximum(m_prev, cm)
        token_rows += [u, jnp.exp(b_end + u - m_new), mm, b_r + mm]
        chunk_rows += [jnp.broadcast_to(jnp.exp(b_end + m_prev - m_new), (n_chunks, 1, c)),
                       jnp.broadcast_to(m_prev, (n_chunks, 1, c))]
    row_ref[...] = jnp.concatenate(token_rows + chunk_rows + [jnp.zeros((n_chunks, 4, c), F32)], axis=1)

    def lane_bcast_col(row):
        return jnp.broadcast_to(row, (LANES, c)).T

    def step(s, first):
        dirs = (0, 1)
        idxs = (s, _bw_chunk(s, n_ctx_chunks, n_chunks))
        starts = [pl.multiple_of(idx * c, c) for idx in idxs]
        q = [q_ref[0, pl.ds(st, c), :] for st in starts]
        kt = [kt_ref[0, idx] for idx in idxs]
        v = [v_ref[0, pl.ds(st, c), :] for st in starts]
        rows = [row_ref[idx] for idx in idxs]
        state = [st_ref[d] for d in dirs]
        nrm = [n_ref[d] for d in dirs]
        qk = [_dot(q[d], kt[d]) for d in dirs]
        mm_b = [lane_bcast_col(rows[d][4 * d + 2:4 * d + 3]) for d in dirs]
        bmm_b = [lane_bcast_col(rows[d][4 * d + 3:4 * d + 4]) for d in dirs]
        m_prev = [rows[d][9 + 2 * d:10 + 2 * d, 0:1] for d in dirs]
        decay = [rows[d][8 + 2 * d:9 + 2 * d, 0:1] for d in dirs]
        sc = [qk[d] * jnp.exp(jnp.where(masks[d], rows[d][4 * d:4 * d + 1]
                                        - jnp.concatenate([mm_b[d]] * (c // LANES), axis=1), NEG_INF))
              for d in dirs]
        w_inter = [jnp.exp(m_prev[d] - mm_b[d]) for d in dirs]
        intra = [_dot(sc[d].astype(BF16), v[d]) for d in dirs]
        inter = [_dot(q[d], state[d].astype(BF16)) for d in dirs]
        qn = [_dot(q[d], nrm[d].astype(BF16)) for d in dirs]
        wkt = [kt[d].astype(F32) * rows[d][4 * d + 1:4 * d + 2] for d in dirs]
        upd = [_dot(wkt[d].astype(BF16), v[d]) for d in dirs]
        for d in dirs:
            num = intra[d] + inter[d] * jnp.concatenate([w_inter[d]] * (dv // LANES), axis=1)
            den = jnp.sum(sc[d], axis=1, keepdims=True) + qn[d] * w_inter[d]
            rcp = 1.0 / jnp.maximum(jnp.abs(den), jnp.exp(-bmm_b[d]))
            _emit(o_ref, starts[d], c, num * jnp.concatenate([rcp] * (dv // LANES), axis=1), first[d])
        for d in dirs:
            st_ref[d] = decay[d] * state[d] + upd[d]
            n_ref[d] = decay[d] * nrm[d] + jnp.sum(wkt[d], axis=1, keepdims=True)

    _scan_steps(step, n_ctx_chunks, n_chunks, 1)


def _mlstm_scan(q, kt, v, g_rows, n_ctx):
    b, l, _ = q.shape
    dk = q.shape[2] // N_HEADS
    dv = v.shape[2] // N_HEADS
    n_chunks = l // SCAN_CHUNK
    return pl.pallas_call(
        functools.partial(_mlstm_scan_body, n_ctx_chunks=n_ctx // SCAN_CHUNK, n_chunks=n_chunks),
        grid=(b, N_HEADS),
        in_specs=[pl.BlockSpec((1, l, dk), lambda i, h: (i, 0, h)),
                  pl.BlockSpec((1, n_chunks, dk, SCAN_CHUNK), lambda i, h: (i, 0, h, 0)),
                  pl.BlockSpec((1, l, dv), lambda i, h: (i, 0, h)),
                  pl.BlockSpec((1, 1, n_chunks, 8, SCAN_CHUNK), lambda i, h: (i, h, 0, 0, 0))],
        out_specs=pl.BlockSpec((1, l, dv), lambda i, h: (i, 0, h)),
        out_shape=jax.ShapeDtypeStruct((b, l, N_HEADS * dv), F32),
        scratch_shapes=[pltpu.VMEM((2, dk, dv), F32),
                        pltpu.VMEM((2, dk, LANES), F32),
                        pltpu.VMEM((n_chunks, 16, SCAN_CHUNK), F32)],
        compiler_params=pltpu.CompilerParams(
            dimension_semantics=("parallel", "parallel"), vmem_limit_bytes=VMEM_LIMIT),
        name="mlstm_scan",
    )(q, kt, v, g_rows)


def _ret_scan_body(q_ref, kt_ref, v_ref, dl_ref, o_ref, st_ref, dm_ref, *, n_ctx_chunks, n_chunks):
    c = SCAN_CHUNK
    st_ref[...] = jnp.zeros(st_ref.shape, F32)
    masks = _tri_masks(c)
    row = lax.broadcasted_iota(jnp.int32, (c, c), 0)
    col = lax.broadcasted_iota(jnp.int32, (c, c), 1)
    dist = jnp.abs(row - col).astype(F32)
    rid = lax.broadcasted_iota(jnp.int32, (c, 1), 0).astype(F32)
    lid = lax.broadcasted_iota(jnp.int32, (1, c), 1).astype(F32)
    log_gamma = []
    for dirn in range(2):
        lg = _log_sigmoid(dl_ref[0, dirn])[0:1, 0:1]
        log_gamma.append(lg)
        dm_ref[dirn] = jnp.where(masks[dirn], jnp.exp(lg * dist), 0.0)

    pos_col = (rid + 1.0, c - rid)
    pos_row = (lid + 1.0, c - lid)

    def step(s, first):
        dirs = (0, 1)
        idxs = (s, _bw_chunk(s, n_ctx_chunks, n_chunks))
        starts = [pl.multiple_of(idx * c, c) for idx in idxs]
        q = [q_ref[0, pl.ds(st, c), :] for st in starts]
        kt = [kt_ref[0, idx] for idx in idxs]
        v = [v_ref[0, pl.ds(st, c), :] for st in starts]
        state = [st_ref[d] for d in dirs]
        sc = [(_dot(q[d], kt[d]) * dm_ref[d]).astype(BF16) for d in dirs]
        intra = [_dot(sc[d], v[d]) for d in dirs]
        inter = [_dot(q[d], state[d].astype(BF16)) for d in dirs]
        kdt = [(kt[d].astype(F32) * jnp.exp(log_gamma[d] * (c - pos_row[d]))).astype(BF16) for d in dirs]
        upd = [_dot(kdt[d], v[d]) for d in dirs]
        for d in dirs:
            _emit(o_ref, starts[d], c, intra[d] + inter[d] * jnp.exp(log_gamma[d] * pos_col[d]), first[d])
        for d in dirs:
            st_ref[d] = jnp.exp(log_gamma[d] * c) * state[d] + upd[d]

    _scan_steps(step, n_ctx_chunks, n_chunks, 2)


def _ret_scan(q, kt, v, dl, n_ctx):
    b, l, _ = q.shape
    dk = q.shape[2] // N_HEADS
    dv = v.shape[2] // N_HEADS
    n_chunks = l // SCAN_CHUNK
    return pl.pallas_call(
        functools.partial(_ret_scan_body, n_ctx_chunks=n_ctx // SCAN_CHUNK, n_chunks=n_chunks),
        grid=(b, N_HEADS),
        in_specs=[pl.BlockSpec((1, l, dk), lambda i, h: (i, 0, h)),
                  pl.BlockSpec((1, n_chunks, dk, SCAN_CHUNK), lambda i, h: (i, 0, h, 0)),
                  pl.BlockSpec((1, l, dv), lambda i, h: (i, 0, h)),
                  pl.BlockSpec((1, 2, 8, LANES), lambda i, h: (h, 0, 0, 0))],
        out_specs=pl.BlockSpec((1, l, dv), lambda i, h: (i, 0, h)),
        out_shape=jax.ShapeDtypeStruct((b, l, N_HEADS * dv), F32),
        scratch_shapes=[pltpu.VMEM((2, dk, dv), F32),
                        pltpu.VMEM((2, SCAN_CHUNK, SCAN_CHUNK), F32)],
        compiler_params=pltpu.CompilerParams(
            dimension_semantics=("parallel", "parallel"), vmem_limit_bytes=VMEM_LIMIT),
        name="ret_scan",
    )(q, kt, v, dl)


def _gla_scan_body(q_ref, kt_ref, v_ref, code_ref, w2_ref, b2_ref, o_ref,
                   st_ref, s_ref, qi_ref, kdt_ref, e_ref, *, n_ctx_chunks, n_chunks):
    c = GLA_CHUNK
    t = GLA_SUB
    n_sub = c // t
    dk = q_ref.shape[2]
    dv = v_ref.shape[2]
    st_ref[...] = jnp.zeros(st_ref.shape, F32)
    masks = _tri_masks(c)
    masks01 = (masks[0].astype(BF16), masks[1].astype(BF16))
    lid = lax.broadcasted_iota(jnp.int32, (1, c), 1)

    def intra(it, carry):
        chains = [(it * GLA_INTRA_CHUNKS + j, dirn) for j in range(GLA_INTRA_CHUNKS) for dirn in range(2)]
        starts = [pl.multiple_of(idx * c, c) for idx, _ in chains]
        qf = [q_ref[0, pl.ds(st, c), :].astype(F32) for st in starts]
        ktf = [kt_ref[0, idx].astype(F32) for idx, _ in chains]
        code = [code_ref[0, pl.ds(st, c), :] for st in starts]
        a = [_log_sigmoid(_dot(code[k], w2_ref[d]) + b2_ref[d]) * (1.0 / GLA_TAU) for k, (_, d) in enumerate(chains)]
        parts = [_split3(x) for x in a]
        b = [None] * len(chains)
        for p in (2, 1, 0):
            for k, (_, d) in enumerate(chains):
                term = _dot(masks01[d], parts[k][p])
                b[k] = term if b[k] is None else b[k] + term
        qp, kp, kdt, e_end = [], [], [], []
        for k, (_, dirn) in enumerate(chains):
            refs = []
            for sb in range(n_sub):
                r0 = sb * t + (t - 1 if dirn else 0)
                refs.append(b[k][r0:r0 + 1] - a[k][r0:r0 + 1])
            own = jnp.concatenate([jnp.broadcast_to(r, (t, dk)) for r in refs], axis=0)
            b_rel = b[k] - own
            q_own = qf[k] * jnp.exp(b_rel)
            k_rel_t = (ktf[k] * jnp.exp(jnp.minimum(-(b_rel.T), GLA_EXP_CLAMP))).astype(BF16)
            r_end = 0 if dirn else c - 1
            b_end = b[k][r_end:r_end + 1]
            kdt.append((ktf[k] * jnp.exp((b_end - b[k]).T)).astype(BF16))
            e_end.append(jnp.broadcast_to(jnp.exp(b_end), (8, dk)))
            col_blocks = []
            for sb in range(n_sub):
                pieces = []
                for j in range(n_sub):
                    if (j <= sb) if dirn else (j >= sb):
                        qj = q_own[j * t:(j + 1) * t]
                        pieces.append(qj if j == sb else qj * jnp.exp(refs[j] - refs[sb]))
                    else:
                        pieces.append(jnp.zeros((t, dk), F32))
                col_blocks.append(jnp.concatenate(pieces, axis=0).astype(BF16))
            qp.append(col_blocks)
            kp.append(jnp.concatenate(
                [jnp.where(jnp.logical_and(lid >= sb * t, lid < (sb + 1) * t), k_rel_t, 0.0)
                 for sb in range(n_sub)], axis=0))
        s = [_dot(jnp.concatenate(qp[k], axis=1), kp[k]) for k in range(len(chains))]
        for k, (idx, dirn) in enumerate(chains):
            s_ref[dirn, pl.ds(starts[k], c), :] = jnp.where(masks[dirn], s[k], 0.0).astype(BF16)
            qi_ref[dirn, pl.ds(starts[k], c), :] = qp[k][n_sub - 1 if dirn else 0]
            kdt_ref[dirn, idx] = kdt[k]
            e_ref[dirn, idx] = e_end[k]
        return carry

    lax.fori_loop(0, n_chunks // GLA_INTRA_CHUNKS, intra, 0)

    def step(s, first):
        dirs = (0, 1)
        idxs = (s, _bw_chunk(s, n_ctx_chunks, n_chunks))
        starts = [pl.multiple_of(idx * c, c) for idx in idxs]
        v = [v_ref[0, pl.ds(st, c), :] for st in starts]
        state = [st_ref[d] for d in dirs]
        intra = [_dot(s_ref[d, pl.ds(starts[d], c), :], v[d]) for d in dirs]
        inter = [_dot(qi_ref[d, pl.ds(starts[d], c), :], state[d].astype(BF16)) for d in dirs]
        upd = [_dot(kdt_ref[d, idxs[d]], v[d]) for d in dirs]
        e_cols = [jnp.concatenate([e_ref[d, idxs[d]]] * (dk // 8), axis=0).T for d in dirs]
        for d in dirs:
            _emit(o_ref, starts[d], c, intra[d] + inter[d], first[d])
        for d in dirs:
            st_ref[d] = jnp.concatenate([e_cols[d]] * (dv // dk), axis=1) * state[d] + upd[d]

    _scan_steps(step, n_ctx_chunks, n_chunks, 2)


def _gla_scan(q, kt, v, code, w2, b2, n_ctx):
    b, l, _ = q.shape
    dk = q.shape[2] // N_HEADS
    dv = v.shape[2] // N_HEADS
    n_chunks = l // GLA_CHUNK
    tok = lambda i, h: (i, 0, h)
    return pl.pallas_call(
        functools.partial(_gla_scan_body, n_ctx_chunks=n_ctx // GLA_CHUNK, n_chunks=n_chunks),
        grid=(b, N_HEADS),
        in_specs=[pl.BlockSpec((1, l, dk), tok),
                  pl.BlockSpec((1, n_chunks, dk, GLA_CHUNK), lambda i, h: (i, 0, h, 0)),
                  pl.BlockSpec((1, l, dv), tok),
                  pl.BlockSpec((1, l, LANES), lambda i, h: (i, 0, 0)),
                  pl.BlockSpec((2, LANES, dk), lambda i, h: (0, 0, h)),
                  pl.BlockSpec((2, 1, dk), lambda i, h: (0, 0, h))],
        out_specs=pl.BlockSpec((1, l, dv), tok),
        out_shape=jax.ShapeDtypeStruct((b, l, N_HEADS * dv), F32),
        scratch_shapes=[pltpu.VMEM((2, dk, dv), F32),
                        pltpu.VMEM((2, l, GLA_CHUNK), BF16),
                        pltpu.VMEM((2, l, dk), BF16),
                        pltpu.VMEM((2, n_chunks, dk, GLA_CHUNK), BF16),
                        pltpu.VMEM((2, n_chunks, 8, dk), F32)],
        compiler_params=pltpu.CompilerParams(
            dimension_semantics=("parallel", "parallel"), vmem_limit_bytes=VMEM_LIMIT),
        name="gla_scan",
    )(q, kt, v, code, w2, b2)


def _out_body(o_ref, z_ref, hw_ref, ow_ref, h_ref, mod_ref, *rest, center, final):
    if final:
        fw_ref, out_ref = rest
    else:
        (out_ref,) = rest
    dv = o_ref.shape[2] // N_HEADS
    proj = None
    for hh in range(N_HEADS):
        cols = slice(hh * dv, (hh + 1) * dv)
        yh = o_ref[0, :, cols]
        if center:
            yh = yh - jnp.mean(yh, axis=-1, keepdims=True)
        yn = yh * lax.rsqrt(jnp.mean(yh * yh, axis=-1, keepdims=True) + NORM_EPS) * hw_ref[:, cols]
        a = (yn * _silu(z_ref[0, :, cols].astype(F32))).astype(BF16)
        term = _dot(a, ow_ref[cols, :])
        proj = term if proj is None else proj + term
    hn = h_ref[0] + mod_ref[0, 0][2:3] * proj
    if final:
        hn = hn * lax.rsqrt(jnp.mean(hn * hn, axis=-1, keepdims=True) + NORM_EPS) * fw_ref[...]
    out_ref[0] = hn


def _out_call(o, z, head_norm_w, out_w, h, modsel, final_w, n_ctx, center):
    b, l, d = h.shape
    di = o.shape[2]
    final = final_w is not None
    skip = n_ctx // TOKEN_TILE if final else 0
    nt = l // TOKEN_TILE - skip
    tok = lambda i, t: (i, t + skip, 0)
    in_specs = [pl.BlockSpec((1, TOKEN_TILE, di), tok),
                pl.BlockSpec((1, TOKEN_TILE, di), tok),
                pl.BlockSpec((1, di), lambda i, t: (0, 0)),
                pl.BlockSpec((di, d), lambda i, t: (0, 0)),
                pl.BlockSpec((1, TOKEN_TILE, d), tok),
                pl.BlockSpec((1, 1, 3, d), lambda i, t: (i, jnp.minimum(t + skip, 1), 0, 0))]
    args = [o, z, head_norm_w.reshape(1, di), out_w, h, modsel]
    if final:
        in_specs.append(pl.BlockSpec((1, d), lambda i, t: (0, 0)))
        args.append(final_w.reshape(1, d))
    return pl.pallas_call(
        functools.partial(_out_body, center=center, final=final),
        grid=(b, nt),
        in_specs=in_specs,
        out_specs=pl.BlockSpec((1, TOKEN_TILE, d), lambda i, t: (i, t, 0)),
        out_shape=jax.ShapeDtypeStruct((b, nt * TOKEN_TILE, d), F32),
        compiler_params=pltpu.CompilerParams(
            dimension_semantics=("parallel", "parallel"), vmem_limit_bytes=VMEM_LIMIT),
        name="out_proj_final" if final else "out_proj",
    )(*args)


def _pad_cols(w, n):
    return jnp.pad(w, ((0, 0), (0, n - w.shape[1])))


def _mlstm_layer(h, modsel, norm_w, in_w, conv_w, conv_b, gate_b, n_ctx):
    b, l, d = h.shape
    di = 2 * d
    qk_w = 2 * d
    wqk = in_w[:, :qk_w].astype(BF16)
    wv = in_w[:, qk_w:qk_w + di].astype(BF16)
    wz = in_w[:, qk_w + di:qk_w + 2 * di].astype(BF16)
    wg = _pad_cols(in_w[:, qk_w + 2 * di:], LANES).astype(BF16)
    gb = _pad_cols(gate_b.reshape(1, -1), LANES)
    qk_raw, v, z, g = _proj_call(
        _proj_mlstm_body, h, modsel, norm_w, [wqk, wv, wz, wg, gb], [],
        [(qk_w, F32, None), (di, BF16, None), (di, BF16, None), (LANES, F32, None)], "proj_mlstm")
    head_qk = qk_w // (2 * N_HEADS)
    q = _conv(qk_raw, conv_w, conv_b, n_ctx, 0, qk_w // 2, 1.0, False)
    kt = _conv(qk_raw, conv_w, conv_b, n_ctx, qk_w // 2, qk_w // 2, head_qk ** -0.5, True)
    n_chunks = l // SCAN_CHUNK
    g_rows = g[:, :, :4 * N_HEADS].reshape(b, n_chunks, SCAN_CHUNK, 4, N_HEADS).transpose(0, 4, 1, 3, 2)
    g_rows = jnp.pad(g_rows, ((0, 0), (0, 0), (0, 0), (0, 4), (0, 0)))
    return _mlstm_scan(q, kt, v, g_rows, n_ctx), z


def _gla_layer(h, modsel, norm_w, in_w, gk_w2, gk_b, n_ctx):
    b, l, d = h.shape
    di = 2 * d
    key = gk_w2.shape[2]
    assert key // N_HEADS == LANES
    wq = in_w[:, :key].astype(BF16)
    wk = in_w[:, key:2 * key].astype(BF16)
    wv = in_w[:, 2 * key:2 * key + di].astype(BF16)
    wz = in_w[:, 2 * key + di:2 * key + 2 * di].astype(BF16)
    wlr = _pad_cols(in_w[:, 2 * key + 2 * di:], LANES).astype(BF16)
    q, kt, v, z, code = _proj_call(
        functools.partial(_proj_gla_body, q_scale=(key // N_HEADS) ** -0.5),
        h, modsel, norm_w, [wq, wk, wv, wz, wlr], [],
        [(key, BF16, None), (key, BF16, GLA_CHUNK), (di, BF16, None), (di, BF16, None),
         (LANES, BF16, None)], "proj_gla")
    w2 = jnp.zeros((2, LANES, key), F32)
    w2 = w2.at[0, :GLA_RANK].set(gk_w2[0]).at[1, GLA_RANK:2 * GLA_RANK].set(gk_w2[1]).astype(BF16)
    return _gla_scan(q, kt, v, code, w2, gk_b.reshape(2, 1, key), n_ctx), z


def _rope_tables(n_ctx, n_lat, head_k):
    quarter = head_k // 4
    pos = jnp.arange(n_lat, dtype=jnp.int32)
    inv_freq = ROPE_BASE ** (-jnp.arange(quarter, dtype=F32) / quarter)
    cos_parts, sin_parts = [], []
    for p in ((pos // GRID_W).astype(F32), (pos % GRID_W).astype(F32)):
        ang = p[:, None] * inv_freq[None, :]
        cos_parts += [jnp.cos(ang), jnp.cos(ang)]
        sin_parts += [-jnp.sin(ang), jnp.sin(ang)]
    cos = jnp.concatenate(cos_parts, axis=1)
    sin = jnp.concatenate(sin_parts, axis=1)
    cos = jnp.concatenate([jnp.ones((n_ctx, head_k), F32), cos], axis=0)
    sin = jnp.concatenate([jnp.zeros((n_ctx, head_k), F32), sin], axis=0)
    return cos, sin


def _ret_layer(h, modsel, norm_w, in_w, decay_logit, n_ctx):
    b, l, d = h.shape
    di = 2 * d
    key = (in_w.shape[1] - 2 * di) // 2
    head_k = key // N_HEADS
    wq = in_w[:, :key].astype(BF16)
    wk = in_w[:, key:2 * key].astype(BF16)
    wv = in_w[:, 2 * key:2 * key + di].astype(BF16)
    wz = in_w[:, 2 * key + di:].astype(BF16)
    cos, sin = _rope_tables(n_ctx, l - n_ctx, head_k)
    q, kt, v, z = _proj_call(
        functools.partial(_proj_ret_body, k_scale=head_k ** -0.5),
        h, modsel, norm_w, [wq, wk, wv, wz], [cos, sin],
        [(key, BF16, None), (key, BF16, SCAN_CHUNK), (di, BF16, None), (di, BF16, None)], "proj_ret")
    dl = jnp.broadcast_to(decay_logit.astype(F32).T[:, :, None, None], (N_HEADS, 2, 8, LANES))
    return _ret_scan(q, kt, v, dl, n_ctx), z


def kernel(x, c, ctx, c_ctx, l0_norm_w, l0_ada_w, l0_ada_b, l0_in_w, l0_conv_w, l0_conv_b, l0_gate_b, l0_head_norm_w, l0_out_w, l1_norm_w, l1_ada_w, l1_ada_b, l1_in_w, l1_gk_w2, l1_gk_b, l1_head_norm_w, l1_out_w, l2_norm_w, l2_ada_w, l2_ada_b, l2_in_w, l2_decay_logit, l2_head_norm_w, l2_out_w, l3_norm_w, l3_ada_w, l3_ada_b, l3_in_w, l3_conv_w, l3_conv_b, l3_gate_b, l3_head_norm_w, l3_out_w, final_norm_w):
    layers = (
        ("mlstm", l0_norm_w, l0_ada_w, l0_ada_b, l0_out_w, l0_head_norm_w, (l0_in_w, l0_conv_w, l0_conv_b, l0_gate_b)),
        ("gla", l1_norm_w, l1_ada_w, l1_ada_b, l1_out_w, l1_head_norm_w, (l1_in_w, l1_gk_w2, l1_gk_b)),
        ("retention", l2_norm_w, l2_ada_w, l2_ada_b, l2_out_w, l2_head_norm_w, (l2_in_w, l2_decay_logit)),
        ("mlstm", l3_norm_w, l3_ada_w, l3_ada_b, l3_out_w, l3_head_norm_w, (l3_in_w, l3_conv_w, l3_conv_b, l3_gate_b)),
    )
    b, _, d = x.shape
    n_ctx = ctx.shape[1]
    assert n_ctx % TOKEN_TILE == 0 and n_ctx % SCAN_CHUNK == 0 and x.shape[1] % SCAN_CHUNK == 0
    assert TOKEN_TILE == SCAN_CHUNK and TOKEN_TILE % GLA_CHUNK == 0
    h = jnp.concatenate([ctx, x], axis=1)
    cc = jnp.concatenate([c, c_ctx[None, :], jnp.zeros((7, d), F32)], axis=0)
    for li, (kind, norm_w, ada_w, ada_b, out_w, head_norm_w, params) in enumerate(layers):
        last = li == len(layers) - 1
        mod = _ada(cc, ada_w, ada_b)
        mod_lat = mod[:b].reshape(b, 1, 3, d)
        mod_ctx = jnp.broadcast_to(mod[b].reshape(1, 1, 3, d), (b, 1, 3, d))
        modsel = jnp.concatenate([mod_ctx, mod_lat], axis=1)
        if kind == "mlstm":
            o, z = _mlstm_layer(h, modsel, norm_w, *params, n_ctx)
        elif kind == "gla":
            o, z = _gla_layer(h, modsel, norm_w, *params, n_ctx)
        else:
            o, z = _ret_layer(h, modsel, norm_w, *params, n_ctx)
        h = _out_call(o, z, head_norm_w, out_w.astype(BF16), h, modsel,
                      final_norm_w if last else None, n_ctx, center=(kind != "gla"))
    return h
```

```python
import functools

import jax
import jax.numpy as jnp
from jax import lax
from jax.experimental import pallas as pl
from jax.experimental.pallas import tpu as pltpu

N_HEADS = 4
GRID_W = 64
GLA_RANK = 16
GLA_TAU = 16.0
ROPE_BASE = 10000.0
NORM_EPS = 1e-6
NEG_INF = -1e30

TOKEN_TILE = 256
SCAN_CHUNK = 256
GLA_CHUNK = 128
GLA_SUB = 16
GLA_EXP_CLAMP = 80.0
GLA_INTRA_CHUNKS = 2
LANES = 128
VMEM_LIMIT = 56 * 1024 * 1024

F32 = jnp.float32
BF16 = jnp.bfloat16


def _silu(x):
    half = 0.5 * x
    return half + half * jnp.tanh(half)


def _log_sigmoid(x):
    return jnp.minimum(x, 0.0) - jnp.log(1.0 + jnp.exp(-jnp.abs(x)))


def _dot(a, b):
    return jnp.dot(a, b, preferred_element_type=F32)


def _split3(a):
    hi = a.astype(BF16)
    r1 = a - hi.astype(F32)
    mid = r1.astype(BF16)
    lo = (r1 - mid.astype(F32)).astype(BF16)
    return hi, mid, lo


def _dot01_left(m01, a):
    hi, mid, lo = _split3(a)
    return _dot(m01, lo) + _dot(m01, mid) + _dot(m01, hi)


def _dot01_right(a, m01):
    hi, mid, lo = _split3(a)
    return _dot(lo, m01) + _dot(mid, m01) + _dot(hi, m01)


def _tri_masks(n):
    row = lax.broadcasted_iota(jnp.int32, (n, n), 0)
    col = lax.broadcasted_iota(jnp.int32, (n, n), 1)
    return col <= row, col >= row


def _bw_chunk(step, n_ctx_chunks, n_chunks):
    return jnp.where(step < n_ctx_chunks, n_ctx_chunks - 1 - step, n_chunks - 1 - (step - n_ctx_chunks))


def _first_touch_groups(n_ctx_chunks, n_chunks):
    def bw_idx(s):
        return n_ctx_chunks - 1 - s if s < n_ctx_chunks else n_chunks - 1 - (s - n_ctx_chunks)
    bw_step = {bw_idx(s): s for s in range(n_chunks)}
    groups = []
    for s in range(n_chunks):
        flags = (s <= bw_step[s], s < bw_idx(s))
        if groups and groups[-1][2] == flags:
            groups[-1][1] = s + 1
        else:
            groups.append([s, s + 1, flags])
    return [tuple(g) for g in groups]


def _scan_steps(step, n_ctx_chunks, n_chunks, unroll):
    for start, stop, first in _first_touch_groups(n_ctx_chunks, n_chunks):
        lax.fori_loop(start, stop, lambda s, carry, first=first: step(s, first) or carry, 0,
                      unroll=unroll if (stop - start) % unroll == 0 else 1)


def _emit(o_ref, acc_ref, start, n, value, first):
    if first:
        acc_ref[pl.ds(start, n), :] = value
    else:
        o_ref[0, pl.ds(start, n), :] = (acc_ref[pl.ds(start, n), :] + value).astype(o_ref.dtype)


def _ada_body(c_ref, w_ref, b_ref, o_ref):
    s = _silu(c_ref[...])
    o_ref[...] = jnp.dot(s, w_ref[...], preferred_element_type=F32,
                         precision=lax.Precision.HIGHEST) + b_ref[...]


def _ada(cc, w, b):
    rows, d = cc.shape
    n = w.shape[1]
    tn = 512
    return pl.pallas_call(
        _ada_body,
        grid=(n // tn,),
        in_specs=[pl.BlockSpec((rows, d), lambda j: (0, 0)),
                  pl.BlockSpec((d, tn), lambda j: (0, j)),
                  pl.BlockSpec((1, tn), lambda j: (0, j))],
        out_specs=pl.BlockSpec((rows, tn), lambda j: (0, j)),
        out_shape=jax.ShapeDtypeStruct((rows, n), F32),
        name="ada",
    )(cc, w, b.reshape(1, n))


def _hidden_operands(h, n_ctx, tile_offset=0):
    if not isinstance(h, tuple):
        d = h.shape[2]
        return [h], [pl.BlockSpec((1, TOKEN_TILE, d), lambda i, t: (i, t + tile_offset, 0))]
    ctx, lat = h
    d = lat.shape[2]
    nct = n_ctx // TOKEN_TILE
    return [ctx, lat], [
        pl.BlockSpec((1, TOKEN_TILE, d), lambda i, t: (i, jnp.minimum(t + tile_offset, nct - 1), 0)),
        pl.BlockSpec((1, TOKEN_TILE, d), lambda i, t: (i, jnp.maximum(t + tile_offset - nct, 0), 0))]


def _with_hidden_tile(body, n_hidden, n_ctx_tiles, tile_offset=0):
    def kernel_fn(*refs):
        if n_hidden == 1:
            x = refs[0][0]
        else:
            is_ctx = pl.program_id(1) + tile_offset < n_ctx_tiles
            x = jnp.where(is_ctx, refs[0][0], refs[1][0])
        body(x, *refs[n_hidden:])
    return kernel_fn


def _modulated(x, mod_ref, nw_ref):
    y = x * lax.rsqrt(jnp.mean(x * x, axis=-1, keepdims=True) + NORM_EPS) * nw_ref[...]
    m = mod_ref[0, 0]
    return (y * (1.0 + m[1:2]) + m[0:1]).astype(BF16)


def _store_chunked_t(ref, x, chunk):
    xt = x.T
    for j in range(x.shape[0] // chunk):
        ref[0, j] = xt[:, j * chunk:(j + 1) * chunk].astype(ref.dtype)


def _proj_mlstm_body(x, mod_ref, nw_ref, wqk_ref, wv_ref, wz_ref, wg_ref, gb_ref,
                     qk_ref, v_ref, z_ref, g_ref):
    u = _modulated(x, mod_ref, nw_ref)
    qk_ref[0] = _dot(u, wqk_ref[...])
    v_ref[0] = _dot(u, wv_ref[...]).astype(BF16)
    z_ref[0] = _dot(u, wz_ref[...]).astype(BF16)
    g = _dot(u, wg_ref[...]) + gb_ref[...]
    lane = lax.broadcasted_iota(jnp.int32, g.shape, 1)
    is_forget = (lane // N_HEADS) % 2 == 1
    g_ref[0] = jnp.where(is_forget, _log_sigmoid(g), g)


def _proj_gla_body(x, mod_ref, nw_ref, wq_ref, wk_ref, wv_ref, wz_ref, wlr_ref,
                   q_ref, kt_ref, v_ref, z_ref, code_ref, *, q_scale):
    u = _modulated(x, mod_ref, nw_ref)
    q_ref[0] = (_dot(u, wq_ref[...]) * q_scale).astype(BF16)
    _store_chunked_t(kt_ref, _dot(u, wk_ref[...]), GLA_CHUNK)
    v_ref[0] = _dot(u, wv_ref[...]).astype(BF16)
    z_ref[0] = _dot(u, wz_ref[...]).astype(BF16)
    code_ref[0] = _dot(u, wlr_ref[...]).astype(BF16)


def _rope(t, cos, sin):
    outs = []
    for g in range(t.shape[1] // LANES):
        tg = t[:, g * LANES:(g + 1) * LANES]
        cg = cos[:, (g % 2) * LANES:(g % 2 + 1) * LANES]
        sg = sin[:, (g % 2) * LANES:(g % 2 + 1) * LANES]
        outs.append(tg * cg + pltpu.roll(tg, LANES // 2, 1) * sg)
    return jnp.concatenate(outs, axis=1)


def _proj_ret_body(x, mod_ref, nw_ref, wq_ref, wk_ref, wv_ref, wz_ref, cos_ref, sin_ref,
                   q_ref, kt_ref, v_ref, z_ref, *, k_scale):
    u = _modulated(x, mod_ref, nw_ref)
    cos = cos_ref[...]
    sin = sin_ref[...]
    q_ref[0] = _rope(_dot(u, wq_ref[...]), cos, sin).astype(BF16)
    _store_chunked_t(kt_ref, _rope(_dot(u, wk_ref[...]), cos, sin) * k_scale, SCAN_CHUNK)
    v_ref[0] = _dot(u, wv_ref[...]).astype(BF16)
    z_ref[0] = _dot(u, wz_ref[...]).astype(BF16)


def _proj_call(body, h, n_ctx, modsel, norm_w, consts, row_tables, outs, name):
    h_args, h_specs = _hidden_operands(h, n_ctx)
    b, _, d = h_args[-1].shape
    l = sum(a.shape[1] for a in h_args)
    nt = l // TOKEN_TILE
    in_specs = h_specs + [pl.BlockSpec((1, 1, 3, d), lambda i, t: (i, jnp.minimum(t, 1), 0, 0)),
                          pl.BlockSpec((1, d), lambda i, t: (0, 0))]
    for a in consts:
        in_specs.append(pl.BlockSpec(a.shape, lambda i, t, nd=a.ndim: (0,) * nd))
    for a in row_tables:
        in_specs.append(pl.BlockSpec((TOKEN_TILE, a.shape[1]), lambda i, t: (t, 0)))
    out_specs, out_shape = [], []
    for n, dt, chunk in outs:
        if chunk is None:
            out_specs.append(pl.BlockSpec((1, TOKEN_TILE, n), lambda i, t: (i, t, 0)))
            out_shape.append(jax.ShapeDtypeStruct((b, l, n), dt))
        else:
            out_specs.append(pl.BlockSpec((1, TOKEN_TILE // chunk, n, chunk), lambda i, t: (i, t, 0, 0)))
            out_shape.append(jax.ShapeDtypeStruct((b, l // chunk, n, chunk), dt))
    return pl.pallas_call(
        _with_hidden_tile(body, len(h_args), n_ctx // TOKEN_TILE),
        grid=(b, nt),
        in_specs=in_specs,
        out_specs=out_specs,
        out_shape=out_shape,
        compiler_params=pltpu.CompilerParams(
            dimension_semantics=("parallel", "parallel"), vmem_limit_bytes=VMEM_LIMIT),
        name=name,
    )(*h_args, modsel, norm_w.reshape(1, d), *consts, *row_tables)


def _conv_body(x_ref, w_ref, b_ref, o_ref, *, n_ctx, n_blocks, scale, transposed):
    blk = SCAN_CHUNK
    l = x_ref.shape[1]
    w = w_ref[...]
    bias = b_ref[...]
    rid = lax.broadcasted_iota(jnp.int32, (blk, 1), 0)

    def emit(chunk, start, acc):
        y = _silu(acc + bias) * scale
        if transposed:
            o_ref[0, chunk] = y.T.astype(BF16)
        else:
            o_ref[0, pl.ds(start, blk), :] = y.astype(BF16)

    def combine(a, bm, cc, first, last):
        return (jnp.where(first, 0.0, pltpu.roll(a, 1, 0)) + bm
                + jnp.where(last, 0.0, pltpu.roll(cc, blk - 1, 0)))

    xc = x_ref[0, 0:blk, :]
    emit(0, 0, combine(xc * w[3:4], xc * w[4:5], xc * w[5:6], rid == 0, rid == blk - 1))

    col = rid % GRID_W

    def lat_block(it, carry):
        base = pl.multiple_of(n_ctx + it * blk, blk)
        mid = x_ref[0, pl.ds(base, blk), :]
        up_edge = x_ref[0, pl.ds(pl.multiple_of(base - GRID_W, GRID_W), GRID_W), :]
        dn_start = pl.multiple_of(jnp.minimum(base + blk, l - GRID_W), GRID_W)
        dn_edge = x_ref[0, pl.ds(dn_start, GRID_W), :]
        up_edge = jnp.where(it > 0, up_edge, 0.0)
        dn_edge = jnp.where(it < n_blocks - 1, dn_edge, 0.0)
        up = jnp.concatenate([up_edge, mid[:blk - GRID_W]], axis=0)
        down = jnp.concatenate([mid[GRID_W:], dn_edge], axis=0)
        a = up * w[0:1] + mid * w[3:4] + down * w[6:7]
        bm = up * w[1:2] + mid * w[4:5] + down * w[7:8]
        cc = up * w[2:3] + mid * w[5:6] + down * w[8:9]
        emit(n_ctx // blk + it, base, combine(a, bm, cc, col == 0, col == GRID_W - 1))
        return carry

    lax.fori_loop(0, n_blocks, lat_block, 0)


def _conv(qk_raw, conv_w, conv_b, n_ctx, col_offset, width, scale, transposed):
    b, l, c = qk_raw.shape
    tc = 256
    blk = SCAN_CHUNK
    assert n_ctx == blk and (l - n_ctx) % blk == 0 and blk % GRID_W == 0
    off = col_offset // tc
    if transposed:
        out_spec = pl.BlockSpec((1, l // blk, tc, blk), lambda i, j: (i, 0, j, 0))
        out_shape = jax.ShapeDtypeStruct((b, l // blk, width, blk), BF16)
    else:
        out_spec = pl.BlockSpec((1, l, tc), lambda i, j: (i, 0, j))
        out_shape = jax.ShapeDtypeStruct((b, l, width), BF16)
    return pl.pallas_call(
        functools.partial(_conv_body, n_ctx=n_ctx, n_blocks=(l - n_ctx) // blk, scale=scale,
                          transposed=transposed),
        grid=(b, width // tc),
        in_specs=[pl.BlockSpec((1, l, tc), lambda i, j: (i, 0, j + off)),
                  pl.BlockSpec((9, tc), lambda i, j: (0, j + off)),
                  pl.BlockSpec((1, tc), lambda i, j: (0, j + off))],
        out_specs=out_spec,
        out_shape=out_shape,
        compiler_params=pltpu.CompilerParams(
            dimension_semantics=("parallel", "parallel"), vmem_limit_bytes=VMEM_LIMIT),
        name="mlstm_conv_kt" if transposed else "mlstm_conv_q",
    )(qk_raw, conv_w.reshape(9, c), conv_b.reshape(1, c))


def _prefix_max(x, lane, backward):
    ax = x.ndim - 1
    n = x.shape[ax]
    sh = 1
    while sh < n:
        if backward:
            x = jnp.maximum(x, jnp.where(lane < n - sh, pltpu.roll(x, n - sh, ax), NEG_INF))
        else:
            x = jnp.maximum(x, jnp.where(lane >= sh, pltpu.roll(x, sh, ax), NEG_INF))
        sh *= 2
    return x


def _mlstm_scan_body(q_ref, kt_ref, v_ref, g_ref, o_ref, acc_ref, st_ref, n_ref, row_ref,
                     *, n_ctx_chunks, n_chunks):
    c = SCAN_CHUNK
    dv = v_ref.shape[2]
    st_ref[...] = jnp.zeros(st_ref.shape, F32)
    n_ref[...] = jnp.zeros(n_ref.shape, F32)
    masks = _tri_masks(c)

    g3 = g_ref[0, 0]
    g2 = g3.reshape(n_chunks * 8, c)
    lane = lax.broadcasted_iota(jnp.int32, (1, 1, c), 2)
    token_rows, chunk_rows = [], []
    for dirn in range(2):
        cum = _dot01_right(g2, masks[1 - dirn].astype(BF16)).reshape(n_chunks, 8, c)
        i_r = g3[:, 2 * dirn:2 * dirn + 1]
        a_r = g3[:, 2 * dirn + 1:2 * dirn + 2]
        b_r = cum[:, 2 * dirn + 1:2 * dirn + 2]
        u = i_r - b_r
        b_end = jnp.sum(a_r, axis=2, keepdims=True)
        g_max = jnp.max(b_end + u, axis=2, keepdims=True)
        cm = _prefix_max(u, lane, bool(dirn))
        m = jnp.zeros((1, 1, 1), F32)
        m_prev_l, m_new_l = [None] * n_chunks, [None] * n_chunks
        for s in range(n_chunks):
            idx = s
            if dirn:
                idx = n_ctx_chunks - 1 - s if s < n_ctx_chunks else n_chunks - 1 - (s - n_ctx_chunks)
            m_prev_l[idx] = m
            m = jnp.maximum(b_end[idx:idx + 1] + m, g_max[idx:idx + 1])
            m_new_l[idx] = m
        m_prev = jnp.concatenate(m_prev_l, axis=0)
        m_new = jnp.concatenate(m_new_l, axis=0)
        mm = jnp.maximum(m_prev, cm)
        token_rows += [u, jnp.exp(b_end + u - m_new), mm, b_r + mm]
        chunk_rows += [jnp.broadcast_to(jnp.exp(b_end + m_prev - m_new), (n_chunks, 1, c)),
                       jnp.broadcast_to(m_prev, (n_chunks, 1, c))]
    row_ref[...] = jnp.concatenate(token_rows + chunk_rows + [jnp.zeros((n_chunks, 4, c), F32)], axis=1)

    def lane_bcast_col(row):
        return jnp.broadcast_to(row, (LANES, c)).T

    def step(s, first):
        dirs = (0, 1)
        idxs = (s, _bw_chunk(s, n_ctx_chunks, n_chunks))
        starts = [pl.multiple_of(idx * c, c) for idx in idxs]
        q = [q_ref[0, pl.ds(st, c), :] for st in starts]
        kt = [kt_ref[0, idx] for idx in idxs]
        v = [v_ref[0, pl.ds(st, c), :] for st in starts]
        rows = [row_ref[idx] for idx in idxs]
        state = [st_ref[d] for d in dirs]
        nrm = [n_ref[d] for d in dirs]
        qk = [_dot(q[d], kt[d]) for d in dirs]
        mm_b = [lane_bcast_col(rows[d][4 * d + 2:4 * d + 3]) for d in dirs]
        bmm_b = [lane_bcast_col(rows[d][4 * d + 3:4 * d + 4]) for d in dirs]
        m_prev = [rows[d][9 + 2 * d:10 + 2 * d, 0:1] for d in dirs]
        decay = [rows[d][8 + 2 * d:9 + 2 * d, 0:1] for d in dirs]
        sc = [qk[d] * jnp.exp(jnp.where(masks[d], rows[d][4 * d:4 * d + 1]
                                        - jnp.concatenate([mm_b[d]] * (c // LANES), axis=1), NEG_INF))
              for d in dirs]
        w_inter = [jnp.exp(m_prev[d] - mm_b[d]) for d in dirs]
        intra = [_dot(sc[d].astype(BF16), v[d]) for d in dirs]
        inter = [_dot(q[d], state[d].astype(BF16)) for d in dirs]
        qn = [_dot(q[d], nrm[d].astype(BF16)) for d in dirs]
        wkt = [kt[d].astype(F32) * rows[d][4 * d + 1:4 * d + 2] for d in dirs]
        upd = [_dot(wkt[d].astype(BF16), v[d]) for d in dirs]
        for d in dirs:
            num = intra[d] + inter[d] * jnp.concatenate([w_inter[d]] * (dv // LANES), axis=1)
            den = jnp.sum(sc[d], axis=1, keepdims=True) + qn[d] * w_inter[d]
            rcp = 1.0 / jnp.maximum(jnp.abs(den), jnp.exp(-bmm_b[d]))
            _emit(o_ref, acc_ref, starts[d], c, num * jnp.concatenate([rcp] * (dv // LANES), axis=1), first[d])
        for d in dirs:
            st_ref[d] = decay[d] * state[d] + upd[d]
            n_ref[d] = decay[d] * nrm[d] + jnp.sum(wkt[d], axis=1, keepdims=True)

    _scan_steps(step, n_ctx_chunks, n_chunks, 1)


def _mlstm_scan(q, kt, v, g_rows, n_ctx):
    b, l, _ = q.shape
    dk = q.shape[2] // N_HEADS
    dv = v.shape[2] // N_HEADS
    n_chunks = l // SCAN_CHUNK
    return pl.pallas_call(
        functools.partial(_mlstm_scan_body, n_ctx_chunks=n_ctx // SCAN_CHUNK, n_chunks=n_chunks),
        grid=(b, N_HEADS),
        in_specs=[pl.BlockSpec((1, l, dk), lambda i, h: (i, 0, h)),
                  pl.BlockSpec((1, n_chunks, dk, SCAN_CHUNK), lambda i, h: (i, 0, h, 0)),
                  pl.BlockSpec((1, l, dv), lambda i, h: (i, 0, h)),
                  pl.BlockSpec((1, 1, n_chunks, 8, SCAN_CHUNK), lambda i, h: (i, h, 0, 0, 0))],
        out_specs=pl.BlockSpec((1, l, dv), lambda i, h: (i, 0, h)),
        out_shape=jax.ShapeDtypeStruct((b, l, N_HEADS * dv), BF16),
        scratch_shapes=[pltpu.VMEM((l, dv), F32),
                        pltpu.VMEM((2, dk, dv), F32),
                        pltpu.VMEM((2, dk, LANES), F32),
                        pltpu.VMEM((n_chunks, 16, SCAN_CHUNK), F32)],
        compiler_params=pltpu.CompilerParams(
            dimension_semantics=("parallel", "parallel"), vmem_limit_bytes=VMEM_LIMIT),
        name="mlstm_scan",
    )(q, kt, v, g_rows)


def _ret_scan_body(q_ref, kt_ref, v_ref, dl_ref, o_ref, acc_ref, st_ref, dm_ref, *, n_ctx_chunks, n_chunks):
    c = SCAN_CHUNK
    st_ref[...] = jnp.zeros(st_ref.shape, F32)
    masks = _tri_masks(c)
    row = lax.broadcasted_iota(jnp.int32, (c, c), 0)
    col = lax.broadcasted_iota(jnp.int32, (c, c), 1)
    dist = jnp.abs(row - col).astype(F32)
    rid = lax.broadcasted_iota(jnp.int32, (c, 1), 0).astype(F32)
    lid = lax.broadcasted_iota(jnp.int32, (1, c), 1).astype(F32)
    log_gamma = []
    for dirn in range(2):
        lg = _log_sigmoid(dl_ref[0, dirn])[0:1, 0:1]
        log_gamma.append(lg)
        dm_ref[dirn] = jnp.where(masks[dirn], jnp.exp(lg * dist), 0.0)

    pos_col = (rid + 1.0, c - rid)
    pos_row = (lid + 1.0, c - lid)

    def step(s, first):
        dirs = (0, 1)
        idxs = (s, _bw_chunk(s, n_ctx_chunks, n_chunks))
        starts = [pl.multiple_of(idx * c, c) for idx in idxs]
        q = [q_ref[0, pl.ds(st, c), :] for st in starts]
        kt = [kt_ref[0, idx] for idx in idxs]
        v = [v_ref[0, pl.ds(st, c), :] for st in starts]
        state = [st_ref[d] for d in dirs]
        sc = [(_dot(q[d], kt[d]) * dm_ref[d]).astype(BF16) for d in dirs]
        intra = [_dot(sc[d], v[d]) for d in dirs]
        inter = [_dot(q[d], state[d].astype(BF16)) for d in dirs]
        kdt = [(kt[d].astype(F32) * jnp.exp(log_gamma[d] * (c - pos_row[d]))).astype(BF16) for d in dirs]
        upd = [_dot(kdt[d], v[d]) for d in dirs]
        for d in dirs:
            _emit(o_ref, acc_ref, starts[d], c, intra[d] + inter[d] * jnp.exp(log_gamma[d] * pos_col[d]), first[d])
        for d in dirs:
            st_ref[d] = jnp.exp(log_gamma[d] * c) * state[d] + upd[d]

    _scan_steps(step, n_ctx_chunks, n_chunks, 2)


def _ret_scan(q, kt, v, dl, n_ctx):
    b, l, _ = q.shape
    dk = q.shape[2] // N_HEADS
    dv = v.shape[2] // N_HEADS
    n_chunks = l // SCAN_CHUNK
    return pl.pallas_call(
        functools.partial(_ret_scan_body, n_ctx_chunks=n_ctx // SCAN_CHUNK, n_chunks=n_chunks),
        grid=(b, N_HEADS),
        in_specs=[pl.BlockSpec((1, l, dk), lambda i, h: (i, 0, h)),
                  pl.BlockSpec((1, n_chunks, dk, SCAN_CHUNK), lambda i, h: (i, 0, h, 0)),
                  pl.BlockSpec((1, l, dv), lambda i, h: (i, 0, h)),
                  pl.BlockSpec((1, 2, 8, LANES), lambda i, h: (h, 0, 0, 0))],
        out_specs=pl.BlockSpec((1, l, dv), lambda i, h: (i, 0, h)),
        out_shape=jax.ShapeDtypeStruct((b, l, N_HEADS * dv), BF16),
        scratch_shapes=[pltpu.VMEM((l, dv), F32),
                        pltpu.VMEM((2, dk, dv), F32),
                        pltpu.VMEM((2, SCAN_CHUNK, SCAN_CHUNK), F32)],
        compiler_params=pltpu.CompilerParams(
            dimension_semantics=("parallel", "parallel"), vmem_limit_bytes=VMEM_LIMIT),
        name="ret_scan",
    )(q, kt, v, dl)


def _gla_scan_body(q_ref, kt_ref, v_ref, code_ref, w2_ref, b2_ref, o_ref,
                   acc_ref, st_ref, s_ref, qi_ref, kdt_ref, e_ref, m01_ref, *, n_ctx_chunks, n_chunks):
    c = GLA_CHUNK
    t = GLA_SUB
    n_sub = c // t
    dk = q_ref.shape[2]
    dv = v_ref.shape[2]
    st_ref[...] = jnp.zeros(st_ref.shape, F32)
    masks = _tri_masks(c)
    for d in range(2):
        m01_ref[d] = masks[d].astype(BF16)
    lid = lax.broadcasted_iota(jnp.int32, (1, c), 1)

    def intra(it, carry):
        chains = [(it * GLA_INTRA_CHUNKS + j, dirn) for j in range(GLA_INTRA_CHUNKS) for dirn in range(2)]
        starts = [pl.multiple_of(idx * c, c) for idx, _ in chains]
        qf = [q_ref[0, pl.ds(st, c), :].astype(F32) for st in starts]
        ktf = [kt_ref[0, idx].astype(F32) for idx, _ in chains]
        code = [code_ref[0, pl.ds(st, c), :] for st in starts]
        a = [_log_sigmoid(_dot(code[k], w2_ref[d]) + b2_ref[d]) * (1.0 / GLA_TAU) for k, (_, d) in enumerate(chains)]
        parts = [_split3(x) for x in a]
        b = [None] * len(chains)
        for p in (2, 1, 0):
            for k, (_, d) in enumerate(chains):
                term = _dot(m01_ref[d], parts[k][p])
                b[k] = term if b[k] is None else b[k] + term
        qp, kp, kdt, e_end = [], [], [], []
        for k, (_, dirn) in enumerate(chains):
            refs = []
            for sb in range(n_sub):
                r0 = sb * t + (t - 1 if dirn else 0)
                refs.append(b[k][r0:r0 + 1] - a[k][r0:r0 + 1])
            own = jnp.concatenate([jnp.broadcast_to(r, (t, dk)) for r in refs], axis=0)
            b_rel = b[k] - own
            q_own = qf[k] * jnp.exp(b_rel)
            k_rel_t = (ktf[k] * jnp.exp(jnp.minimum(-(b_rel.T), GLA_EXP_CLAMP))).astype(BF16)
            r_end = 0 if dirn else c - 1
            b_end = b[k][r_end:r_end + 1]
            kdt.append((ktf[k] * jnp.exp((b_end - b[k]).T)).astype(BF16))
            e_end.append(jnp.broadcast_to(jnp.exp(b_end), (8, dk)))
            col_blocks = []
            for sb in range(n_sub):
                pieces = []
                for j in range(n_sub):
                    if (j <= sb) if dirn else (j >= sb):
                        qj = q_own[j * t:(j + 1) * t]
                        pieces.append(qj if j == sb else qj * jnp.exp(refs[j] - refs[sb]))
                    else:
                        pieces.append(jnp.zeros((t, dk), F32))
                col_blocks.append(jnp.concatenate(pieces, axis=0).astype(BF16))
            qp.append(col_blocks)
            kp.append(jnp.concatenate(
                [jnp.where(jnp.logical_and(lid >= sb * t, lid < (sb + 1) * t), k_rel_t, 0.0)
                 for sb in range(n_sub)], axis=0))
        s = [_dot(jnp.concatenate(qp[k], axis=1), kp[k]) for k in range(len(chains))]
        for k, (idx, dirn) in enumerate(chains):
            s_ref[dirn, pl.ds(starts[k], c), :] = jnp.where(masks[dirn], s[k], 0.0).astype(BF16)
            qi_ref[dirn, pl.ds(starts[k], c), :] = qp[k][n_sub - 1 if dirn else 0]
            kdt_ref[dirn, idx] = kdt[k]
            e_ref[dirn, idx] = e_end[k]
        return carry

    lax.fori_loop(0, n_chunks // GLA_INTRA_CHUNKS, intra, 0)

    def step(s, first):
        dirs = (0, 1)
        idxs = (s, _bw_chunk(s, n_ctx_chunks, n_chunks))
        starts = [pl.multiple_of(idx * c, c) for idx in idxs]
        v = [v_ref[0, pl.ds(st, c), :] for st in starts]
        state = [st_ref[d] for d in dirs]
        intra = [_dot(s_ref[d, pl.ds(starts[d], c), :], v[d]) for d in dirs]
        inter = [_dot(qi_ref[d, pl.ds(starts[d], c), :], state[d].astype(BF16)) for d in dirs]
        upd = [_dot(kdt_ref[d, idxs[d]], v[d]) for d in dirs]
        e_cols = [jnp.concatenate([e_ref[d, idxs[d]]] * (dk // 8), axis=0).T for d in dirs]
        for d in dirs:
            _emit(o_ref, acc_ref, starts[d], c, intra[d] + inter[d], first[d])
        for d in dirs:
            st_ref[d] = jnp.concatenate([e_cols[d]] * (dv // dk), axis=1) * state[d] + upd[d]

    _scan_steps(step, n_ctx_chunks, n_chunks, 2)


def _gla_scan(q, kt, v, code, w2, b2, n_ctx):
    b, l, _ = q.shape
    dk = q.shape[2] // N_HEADS
    dv = v.shape[2] // N_HEADS
    n_chunks = l // GLA_CHUNK
    tok = lambda i, h: (i, 0, h)
    return pl.pallas_call(
        functools.partial(_gla_scan_body, n_ctx_chunks=n_ctx // GLA_CHUNK, n_chunks=n_chunks),
        grid=(b, N_HEADS),
        in_specs=[pl.BlockSpec((1, l, dk), tok),
                  pl.BlockSpec((1, n_chunks, dk, GLA_CHUNK), lambda i, h: (i, 0, h, 0)),
                  pl.BlockSpec((1, l, dv), tok),
                  pl.BlockSpec((1, l, LANES), lambda i, h: (i, 0, 0)),
                  pl.BlockSpec((2, LANES, dk), lambda i, h: (0, 0, h)),
                  pl.BlockSpec((2, 1, dk), lambda i, h: (0, 0, h))],
        out_specs=pl.BlockSpec((1, l, dv), tok),
        out_shape=jax.ShapeDtypeStruct((b, l, N_HEADS * dv), BF16),
        scratch_shapes=[pltpu.VMEM((l, dv), F32),
                        pltpu.VMEM((2, dk, dv), F32),
                        pltpu.VMEM((2, l, GLA_CHUNK), BF16),
                        pltpu.VMEM((2, l, dk), BF16),
                        pltpu.VMEM((2, n_chunks, dk, GLA_CHUNK), BF16),
                        pltpu.VMEM((2, n_chunks, 8, dk), F32),
                        pltpu.VMEM((2, GLA_CHUNK, GLA_CHUNK), BF16)],
        compiler_params=pltpu.CompilerParams(
            dimension_semantics=("parallel", "parallel"), vmem_limit_bytes=VMEM_LIMIT),
        name="gla_scan",
    )(q, kt, v, code, w2, b2)


def _out_body(x, o_ref, z_ref, hw_ref, ow_ref, mod_ref, *rest, center, final):
    if final:
        fw_ref, out_ref = rest
    else:
        (out_ref,) = rest
    dv = o_ref.shape[2] // N_HEADS
    proj = None
    for hh in range(N_HEADS):
        cols = slice(hh * dv, (hh + 1) * dv)
        yh = o_ref[0, :, cols].astype(F32)
        if center:
            yh = yh - jnp.mean(yh, axis=-1, keepdims=True)
        yn = yh * lax.rsqrt(jnp.mean(yh * yh, axis=-1, keepdims=True) + NORM_EPS) * hw_ref[:, cols]
        a = (yn * _silu(z_ref[0, :, cols].astype(F32))).astype(BF16)
        term = _dot(a, ow_ref[cols, :])
        proj = term if proj is None else proj + term
    hn = x + mod_ref[0, 0][2:3] * proj
    if final:
        hn = hn * lax.rsqrt(jnp.mean(hn * hn, axis=-1, keepdims=True) + NORM_EPS) * fw_ref[...]
    out_ref[0] = hn


def _out_call(o, z, head_norm_w, out_w, h, modsel, final_w, n_ctx, center):
    b, l, di = o.shape
    d = out_w.shape[1]
    final = final_w is not None
    skip = n_ctx // TOKEN_TILE if final else 0
    nt = l // TOKEN_TILE - skip
    tok = lambda i, t: (i, t + skip, 0)
    h_args, h_specs = _hidden_operands(h, n_ctx, skip)
    in_specs = h_specs + [pl.BlockSpec((1, TOKEN_TILE, di), tok),
                          pl.BlockSpec((1, TOKEN_TILE, di), tok),
                          pl.BlockSpec((1, di), lambda i, t: (0, 0)),
                          pl.BlockSpec((di, d), lambda i, t: (0, 0)),
                          pl.BlockSpec((1, 1, 3, d), lambda i, t: (i, jnp.minimum(t + skip, 1), 0, 0))]
    args = h_args + [o, z, head_norm_w.reshape(1, di), out_w, modsel]
    if final:
        in_specs.append(pl.BlockSpec((1, d), lambda i, t: (0, 0)))
        args.append(final_w.reshape(1, d))
    return pl.pallas_call(
        _with_hidden_tile(functools.partial(_out_body, center=center, final=final),
                          len(h_args), n_ctx // TOKEN_TILE, skip),
        grid=(b, nt),
        in_specs=in_specs,
        out_specs=pl.BlockSpec((1, TOKEN_TILE, d), lambda i, t: (i, t, 0)),
        out_shape=jax.ShapeDtypeStruct((b, nt * TOKEN_TILE, d), F32),
        compiler_params=pltpu.CompilerParams(
            dimension_semantics=("parallel", "parallel"), vmem_limit_bytes=VMEM_LIMIT),
        name="out_proj_final" if final else "out_proj",
    )(*args)


def _pad_cols(w, n):
    return jnp.pad(w, ((0, 0), (0, n - w.shape[1])))


def _mlstm_layer(h, b, l, modsel, norm_w, in_w, conv_w, conv_b, gate_b, n_ctx):
    d = in_w.shape[0]
    di = 2 * d
    qk_w = 2 * d
    wqk = in_w[:, :qk_w].astype(BF16)
    wv = in_w[:, qk_w:qk_w + di].astype(BF16)
    wz = in_w[:, qk_w + di:qk_w + 2 * di].astype(BF16)
    wg = _pad_cols(in_w[:, qk_w + 2 * di:], LANES).astype(BF16)
    gb = _pad_cols(gate_b.reshape(1, -1), LANES)
    qk_raw, v, z, g = _proj_call(
        _proj_mlstm_body, h, n_ctx, modsel, norm_w, [wqk, wv, wz, wg, gb], [],
        [(qk_w, F32, None), (di, BF16, None), (di, BF16, None), (LANES, F32, None)], "proj_mlstm")
    head_qk = qk_w // (2 * N_HEADS)
    q = _conv(qk_raw, conv_w, conv_b, n_ctx, 0, qk_w // 2, 1.0, False)
    kt = _conv(qk_raw, conv_w, conv_b, n_ctx, qk_w // 2, qk_w // 2, head_qk ** -0.5, True)
    n_chunks = l // SCAN_CHUNK
    g_rows = g[:, :, :4 * N_HEADS].reshape(b, n_chunks, SCAN_CHUNK, 4, N_HEADS).transpose(0, 4, 1, 3, 2)
    g_rows = jnp.pad(g_rows, ((0, 0), (0, 0), (0, 0), (0, 4), (0, 0)))
    return _mlstm_scan(q, kt, v, g_rows, n_ctx), z


def _gla_layer(h, b, l, modsel, norm_w, in_w, gk_w2, gk_b, n_ctx):
    d = in_w.shape[0]
    di = 2 * d
    key = gk_w2.shape[2]
    assert key // N_HEADS == LANES
    wq = in_w[:, :key].astype(BF16)
    wk = in_w[:, key:2 * key].astype(BF16)
    wv = in_w[:, 2 * key:2 * key + di].astype(BF16)
    wz = in_w[:, 2 * key + di:2 * key + 2 * di].astype(BF16)
    wlr = _pad_cols(in_w[:, 2 * key + 2 * di:], LANES).astype(BF16)
    q, kt, v, z, code = _proj_call(
        functools.partial(_proj_gla_body, q_scale=(key // N_HEADS) ** -0.5),
        h, n_ctx, modsel, norm_w, [wq, wk, wv, wz, wlr], [],
        [(key, BF16, None), (key, BF16, GLA_CHUNK), (di, BF16, None), (di, BF16, None),
         (LANES, BF16, None)], "proj_gla")
    w2 = jnp.zeros((2, LANES, key), F32)
    w2 = w2.at[0, :GLA_RANK].set(gk_w2[0]).at[1, GLA_RANK:2 * GLA_RANK].set(gk_w2[1]).astype(BF16)
    return _gla_scan(q, kt, v, code, w2, gk_b.reshape(2, 1, key), n_ctx), z


def _rope_tables(n_ctx, n_lat, head_k):
    quarter = head_k // 4
    pos = jnp.arange(n_lat, dtype=jnp.int32)
    inv_freq = ROPE_BASE ** (-jnp.arange(quarter, dtype=F32) / quarter)
    cos_parts, sin_parts = [], []
    for p in ((pos // GRID_W).astype(F32), (pos % GRID_W).astype(F32)):
        ang = p[:, None] * inv_freq[None, :]
        cos_parts += [jnp.cos(ang), jnp.cos(ang)]
        sin_parts += [-jnp.sin(ang), jnp.sin(ang)]
    cos = jnp.concatenate(cos_parts, axis=1)
    sin = jnp.concatenate(sin_parts, axis=1)
    cos = jnp.concatenate([jnp.ones((n_ctx, head_k), F32), cos], axis=0)
    sin = jnp.concatenate([jnp.zeros((n_ctx, head_k), F32), sin], axis=0)
    return cos, sin


def _ret_layer(h, b, l, modsel, norm_w, in_w, decay_logit, n_ctx):
    d = in_w.shape[0]
    di = 2 * d
    key = (in_w.shape[1] - 2 * di) // 2
    head_k = key // N_HEADS
    wq = in_w[:, :key].astype(BF16)
    wk = in_w[:, key:2 * key].astype(BF16)
    wv = in_w[:, 2 * key:2 * key + di].astype(BF16)
    wz = in_w[:, 2 * key + di:].astype(BF16)
    cos, sin = _rope_tables(n_ctx, l - n_ctx, head_k)
    q, kt, v, z = _proj_call(
        functools.partial(_proj_ret_body, k_scale=head_k ** -0.5),
        h, n_ctx, modsel, norm_w, [wq, wk, wv, wz], [cos, sin],
        [(key, BF16, None), (key, BF16, SCAN_CHUNK), (di, BF16, None), (di, BF16, None)], "proj_ret")
    dl = jnp.broadcast_to(decay_logit.astype(F32).T[:, :, None, None], (N_HEADS, 2, 8, LANES))
    return _ret_scan(q, kt, v, dl, n_ctx), z


def kernel(x, c, ctx, c_ctx, l0_norm_w, l0_ada_w, l0_ada_b, l0_in_w, l0_conv_w, l0_conv_b, l0_gate_b, l0_head_norm_w, l0_out_w, l1_norm_w, l1_ada_w, l1_ada_b, l1_in_w, l1_gk_w2, l1_gk_b, l1_head_norm_w, l1_out_w, l2_norm_w, l2_ada_w, l2_ada_b, l2_in_w, l2_decay_logit, l2_head_norm_w, l2_out_w, l3_norm_w, l3_ada_w, l3_ada_b, l3_in_w, l3_conv_w, l3_conv_b, l3_gate_b, l3_head_norm_w, l3_out_w, final_norm_w):
    layers = (
        ("mlstm", l0_norm_w, l0_ada_w, l0_ada_b, l0_out_w, l0_head_norm_w, (l0_in_w, l0_conv_w, l0_conv_b, l0_gate_b)),
        ("gla", l1_norm_w, l1_ada_w, l1_ada_b, l1_out_w, l1_head_norm_w, (l1_in_w, l1_gk_w2, l1_gk_b)),
        ("retention", l2_norm_w, l2_ada_w, l2_ada_b, l2_out_w, l2_head_norm_w, (l2_in_w, l2_decay_logit)),
        ("mlstm", l3_norm_w, l3_ada_w, l3_ada_b, l3_out_w, l3_head_norm_w, (l3_in_w, l3_conv_w, l3_conv_b, l3_gate_b)),
    )
    b, _, d = x.shape
    n_ctx = ctx.shape[1]
    assert n_ctx % TOKEN_TILE == 0 and n_ctx % SCAN_CHUNK == 0 and x.shape[1] % SCAN_CHUNK == 0
    assert TOKEN_TILE == SCAN_CHUNK and TOKEN_TILE % GLA_CHUNK == 0
    l = n_ctx + x.shape[1]
    h = (ctx, x)
    cc = jnp.concatenate([c, c_ctx[None, :], jnp.zeros((7, d), F32)], axis=0)
    for li, (kind, norm_w, ada_w, ada_b, out_w, head_norm_w, params) in enumerate(layers):
        last = li == len(layers) - 1
        mod = _ada(cc, ada_w, ada_b)
        mod_lat = mod[:b].reshape(b, 1, 3, d)
        mod_ctx = jnp.broadcast_to(mod[b].reshape(1, 1, 3, d), (b, 1, 3, d))
        modsel = jnp.concatenate([mod_ctx, mod_lat], axis=1)
        if kind == "mlstm":
            o, z = _mlstm_layer(h, b, l, modsel, norm_w, *params, n_ctx)
        elif kind == "gla":
            o, z = _gla_layer(h, b, l, modsel, norm_w, *params, n_ctx)
        else:
            o, z = _ret_layer(h, b, l, modsel, norm_w, *params, n_ctx)
        h = _out_call(o, z, head_norm_w, out_w.astype(BF16), h, modsel,
                      final_norm_w if last else None, n_ctx, center=(kind != "gla"))
    return h
```

```python
import functools

import jax
import jax.numpy as jnp
from jax import lax
from jax.experimental import pallas as pl
from jax.experimental.pallas import tpu as pltpu

N_HEADS = 4
GRID_W = 64
GLA_RANK = 16
GLA_TAU = 16.0
ROPE_BASE = 10000.0
NORM_EPS = 1e-6
NEG_INF = -1e30

TOKEN_TILE = 256
SCAN_CHUNK = 256
GLA_CHUNK = 128
GLA_SUB = 16
GLA_EXP_CLAMP = 80.0
GLA_INTRA_CHUNKS = 2
LANES = 128
VMEM_LIMIT = 56 * 1024 * 1024

F32 = jnp.float32
BF16 = jnp.bfloat16


def _silu(x):
    half = 0.5 * x
    return half + half * jnp.tanh(half)


def _log_sigmoid(x):
    return jnp.minimum(x, 0.0) - jnp.log(1.0 + jnp.exp(-jnp.abs(x)))


def _dot(a, b):
    return jnp.dot(a, b, preferred_element_type=F32)


def _split3(a):
    hi = a.astype(BF16)
    r1 = a - hi.astype(F32)
    mid = r1.astype(BF16)
    lo = (r1 - mid.astype(F32)).astype(BF16)
    return hi, mid, lo


def _dot01_left(m01, a):
    hi, mid, lo = _split3(a)
    return _dot(m01, lo) + _dot(m01, mid) + _dot(m01, hi)


def _dot01_right(a, m01):
    hi, mid, lo = _split3(a)
    return _dot(lo, m01) + _dot(mid, m01) + _dot(hi, m01)


def _tri_masks(n):
    row = lax.broadcasted_iota(jnp.int32, (n, n), 0)
    col = lax.broadcasted_iota(jnp.int32, (n, n), 1)
    return col <= row, col >= row


def _bw_chunk(step, n_ctx_chunks, n_chunks):
    return jnp.where(step < n_ctx_chunks, n_ctx_chunks - 1 - step, n_chunks - 1 - (step - n_ctx_chunks))


def _first_touch_groups(n_ctx_chunks, n_chunks):
    def bw_idx(s):
        return n_ctx_chunks - 1 - s if s < n_ctx_chunks else n_chunks - 1 - (s - n_ctx_chunks)
    bw_step = {bw_idx(s): s for s in range(n_chunks)}
    groups = []
    for s in range(n_chunks):
        flags = (s <= bw_step[s], s < bw_idx(s))
        if groups and groups[-1][2] == flags:
            groups[-1][1] = s + 1
        else:
            groups.append([s, s + 1, flags])
    return [tuple(g) for g in groups]


def _scan_steps(step, n_ctx_chunks, n_chunks, unroll):
    for start, stop, first in _first_touch_groups(n_ctx_chunks, n_chunks):
        lax.fori_loop(start, stop, lambda s, carry, first=first: step(s, first) or carry, 0,
                      unroll=unroll if (stop - start) % unroll == 0 else 1)


def _emit(o_ref, acc_ref, start, n, value, first):
    if first:
        acc_ref[pl.ds(start, n), :] = value
    else:
        o_ref[0, pl.ds(start, n), :] = (acc_ref[pl.ds(start, n), :] + value).astype(o_ref.dtype)


def _ada_body(c_ref, w_ref, b_ref, o_ref):
    s = _silu(c_ref[...])
    o_ref[...] = jnp.dot(s, w_ref[...], preferred_element_type=F32,
                         precision=lax.Precision.HIGHEST) + b_ref[...]


def _ada(cc, w, b):
    rows, d = cc.shape
    n = w.shape[1]
    tn = 512
    return pl.pallas_call(
        _ada_body,
        grid=(n // tn,),
        in_specs=[pl.BlockSpec((rows, d), lambda j: (0, 0)),
                  pl.BlockSpec((d, tn), lambda j: (0, j)),
                  pl.BlockSpec((1, tn), lambda j: (0, j))],
        out_specs=pl.BlockSpec((rows, tn), lambda j: (0, j)),
        out_shape=jax.ShapeDtypeStruct((rows, n), F32),
        name="ada",
    )(cc, w, b.reshape(1, n))


def _hidden_operands(h, n_ctx, tile_offset=0):
    if not isinstance(h, tuple):
        d = h.shape[2]
        return [h], [pl.BlockSpec((1, TOKEN_TILE, d), lambda i, t: (i, t + tile_offset, 0))]
    ctx, lat = h
    d = lat.shape[2]
    nct = n_ctx // TOKEN_TILE
    return [ctx, lat], [
        pl.BlockSpec((1, TOKEN_TILE, d), lambda i, t: (i, jnp.minimum(t + tile_offset, nct - 1), 0)),
        pl.BlockSpec((1, TOKEN_TILE, d), lambda i, t: (i, jnp.maximum(t + tile_offset - nct, 0), 0))]


def _with_hidden_tile(body, n_hidden, n_ctx_tiles, tile_offset=0):
    def kernel_fn(*refs):
        if n_hidden == 1:
            x = refs[0][0]
        else:
            is_ctx = pl.program_id(1) + tile_offset < n_ctx_tiles
            x = jnp.where(is_ctx, refs[0][0], refs[1][0])
        body(x, *refs[n_hidden:])
    return kernel_fn


def _modulated(x, mod_ref, nw_ref):
    y = x * lax.rsqrt(jnp.mean(x * x, axis=-1, keepdims=True) + NORM_EPS) * nw_ref[...]
    m = mod_ref[0, 0]
    return (y * (1.0 + m[1:2]) + m[0:1]).astype(BF16)


def _store_chunked_t(ref, x, chunk):
    xt = x.T
    for j in range(x.shape[0] // chunk):
        ref[0, j] = xt[:, j * chunk:(j + 1) * chunk].astype(ref.dtype)


def _proj_mlstm_body(x, mod_ref, nw_ref, wqk_ref, wv_ref, wz_ref, wg_ref, gb_ref,
                     qk_ref, v_ref, z_ref, g_ref):
    u = _modulated(x, mod_ref, nw_ref)
    qk_ref[0] = _dot(u, wqk_ref[...])
    v_ref[0] = _dot(u, wv_ref[...]).astype(BF16)
    z_ref[0] = _dot(u, wz_ref[...]).astype(BF16)
    g = _dot(u, wg_ref[...]) + gb_ref[...]
    lane = lax.broadcasted_iota(jnp.int32, g.shape, 1)
    is_forget = (lane // N_HEADS) % 2 == 1
    g_ref[0] = jnp.where(is_forget, _log_sigmoid(g), g)


def _proj_gla_body(x, mod_ref, nw_ref, wq_ref, wk_ref, wv_ref, wz_ref, wlr_ref,
                   q_ref, kt_ref, v_ref, z_ref, code_ref, *, q_scale):
    u = _modulated(x, mod_ref, nw_ref)
    q_ref[0] = (_dot(u, wq_ref[...]) * q_scale).astype(BF16)
    _store_chunked_t(kt_ref, _dot(u, wk_ref[...]), GLA_CHUNK)
    v_ref[0] = _dot(u, wv_ref[...]).astype(BF16)
    z_ref[0] = _dot(u, wz_ref[...]).astype(BF16)
    code_ref[0] = _dot(u, wlr_ref[...]).astype(BF16)


def _rope(t, cos, sin):
    outs = []
    for g in range(t.shape[1] // LANES):
        tg = t[:, g * LANES:(g + 1) * LANES]
        cg = cos[:, (g % 2) * LANES:(g % 2 + 1) * LANES]
        sg = sin[:, (g % 2) * LANES:(g % 2 + 1) * LANES]
        outs.append(tg * cg + pltpu.roll(tg, LANES // 2, 1) * sg)
    return jnp.concatenate(outs, axis=1)


def _proj_ret_body(x, mod_ref, nw_ref, wq_ref, wk_ref, wv_ref, wz_ref, cos_ref, sin_ref,
                   q_ref, kt_ref, v_ref, z_ref, *, k_scale):
    u = _modulated(x, mod_ref, nw_ref)
    cos = cos_ref[...]
    sin = sin_ref[...]
    q_ref[0] = _rope(_dot(u, wq_ref[...]), cos, sin).astype(BF16)
    _store_chunked_t(kt_ref, _rope(_dot(u, wk_ref[...]), cos, sin) * k_scale, SCAN_CHUNK)
    v_ref[0] = _dot(u, wv_ref[...]).astype(BF16)
    z_ref[0] = _dot(u, wz_ref[...]).astype(BF16)


def _proj_specs(spec, b, l, d):
    in_specs = [pl.BlockSpec((1, 1, 3, d), lambda i, t: (i, jnp.minimum(t, 1), 0, 0)),
                pl.BlockSpec((1, d), lambda i, t: (0, 0))]
    for a in spec["consts"]:
        in_specs.append(pl.BlockSpec(a.shape, lambda i, t, nd=a.ndim: (0,) * nd, pipeline_mode=pl.Buffered(1)))
    for a in spec["tables"]:
        in_specs.append(pl.BlockSpec((TOKEN_TILE, a.shape[1]), lambda i, t: (t, 0)))
    out_specs, out_shape = [], []
    for n, dt, chunk in spec["outs"]:
        if chunk is None:
            out_specs.append(pl.BlockSpec((1, TOKEN_TILE, n), lambda i, t: (i, t, 0)))
            out_shape.append(jax.ShapeDtypeStruct((b, l, n), dt))
        else:
            out_specs.append(pl.BlockSpec((1, TOKEN_TILE // chunk, n, chunk), lambda i, t: (i, t, 0, 0)))
            out_shape.append(jax.ShapeDtypeStruct((b, l // chunk, n, chunk), dt))
    return in_specs, out_specs, out_shape


def _proj_call(spec, h, n_ctx, modsel, norm_w):
    h_args, h_specs = _hidden_operands(h, n_ctx)
    b, _, d = h_args[-1].shape
    l = sum(a.shape[1] for a in h_args)
    in_specs, out_specs, out_shape = _proj_specs(spec, b, l, d)
    return pl.pallas_call(
        _with_hidden_tile(spec["body"], len(h_args), n_ctx // TOKEN_TILE),
        grid=(b, l // TOKEN_TILE),
        in_specs=h_specs + in_specs,
        out_specs=out_specs,
        out_shape=out_shape,
        compiler_params=pltpu.CompilerParams(
            dimension_semantics=("parallel", "parallel"), vmem_limit_bytes=VMEM_LIMIT),
        name="proj_" + spec["name"],
    )(*h_args, modsel, norm_w.reshape(1, d), *spec["consts"], *spec["tables"])


def _conv_body(x_ref, w_ref, b_ref, o_ref, *, n_ctx, n_blocks, scale, transposed):
    blk = SCAN_CHUNK
    l = x_ref.shape[1]
    w = w_ref[...]
    bias = b_ref[...]
    rid = lax.broadcasted_iota(jnp.int32, (blk, 1), 0)

    def emit(chunk, start, acc):
        y = _silu(acc + bias) * scale
        if transposed:
            o_ref[0, chunk] = y.T.astype(BF16)
        else:
            o_ref[0, pl.ds(start, blk), :] = y.astype(BF16)

    def combine(a, bm, cc, first, last):
        return (jnp.where(first, 0.0, pltpu.roll(a, 1, 0)) + bm
                + jnp.where(last, 0.0, pltpu.roll(cc, blk - 1, 0)))

    xc = x_ref[0, 0:blk, :]
    emit(0, 0, combine(xc * w[3:4], xc * w[4:5], xc * w[5:6], rid == 0, rid == blk - 1))

    col = rid % GRID_W

    def lat_block(it, carry):
        base = pl.multiple_of(n_ctx + it * blk, blk)
        mid = x_ref[0, pl.ds(base, blk), :]
        up_edge = x_ref[0, pl.ds(pl.multiple_of(base - GRID_W, GRID_W), GRID_W), :]
        dn_start = pl.multiple_of(jnp.minimum(base + blk, l - GRID_W), GRID_W)
        dn_edge = x_ref[0, pl.ds(dn_start, GRID_W), :]
        up_edge = jnp.where(it > 0, up_edge, 0.0)
        dn_edge = jnp.where(it < n_blocks - 1, dn_edge, 0.0)
        up = jnp.concatenate([up_edge, mid[:blk - GRID_W]], axis=0)
        down = jnp.concatenate([mid[GRID_W:], dn_edge], axis=0)
        a = up * w[0:1] + mid * w[3:4] + down * w[6:7]
        bm = up * w[1:2] + mid * w[4:5] + down * w[7:8]
        cc = up * w[2:3] + mid * w[5:6] + down * w[8:9]
        emit(n_ctx // blk + it, base, combine(a, bm, cc, col == 0, col == GRID_W - 1))
        return carry

    lax.fori_loop(0, n_blocks, lat_block, 0)


def _conv(qk_raw, conv_w, conv_b, n_ctx, col_offset, width, scale, transposed):
    b, l, c = qk_raw.shape
    tc = 256
    blk = SCAN_CHUNK
    assert n_ctx == blk and (l - n_ctx) % blk == 0 and blk % GRID_W == 0
    off = col_offset // tc
    if transposed:
        out_spec = pl.BlockSpec((1, l // blk, tc, blk), lambda i, j: (i, 0, j, 0))
        out_shape = jax.ShapeDtypeStruct((b, l // blk, width, blk), BF16)
    else:
        out_spec = pl.BlockSpec((1, l, tc), lambda i, j: (i, 0, j))
        out_shape = jax.ShapeDtypeStruct((b, l, width), BF16)
    return pl.pallas_call(
        functools.partial(_conv_body, n_ctx=n_ctx, n_blocks=(l - n_ctx) // blk, scale=scale,
                          transposed=transposed),
        grid=(b, width // tc),
        in_specs=[pl.BlockSpec((1, l, tc), lambda i, j: (i, 0, j + off)),
                  pl.BlockSpec((9, tc), lambda i, j: (0, j + off)),
                  pl.BlockSpec((1, tc), lambda i, j: (0, j + off))],
        out_specs=out_spec,
        out_shape=out_shape,
        compiler_params=pltpu.CompilerParams(
            dimension_semantics=("parallel", "parallel"), vmem_limit_bytes=VMEM_LIMIT),
        name="mlstm_conv_kt" if transposed else "mlstm_conv_q",
    )(qk_raw, conv_w.reshape(9, c), conv_b.reshape(1, c))


def _prefix_max(x, lane, backward):
    ax = x.ndim - 1
    n = x.shape[ax]
    sh = 1
    while sh < n:
        if backward:
            x = jnp.maximum(x, jnp.where(lane < n - sh, pltpu.roll(x, n - sh, ax), NEG_INF))
        else:
            x = jnp.maximum(x, jnp.where(lane >= sh, pltpu.roll(x, sh, ax), NEG_INF))
        sh *= 2
    return x


def _mlstm_scan_body(q_ref, kt_ref, v_ref, g_ref, o_ref, acc_ref, st_ref, n_ref, row_ref,
                     *, n_ctx_chunks, n_chunks):
    c = SCAN_CHUNK
    dv = v_ref.shape[2]
    st_ref[...] = jnp.zeros(st_ref.shape, F32)
    n_ref[...] = jnp.zeros(n_ref.shape, F32)
    masks = _tri_masks(c)

    g3 = g_ref[0, 0]
    g2 = g3.reshape(n_chunks * 8, c)
    lane = lax.broadcasted_iota(jnp.int32, (1, 1, c), 2)
    token_rows, chunk_rows = [], []
    for dirn in range(2):
        cum = _dot01_right(g2, masks[1 - dirn].astype(BF16)).reshape(n_chunks, 8, c)
        i_r = g3[:, 2 * dirn:2 * dirn + 1]
        a_r = g3[:, 2 * dirn + 1:2 * dirn + 2]
        b_r = cum[:, 2 * dirn + 1:2 * dirn + 2]
        u = i_r - b_r
        b_end = jnp.sum(a_r, axis=2, keepdims=True)
        g_max = jnp.max(b_end + u, axis=2, keepdims=True)
        cm = _prefix_max(u, lane, bool(dirn))
        m = jnp.zeros((1, 1, 1), F32)
        m_prev_l, m_new_l = [None] * n_chunks, [None] * n_chunks
        for s in range(n_chunks):
            idx = s
            if dirn:
                idx = n_ctx_chunks - 1 - s if s < n_ctx_chunks else n_chunks - 1 - (s - n_ctx_chunks)
            m_prev_l[idx] = m
            m = jnp.maximum(b_end[idx:idx + 1] + m, g_max[idx:idx + 1])
            m_new_l[idx] = m
        m_prev = jnp.concatenate(m_prev_l, axis=0)
        m_new = jnp.concatenate(m_new_l, axis=0)
        mm = jnp.maximum(m_prev, cm)
        token_rows += [u, jnp.exp(b_end + u - m_new), mm, b_r + mm]
        chunk_rows += [jnp.broadcast_to(jnp.exp(b_end + m_prev - m_new), (n_chunks, 1, c)),
                       jnp.broadcast_to(m_prev, (n_chunks, 1, c))]
    row_ref[...] = jnp.concatenate(token_rows + chunk_rows + [jnp.zeros((n_chunks, 4, c), F32)], axis=1)

    def lane_bcast_col(row):
        return jnp.broadcast_to(row, (LANES, c)).T

    def step(s, first):
        dirs = (0, 1)
        idxs = (s, _bw_chunk(s, n_ctx_chunks, n_chunks))
        starts = [pl.multiple_of(idx * c, c) for idx in idxs]
        q = [q_ref[0, pl.ds(st, c), :] for st in starts]
        kt = [kt_ref[0, idx] for idx in idxs]
        v = [v_ref[0, pl.ds(st, c), :] for st in starts]
        rows = [row_ref[idx] for idx in idxs]
        state = [st_ref[d] for d in dirs]
        nrm = [n_ref[d] for d in dirs]
        qk = [_dot(q[d], kt[d]) for d in dirs]
        mm_b = [lane_bcast_col(rows[d][4 * d + 2:4 * d + 3]) for d in dirs]
        bmm_b = [lane_bcast_col(rows[d][4 * d + 3:4 * d + 4]) for d in dirs]
        m_prev = [rows[d][9 + 2 * d:10 + 2 * d, 0:1] for d in dirs]
        decay = [rows[d][8 + 2 * d:9 + 2 * d, 0:1] for d in dirs]
        sc = [qk[d] * jnp.exp(jnp.where(masks[d], rows[d][4 * d:4 * d + 1]
                                        - jnp.concatenate([mm_b[d]] * (c // LANES), axis=1), NEG_INF))
              for d in dirs]
        w_inter = [jnp.exp(m_prev[d] - mm_b[d]) for d in dirs]
        intra = [_dot(sc[d].astype(BF16), v[d]) for d in dirs]
        inter = [_dot(q[d], state[d].astype(BF16)) for d in dirs]
        qn = [_dot(q[d], nrm[d].astype(BF16)) for d in dirs]
        wkt = [kt[d].astype(F32) * rows[d][4 * d + 1:4 * d + 2] for d in dirs]
        upd = [_dot(wkt[d].astype(BF16), v[d]) for d in dirs]
        for d in dirs:
            num = intra[d] + inter[d] * jnp.concatenate([w_inter[d]] * (dv // LANES), axis=1)
            den = jnp.sum(sc[d], axis=1, keepdims=True) + qn[d] * w_inter[d]
            rcp = 1.0 / jnp.maximum(jnp.abs(den), jnp.exp(-bmm_b[d]))
            _emit(o_ref, acc_ref, starts[d], c, num * jnp.concatenate([rcp] * (dv // LANES), axis=1), first[d])
        for d in dirs:
            st_ref[d] = decay[d] * state[d] + upd[d]
            n_ref[d] = decay[d] * nrm[d] + jnp.sum(wkt[d], axis=1, keepdims=True)

    _scan_steps(step, n_ctx_chunks, n_chunks, 1)


def _mlstm_scan(q, kt, v, g_rows, n_ctx):
    b, l, _ = q.shape
    dk = q.shape[2] // N_HEADS
    dv = v.shape[2] // N_HEADS
    n_chunks = l // SCAN_CHUNK
    return pl.pallas_call(
        functools.partial(_mlstm_scan_body, n_ctx_chunks=n_ctx // SCAN_CHUNK, n_chunks=n_chunks),
        grid=(b, N_HEADS),
        in_specs=[pl.BlockSpec((1, l, dk), lambda i, h: (i, 0, h)),
                  pl.BlockSpec((1, n_chunks, dk, SCAN_CHUNK), lambda i, h: (i, 0, h, 0)),
                  pl.BlockSpec((1, l, dv), lambda i, h: (i, 0, h)),
                  pl.BlockSpec((1, 1, n_chunks, 8, SCAN_CHUNK), lambda i, h: (i, h, 0, 0, 0))],
        out_specs=pl.BlockSpec((1, l, dv), lambda i, h: (i, 0, h)),
        out_shape=jax.ShapeDtypeStruct((b, l, N_HEADS * dv), BF16),
        scratch_shapes=[pltpu.VMEM((l, dv), F32),
                        pltpu.VMEM((2, dk, dv), F32),
                        pltpu.VMEM((2, dk, LANES), F32),
                        pltpu.VMEM((n_chunks, 16, SCAN_CHUNK), F32)],
        compiler_params=pltpu.CompilerParams(
            dimension_semantics=("parallel", "parallel"), vmem_limit_bytes=VMEM_LIMIT),
        name="mlstm_scan",
    )(q, kt, v, g_rows)


def _ret_scan_body(q_ref, kt_ref, v_ref, dl_ref, o_ref, acc_ref, st_ref, dm_ref, *, n_ctx_chunks, n_chunks):
    c = SCAN_CHUNK
    st_ref[...] = jnp.zeros(st_ref.shape, F32)
    masks = _tri_masks(c)
    row = lax.broadcasted_iota(jnp.int32, (c, c), 0)
    col = lax.broadcasted_iota(jnp.int32, (c, c), 1)
    dist = jnp.abs(row - col).astype(F32)
    rid = lax.broadcasted_iota(jnp.int32, (c, 1), 0).astype(F32)
    lid = lax.broadcasted_iota(jnp.int32, (1, c), 1).astype(F32)
    log_gamma = []
    for dirn in range(2):
        lg = _log_sigmoid(dl_ref[0, dirn])[0:1, 0:1]
        log_gamma.append(lg)
        dm_ref[dirn] = jnp.where(masks[dirn], jnp.exp(lg * dist), 0.0)

    pos_col = (rid + 1.0, c - rid)
    pos_row = (lid + 1.0, c - lid)

    def step(s, first):
        dirs = (0, 1)
        idxs = (s, _bw_chunk(s, n_ctx_chunks, n_chunks))
        starts = [pl.multiple_of(idx * c, c) for idx in idxs]
        q = [q_ref[0, pl.ds(st, c), :] for st in starts]
        kt = [kt_ref[0, idx] for idx in idxs]
        v = [v_ref[0, pl.ds(st, c), :] for st in starts]
        state = [st_ref[d] for d in dirs]
        sc = [(_dot(q[d], kt[d]) * dm_ref[d]).astype(BF16) for d in dirs]
        intra = [_dot(sc[d], v[d]) for d in dirs]
        inter = [_dot(q[d], state[d].astype(BF16)) for d in dirs]
        kdt = [(kt[d].astype(F32) * jnp.exp(log_gamma[d] * (c - pos_row[d]))).astype(BF16) for d in dirs]
        upd = [_dot(kdt[d], v[d]) for d in dirs]
        for d in dirs:
            _emit(o_ref, acc_ref, starts[d], c, intra[d] + inter[d] * jnp.exp(log_gamma[d] * pos_col[d]), first[d])
        for d in dirs:
            st_ref[d] = jnp.exp(log_gamma[d] * c) * state[d] + upd[d]

    _scan_steps(step, n_ctx_chunks, n_chunks, 2)


def _ret_scan(q, kt, v, dl, n_ctx):
    b, l, _ = q.shape
    dk = q.shape[2] // N_HEADS
    dv = v.shape[2] // N_HEADS
    n_chunks = l // SCAN_CHUNK
    return pl.pallas_call(
        functools.partial(_ret_scan_body, n_ctx_chunks=n_ctx // SCAN_CHUNK, n_chunks=n_chunks),
        grid=(b, N_HEADS),
        in_specs=[pl.BlockSpec((1, l, dk), lambda i, h: (i, 0, h)),
                  pl.BlockSpec((1, n_chunks, dk, SCAN_CHUNK), lambda i, h: (i, 0, h, 0)),
                  pl.BlockSpec((1, l, dv), lambda i, h: (i, 0, h)),
                  pl.BlockSpec((1, 2, 8, LANES), lambda i, h: (h, 0, 0, 0))],
        out_specs=pl.BlockSpec((1, l, dv), lambda i, h: (i, 0, h)),
        out_shape=jax.ShapeDtypeStruct((b, l, N_HEADS * dv), BF16),
        scratch_shapes=[pltpu.VMEM((l, dv), F32),
                        pltpu.VMEM((2, dk, dv), F32),
                        pltpu.VMEM((2, SCAN_CHUNK, SCAN_CHUNK), F32)],
        compiler_params=pltpu.CompilerParams(
            dimension_semantics=("parallel", "parallel"), vmem_limit_bytes=VMEM_LIMIT),
        name="ret_scan",
    )(q, kt, v, dl)


def _gla_scan_body(q_ref, kt_ref, v_ref, code_ref, w2_ref, b2_ref, o_ref,
                   acc_ref, st_ref, s_ref, qi_ref, kdt_ref, e_ref, m01_ref, *, n_ctx_chunks, n_chunks):
    c = GLA_CHUNK
    t = GLA_SUB
    n_sub = c // t
    dk = q_ref.shape[2]
    dv = v_ref.shape[2]
    st_ref[...] = jnp.zeros(st_ref.shape, F32)
    masks = _tri_masks(c)
    for d in range(2):
        m01_ref[d] = masks[d].astype(BF16)
    lid = lax.broadcasted_iota(jnp.int32, (1, c), 1)

    def intra(it, carry):
        chains = [(it * GLA_INTRA_CHUNKS + j, dirn) for j in range(GLA_INTRA_CHUNKS) for dirn in range(2)]
        starts = [pl.multiple_of(idx * c, c) for idx, _ in chains]
        qf = [q_ref[0, pl.ds(st, c), :].astype(F32) for st in starts]
        ktf = [kt_ref[0, idx].astype(F32) for idx, _ in chains]
        code = [code_ref[0, pl.ds(st, c), :] for st in starts]
        a = [_log_sigmoid(_dot(code[k], w2_ref[d]) + b2_ref[d]) * (1.0 / GLA_TAU) for k, (_, d) in enumerate(chains)]
        parts = [_split3(x) for x in a]
        b = [None] * len(chains)
        for p in (2, 1, 0):
            for k, (_, d) in enumerate(chains):
                term = _dot(m01_ref[d], parts[k][p])
                b[k] = term if b[k] is None else b[k] + term
        qp, kp, kdt, e_end = [], [], [], []
        for k, (_, dirn) in enumerate(chains):
            refs = []
            for sb in range(n_sub):
                r0 = sb * t + (t - 1 if dirn else 0)
                refs.append(b[k][r0:r0 + 1] - a[k][r0:r0 + 1])
            own = jnp.concatenate([jnp.broadcast_to(r, (t, dk)) for r in refs], axis=0)
            b_rel = b[k] - own
            q_own = qf[k] * jnp.exp(b_rel)
            k_rel_t = (ktf[k] * jnp.exp(jnp.minimum(-(b_rel.T), GLA_EXP_CLAMP))).astype(BF16)
            r_end = 0 if dirn else c - 1
            b_end = b[k][r_end:r_end + 1]
            kdt.append((ktf[k] * jnp.exp((b_end - b[k]).T)).astype(BF16))
            e_end.append(jnp.broadcast_to(jnp.exp(b_end), (8, dk)))
            col_blocks = []
            for sb in range(n_sub):
                pieces = []
                for j in range(n_sub):
                    if (j <= sb) if dirn else (j >= sb):
                        qj = q_own[j * t:(j + 1) * t]
                        pieces.append(qj if j == sb else qj * jnp.exp(refs[j] - refs[sb]))
                    else:
                        pieces.append(jnp.zeros((t, dk), F32))
                col_blocks.append(jnp.concatenate(pieces, axis=0).astype(BF16))
            qp.append(col_blocks)
            kp.append(jnp.concatenate(
                [jnp.where(jnp.logical_and(lid >= sb * t, lid < (sb + 1) * t), k_rel_t, 0.0)
                 for sb in range(n_sub)], axis=0))
        s = [_dot(jnp.concatenate(qp[k], axis=1), kp[k]) for k in range(len(chains))]
        for k, (idx, dirn) in enumerate(chains):
            s_ref[dirn, pl.ds(starts[k], c), :] = jnp.where(masks[dirn], s[k], 0.0).astype(BF16)
            qi_ref[dirn, pl.ds(starts[k], c), :] = qp[k][n_sub - 1 if dirn else 0]
            kdt_ref[dirn, idx] = kdt[k]
            e_ref[dirn, idx] = e_end[k]
        return carry

    lax.fori_loop(0, n_chunks // GLA_INTRA_CHUNKS, intra, 0)

    def step(s, first):
        dirs = (0, 1)
        idxs = (s, _bw_chunk(s, n_ctx_chunks, n_chunks))
        starts = [pl.multiple_of(idx * c, c) for idx in idxs]
        v = [v_ref[0, pl.ds(st, c), :] for st in starts]
        state = [st_ref[d] for d in dirs]
        intra = [_dot(s_ref[d, pl.ds(starts[d], c), :], v[d]) for d in dirs]
        inter = [_dot(qi_ref[d, pl.ds(starts[d], c), :], state[d].astype(BF16)) for d in dirs]
        upd = [_dot(kdt_ref[d, idxs[d]], v[d]) for d in dirs]
        e_cols = [jnp.concatenate([e_ref[d, idxs[d]]] * (dk // 8), axis=0).T for d in dirs]
        for d in dirs:
            _emit(o_ref, acc_ref, starts[d], c, intra[d] + inter[d], first[d])
        for d in dirs:
            st_ref[d] = jnp.concatenate([e_cols[d]] * (dv // dk), axis=1) * state[d] + upd[d]

    _scan_steps(step, n_ctx_chunks, n_chunks, 2)


def _gla_scan(q, kt, v, code, w2, b2, n_ctx):
    b, l, _ = q.shape
    dk = q.shape[2] // N_HEADS
    dv = v.shape[2] // N_HEADS
    n_chunks = l // GLA_CHUNK
    tok = lambda i, h: (i, 0, h)
    return pl.pallas_call(
        functools.partial(_gla_scan_body, n_ctx_chunks=n_ctx // GLA_CHUNK, n_chunks=n_chunks),
        grid=(b, N_HEADS),
        in_specs=[pl.BlockSpec((1, l, dk), tok),
                  pl.BlockSpec((1, n_chunks, dk, GLA_CHUNK), lambda i, h: (i, 0, h, 0)),
                  pl.BlockSpec((1, l, dv), tok),
                  pl.BlockSpec((1, l, LANES), lambda i, h: (i, 0, 0)),
                  pl.BlockSpec((2, LANES, dk), lambda i, h: (0, 0, h)),
                  pl.BlockSpec((2, 1, dk), lambda i, h: (0, 0, h))],
        out_specs=pl.BlockSpec((1, l, dv), tok),
        out_shape=jax.ShapeDtypeStruct((b, l, N_HEADS * dv), BF16),
        scratch_shapes=[pltpu.VMEM((l, dv), F32),
                        pltpu.VMEM((2, dk, dv), F32),
                        pltpu.VMEM((2, l, GLA_CHUNK), BF16),
                        pltpu.VMEM((2, l, dk), BF16),
                        pltpu.VMEM((2, n_chunks, dk, GLA_CHUNK), BF16),
                        pltpu.VMEM((2, n_chunks, 8, dk), F32),
                        pltpu.VMEM((2, GLA_CHUNK, GLA_CHUNK), BF16)],
        compiler_params=pltpu.CompilerParams(
            dimension_semantics=("parallel", "parallel"), vmem_limit_bytes=VMEM_LIMIT),
        name="gla_scan",
    )(q, kt, v, code, w2, b2)


def _out_core(x, o_ref, z_ref, hw_ref, ow_ref, mod_ref, center):
    dv = o_ref.shape[2] // N_HEADS
    proj = None
    for hh in range(N_HEADS):
        cols = slice(hh * dv, (hh + 1) * dv)
        yh = o_ref[0, :, cols].astype(F32)
        if center:
            yh = yh - jnp.mean(yh, axis=-1, keepdims=True)
        yn = yh * lax.rsqrt(jnp.mean(yh * yh, axis=-1, keepdims=True) + NORM_EPS) * hw_ref[:, cols]
        a = (yn * _silu(z_ref[0, :, cols].astype(F32))).astype(BF16)
        term = _dot(a, ow_ref[cols, :])
        proj = term if proj is None else proj + term
    return x + mod_ref[0, 0][2:3] * proj


def _out_final_body(x, o_ref, z_ref, hw_ref, ow_ref, mod_ref, fw_ref, out_ref, *, center):
    hn = _out_core(x, o_ref, z_ref, hw_ref, ow_ref, mod_ref, center)
    out_ref[0] = hn * lax.rsqrt(jnp.mean(hn * hn, axis=-1, keepdims=True) + NORM_EPS) * fw_ref[...]


def _out_proj_body(x, o_ref, z_ref, hw_ref, ow_ref, mod_ref, *rest, center, proj_body, n_proj_in):
    proj_in, h_out_ref, proj_out = rest[:n_proj_in], rest[n_proj_in], rest[n_proj_in + 1:]
    hn = _out_core(x, o_ref, z_ref, hw_ref, ow_ref, mod_ref, center)
    h_out_ref[0] = hn
    proj_body(hn, *proj_in, *proj_out)


def _out_operands(o, z, head_norm_w, out_w, modsel, skip):
    _, _, di = o.shape
    d = out_w.shape[1]
    tok = lambda i, t: (i, t + skip, 0)
    specs = [pl.BlockSpec((1, TOKEN_TILE, di), tok),
             pl.BlockSpec((1, TOKEN_TILE, di), tok),
             pl.BlockSpec((1, di), lambda i, t: (0, 0)),
             pl.BlockSpec((di, d), lambda i, t: (0, 0), pipeline_mode=pl.Buffered(1)),
             pl.BlockSpec((1, 1, 3, d), lambda i, t: (i, jnp.minimum(t + skip, 1), 0, 0))]
    return [o, z, head_norm_w.reshape(1, di), out_w, modsel], specs


def _out_final_call(o, z, head_norm_w, out_w, h, modsel, final_w, n_ctx, center):
    b, l, _ = o.shape
    d = out_w.shape[1]
    skip = n_ctx // TOKEN_TILE
    nt = l // TOKEN_TILE - skip
    h_args, h_specs = _hidden_operands(h, n_ctx, skip)
    args, specs = _out_operands(o, z, head_norm_w, out_w, modsel, skip)
    return pl.pallas_call(
        _with_hidden_tile(functools.partial(_out_final_body, center=center),
                          len(h_args), n_ctx // TOKEN_TILE, skip),
        grid=(b, nt),
        in_specs=h_specs + specs + [pl.BlockSpec((1, d), lambda i, t: (0, 0))],
        out_specs=pl.BlockSpec((1, TOKEN_TILE, d), lambda i, t: (i, t, 0)),
        out_shape=jax.ShapeDtypeStruct((b, nt * TOKEN_TILE, d), F32),
        compiler_params=pltpu.CompilerParams(
            dimension_semantics=("parallel", "parallel"), vmem_limit_bytes=VMEM_LIMIT),
        name="out_final",
    )(*h_args, *args, final_w.reshape(1, d))


def _out_proj_call(o, z, head_norm_w, out_w, h, modsel, n_ctx, center, spec, next_modsel, next_norm_w):
    b, l, _ = o.shape
    d = out_w.shape[1]
    h_args, h_specs = _hidden_operands(h, n_ctx)
    args, specs = _out_operands(o, z, head_norm_w, out_w, modsel, 0)
    p_in, p_out_specs, p_out_shape = _proj_specs(spec, b, l, d)
    res = pl.pallas_call(
        _with_hidden_tile(functools.partial(_out_proj_body, center=center, proj_body=spec["body"],
                                            n_proj_in=len(p_in)),
                          len(h_args), n_ctx // TOKEN_TILE),
        grid=(b, l // TOKEN_TILE),
        in_specs=h_specs + specs + p_in,
        out_specs=[pl.BlockSpec((1, TOKEN_TILE, d), lambda i, t: (i, t, 0))] + p_out_specs,
        out_shape=[jax.ShapeDtypeStruct((b, l, d), F32)] + p_out_shape,
        compiler_params=pltpu.CompilerParams(
            dimension_semantics=("parallel", "parallel"), vmem_limit_bytes=VMEM_LIMIT),
        name="out_proj_" + spec["name"],
    )(*h_args, *args, next_modsel, next_norm_w.reshape(1, d), *spec["consts"], *spec["tables"])
    return res[0], res[1:]


def _pad_cols(w, n):
    return jnp.pad(w, ((0, 0), (0, n - w.shape[1])))


def _mlstm_proj_spec(l, n_ctx, in_w, conv_w, conv_b, gate_b):
    d = in_w.shape[0]
    di = 2 * d
    qk_w = 2 * d
    wqk = in_w[:, :qk_w].astype(BF16)
    wv = in_w[:, qk_w:qk_w + di].astype(BF16)
    wz = in_w[:, qk_w + di:qk_w + 2 * di].astype(BF16)
    wg = _pad_cols(in_w[:, qk_w + 2 * di:], LANES).astype(BF16)
    gb = _pad_cols(gate_b.reshape(1, -1), LANES)
    return dict(body=_proj_mlstm_body, consts=[wqk, wv, wz, wg, gb], tables=[], name="mlstm",
                outs=[(qk_w, F32, None), (di, BF16, None), (di, BF16, None), (LANES, F32, None)])


def _mlstm_mix(outs, n_ctx, in_w, conv_w, conv_b, gate_b):
    qk_raw, v, z, g = outs
    b, l, qk_w = qk_raw.shape
    head_qk = qk_w // (2 * N_HEADS)
    q = _conv(qk_raw, conv_w, conv_b, n_ctx, 0, qk_w // 2, 1.0, False)
    kt = _conv(qk_raw, conv_w, conv_b, n_ctx, qk_w // 2, qk_w // 2, head_qk ** -0.5, True)
    n_chunks = l // SCAN_CHUNK
    g_rows = g[:, :, :4 * N_HEADS].reshape(b, n_chunks, SCAN_CHUNK, 4, N_HEADS).transpose(0, 4, 1, 3, 2)
    g_rows = jnp.pad(g_rows, ((0, 0), (0, 0), (0, 0), (0, 4), (0, 0)))
    return _mlstm_scan(q, kt, v, g_rows, n_ctx), z


def _gla_proj_spec(l, n_ctx, in_w, gk_w2, gk_b):
    d = in_w.shape[0]
    di = 2 * d
    key = gk_w2.shape[2]
    assert key // N_HEADS == LANES
    wq = in_w[:, :key].astype(BF16)
    wk = in_w[:, key:2 * key].astype(BF16)
    wv = in_w[:, 2 * key:2 * key + di].astype(BF16)
    wz = in_w[:, 2 * key + di:2 * key + 2 * di].astype(BF16)
    wlr = _pad_cols(in_w[:, 2 * key + 2 * di:], LANES).astype(BF16)
    return dict(body=functools.partial(_proj_gla_body, q_scale=(key // N_HEADS) ** -0.5),
                consts=[wq, wk, wv, wz, wlr], tables=[], name="gla",
                outs=[(key, BF16, None), (key, BF16, GLA_CHUNK), (di, BF16, None), (di, BF16, None),
                      (LANES, BF16, None)])


def _gla_mix(outs, n_ctx, in_w, gk_w2, gk_b):
    q, kt, v, z, code = outs
    key = gk_w2.shape[2]
    w2 = jnp.zeros((2, LANES, key), F32)
    w2 = w2.at[0, :GLA_RANK].set(gk_w2[0]).at[1, GLA_RANK:2 * GLA_RANK].set(gk_w2[1]).astype(BF16)
    return _gla_scan(q, kt, v, code, w2, gk_b.reshape(2, 1, key), n_ctx), z


def _rope_tables(n_ctx, n_lat, head_k):
    quarter = head_k // 4
    pos = jnp.arange(n_lat, dtype=jnp.int32)
    inv_freq = ROPE_BASE ** (-jnp.arange(quarter, dtype=F32) / quarter)
    cos_parts, sin_parts = [], []
    for p in ((pos // GRID_W).astype(F32), (pos % GRID_W).astype(F32)):
        ang = p[:, None] * inv_freq[None, :]
        cos_parts += [jnp.cos(ang), jnp.cos(ang)]
        sin_parts += [-jnp.sin(ang), jnp.sin(ang)]
    cos = jnp.concatenate(cos_parts, axis=1)
    sin = jnp.concatenate(sin_parts, axis=1)
    cos = jnp.concatenate([jnp.ones((n_ctx, head_k), F32), cos], axis=0)
    sin = jnp.concatenate([jnp.zeros((n_ctx, head_k), F32), sin], axis=0)
    return cos, sin


def _ret_proj_spec(l, n_ctx, in_w, decay_logit):
    d = in_w.shape[0]
    di = 2 * d
    key = (in_w.shape[1] - 2 * di) // 2
    head_k = key // N_HEADS
    wq = in_w[:, :key].astype(BF16)
    wk = in_w[:, key:2 * key].astype(BF16)
    wv = in_w[:, 2 * key:2 * key + di].astype(BF16)
    wz = in_w[:, 2 * key + di:].astype(BF16)
    cos, sin = _rope_tables(n_ctx, l - n_ctx, head_k)
    return dict(body=functools.partial(_proj_ret_body, k_scale=head_k ** -0.5),
                consts=[wq, wk, wv, wz], tables=[cos, sin], name="ret",
                outs=[(key, BF16, None), (key, BF16, SCAN_CHUNK), (di, BF16, None), (di, BF16, None)])


def _ret_mix(outs, n_ctx, in_w, decay_logit):
    q, kt, v, z = outs
    dl = jnp.broadcast_to(decay_logit.astype(F32).T[:, :, None, None], (N_HEADS, 2, 8, LANES))
    return _ret_scan(q, kt, v, dl, n_ctx), z


_PROJ_SPEC = {"mlstm": _mlstm_proj_spec, "gla": _gla_proj_spec, "retention": _ret_proj_spec}
_MIX = {"mlstm": _mlstm_mix, "gla": _gla_mix, "retention": _ret_mix}


def kernel(x, c, ctx, c_ctx, l0_norm_w, l0_ada_w, l0_ada_b, l0_in_w, l0_conv_w, l0_conv_b, l0_gate_b, l0_head_norm_w, l0_out_w, l1_norm_w, l1_ada_w, l1_ada_b, l1_in_w, l1_gk_w2, l1_gk_b, l1_head_norm_w, l1_out_w, l2_norm_w, l2_ada_w, l2_ada_b, l2_in_w, l2_decay_logit, l2_head_norm_w, l2_out_w, l3_norm_w, l3_ada_w, l3_ada_b, l3_in_w, l3_conv_w, l3_conv_b, l3_gate_b, l3_head_norm_w, l3_out_w, final_norm_w):
    layers = (
        ("mlstm", l0_norm_w, l0_ada_w, l0_ada_b, l0_out_w, l0_head_norm_w, (l0_in_w, l0_conv_w, l0_conv_b, l0_gate_b)),
        ("gla", l1_norm_w, l1_ada_w, l1_ada_b, l1_out_w, l1_head_norm_w, (l1_in_w, l1_gk_w2, l1_gk_b)),
        ("retention", l2_norm_w, l2_ada_w, l2_ada_b, l2_out_w, l2_head_norm_w, (l2_in_w, l2_decay_logit)),
        ("mlstm", l3_norm_w, l3_ada_w, l3_ada_b, l3_out_w, l3_head_norm_w, (l3_in_w, l3_conv_w, l3_conv_b, l3_gate_b)),
    )
    b, _, d = x.shape
    n_ctx = ctx.shape[1]
    assert n_ctx % TOKEN_TILE == 0 and n_ctx % SCAN_CHUNK == 0 and x.shape[1] % SCAN_CHUNK == 0
    assert TOKEN_TILE == SCAN_CHUNK and TOKEN_TILE % GLA_CHUNK == 0
    l = n_ctx + x.shape[1]
    h = (ctx, x)
    cc = jnp.concatenate([c, c_ctx[None, :], jnp.zeros((7, d), F32)], axis=0)
    modsels = []
    for _, _, ada_w, ada_b, _, _, _ in layers:
        mod = _ada(cc, ada_w, ada_b)
        mod_lat = mod[:b].reshape(b, 1, 3, d)
        mod_ctx = jnp.broadcast_to(mod[b].reshape(1, 1, 3, d), (b, 1, 3, d))
        modsels.append(jnp.concatenate([mod_ctx, mod_lat], axis=1))
    specs = [_PROJ_SPEC[kind](l, n_ctx, *params) for kind, _, _, _, _, _, params in layers]
    proj_outs = _proj_call(specs[0], h, n_ctx, modsels[0], layers[0][1])
    for li, (kind, _, _, _, out_w, head_norm_w, params) in enumerate(layers):
        o, z = _MIX[kind](proj_outs, n_ctx, *params)
        center = kind != "gla"
        if li == len(layers) - 1:
            return _out_final_call(o, z, head_norm_w, out_w.astype(BF16), h, modsels[li], final_norm_w,
                                   n_ctx, center)
        h, proj_outs = _out_proj_call(o, z, head_norm_w, out_w.astype(BF16), h, modsels[li], n_ctx, center,
                                      specs[li + 1], modsels[li + 1], layers[li + 1][1])
```

```python
import functools

import jax
import jax.numpy as jnp
from jax import lax
from jax.experimental import pallas as pl
from jax.experimental.pallas import tpu as pltpu

N_HEADS = 4
GRID_W = 64
GLA_RANK = 16
GLA_TAU = 16.0
ROPE_BASE = 10000.0
NORM_EPS = 1e-6
NEG_INF = -1e30

TOKEN_TILE = 256
SCAN_CHUNK = 256
GLA_CHUNK = 128
GLA_SUB = 16
GLA_EXP_CLAMP = 80.0
GLA_INTRA_CHUNKS = 8
LANES = 128
VMEM_LIMIT = 56 * 1024 * 1024

F32 = jnp.float32
BF16 = jnp.bfloat16


def _silu(x):
    half = 0.5 * x
    return half + half * jnp.tanh(half)


def _log_sigmoid(x):
    return jnp.minimum(x, 0.0) - jnp.log(1.0 + jnp.exp(-jnp.abs(x)))


def _dot(a, b):
    return jnp.dot(a, b, preferred_element_type=F32)


def _split3(a):
    hi = a.astype(BF16)
    r1 = a - hi.astype(F32)
    mid = r1.astype(BF16)
    lo = (r1 - mid.astype(F32)).astype(BF16)
    return hi, mid, lo


def _dot01_left(m01, a):
    hi, mid, lo = _split3(a)
    return _dot(m01, lo) + _dot(m01, mid) + _dot(m01, hi)


def _dot01_right(a, m01):
    hi, mid, lo = _split3(a)
    return _dot(lo, m01) + _dot(mid, m01) + _dot(hi, m01)


def _tri_masks(n):
    row = lax.broadcasted_iota(jnp.int32, (n, n), 0)
    col = lax.broadcasted_iota(jnp.int32, (n, n), 1)
    return col <= row, col >= row


def _bw_chunk(step, n_ctx_chunks, n_chunks):
    return jnp.where(step < n_ctx_chunks, n_ctx_chunks - 1 - step, n_chunks - 1 - (step - n_ctx_chunks))


def _first_touch_groups(n_ctx_chunks, n_chunks):
    def bw_idx(s):
        return n_ctx_chunks - 1 - s if s < n_ctx_chunks else n_chunks - 1 - (s - n_ctx_chunks)
    bw_step = {bw_idx(s): s for s in range(n_chunks)}
    groups = []
    for s in range(n_chunks):
        flags = (s <= bw_step[s], s < bw_idx(s))
        if groups and groups[-1][2] == flags:
            groups[-1][1] = s + 1
        else:
            groups.append([s, s + 1, flags])
    return [tuple(g) for g in groups]


def _scan_steps(step, n_ctx_chunks, n_chunks, unroll):
    for start, stop, first in _first_touch_groups(n_ctx_chunks, n_chunks):
        lax.fori_loop(start, stop, lambda s, carry, first=first: step(s, first) or carry, 0,
                      unroll=unroll if (stop - start) % unroll == 0 else 1)


def _emit(o_ref, acc_ref, start, n, value, first):
    if first:
        acc_ref[pl.ds(start, n), :] = value
    else:
        o_ref[0, pl.ds(start, n), :] = (acc_ref[pl.ds(start, n), :] + value).astype(o_ref.dtype)


def _ada_body(c_ref, w_ref, b_ref, o_ref):
    s = _silu(c_ref[...])
    o_ref[...] = jnp.dot(s, w_ref[...], preferred_element_type=F32,
                         precision=lax.Precision.HIGHEST) + b_ref[...]


def _ada(cc, w, b):
    rows, d = cc.shape
    n = w.shape[1]
    tn = 512
    return pl.pallas_call(
        _ada_body,
        grid=(n // tn,),
        in_specs=[pl.BlockSpec((rows, d), lambda j: (0, 0)),
                  pl.BlockSpec((d, tn), lambda j: (0, j)),
                  pl.BlockSpec((1, tn), lambda j: (0, j))],
        out_specs=pl.BlockSpec((rows, tn), lambda j: (0, j)),
        out_shape=jax.ShapeDtypeStruct((rows, n), F32),
        name="ada",
    )(cc, w, b.reshape(1, n))


def _hidden_operands(h, n_ctx, tile_offset=0):
    if not isinstance(h, tuple):
        d = h.shape[2]
        return [h], [pl.BlockSpec((1, TOKEN_TILE, d), lambda i, t: (i, t + tile_offset, 0))]
    ctx, lat = h
    d = lat.shape[2]
    nct = n_ctx // TOKEN_TILE
    return [ctx, lat], [
        pl.BlockSpec((1, TOKEN_TILE, d), lambda i, t: (i, jnp.minimum(t + tile_offset, nct - 1), 0)),
        pl.BlockSpec((1, TOKEN_TILE, d), lambda i, t: (i, jnp.maximum(t + tile_offset - nct, 0), 0))]


def _with_hidden_tile(body, n_hidden, n_ctx_tiles, tile_offset=0):
    def kernel_fn(*refs):
        if n_hidden == 1:
            x = refs[0][0]
        else:
            is_ctx = pl.program_id(1) + tile_offset < n_ctx_tiles
            x = jnp.where(is_ctx, refs[0][0], refs[1][0])
        body(x, *refs[n_hidden:])
    return kernel_fn


def _modulated(x, mod_ref, nw_ref):
    y = x * lax.rsqrt(jnp.mean(x * x, axis=-1, keepdims=True) + NORM_EPS) * nw_ref[...]
    m = mod_ref[0, 0]
    return (y * (1.0 + m[1:2]) + m[0:1]).astype(BF16)


def _store_chunked_t(ref, x, chunk):
    xt = x.T
    for j in range(x.shape[0] // chunk):
        ref[0, j] = xt[:, j * chunk:(j + 1) * chunk].astype(ref.dtype)


def _proj_mlstm_body(x, mod_ref, nw_ref, w_ref, wg_ref, gb_ref, qk_ref, v_ref, z_ref, g_ref):
    u = _modulated(x, mod_ref, nw_ref)
    qk_w, di = qk_ref.shape[2], v_ref.shape[2]
    qk_ref[0] = _dot(u, w_ref[:, :qk_w])
    v_ref[0] = _dot(u, w_ref[:, qk_w:qk_w + di]).astype(BF16)
    z_ref[0] = _dot(u, w_ref[:, qk_w + di:qk_w + 2 * di]).astype(BF16)
    g = _dot(u, wg_ref[...]) + gb_ref[...]
    lane = lax.broadcasted_iota(jnp.int32, g.shape, 1)
    is_forget = (lane // N_HEADS) % 2 == 1
    g_ref[0] = jnp.where(is_forget, _log_sigmoid(g), g)


def _proj_gla_body(x, mod_ref, nw_ref, w_ref, wlr_ref, q_ref, kt_ref, v_ref, z_ref, code_ref, *, q_scale):
    u = _modulated(x, mod_ref, nw_ref)
    key, di = q_ref.shape[2], v_ref.shape[2]
    q_ref[0] = (_dot(u, w_ref[:, :key]) * q_scale).astype(BF16)
    _store_chunked_t(kt_ref, _dot(u, w_ref[:, key:2 * key]), GLA_CHUNK)
    v_ref[0] = _dot(u, w_ref[:, 2 * key:2 * key + di]).astype(BF16)
    z_ref[0] = _dot(u, w_ref[:, 2 * key + di:2 * key + 2 * di]).astype(BF16)
    code_ref[0] = _dot(u, wlr_ref[...]).astype(BF16)


def _rope(t, cos, sin):
    outs = []
    for g in range(t.shape[1] // LANES):
        tg = t[:, g * LANES:(g + 1) * LANES]
        cg = cos[:, (g % 2) * LANES:(g % 2 + 1) * LANES]
        sg = sin[:, (g % 2) * LANES:(g % 2 + 1) * LANES]
        outs.append(tg * cg + pltpu.roll(tg, LANES // 2, 1) * sg)
    return jnp.concatenate(outs, axis=1)


def _proj_ret_body(x, mod_ref, nw_ref, w_ref, cos_ref, sin_ref, q_ref, kt_ref, v_ref, z_ref, *, k_scale):
    u = _modulated(x, mod_ref, nw_ref)
    key, di = q_ref.shape[2], v_ref.shape[2]
    cos = cos_ref[...]
    sin = sin_ref[...]
    q_ref[0] = _rope(_dot(u, w_ref[:, :key]), cos, sin).astype(BF16)
    _store_chunked_t(kt_ref, _rope(_dot(u, w_ref[:, key:2 * key]), cos, sin) * k_scale, SCAN_CHUNK)
    v_ref[0] = _dot(u, w_ref[:, 2 * key:2 * key + di]).astype(BF16)
    z_ref[0] = _dot(u, w_ref[:, 2 * key + di:2 * key + 2 * di]).astype(BF16)


def _proj_specs(spec, b, l, d):
    in_specs = [pl.BlockSpec((1, 1, 3, d), lambda i, t: (i, jnp.minimum(t, 1), 0, 0)),
                pl.BlockSpec((1, d), lambda i, t: (0, 0))]
    for a in spec["consts"]:
        in_specs.append(pl.BlockSpec(a.shape, lambda i, t, nd=a.ndim: (0,) * nd, pipeline_mode=pl.Buffered(1)))
    for a in spec["tables"]:
        in_specs.append(pl.BlockSpec((TOKEN_TILE, a.shape[1]), lambda i, t: (t, 0)))
    out_specs, out_shape = [], []
    for n, dt, chunk in spec["outs"]:
        if chunk is None:
            out_specs.append(pl.BlockSpec((1, TOKEN_TILE, n), lambda i, t: (i, t, 0)))
            out_shape.append(jax.ShapeDtypeStruct((b, l, n), dt))
        else:
            out_specs.append(pl.BlockSpec((1, TOKEN_TILE // chunk, n, chunk), lambda i, t: (i, t, 0, 0)))
            out_shape.append(jax.ShapeDtypeStruct((b, l // chunk, n, chunk), dt))
    return in_specs, out_specs, out_shape


def _proj_call(spec, h, n_ctx, modsel, norm_w):
    h_args, h_specs = _hidden_operands(h, n_ctx)
    b, _, d = h_args[-1].shape
    l = sum(a.shape[1] for a in h_args)
    in_specs, out_specs, out_shape = _proj_specs(spec, b, l, d)
    return pl.pallas_call(
        _with_hidden_tile(spec["body"], len(h_args), n_ctx // TOKEN_TILE),
        grid=(b, l // TOKEN_TILE),
        in_specs=h_specs + in_specs,
        out_specs=out_specs,
        out_shape=out_shape,
        compiler_params=pltpu.CompilerParams(
            dimension_semantics=("parallel", "parallel"), vmem_limit_bytes=VMEM_LIMIT),
        name="proj_" + spec["name"],
    )(*h_args, modsel, norm_w.reshape(1, d), *spec["consts"], *spec["tables"])


def _conv_body(x_ref, w_ref, b_ref, o_ref, *, n_ctx, n_blocks, scale, transposed):
    blk = SCAN_CHUNK
    l = x_ref.shape[1]
    w = w_ref[...]
    bias = b_ref[...]
    rid = lax.broadcasted_iota(jnp.int32, (blk, 1), 0)

    def emit(chunk, start, acc):
        y = _silu(acc + bias) * scale
        if transposed:
            o_ref[0, chunk] = y.T.astype(BF16)
        else:
            o_ref[0, pl.ds(start, blk), :] = y.astype(BF16)

    def combine(a, bm, cc, first, last):
        return (jnp.where(first, 0.0, pltpu.roll(a, 1, 0)) + bm
                + jnp.where(last, 0.0, pltpu.roll(cc, blk - 1, 0)))

    xc = x_ref[0, 0:blk, :]
    emit(0, 0, combine(xc * w[3:4], xc * w[4:5], xc * w[5:6], rid == 0, rid == blk - 1))

    col = rid % GRID_W

    def lat_block(it, carry):
        base = pl.multiple_of(n_ctx + it * blk, blk)
        mid = x_ref[0, pl.ds(base, blk), :]
        up_edge = x_ref[0, pl.ds(pl.multiple_of(base - GRID_W, GRID_W), GRID_W), :]
        dn_start = pl.multiple_of(jnp.minimum(base + blk, l - GRID_W), GRID_W)
        dn_edge = x_ref[0, pl.ds(dn_start, GRID_W), :]
        up_edge = jnp.where(it > 0, up_edge, 0.0)
        dn_edge = jnp.where(it < n_blocks - 1, dn_edge, 0.0)
        up = jnp.concatenate([up_edge, mid[:blk - GRID_W]], axis=0)
        down = jnp.concatenate([mid[GRID_W:], dn_edge], axis=0)
        a = up * w[0:1] + mid * w[3:4] + down * w[6:7]
        bm = up * w[1:2] + mid * w[4:5] + down * w[7:8]
        cc = up * w[2:3] + mid * w[5:6] + down * w[8:9]
        emit(n_ctx // blk + it, base, combine(a, bm, cc, col == 0, col == GRID_W - 1))
        return carry

    lax.fori_loop(0, n_blocks, lat_block, 0)


def _conv(qk_raw, conv_w, conv_b, n_ctx, col_offset, width, scale, transposed):
    b, l, c = qk_raw.shape
    tc = 256
    blk = SCAN_CHUNK
    assert n_ctx == blk and (l - n_ctx) % blk == 0 and blk % GRID_W == 0
    off = col_offset // tc
    if transposed:
        out_spec = pl.BlockSpec((1, l // blk, tc, blk), lambda i, j: (i, 0, j, 0))
        out_shape = jax.ShapeDtypeStruct((b, l // blk, width, blk), BF16)
    else:
        out_spec = pl.BlockSpec((1, l, tc), lambda i, j: (i, 0, j))
        out_shape = jax.ShapeDtypeStruct((b, l, width), BF16)
    return pl.pallas_call(
        functools.partial(_conv_body, n_ctx=n_ctx, n_blocks=(l - n_ctx) // blk, scale=scale,
                          transposed=transposed),
        grid=(b, width // tc),
        in_specs=[pl.BlockSpec((1, l, tc), lambda i, j: (i, 0, j + off)),
                  pl.BlockSpec((9, tc), lambda i, j: (0, j + off)),
                  pl.BlockSpec((1, tc), lambda i, j: (0, j + off))],
        out_specs=out_spec,
        out_shape=out_shape,
        compiler_params=pltpu.CompilerParams(
            dimension_semantics=("parallel", "parallel"), vmem_limit_bytes=VMEM_LIMIT),
        name="mlstm_conv_kt" if transposed else "mlstm_conv_q",
    )(qk_raw, conv_w.reshape(9, c), conv_b.reshape(1, c))


def _prefix_max(x, lane, backward):
    ax = x.ndim - 1
    n = x.shape[ax]
    sh = 1
    while sh < n:
        if backward:
            x = jnp.maximum(x, jnp.where(lane < n - sh, pltpu.roll(x, n - sh, ax), NEG_INF))
        else:
            x = jnp.maximum(x, jnp.where(lane >= sh, pltpu.roll(x, sh, ax), NEG_INF))
        sh *= 2
    return x


def _mlstm_scan_body(q_ref, kt_ref, v_ref, g_ref, o_ref, acc_ref, st_ref, n_ref, row_ref,
                     *, n_ctx_chunks, n_chunks):
    c = SCAN_CHUNK
    dv = v_ref.shape[2]
    st_ref[...] = jnp.zeros(st_ref.shape, F32)
    n_ref[...] = jnp.zeros(n_ref.shape, F32)
    masks = _tri_masks(c)

    g3 = g_ref[0, 0]
    g2 = g3.reshape(n_chunks * 8, c)
    lane = lax.broadcasted_iota(jnp.int32, (1, 1, c), 2)
    token_rows, chunk_rows = [], []
    for dirn in range(2):
        cum = _dot01_right(g2, masks[1 - dirn].astype(BF16)).reshape(n_chunks, 8, c)
        i_r = g3[:, 2 * dirn:2 * dirn + 1]
        a_r = g3[:, 2 * dirn + 1:2 * dirn + 2]
        b_r = cum[:, 2 * dirn + 1:2 * dirn + 2]
        u = i_r - b_r
        b_end = jnp.sum(a_r, axis=2, keepdims=True)
        g_max = jnp.max(b_end + u, axis=2, keepdims=True)
        cm = _prefix_max(u, lane, bool(dirn))
        m = jnp.zeros((1, 1, 1), F32)
        m_prev_l, m_new_l = [None] * n_chunks, [None] * n_chunks
        for s in range(n_chunks):
            idx = s
            if dirn:
                idx = n_ctx_chunks - 1 - s if s < n_ctx_chunks else n_chunks - 1 - (s - n_ctx_chunks)
            m_prev_l[idx] = m
            m = jnp.maximum(b_end[idx:idx + 1] + m, g_max[idx:idx + 1])
            m_new_l[idx] = m
        m_prev = jnp.concatenate(m_prev_l, axis=0)
        m_new = jnp.concatenate(m_new_l, axis=0)
        mm = jnp.maximum(m_prev, cm)
        token_rows += [u, jnp.exp(b_end + u - m_new), mm, b_r + mm]
        chunk_rows += [jnp.broadcast_to(jnp.exp(b_end + m_prev - m_new), (n_chunks, 1, c)),
                       jnp.broadcast_to(m_prev, (n_chunks, 1, c))]
    row_ref[...] = jnp.concatenate(token_rows + chunk_rows + [jnp.zeros((n_chunks, 4, c), F32)], axis=1)

    def lane_bcast_col(row):
        return jnp.broadcast_to(row, (LANES, c)).T

    def step(s, first):
        dirs = (0, 1)
        idxs = (s, _bw_chunk(s, n_ctx_chunks, n_chunks))
        starts = [pl.multiple_of(idx * c, c) for idx in idxs]
        q = [q_ref[0, pl.ds(st, c), :] for st in starts]
        kt = [kt_ref[0, idx] for idx in idxs]
        v = [v_ref[0, pl.ds(st, c), :] for st in starts]
        rows = [row_ref[idx] for idx in idxs]
        state = [st_ref[d] for d in dirs]
        nrm = [n_ref[d] for d in dirs]
        qk = [_dot(q[d], kt[d]) for d in dirs]
        mm_b = [lane_bcast_col(rows[d][4 * d + 2:4 * d + 3]) for d in dirs]
        bmm_b = [lane_bcast_col(rows[d][4 * d + 3:4 * d + 4]) for d in dirs]
        m_prev = [rows[d][9 + 2 * d:10 + 2 * d, 0:1] for d in dirs]
        decay = [rows[d][8 + 2 * d:9 + 2 * d, 0:1] for d in dirs]
        sc = [qk[d] * jnp.exp(jnp.where(masks[d], rows[d][4 * d:4 * d + 1]
                                        - jnp.concatenate([mm_b[d]] * (c // LANES), axis=1), NEG_INF))
              for d in dirs]
        w_inter = [jnp.exp(m_prev[d] - mm_b[d]) for d in dirs]
        intra = [_dot(sc[d].astype(BF16), v[d]) for d in dirs]
        inter = [_dot(q[d], state[d].astype(BF16)) for d in dirs]
        qn = [_dot(q[d], nrm[d].astype(BF16)) for d in dirs]
        wkt = [kt[d].astype(F32) * rows[d][4 * d + 1:4 * d + 2] for d in dirs]
        upd = [_dot(wkt[d].astype(BF16), v[d]) for d in dirs]
        for d in dirs:
            num = intra[d] + inter[d] * jnp.concatenate([w_inter[d]] * (dv // LANES), axis=1)
            den = jnp.sum(sc[d], axis=1, keepdims=True) + qn[d] * w_inter[d]
            rcp = 1.0 / jnp.maximum(jnp.abs(den), jnp.exp(-bmm_b[d]))
            _emit(o_ref, acc_ref, starts[d], c, num * jnp.concatenate([rcp] * (dv // LANES), axis=1), first[d])
        for d in dirs:
            st_ref[d] = decay[d] * state[d] + upd[d]
            n_ref[d] = decay[d] * nrm[d] + jnp.sum(wkt[d], axis=1, keepdims=True)

    _scan_steps(step, n_ctx_chunks, n_chunks, 1)


def _mlstm_scan(q, kt, v, g_rows, n_ctx):
    b, l, _ = q.shape
    dk = q.shape[2] // N_HEADS
    dv = v.shape[2] // N_HEADS
    n_chunks = l // SCAN_CHUNK
    return pl.pallas_call(
        functools.partial(_mlstm_scan_body, n_ctx_chunks=n_ctx // SCAN_CHUNK, n_chunks=n_chunks),
        grid=(b, N_HEADS),
        in_specs=[pl.BlockSpec((1, l, dk), lambda i, h: (i, 0, h)),
                  pl.BlockSpec((1, n_chunks, dk, SCAN_CHUNK), lambda i, h: (i, 0, h, 0)),
                  pl.BlockSpec((1, l, dv), lambda i, h: (i, 0, h)),
                  pl.BlockSpec((1, 1, n_chunks, 8, SCAN_CHUNK), lambda i, h: (i, h, 0, 0, 0))],
        out_specs=pl.BlockSpec((1, l, dv), lambda i, h: (i, 0, h)),
        out_shape=jax.ShapeDtypeStruct((b, l, N_HEADS * dv), BF16),
        scratch_shapes=[pltpu.VMEM((l, dv), F32),
                        pltpu.VMEM((2, dk, dv), F32),
                        pltpu.VMEM((2, dk, LANES), F32),
                        pltpu.VMEM((n_chunks, 16, SCAN_CHUNK), F32)],
        compiler_params=pltpu.CompilerParams(
            dimension_semantics=("parallel", "parallel"), vmem_limit_bytes=VMEM_LIMIT),
        name="mlstm_scan",
    )(q, kt, v, g_rows)


def _ret_scan_body(q_ref, kt_ref, v_ref, dl_ref, o_ref, acc_ref, st_ref, dm_ref, *, n_ctx_chunks, n_chunks):
    c = SCAN_CHUNK
    st_ref[...] = jnp.zeros(st_ref.shape, F32)
    masks = _tri_masks(c)
    row = lax.broadcasted_iota(jnp.int32, (c, c), 0)
    col = lax.broadcasted_iota(jnp.int32, (c, c), 1)
    dist = jnp.abs(row - col).astype(F32)
    rid = lax.broadcasted_iota(jnp.int32, (c, 1), 0).astype(F32)
    lid = lax.broadcasted_iota(jnp.int32, (1, c), 1).astype(F32)
    log_gamma = []
    for dirn in range(2):
        lg = _log_sigmoid(dl_ref[0, dirn])[0:1, 0:1]
        log_gamma.append(lg)
        dm_ref[dirn] = jnp.where(masks[dirn], jnp.exp(lg * dist), 0.0)

    pos_col = (rid + 1.0, c - rid)
    pos_row = (lid + 1.0, c - lid)

    def step(s, first):
        dirs = (0, 1)
        idxs = (s, _bw_chunk(s, n_ctx_chunks, n_chunks))
        starts = [pl.multiple_of(idx * c, c) for idx in idxs]
        q = [q_ref[0, pl.ds(st, c), :] for st in starts]
        kt = [kt_ref[0, idx] for idx in idxs]
        v = [v_ref[0, pl.ds(st, c), :] for st in starts]
        state = [st_ref[d] for d in dirs]
        sc = [(_dot(q[d], kt[d]) * dm_ref[d]).astype(BF16) for d in dirs]
        intra = [_dot(sc[d], v[d]) for d in dirs]
        inter = [_dot(q[d], state[d].astype(BF16)) for d in dirs]
        kdt = [(kt[d].astype(F32) * jnp.exp(log_gamma[d] * (c - pos_row[d]))).astype(BF16) for d in dirs]
        upd = [_dot(kdt[d], v[d]) for d in dirs]
        for d in dirs:
            _emit(o_ref, acc_ref, starts[d], c, intra[d] + inter[d] * jnp.exp(log_gamma[d] * pos_col[d]), first[d])
        for d in dirs:
            st_ref[d] = jnp.exp(log_gamma[d] * c) * state[d] + upd[d]

    _scan_steps(step, n_ctx_chunks, n_chunks, 4)


def _ret_scan(q, kt, v, dl, n_ctx):
    b, l, _ = q.shape
    dk = q.shape[2] // N_HEADS
    dv = v.shape[2] // N_HEADS
    n_chunks = l // SCAN_CHUNK
    return pl.pallas_call(
        functools.partial(_ret_scan_body, n_ctx_chunks=n_ctx // SCAN_CHUNK, n_chunks=n_chunks),
        grid=(b, N_HEADS),
        in_specs=[pl.BlockSpec((1, l, dk), lambda i, h: (i, 0, h)),
                  pl.BlockSpec((1, n_chunks, dk, SCAN_CHUNK), lambda i, h: (i, 0, h, 0)),
                  pl.BlockSpec((1, l, dv), lambda i, h: (i, 0, h)),
                  pl.BlockSpec((1, 2, 8, LANES), lambda i, h: (h, 0, 0, 0))],
        out_specs=pl.BlockSpec((1, l, dv), lambda i, h: (i, 0, h)),
        out_shape=jax.ShapeDtypeStruct((b, l, N_HEADS * dv), BF16),
        scratch_shapes=[pltpu.VMEM((l, dv), F32),
                        pltpu.VMEM((2, dk, dv), F32),
                        pltpu.VMEM((2, SCAN_CHUNK, SCAN_CHUNK), F32)],
        compiler_params=pltpu.CompilerParams(
            dimension_semantics=("parallel", "parallel"), vmem_limit_bytes=VMEM_LIMIT),
        name="ret_scan",
    )(q, kt, v, dl)


def _gla_scan_body(q_ref, kt_ref, v_ref, code_ref, w2_ref, b2_ref, o_ref,
                   acc_ref, st_ref, s_ref, qi_ref, kdt_ref, e_ref, m01_ref, *, n_ctx_chunks, n_chunks):
    c = GLA_CHUNK
    t = GLA_SUB
    n_sub = c // t
    dk = q_ref.shape[2]
    dv = v_ref.shape[2]
    st_ref[...] = jnp.zeros(st_ref.shape, F32)
    masks = _tri_masks(c)
    for d in range(2):
        m01_ref[d] = masks[d].astype(BF16)
    lid = lax.broadcasted_iota(jnp.int32, (1, c), 1)

    def intra(it, carry, first_chunk, group):
        chains = [(first_chunk + it * group + j, dirn) for j in range(group) for dirn in range(2)]
        starts = [pl.multiple_of(idx * c, c) for idx, _ in chains]
        qf = [q_ref[0, pl.ds(st, c), :].astype(F32) for st in starts]
        ktf = [kt_ref[0, idx].astype(F32) for idx, _ in chains]
        code = [code_ref[0, pl.ds(st, c), :] for st in starts]
        a = [_log_sigmoid(_dot(code[k], w2_ref[d]) + b2_ref[d]) * (1.0 / GLA_TAU) for k, (_, d) in enumerate(chains)]
        parts = [_split3(x) for x in a]
        b = [None] * len(chains)
        for p in (2, 1, 0):
            for k, (_, d) in enumerate(chains):
                term = _dot(m01_ref[d], parts[k][p])
                b[k] = term if b[k] is None else b[k] + term
        qp, kp, kdt, e_end = [], [], [], []
        for k, (_, dirn) in enumerate(chains):
            refs = []
            for sb in range(n_sub):
                r0 = sb * t + (t - 1 if dirn else 0)
                refs.append(b[k][r0:r0 + 1] - a[k][r0:r0 + 1])
            own = jnp.concatenate([jnp.broadcast_to(r, (t, dk)) for r in refs], axis=0)
            b_rel = b[k] - own
            q_own = qf[k] * jnp.exp(b_rel)
            k_rel_t = (ktf[k] * jnp.exp(jnp.minimum(-(b_rel.T), GLA_EXP_CLAMP))).astype(BF16)
            r_end = 0 if dirn else c - 1
            b_end = b[k][r_end:r_end + 1]
            kdt.append((ktf[k] * jnp.exp((b_end - b[k]).T)).astype(BF16))
            e_end.append(jnp.broadcast_to(jnp.exp(b_end), (8, dk)))
            col_blocks = []
            for sb in range(n_sub):
                pieces = []
                for j in range(n_sub):
                    if (j <= sb) if dirn else (j >= sb):
                        qj = q_own[j * t:(j + 1) * t]
                        pieces.append(qj if j == sb else qj * jnp.exp(refs[j] - refs[sb]))
                    else:
                        pieces.append(jnp.zeros((t, dk), F32))
                col_blocks.append(jnp.concatenate(pieces, axis=0).astype(BF16))
            qp.append(col_blocks)
            kp.append(jnp.concatenate(
                [jnp.where(jnp.logical_and(lid >= sb * t, lid < (sb + 1) * t), k_rel_t, 0.0)
                 for sb in range(n_sub)], axis=0))
        s = [_dot(jnp.concatenate(qp[k], axis=1), kp[k]) for k in range(len(chains))]
        for k, (idx, dirn) in enumerate(chains):
            s_ref[dirn, pl.ds(starts[k], c), :] = jnp.where(masks[dirn], s[k], 0.0).astype(BF16)
            qi_ref[dirn, pl.ds(starts[k], c), :] = qp[k][n_sub - 1 if dirn else 0]
            kdt_ref[dirn, idx] = kdt[k]
            e_ref[dirn, idx] = e_end[k]
        return carry

    n_full = n_chunks // GLA_INTRA_CHUNKS
    lax.fori_loop(0, n_full, functools.partial(intra, first_chunk=0, group=GLA_INTRA_CHUNKS), 0)
    rest = n_chunks - n_full * GLA_INTRA_CHUNKS
    if rest:
        intra(0, 0, n_full * GLA_INTRA_CHUNKS, rest)

    def step(s, first):
        dirs = (0, 1)
        idxs = (s, _bw_chunk(s, n_ctx_chunks, n_chunks))
        starts = [pl.multiple_of(idx * c, c) for idx in idxs]
        v = [v_ref[0, pl.ds(st, c), :] for st in starts]
        state = [st_ref[d] for d in dirs]
        intra = [_dot(s_ref[d, pl.ds(starts[d], c), :], v[d]) for d in dirs]
        inter = [_dot(qi_ref[d, pl.ds(starts[d], c), :], state[d].astype(BF16)) for d in dirs]
        upd = [_dot(kdt_ref[d, idxs[d]], v[d]) for d in dirs]
        e_cols = [jnp.concatenate([e_ref[d, idxs[d]]] * (dk // 8), axis=0).T for d in dirs]
        for d in dirs:
            _emit(o_ref, acc_ref, starts[d], c, intra[d] + inter[d], first[d])
        for d in dirs:
            st_ref[d] = jnp.concatenate([e_cols[d]] * (dv // dk), axis=1) * state[d] + upd[d]

    _scan_steps(step, n_ctx_chunks, n_chunks, 4)


def _gla_scan(q, kt, v, code, w2, b2, n_ctx):
    b, l, _ = q.shape
    dk = q.shape[2] // N_HEADS
    dv = v.shape[2] // N_HEADS
    n_chunks = l // GLA_CHUNK
    tok = lambda i, h: (i, 0, h)
    return pl.pallas_call(
        functools.partial(_gla_scan_body, n_ctx_chunks=n_ctx // GLA_CHUNK, n_chunks=n_chunks),
        grid=(b, N_HEADS),
        in_specs=[pl.BlockSpec((1, l, dk), tok),
                  pl.BlockSpec((1, n_chunks, dk, GLA_CHUNK), lambda i, h: (i, 0, h, 0)),
                  pl.BlockSpec((1, l, dv), tok),
                  pl.BlockSpec((1, l, LANES), lambda i, h: (i, 0, 0)),
                  pl.BlockSpec((2, LANES, dk), lambda i, h: (0, 0, h)),
                  pl.BlockSpec((2, 1, dk), lambda i, h: (0, 0, h))],
        out_specs=pl.BlockSpec((1, l, dv), tok),
        out_shape=jax.ShapeDtypeStruct((b, l, N_HEADS * dv), BF16),
        scratch_shapes=[pltpu.VMEM((l, dv), F32),
                        pltpu.VMEM((2, dk, dv), F32),
                        pltpu.VMEM((2, l, GLA_CHUNK), BF16),
                        pltpu.VMEM((2, l, dk), BF16),
                        pltpu.VMEM((2, n_chunks, dk, GLA_CHUNK), BF16),
                        pltpu.VMEM((2, n_chunks, 8, dk), F32),
                        pltpu.VMEM((2, GLA_CHUNK, GLA_CHUNK), BF16)],
        compiler_params=pltpu.CompilerParams(
            dimension_semantics=("parallel", "parallel"), vmem_limit_bytes=VMEM_LIMIT),
        name="gla_scan",
    )(q, kt, v, code, w2, b2)


def _out_core(x, o_ref, z_ref, hw_ref, ow_ref, mod_ref, center):
    dv = o_ref.shape[2] // N_HEADS
    proj = None
    for hh in range(N_HEADS):
        cols = slice(hh * dv, (hh + 1) * dv)
        yh = o_ref[0, :, cols].astype(F32)
        if center:
            yh = yh - jnp.mean(yh, axis=-1, keepdims=True)
        yn = yh * lax.rsqrt(jnp.mean(yh * yh, axis=-1, keepdims=True) + NORM_EPS) * hw_ref[:, cols]
        a = (yn * _silu(z_ref[0, :, cols].astype(F32))).astype(BF16)
        term = _dot(a, ow_ref[cols, :])
        proj = term if proj is None else proj + term
    return x + mod_ref[0, 0][2:3] * proj


def _out_final_body(x, o_ref, z_ref, hw_ref, ow_ref, mod_ref, fw_ref, out_ref, *, center):
    hn = _out_core(x, o_ref, z_ref, hw_ref, ow_ref, mod_ref, center)
    out_ref[0] = hn * lax.rsqrt(jnp.mean(hn * hn, axis=-1, keepdims=True) + NORM_EPS) * fw_ref[...]


def _out_proj_body(x, o_ref, z_ref, hw_ref, ow_ref, mod_ref, *rest, center, proj_body, n_proj_in):
    proj_in, h_out_ref, proj_out = rest[:n_proj_in], rest[n_proj_in], rest[n_proj_in + 1:]
    hn = _out_core(x, o_ref, z_ref, hw_ref, ow_ref, mod_ref, center)
    h_out_ref[0] = hn
    proj_body(hn, *proj_in, *proj_out)


def _out_operands(o, z, head_norm_w, out_w, modsel, skip):
    _, _, di = o.shape
    d = out_w.shape[1]
    tok = lambda i, t: (i, t + skip, 0)
    specs = [pl.BlockSpec((1, TOKEN_TILE, di), tok),
             pl.BlockSpec((1, TOKEN_TILE, di), tok),
             pl.BlockSpec((1, di), lambda i, t: (0, 0)),
             pl.BlockSpec((di, d), lambda i, t: (0, 0), pipeline_mode=pl.Buffered(1)),
             pl.BlockSpec((1, 1, 3, d), lambda i, t: (i, jnp.minimum(t + skip, 1), 0, 0))]
    return [o, z, head_norm_w.reshape(1, di), out_w, modsel], specs


def _out_final_call(o, z, head_norm_w, out_w, h, modsel, final_w, n_ctx, center):
    b, l, _ = o.shape
    d = out_w.shape[1]
    skip = n_ctx // TOKEN_TILE
    nt = l // TOKEN_TILE - skip
    h_args, h_specs = _hidden_operands(h, n_ctx, skip)
    args, specs = _out_operands(o, z, head_norm_w, out_w, modsel, skip)
    return pl.pallas_call(
        _with_hidden_tile(functools.partial(_out_final_body, center=center),
                          len(h_args), n_ctx // TOKEN_TILE, skip),
        grid=(b, nt),
        in_specs=h_specs + specs + [pl.BlockSpec((1, d), lambda i, t: (0, 0))],
        out_specs=pl.BlockSpec((1, TOKEN_TILE, d), lambda i, t: (i, t, 0)),
        out_shape=jax.ShapeDtypeStruct((b, nt * TOKEN_TILE, d), F32),
        compiler_params=pltpu.CompilerParams(
            dimension_semantics=("parallel", "parallel"), vmem_limit_bytes=VMEM_LIMIT),
        name="out_final",
    )(*h_args, *args, final_w.reshape(1, d))


def _out_proj_call(o, z, head_norm_w, out_w, h, modsel, n_ctx, center, spec, next_modsel, next_norm_w):
    b, l, _ = o.shape
    d = out_w.shape[1]
    h_args, h_specs = _hidden_operands(h, n_ctx)
    args, specs = _out_operands(o, z, head_norm_w, out_w, modsel, 0)
    p_in, p_out_specs, p_out_shape = _proj_specs(spec, b, l, d)
    res = pl.pallas_call(
        _with_hidden_tile(functools.partial(_out_proj_body, center=center, proj_body=spec["body"],
                                            n_proj_in=len(p_in)),
                          len(h_args), n_ctx // TOKEN_TILE),
        grid=(b, l // TOKEN_TILE),
        in_specs=h_specs + specs + p_in,
        out_specs=[pl.BlockSpec((1, TOKEN_TILE, d), lambda i, t: (i, t, 0))] + p_out_specs,
        out_shape=[jax.ShapeDtypeStruct((b, l, d), F32)] + p_out_shape,
        compiler_params=pltpu.CompilerParams(
            dimension_semantics=("parallel", "parallel"), vmem_limit_bytes=VMEM_LIMIT),
        name="out_proj_" + spec["name"],
    )(*h_args, *args, next_modsel, next_norm_w.reshape(1, d), *spec["consts"], *spec["tables"])
    return res[0], res[1:]


def _pad_cols(w, n):
    return jnp.pad(w, ((0, 0), (0, n - w.shape[1])))


def _mlstm_proj_spec(l, n_ctx, in_w, conv_w, conv_b, gate_b):
    d = in_w.shape[0]
    di = 2 * d
    qk_w = 2 * d
    wg = _pad_cols(in_w[:, qk_w + 2 * di:], LANES).astype(BF16)
    gb = _pad_cols(gate_b.reshape(1, -1), LANES)
    return dict(body=_proj_mlstm_body, consts=[in_w.astype(BF16), wg, gb], tables=[], name="mlstm",
                outs=[(qk_w, F32, None), (di, BF16, None), (di, BF16, None), (LANES, F32, None)])


def _mlstm_mix(outs, n_ctx, in_w, conv_w, conv_b, gate_b):
    qk_raw, v, z, g = outs
    b, l, qk_w = qk_raw.shape
    head_qk = qk_w // (2 * N_HEADS)
    q = _conv(qk_raw, conv_w, conv_b, n_ctx, 0, qk_w // 2, 1.0, False)
    kt = _conv(qk_raw, conv_w, conv_b, n_ctx, qk_w // 2, qk_w // 2, head_qk ** -0.5, True)
    n_chunks = l // SCAN_CHUNK
    g_rows = g[:, :, :4 * N_HEADS].reshape(b, n_chunks, SCAN_CHUNK, 4, N_HEADS).transpose(0, 4, 1, 3, 2)
    g_rows = jnp.pad(g_rows, ((0, 0), (0, 0), (0, 0), (0, 4), (0, 0)))
    return _mlstm_scan(q, kt, v, g_rows, n_ctx), z


def _gla_proj_spec(l, n_ctx, in_w, gk_w2, gk_b):
    d = in_w.shape[0]
    di = 2 * d
    key = gk_w2.shape[2]
    assert key // N_HEADS == LANES
    wlr = _pad_cols(in_w[:, 2 * key + 2 * di:], LANES).astype(BF16)
    return dict(body=functools.partial(_proj_gla_body, q_scale=(key // N_HEADS) ** -0.5),
                consts=[in_w.astype(BF16), wlr], tables=[], name="gla",
                outs=[(key, BF16, None), (key, BF16, GLA_CHUNK), (di, BF16, None), (di, BF16, None),
                      (LANES, BF16, None)])


def _gla_mix(outs, n_ctx, in_w, gk_w2, gk_b):
    q, kt, v, z, code = outs
    key = gk_w2.shape[2]
    w2 = jnp.zeros((2, LANES, key), F32)
    w2 = w2.at[0, :GLA_RANK].set(gk_w2[0]).at[1, GLA_RANK:2 * GLA_RANK].set(gk_w2[1]).astype(BF16)
    return _gla_scan(q, kt, v, code, w2, gk_b.reshape(2, 1, key), n_ctx), z


def _rope_tables(n_ctx, n_lat, head_k):
    quarter = head_k // 4
    pos = jnp.arange(n_lat, dtype=jnp.int32)
    inv_freq = ROPE_BASE ** (-jnp.arange(quarter, dtype=F32) / quarter)
    cos_parts, sin_parts = [], []
    for p in ((pos // GRID_W).astype(F32), (pos % GRID_W).astype(F32)):
        ang = p[:, None] * inv_freq[None, :]
        cos_parts += [jnp.cos(ang), jnp.cos(ang)]
        sin_parts += [-jnp.sin(ang), jnp.sin(ang)]
    cos = jnp.concatenate(cos_parts, axis=1)
    sin = jnp.concatenate(sin_parts, axis=1)
    cos = jnp.concatenate([jnp.ones((n_ctx, head_k), F32), cos], axis=0)
    sin = jnp.concatenate([jnp.zeros((n_ctx, head_k), F32), sin], axis=0)
    return cos, sin


def _ret_proj_spec(l, n_ctx, in_w, decay_logit):
    d = in_w.shape[0]
    di = 2 * d
    key = (in_w.shape[1] - 2 * di) // 2
    head_k = key // N_HEADS
    cos, sin = _rope_tables(n_ctx, l - n_ctx, head_k)
    return dict(body=functools.partial(_proj_ret_body, k_scale=head_k ** -0.5),
                consts=[in_w.astype(BF16)], tables=[cos, sin], name="ret",
                outs=[(key, BF16, None), (key, BF16, SCAN_CHUNK), (di, BF16, None), (di, BF16, None)])


def _ret_mix(outs, n_ctx, in_w, decay_logit):
    q, kt, v, z = outs
    dl = jnp.broadcast_to(decay_logit.astype(F32).T[:, :, None, None], (N_HEADS, 2, 8, LANES))
    return _ret_scan(q, kt, v, dl, n_ctx), z


_PROJ_SPEC = {"mlstm": _mlstm_proj_spec, "gla": _gla_proj_spec, "retention": _ret_proj_spec}
_MIX = {"mlstm": _mlstm_mix, "gla": _gla_mix, "retention": _ret_mix}


def kernel(x, c, ctx, c_ctx, l0_norm_w, l0_ada_w, l0_ada_b, l0_in_w, l0_conv_w, l0_conv_b, l0_gate_b, l0_head_norm_w, l0_out_w, l1_norm_w, l1_ada_w, l1_ada_b, l1_in_w, l1_gk_w2, l1_gk_b, l1_head_norm_w, l1_out_w, l2_norm_w, l2_ada_w, l2_ada_b, l2_in_w, l2_decay_logit, l2_head_norm_w, l2_out_w, l3_norm_w, l3_ada_w, l3_ada_b, l3_in_w, l3_conv_w, l3_conv_b, l3_gate_b, l3_head_norm_w, l3_out_w, final_norm_w):
    layers = (
        ("mlstm", l0_norm_w, l0_ada_w, l0_ada_b, l0_out_w, l0_head_norm_w, (l0_in_w, l0_conv_w, l0_conv_b, l0_gate_b)),
        ("gla", l1_norm_w, l1_ada_w, l1_ada_b, l1_out_w, l1_head_norm_w, (l1_in_w, l1_gk_w2, l1_gk_b)),
        ("retention", l2_norm_w, l2_ada_w, l2_ada_b, l2_out_w, l2_head_norm_w, (l2_in_w, l2_decay_logit)),
        ("mlstm", l3_norm_w, l3_ada_w, l3_ada_b, l3_out_w, l3_head_norm_w, (l3_in_w, l3_conv_w, l3_conv_b, l3_gate_b)),
    )
    b, _, d = x.shape
    n_ctx = ctx.shape[1]
    assert n_ctx % TOKEN_TILE == 0 and n_ctx % SCAN_CHUNK == 0 and x.shape[1] % SCAN_CHUNK == 0
    assert TOKEN_TILE == SCAN_CHUNK and TOKEN_TILE % GLA_CHUNK == 0
    l = n_ctx + x.shape[1]
    h = (ctx, x)
    cc = jnp.concatenate([c, c_ctx[None, :], jnp.zeros((7, d), F32)], axis=0)
    modsels = []
    for _, _, ada_w, ada_b, _, _, _ in layers:
        mod = _ada(cc, ada_w, ada_b)
        mod_lat = mod[:b].reshape(b, 1, 3, d)
        mod_ctx = jnp.broadcast_to(mod[b].reshape(1, 1, 3, d), (b, 1, 3, d))
        modsels.append(jnp.concatenate([mod_ctx, mod_lat], axis=1))
    specs = [_PROJ_SPEC[kind](l, n_ctx, *params) for kind, _, _, _, _, _, params in layers]
    proj_outs = _proj_call(specs[0], h, n_ctx, modsels[0], layers[0][1])
    for li, (kind, _, _, _, out_w, head_norm_w, params) in enumerate(layers):
        o, z = _MIX[kind](proj_outs, n_ctx, *params)
        center = kind != "gla"
        if li == len(layers) - 1:
            return _out_final_call(o, z, head_norm_w, out_w.astype(BF16), h, modsels[li], final_norm_w,
                                   n_ctx, center)
        h, proj_outs = _out_proj_call(o, z, head_norm_w, out_w.astype(BF16), h, modsels[li], n_ctx, center,
                                      specs[li + 1], modsels[li + 1], layers[li + 1][1])
```

```python
import functools

import jax
import jax.numpy as jnp
from jax import lax
from jax.experimental import pallas as pl
from jax.experimental.pallas import tpu as pltpu

N_HEADS = 4
GRID_W = 64
GLA_RANK = 16
GLA_TAU = 16.0
ROPE_BASE = 10000.0
NORM_EPS = 1e-6
NEG_INF = -1e30

TOKEN_TILE = 256
SCAN_CHUNK = 256
GLA_CHUNK = 128
GLA_SUB = 16
GLA_EXP_CLAMP = 80.0
GLA_INTRA_CHUNKS = 8
LANES = 128
VMEM_LIMIT = 56 * 1024 * 1024

F32 = jnp.float32
BF16 = jnp.bfloat16


def _silu(x):
    half = 0.5 * x
    return half + half * jnp.tanh(half)


def _log_sigmoid(x):
    return jnp.minimum(x, 0.0) - jnp.log(1.0 + jnp.exp(-jnp.abs(x)))


def _dot(a, b):
    return jnp.dot(a, b, preferred_element_type=F32)


def _split3(a):
    hi = a.astype(BF16)
    r1 = a - hi.astype(F32)
    mid = r1.astype(BF16)
    lo = (r1 - mid.astype(F32)).astype(BF16)
    return hi, mid, lo


def _dot01_left(m01, a):
    hi, mid, lo = _split3(a)
    return _dot(m01, lo) + _dot(m01, mid) + _dot(m01, hi)


def _dot01_right(a, m01):
    hi, mid, lo = _split3(a)
    return _dot(lo, m01) + _dot(mid, m01) + _dot(hi, m01)


def _tri_masks(n):
    row = lax.broadcasted_iota(jnp.int32, (n, n), 0)
    col = lax.broadcasted_iota(jnp.int32, (n, n), 1)
    return col <= row, col >= row


_HEAD_SLAB = lambda i, h: (i, h, 0, 0)


def _bw_chunk(step, n_ctx_chunks, n_chunks):
    return jnp.where(step < n_ctx_chunks, n_ctx_chunks - 1 - step, n_chunks - 1 - (step - n_ctx_chunks))


def _first_touch_groups(n_ctx_chunks, n_chunks):
    def bw_idx(s):
        return n_ctx_chunks - 1 - s if s < n_ctx_chunks else n_chunks - 1 - (s - n_ctx_chunks)
    bw_step = {bw_idx(s): s for s in range(n_chunks)}
    groups = []
    for s in range(n_chunks):
        flags = (s <= bw_step[s], s < bw_idx(s))
        if groups and groups[-1][2] == flags:
            groups[-1][1] = s + 1
        else:
            groups.append([s, s + 1, flags])
    return [tuple(g) for g in groups]


def _scan_steps(step, n_ctx_chunks, n_chunks, unroll):
    for start, stop, first in _first_touch_groups(n_ctx_chunks, n_chunks):
        lax.fori_loop(start, stop, lambda s, carry, first=first: step(s, first) or carry, 0,
                      unroll=unroll if (stop - start) % unroll == 0 else 1)


def _emit(o_ref, acc_ref, start, n, value, first):
    if first:
        acc_ref[pl.ds(start, n), :] = value
    else:
        o_ref[0, pl.ds(start, n), :] = (acc_ref[pl.ds(start, n), :] + value).astype(o_ref.dtype)


def _ada_body(c_ref, w_ref, b_ref, o_ref):
    s = _silu(c_ref[...])
    o_ref[...] = jnp.dot(s, w_ref[...], preferred_element_type=F32,
                         precision=lax.Precision.HIGHEST) + b_ref[...]


def _ada(cc, w, b):
    rows, d = cc.shape
    n = w.shape[1]
    tn = 512
    return pl.pallas_call(
        _ada_body,
        grid=(n // tn,),
        in_specs=[pl.BlockSpec((rows, d), lambda j: (0, 0)),
                  pl.BlockSpec((d, tn), lambda j: (0, j)),
                  pl.BlockSpec((1, tn), lambda j: (0, j))],
        out_specs=pl.BlockSpec((rows, tn), lambda j: (0, j)),
        out_shape=jax.ShapeDtypeStruct((rows, n), F32),
        name="ada",
    )(cc, w, b.reshape(1, n))


def _hidden_operands(h, n_ctx, tile_offset=0):
    if not isinstance(h, tuple):
        d = h.shape[2]
        return [h], [pl.BlockSpec((1, TOKEN_TILE, d), lambda i, t: (i, t + tile_offset, 0))]
    ctx, lat = h
    d = lat.shape[2]
    nct = n_ctx // TOKEN_TILE
    return [ctx, lat], [
        pl.BlockSpec((1, TOKEN_TILE, d), lambda i, t: (i, jnp.minimum(t + tile_offset, nct - 1), 0)),
        pl.BlockSpec((1, TOKEN_TILE, d), lambda i, t: (i, jnp.maximum(t + tile_offset - nct, 0), 0))]


def _with_hidden_tile(body, n_hidden, n_ctx_tiles, tile_offset=0):
    def kernel_fn(*refs):
        if n_hidden == 1:
            x = refs[0][0]
        else:
            is_ctx = pl.program_id(1) + tile_offset < n_ctx_tiles
            x = jnp.where(is_ctx, refs[0][0], refs[1][0])
        body(x, *refs[n_hidden:])
    return kernel_fn


def _modulated(x, mod_ref, nw_ref):
    y = x * lax.rsqrt(jnp.mean(x * x, axis=-1, keepdims=True) + NORM_EPS) * nw_ref[...]
    m = mod_ref[0, 0]
    return (y * (1.0 + m[1:2]) + m[0:1]).astype(BF16)


def _store_heads(ref, x):
    w = ref.shape[3]
    for hh in range(ref.shape[1]):
        ref[0, hh] = x[:, hh * w:(hh + 1) * w].astype(ref.dtype)


def _store_chunked_t(ref, x, chunk):
    xt = x.T
    for j in range(x.shape[0] // chunk):
        ref[0, j] = xt[:, j * chunk:(j + 1) * chunk].astype(ref.dtype)


def _proj_mlstm_body(x, mod_ref, nw_ref, w_ref, wg_ref, gb_ref, qk_ref, v_ref, z_ref, g_ref):
    u = _modulated(x, mod_ref, nw_ref)
    qk_w, di = qk_ref.shape[2], z_ref.shape[2]
    qk_ref[0] = _dot(u, w_ref[:, :qk_w])
    _store_heads(v_ref, _dot(u, w_ref[:, qk_w:qk_w + di]))
    z_ref[0] = _dot(u, w_ref[:, qk_w + di:qk_w + 2 * di]).astype(BF16)
    g = _dot(u, wg_ref[...]) + gb_ref[...]
    lane = lax.broadcasted_iota(jnp.int32, g.shape, 1)
    is_forget = (lane // N_HEADS) % 2 == 1
    g_ref[0] = jnp.where(is_forget, _log_sigmoid(g), g)


def _proj_gla_body(x, mod_ref, nw_ref, w_ref, wlr_ref, q_ref, kt_ref, v_ref, z_ref, code_ref, *, q_scale):
    u = _modulated(x, mod_ref, nw_ref)
    key, di = kt_ref.shape[2], z_ref.shape[2]
    _store_heads(q_ref, _dot(u, w_ref[:, :key]) * q_scale)
    _store_chunked_t(kt_ref, _dot(u, w_ref[:, key:2 * key]), GLA_CHUNK)
    _store_heads(v_ref, _dot(u, w_ref[:, 2 * key:2 * key + di]))
    z_ref[0] = _dot(u, w_ref[:, 2 * key + di:2 * key + 2 * di]).astype(BF16)
    code_ref[0] = _dot(u, wlr_ref[...]).astype(BF16)


def _rope(t, cos, sin):
    outs = []
    for g in range(t.shape[1] // LANES):
        tg = t[:, g * LANES:(g + 1) * LANES]
        cg = cos[:, (g % 2) * LANES:(g % 2 + 1) * LANES]
        sg = sin[:, (g % 2) * LANES:(g % 2 + 1) * LANES]
        outs.append(tg * cg + pltpu.roll(tg, LANES // 2, 1) * sg)
    return jnp.concatenate(outs, axis=1)


def _proj_ret_body(x, mod_ref, nw_ref, w_ref, cos_ref, sin_ref, q_ref, kt_ref, v_ref, z_ref, *, k_scale):
    u = _modulated(x, mod_ref, nw_ref)
    key, di = kt_ref.shape[2], z_ref.shape[2]
    cos = cos_ref[...]
    sin = sin_ref[...]
    _store_heads(q_ref, _rope(_dot(u, w_ref[:, :key]), cos, sin))
    _store_chunked_t(kt_ref, _rope(_dot(u, w_ref[:, key:2 * key]), cos, sin) * k_scale, SCAN_CHUNK)
    _store_heads(v_ref, _dot(u, w_ref[:, 2 * key:2 * key + di]))
    z_ref[0] = _dot(u, w_ref[:, 2 * key + di:2 * key + 2 * di]).astype(BF16)


def _proj_specs(spec, b, l, d):
    in_specs = [pl.BlockSpec((1, 1, 3, d), lambda i, t: (i, jnp.minimum(t, 1), 0, 0)),
                pl.BlockSpec((1, d), lambda i, t: (0, 0))]
    for a in spec["consts"]:
        in_specs.append(pl.BlockSpec(a.shape, lambda i, t, nd=a.ndim: (0,) * nd, pipeline_mode=pl.Buffered(1)))
    for a in spec["tables"]:
        in_specs.append(pl.BlockSpec((TOKEN_TILE, a.shape[1]), lambda i, t: (t, 0)))
    out_specs, out_shape = [], []
    for n, dt, chunk in spec["outs"]:
        if chunk is None:
            out_specs.append(pl.BlockSpec((1, TOKEN_TILE, n), lambda i, t: (i, t, 0)))
            out_shape.append(jax.ShapeDtypeStruct((b, l, n), dt))
        elif chunk == "heads":
            out_specs.append(pl.BlockSpec((1, N_HEADS, TOKEN_TILE, n // N_HEADS), lambda i, t: (i, 0, t, 0)))
            out_shape.append(jax.ShapeDtypeStruct((b, N_HEADS, l, n // N_HEADS), dt))
        else:
            out_specs.append(pl.BlockSpec((1, TOKEN_TILE // chunk, n, chunk), lambda i, t: (i, t, 0, 0)))
            out_shape.append(jax.ShapeDtypeStruct((b, l // chunk, n, chunk), dt))
    return in_specs, out_specs, out_shape


def _proj_call(spec, h, n_ctx, modsel, norm_w):
    h_args, h_specs = _hidden_operands(h, n_ctx)
    b, _, d = h_args[-1].shape
    l = sum(a.shape[1] for a in h_args)
    in_specs, out_specs, out_shape = _proj_specs(spec, b, l, d)
    return pl.pallas_call(
        _with_hidden_tile(spec["body"], len(h_args), n_ctx // TOKEN_TILE),
        grid=(b, l // TOKEN_TILE),
        in_specs=h_specs + in_specs,
        out_specs=out_specs,
        out_shape=out_shape,
        compiler_params=pltpu.CompilerParams(
            dimension_semantics=("parallel", "parallel"), vmem_limit_bytes=VMEM_LIMIT),
        name="proj_" + spec["name"],
    )(*h_args, modsel, norm_w.reshape(1, d), *spec["consts"], *spec["tables"])


def _conv_body(x_ref, w_ref, b_ref, o_ref, *, n_ctx, n_blocks, scale, transposed):
    blk = SCAN_CHUNK
    l = x_ref.shape[1]
    w = w_ref[...]
    bias = b_ref[...]
    rid = lax.broadcasted_iota(jnp.int32, (blk, 1), 0)

    def emit(chunk, start, acc):
        y = _silu(acc + bias) * scale
        if transposed:
            o_ref[0, chunk] = y.T.astype(BF16)
        else:
            o_ref[0, pl.ds(start, blk), :] = y.astype(BF16)

    def combine(a, bm, cc, first, last):
        return (jnp.where(first, 0.0, pltpu.roll(a, 1, 0)) + bm
                + jnp.where(last, 0.0, pltpu.roll(cc, blk - 1, 0)))

    xc = x_ref[0, 0:blk, :]
    emit(0, 0, combine(xc * w[3:4], xc * w[4:5], xc * w[5:6], rid == 0, rid == blk - 1))

    col = rid % GRID_W

    def lat_block(it, carry):
        base = pl.multiple_of(n_ctx + it * blk, blk)
        mid = x_ref[0, pl.ds(base, blk), :]
        up_edge = x_ref[0, pl.ds(pl.multiple_of(base - GRID_W, GRID_W), GRID_W), :]
        dn_start = pl.multiple_of(jnp.minimum(base + blk, l - GRID_W), GRID_W)
        dn_edge = x_ref[0, pl.ds(dn_start, GRID_W), :]
        up_edge = jnp.where(it > 0, up_edge, 0.0)
        dn_edge = jnp.where(it < n_blocks - 1, dn_edge, 0.0)
        up = jnp.concatenate([up_edge, mid[:blk - GRID_W]], axis=0)
        down = jnp.concatenate([mid[GRID_W:], dn_edge], axis=0)
        a = up * w[0:1] + mid * w[3:4] + down * w[6:7]
        bm = up * w[1:2] + mid * w[4:5] + down * w[7:8]
        cc = up * w[2:3] + mid * w[5:6] + down * w[8:9]
        emit(n_ctx // blk + it, base, combine(a, bm, cc, col == 0, col == GRID_W - 1))
        return carry

    lax.fori_loop(0, n_blocks, lat_block, 0)


def _conv(qk_raw, conv_w, conv_b, n_ctx, col_offset, width, scale, transposed):
    b, l, c = qk_raw.shape
    tc = 256
    blk = SCAN_CHUNK
    assert n_ctx == blk and (l - n_ctx) % blk == 0 and blk % GRID_W == 0
    off = col_offset // tc
    if transposed:
        out_spec = pl.BlockSpec((1, l // blk, tc, blk), lambda i, j: (i, 0, j, 0))
        out_shape = jax.ShapeDtypeStruct((b, l // blk, width, blk), BF16)
    else:
        out_spec = pl.BlockSpec((1, None, l, tc), lambda i, j: (i, j, 0, 0))
        out_shape = jax.ShapeDtypeStruct((b, width // tc, l, tc), BF16)
    return pl.pallas_call(
        functools.partial(_conv_body, n_ctx=n_ctx, n_blocks=(l - n_ctx) // blk, scale=scale,
                          transposed=transposed),
        grid=(b, width // tc),
        in_specs=[pl.BlockSpec((1, l, tc), lambda i, j: (i, 0, j + off)),
                  pl.BlockSpec((9, tc), lambda i, j: (0, j + off)),
                  pl.BlockSpec((1, tc), lambda i, j: (0, j + off))],
        out_specs=out_spec,
        out_shape=out_shape,
        compiler_params=pltpu.CompilerParams(
            dimension_semantics=("parallel", "parallel"), vmem_limit_bytes=VMEM_LIMIT),
        name="mlstm_conv_kt" if transposed else "mlstm_conv_q",
    )(qk_raw, conv_w.reshape(9, c), conv_b.reshape(1, c))


def _prefix_max(x, lane, backward):
    ax = x.ndim - 1
    n = x.shape[ax]
    sh = 1
    while sh < n:
        if backward:
            x = jnp.maximum(x, jnp.where(lane < n - sh, pltpu.roll(x, n - sh, ax), NEG_INF))
        else:
            x = jnp.maximum(x, jnp.where(lane >= sh, pltpu.roll(x, sh, ax), NEG_INF))
        sh *= 2
    return x


def _mlstm_scan_body(q_ref, kt_ref, v_ref, g_ref, o_ref, acc_ref, st_ref, n_ref, row_ref,
                     *, n_ctx_chunks, n_chunks):
    c = SCAN_CHUNK
    dv = v_ref.shape[2]
    st_ref[...] = jnp.zeros(st_ref.shape, F32)
    n_ref[...] = jnp.zeros(n_ref.shape, F32)
    masks = _tri_masks(c)

    g3 = g_ref[0, 0]
    g2 = g3.reshape(n_chunks * 8, c)
    lane = lax.broadcasted_iota(jnp.int32, (1, 1, c), 2)
    token_rows, chunk_rows = [], []
    for dirn in range(2):
        cum = _dot01_right(g2, masks[1 - dirn].astype(BF16)).reshape(n_chunks, 8, c)
        i_r = g3[:, 2 * dirn:2 * dirn + 1]
        a_r = g3[:, 2 * dirn + 1:2 * dirn + 2]
        b_r = cum[:, 2 * dirn + 1:2 * dirn + 2]
        u = i_r - b_r
        b_end = jnp.sum(a_r, axis=2, keepdims=True)
        g_max = jnp.max(b_end + u, axis=2, keepdims=True)
        cm = _prefix_max(u, lane, bool(dirn))
        m = jnp.zeros((1, 1, 1), F32)
        m_prev_l, m_new_l = [None] * n_chunks, [None] * n_chunks
        for s in range(n_chunks):
            idx = s
            if dirn:
                idx = n_ctx_chunks - 1 - s if s < n_ctx_chunks else n_chunks - 1 - (s - n_ctx_chunks)
            m_prev_l[idx] = m
            m = jnp.maximum(b_end[idx:idx + 1] + m, g_max[idx:idx + 1])
            m_new_l[idx] = m
        m_prev = jnp.concatenate(m_prev_l, axis=0)
        m_new = jnp.concatenate(m_new_l, axis=0)
        mm = jnp.maximum(m_prev, cm)
        token_rows += [u, jnp.exp(b_end + u - m_new), mm, b_r + mm]
        chunk_rows += [jnp.broadcast_to(jnp.exp(b_end + m_prev - m_new), (n_chunks, 1, c)),
                       jnp.broadcast_to(m_prev, (n_chunks, 1, c))]
    row_ref[...] = jnp.concatenate(token_rows + chunk_rows + [jnp.zeros((n_chunks, 4, c), F32)], axis=1)

    def lane_bcast_col(row):
        return jnp.broadcast_to(row, (LANES, c)).T

    def step(s, first):
        dirs = (0, 1)
        idxs = (s, _bw_chunk(s, n_ctx_chunks, n_chunks))
        starts = [pl.multiple_of(idx * c, c) for idx in idxs]
        q = [q_ref[0, pl.ds(st, c), :] for st in starts]
        kt = [kt_ref[0, idx] for idx in idxs]
        v = [v_ref[0, pl.ds(st, c), :] for st in starts]
        rows = [row_ref[idx] for idx in idxs]
        state = [st_ref[d] for d in dirs]
        nrm = [n_ref[d] for d in dirs]
        qk = [_dot(q[d], kt[d]) for d in dirs]
        mm_b = [lane_bcast_col(rows[d][4 * d + 2:4 * d + 3]) for d in dirs]
        bmm_b = [lane_bcast_col(rows[d][4 * d + 3:4 * d + 4]) for d in dirs]
        m_prev = [rows[d][9 + 2 * d:10 + 2 * d, 0:1] for d in dirs]
        decay = [rows[d][8 + 2 * d:9 + 2 * d, 0:1] for d in dirs]
        sc = [qk[d] * jnp.exp(jnp.where(masks[d], rows[d][4 * d:4 * d + 1]
                                        - jnp.concatenate([mm_b[d]] * (c // LANES), axis=1), NEG_INF))
              for d in dirs]
        w_inter = [jnp.exp(m_prev[d] - mm_b[d]) for d in dirs]
        intra = [_dot(sc[d].astype(BF16), v[d]) for d in dirs]
        inter = [_dot(q[d], state[d].astype(BF16)) for d in dirs]
        qn = [_dot(q[d], nrm[d].astype(BF16)) for d in dirs]
        wkt = [kt[d].astype(F32) * rows[d][4 * d + 1:4 * d + 2] for d in dirs]
        upd = [_dot(wkt[d].astype(BF16), v[d]) for d in dirs]
        for d in dirs:
            num = intra[d] + inter[d] * jnp.concatenate([w_inter[d]] * (dv // LANES), axis=1)
            den = jnp.sum(sc[d], axis=1, keepdims=True) + qn[d] * w_inter[d]
            rcp = 1.0 / jnp.maximum(jnp.abs(den), jnp.exp(-bmm_b[d]))
            _emit(o_ref, acc_ref, starts[d], c, num * jnp.concatenate([rcp] * (dv // LANES), axis=1), first[d])
        for d in dirs:
            st_ref[d] = decay[d] * state[d] + upd[d]
            n_ref[d] = decay[d] * nrm[d] + jnp.sum(wkt[d], axis=1, keepdims=True)

    _scan_steps(step, n_ctx_chunks, n_chunks, 1)


def _mlstm_scan(q, kt, v, g_rows, n_ctx):
    b, _, l, dk = q.shape
    dv = v.shape[3]
    n_chunks = l // SCAN_CHUNK
    return pl.pallas_call(
        functools.partial(_mlstm_scan_body, n_ctx_chunks=n_ctx // SCAN_CHUNK, n_chunks=n_chunks),
        grid=(b, N_HEADS),
        in_specs=[pl.BlockSpec((1, None, l, dk), _HEAD_SLAB),
                  pl.BlockSpec((1, n_chunks, dk, SCAN_CHUNK), lambda i, h: (i, 0, h, 0)),
                  pl.BlockSpec((1, None, l, dv), _HEAD_SLAB),
                  pl.BlockSpec((1, 1, n_chunks, 8, SCAN_CHUNK), lambda i, h: (i, h, 0, 0, 0))],
        out_specs=pl.BlockSpec((1, None, l, dv), _HEAD_SLAB),
        out_shape=jax.ShapeDtypeStruct((b, N_HEADS, l, dv), BF16),
        scratch_shapes=[pltpu.VMEM((l, dv), F32),
                        pltpu.VMEM((2, dk, dv), F32),
                        pltpu.VMEM((2, dk, LANES), F32),
                        pltpu.VMEM((n_chunks, 16, SCAN_CHUNK), F32)],
        compiler_params=pltpu.CompilerParams(
            dimension_semantics=("parallel", "parallel"), vmem_limit_bytes=VMEM_LIMIT),
        name="mlstm_scan",
    )(q, kt, v, g_rows)


def _ret_scan_body(q_ref, kt_ref, v_ref, dl_ref, o_ref, acc_ref, st_ref, dm_ref, *, n_ctx_chunks, n_chunks):
    c = SCAN_CHUNK
    st_ref[...] = jnp.zeros(st_ref.shape, F32)
    masks = _tri_masks(c)
    row = lax.broadcasted_iota(jnp.int32, (c, c), 0)
    col = lax.broadcasted_iota(jnp.int32, (c, c), 1)
    dist = jnp.abs(row - col).astype(F32)
    rid = lax.broadcasted_iota(jnp.int32, (c, 1), 0).astype(F32)
    lid = lax.broadcasted_iota(jnp.int32, (1, c), 1).astype(F32)
    log_gamma = []
    for dirn in range(2):
        lg = _log_sigmoid(dl_ref[0, dirn])[0:1, 0:1]
        log_gamma.append(lg)
        dm_ref[dirn] = jnp.where(masks[dirn], jnp.exp(lg * dist), 0.0)

    pos_col = (rid + 1.0, c - rid)
    pos_row = (lid + 1.0, c - lid)

    def step(s, first):
        dirs = (0, 1)
        idxs = (s, _bw_chunk(s, n_ctx_chunks, n_chunks))
        starts = [pl.multiple_of(idx * c, c) for idx in idxs]
        q = [q_ref[0, pl.ds(st, c), :] for st in starts]
        kt = [kt_ref[0, idx] for idx in idxs]
        v = [v_ref[0, pl.ds(st, c), :] for st in starts]
        state = [st_ref[d] for d in dirs]
        sc = [(_dot(q[d], kt[d]) * dm_ref[d]).astype(BF16) for d in dirs]
        intra = [_dot(sc[d], v[d]) for d in dirs]
        inter = [_dot(q[d], state[d].astype(BF16)) for d in dirs]
        kdt = [(kt[d].astype(F32) * jnp.exp(log_gamma[d] * (c - pos_row[d]))).astype(BF16) for d in dirs]
        upd = [_dot(kdt[d], v[d]) for d in dirs]
        for d in dirs:
            _emit(o_ref, acc_ref, starts[d], c, intra[d] + inter[d] * jnp.exp(log_gamma[d] * pos_col[d]), first[d])
        for d in dirs:
            st_ref[d] = jnp.exp(log_gamma[d] * c) * state[d] + upd[d]

    _scan_steps(step, n_ctx_chunks, n_chunks, 4)


def _ret_scan(q, kt, v, dl, n_ctx):
    b, _, l, dk = q.shape
    dv = v.shape[3]
    n_chunks = l // SCAN_CHUNK
    return pl.pallas_call(
        functools.partial(_ret_scan_body, n_ctx_chunks=n_ctx // SCAN_CHUNK, n_chunks=n_chunks),
        grid=(b, N_HEADS),
        in_specs=[pl.BlockSpec((1, None, l, dk), _HEAD_SLAB),
                  pl.BlockSpec((1, n_chunks, dk, SCAN_CHUNK), lambda i, h: (i, 0, h, 0)),
                  pl.BlockSpec((1, None, l, dv), _HEAD_SLAB),
                  pl.BlockSpec((1, 2, 8, LANES), lambda i, h: (h, 0, 0, 0))],
        out_specs=pl.BlockSpec((1, None, l, dv), _HEAD_SLAB),
        out_shape=jax.ShapeDtypeStruct((b, N_HEADS, l, dv), BF16),
        scratch_shapes=[pltpu.VMEM((l, dv), F32),
                        pltpu.VMEM((2, dk, dv), F32),
                        pltpu.VMEM((2, SCAN_CHUNK, SCAN_CHUNK), F32)],
        compiler_params=pltpu.CompilerParams(
            dimension_semantics=("parallel", "parallel"), vmem_limit_bytes=VMEM_LIMIT),
        name="ret_scan",
    )(q, kt, v, dl)


def _gla_scan_body(q_ref, kt_ref, v_ref, code_ref, w2_ref, b2_ref, o_ref,
                   acc_ref, st_ref, s_ref, qi_ref, kdt_ref, e_ref, m01_ref, *, n_ctx_chunks, n_chunks):
    c = GLA_CHUNK
    t = GLA_SUB
    n_sub = c // t
    dk = q_ref.shape[2]
    dv = v_ref.shape[2]
    st_ref[...] = jnp.zeros(st_ref.shape, F32)
    masks = _tri_masks(c)
    for d in range(2):
        m01_ref[d] = masks[d].astype(BF16)
    lid = lax.broadcasted_iota(jnp.int32, (1, c), 1)

    def intra(it, carry, first_chunk, group):
        chains = [(first_chunk + it * group + j, dirn) for j in range(group) for dirn in range(2)]
        starts = [pl.multiple_of(idx * c, c) for idx, _ in chains]
        qf = [q_ref[0, pl.ds(st, c), :].astype(F32) for st in starts]
        ktf = [kt_ref[0, idx].astype(F32) for idx, _ in chains]
        code = [code_ref[0, pl.ds(st, c), :] for st in starts]
        a = [_log_sigmoid(_dot(code[k], w2_ref[d]) + b2_ref[d]) * (1.0 / GLA_TAU) for k, (_, d) in enumerate(chains)]
        parts = [_split3(x) for x in a]
        b = [None] * len(chains)
        for p in (2, 1, 0):
            for k, (_, d) in enumerate(chains):
                term = _dot(m01_ref[d], parts[k][p])
                b[k] = term if b[k] is None else b[k] + term
        qp, kp, kdt, e_end = [], [], [], []
        for k, (_, dirn) in enumerate(chains):
            refs = []
            for sb in range(n_sub):
                r0 = sb * t + (t - 1 if dirn else 0)
                refs.append(b[k][r0:r0 + 1] - a[k][r0:r0 + 1])
            own = jnp.concatenate([jnp.broadcast_to(r, (t, dk)) for r in refs], axis=0)
            b_rel = b[k] - own
            q_own = qf[k] * jnp.exp(b_rel)
            k_rel_t = (ktf[k] * jnp.exp(jnp.minimum(-(b_rel.T), GLA_EXP_CLAMP))).astype(BF16)
            r_end = 0 if dirn else c - 1
            b_end = b[k][r_end:r_end + 1]
            kdt.append((ktf[k] * jnp.exp((b_end - b[k]).T)).astype(BF16))
            e_end.append(jnp.broadcast_to(jnp.exp(b_end), (8, dk)))
            col_blocks = []
            for sb in range(n_sub):
                pieces = []
                for j in range(n_sub):
                    if (j <= sb) if dirn else (j >= sb):
                        qj = q_own[j * t:(j + 1) * t]
                        pieces.append(qj if j == sb else qj * jnp.exp(refs[j] - refs[sb]))
                    else:
                        pieces.append(jnp.zeros((t, dk), F32))
                col_blocks.append(jnp.concatenate(pieces, axis=0).astype(BF16))
            qp.append(col_blocks)
            kp.append(jnp.concatenate(
                [jnp.where(jnp.logical_and(lid >= sb * t, lid < (sb + 1) * t), k_rel_t, 0.0)
                 for sb in range(n_sub)], axis=0))
        s = [_dot(jnp.concatenate(qp[k], axis=1), kp[k]) for k in range(len(chains))]
        for k, (idx, dirn) in enumerate(chains):
            s_ref[dirn, pl.ds(starts[k], c), :] = jnp.where(masks[dirn], s[k], 0.0).astype(BF16)
            qi_ref[dirn, pl.ds(starts[k], c), :] = qp[k][n_sub - 1 if dirn else 0]
            kdt_ref[dirn, idx] = kdt[k]
            e_ref[dirn, idx] = e_end[k]
        return carry

    n_full = n_chunks // GLA_INTRA_CHUNKS
    lax.fori_loop(0, n_full, functools.partial(intra, first_chunk=0, group=GLA_INTRA_CHUNKS), 0)
    rest = n_chunks - n_full * GLA_INTRA_CHUNKS
    if rest:
        intra(0, 0, n_full * GLA_INTRA_CHUNKS, rest)

    def step(s, first):
        dirs = (0, 1)
        idxs = (s, _bw_chunk(s, n_ctx_chunks, n_chunks))
        starts = [pl.multiple_of(idx * c, c) for idx in idxs]
        v = [v_ref[0, pl.ds(st, c), :] for st in starts]
        state = [st_ref[d] for d in dirs]
        intra = [_dot(s_ref[d, pl.ds(starts[d], c), :], v[d]) for d in dirs]
        inter = [_dot(qi_ref[d, pl.ds(starts[d], c), :], state[d].astype(BF16)) for d in dirs]
        upd = [_dot(kdt_ref[d, idxs[d]], v[d]) for d in dirs]
        e_cols = [jnp.concatenate([e_ref[d, idxs[d]]] * (dk // 8), axis=0).T for d in dirs]
        for d in dirs:
            _emit(o_ref, acc_ref, starts[d], c, intra[d] + inter[d], first[d])
        for d in dirs:
            st_ref[d] = jnp.concatenate([e_cols[d]] * (dv // dk), axis=1) * state[d] + upd[d]

    _scan_steps(step, n_ctx_chunks, n_chunks, 4)


def _gla_scan(q, kt, v, code, w2, b2, n_ctx):
    b, _, l, dk = q.shape
    dv = v.shape[3]
    n_chunks = l // GLA_CHUNK
    return pl.pallas_call(
        functools.partial(_gla_scan_body, n_ctx_chunks=n_ctx // GLA_CHUNK, n_chunks=n_chunks),
        grid=(b, N_HEADS),
        in_specs=[pl.BlockSpec((1, None, l, dk), _HEAD_SLAB),
                  pl.BlockSpec((1, n_chunks, dk, GLA_CHUNK), lambda i, h: (i, 0, h, 0)),
                  pl.BlockSpec((1, None, l, dv), _HEAD_SLAB),
                  pl.BlockSpec((1, l, LANES), lambda i, h: (i, 0, 0)),
                  pl.BlockSpec((2, LANES, dk), lambda i, h: (0, 0, h)),
                  pl.BlockSpec((2, 1, dk), lambda i, h: (0, 0, h))],
        out_specs=pl.BlockSpec((1, None, l, dv), _HEAD_SLAB),
        out_shape=jax.ShapeDtypeStruct((b, N_HEADS, l, dv), BF16),
        scratch_shapes=[pltpu.VMEM((l, dv), F32),
                        pltpu.VMEM((2, dk, dv), F32),
                        pltpu.VMEM((2, l, GLA_CHUNK), BF16),
                        pltpu.VMEM((2, l, dk), BF16),
                        pltpu.VMEM((2, n_chunks, dk, GLA_CHUNK), BF16),
                        pltpu.VMEM((2, n_chunks, 8, dk), F32),
                        pltpu.VMEM((2, GLA_CHUNK, GLA_CHUNK), BF16)],
        compiler_params=pltpu.CompilerParams(
            dimension_semantics=("parallel", "parallel"), vmem_limit_bytes=VMEM_LIMIT),
        name="gla_scan",
    )(q, kt, v, code, w2, b2)


def _out_core(x, o_ref, z_ref, hw_ref, ow_ref, mod_ref, center):
    dv = o_ref.shape[3]
    proj = None
    for hh in range(N_HEADS):
        cols = slice(hh * dv, (hh + 1) * dv)
        yh = o_ref[0, hh].astype(F32)
        if center:
            yh = yh - jnp.mean(yh, axis=-1, keepdims=True)
        yn = yh * lax.rsqrt(jnp.mean(yh * yh, axis=-1, keepdims=True) + NORM_EPS) * hw_ref[:, cols]
        a = (yn * _silu(z_ref[0, :, cols].astype(F32))).astype(BF16)
        term = _dot(a, ow_ref[cols, :])
        proj = term if proj is None else proj + term
    return x + mod_ref[0, 0][2:3] * proj


def _out_final_body(x, o_ref, z_ref, hw_ref, ow_ref, mod_ref, fw_ref, out_ref, *, center):
    hn = _out_core(x, o_ref, z_ref, hw_ref, ow_ref, mod_ref, center)
    out_ref[0] = hn * lax.rsqrt(jnp.mean(hn * hn, axis=-1, keepdims=True) + NORM_EPS) * fw_ref[...]


def _out_proj_body(x, o_ref, z_ref, hw_ref, ow_ref, mod_ref, *rest, center, proj_body, n_proj_in):
    proj_in, h_out_ref, proj_out = rest[:n_proj_in], rest[n_proj_in], rest[n_proj_in + 1:]
    hn = _out_core(x, o_ref, z_ref, hw_ref, ow_ref, mod_ref, center)
    h_out_ref[0] = hn
    proj_body(hn, *proj_in, *proj_out)


def _out_operands(o, z, head_norm_w, out_w, modsel, skip):
    _, n_heads, _, dv = o.shape
    di = n_heads * dv
    d = out_w.shape[1]
    tok = lambda i, t: (i, t + skip, 0)
    specs = [pl.BlockSpec((1, n_heads, TOKEN_TILE, dv), lambda i, t: (i, 0, t + skip, 0)),
             pl.BlockSpec((1, TOKEN_TILE, di), tok),
             pl.BlockSpec((1, di), lambda i, t: (0, 0)),
             pl.BlockSpec((di, d), lambda i, t: (0, 0), pipeline_mode=pl.Buffered(1)),
             pl.BlockSpec((1, 1, 3, d), lambda i, t: (i, jnp.minimum(t + skip, 1), 0, 0))]
    return [o, z, head_norm_w.reshape(1, di), out_w, modsel], specs


def _out_final_call(o, z, head_norm_w, out_w, h, modsel, final_w, n_ctx, center):
    b, _, l, _ = o.shape
    d = out_w.shape[1]
    skip = n_ctx // TOKEN_TILE
    nt = l // TOKEN_TILE - skip
    h_args, h_specs = _hidden_operands(h, n_ctx, skip)
    args, specs = _out_operands(o, z, head_norm_w, out_w, modsel, skip)
    return pl.pallas_call(
        _with_hidden_tile(functools.partial(_out_final_body, center=center),
                          len(h_args), n_ctx // TOKEN_TILE, skip),
        grid=(b, nt),
        in_specs=h_specs + specs + [pl.BlockSpec((1, d), lambda i, t: (0, 0))],
        out_specs=pl.BlockSpec((1, TOKEN_TILE, d), lambda i, t: (i, t, 0)),
        out_shape=jax.ShapeDtypeStruct((b, nt * TOKEN_TILE, d), F32),
        compiler_params=pltpu.CompilerParams(
            dimension_semantics=("parallel", "parallel"), vmem_limit_bytes=VMEM_LIMIT),
        name="out_final",
    )(*h_args, *args, final_w.reshape(1, d))


def _out_proj_call(o, z, head_norm_w, out_w, h, modsel, n_ctx, center, spec, next_modsel, next_norm_w):
    b, _, l, _ = o.shape
    d = out_w.shape[1]
    h_args, h_specs = _hidden_operands(h, n_ctx)
    args, specs = _out_operands(o, z, head_norm_w, out_w, modsel, 0)
    p_in, p_out_specs, p_out_shape = _proj_specs(spec, b, l, d)
    res = pl.pallas_call(
        _with_hidden_tile(functools.partial(_out_proj_body, center=center, proj_body=spec["body"],
                                            n_proj_in=len(p_in)),
                          len(h_args), n_ctx // TOKEN_TILE),
        grid=(b, l // TOKEN_TILE),
        in_specs=h_specs + specs + p_in,
        out_specs=[pl.BlockSpec((1, TOKEN_TILE, d), lambda i, t: (i, t, 0))] + p_out_specs,
        out_shape=[jax.ShapeDtypeStruct((b, l, d), F32)] + p_out_shape,
        compiler_params=pltpu.CompilerParams(
            dimension_semantics=("parallel", "parallel"), vmem_limit_bytes=VMEM_LIMIT),
        name="out_proj_" + spec["name"],
    )(*h_args, *args, next_modsel, next_norm_w.reshape(1, d), *spec["consts"], *spec["tables"])
    return res[0], res[1:]


def _pad_cols(w, n):
    return jnp.pad(w, ((0, 0), (0, n - w.shape[1])))


def _mlstm_proj_spec(l, n_ctx, in_w, conv_w, conv_b, gate_b):
    d = in_w.shape[0]
    di = 2 * d
    qk_w = 2 * d
    wg = _pad_cols(in_w[:, qk_w + 2 * di:], LANES).astype(BF16)
    gb = _pad_cols(gate_b.reshape(1, -1), LANES)
    return dict(body=_proj_mlstm_body, consts=[in_w.astype(BF16), wg, gb], tables=[], name="mlstm",
                outs=[(qk_w, F32, None), (di, BF16, "heads"), (di, BF16, None), (LANES, F32, None)])


def _mlstm_mix(outs, n_ctx, in_w, conv_w, conv_b, gate_b):
    qk_raw, v, z, g = outs
    b, l, qk_w = qk_raw.shape
    head_qk = qk_w // (2 * N_HEADS)
    q = _conv(qk_raw, conv_w, conv_b, n_ctx, 0, qk_w // 2, 1.0, False)
    kt = _conv(qk_raw, conv_w, conv_b, n_ctx, qk_w // 2, qk_w // 2, head_qk ** -0.5, True)
    n_chunks = l // SCAN_CHUNK
    g_rows = g[:, :, :4 * N_HEADS].reshape(b, n_chunks, SCAN_CHUNK, 4, N_HEADS).transpose(0, 4, 1, 3, 2)
    g_rows = jnp.pad(g_rows, ((0, 0), (0, 0), (0, 0), (0, 4), (0, 0)))
    return _mlstm_scan(q, kt, v, g_rows, n_ctx), z


def _gla_proj_spec(l, n_ctx, in_w, gk_w2, gk_b):
    d = in_w.shape[0]
    di = 2 * d
    key = gk_w2.shape[2]
    assert key // N_HEADS == LANES
    wlr = _pad_cols(in_w[:, 2 * key + 2 * di:], LANES).astype(BF16)
    return dict(body=functools.partial(_proj_gla_body, q_scale=(key // N_HEADS) ** -0.5),
                consts=[in_w.astype(BF16), wlr], tables=[], name="gla",
                outs=[(key, BF16, "heads"), (key, BF16, GLA_CHUNK), (di, BF16, "heads"), (di, BF16, None),
                      (LANES, BF16, None)])


def _gla_mix(outs, n_ctx, in_w, gk_w2, gk_b):
    q, kt, v, z, code = outs
    key = gk_w2.shape[2]
    w2 = jnp.zeros((2, LANES, key), F32)
    w2 = w2.at[0, :GLA_RANK].set(gk_w2[0]).at[1, GLA_RANK:2 * GLA_RANK].set(gk_w2[1]).astype(BF16)
    return _gla_scan(q, kt, v, code, w2, gk_b.reshape(2, 1, key), n_ctx), z


def _rope_tables(n_ctx, n_lat, head_k):
    quarter = head_k // 4
    pos = jnp.arange(n_lat, dtype=jnp.int32)
    inv_freq = ROPE_BASE ** (-jnp.arange(quarter, dtype=F32) / quarter)
    cos_parts, sin_parts = [], []
    for p in ((pos // GRID_W).astype(F32), (pos % GRID_W).astype(F32)):
        ang = p[:, None] * inv_freq[None, :]
        cos_parts += [jnp.cos(ang), jnp.cos(ang)]
        sin_parts += [-jnp.sin(ang), jnp.sin(ang)]
    cos = jnp.concatenate(cos_parts, axis=1)
    sin = jnp.concatenate(sin_parts, axis=1)
    cos = jnp.concatenate([jnp.ones((n_ctx, head_k), F32), cos], axis=0)
    sin = jnp.concatenate([jnp.zeros((n_ctx, head_k), F32), sin], axis=0)
    return cos, sin


def _ret_proj_spec(l, n_ctx, in_w, decay_logit):
    d = in_w.shape[0]
    di = 2 * d
    key = (in_w.shape[1] - 2 * di) // 2
    head_k = key // N_HEADS
    cos, sin = _rope_tables(n_ctx, l - n_ctx, head_k)
    return dict(body=functools.partial(_proj_ret_body, k_scale=head_k ** -0.5),
                consts=[in_w.astype(BF16)], tables=[cos, sin], name="ret",
                outs=[(key, BF16, "heads"), (key, BF16, SCAN_CHUNK), (di, BF16, "heads"), (di, BF16, None)])


def _ret_mix(outs, n_ctx, in_w, decay_logit):
    q, kt, v, z = outs
    dl = jnp.broadcast_to(decay_logit.astype(F32).T[:, :, None, None], (N_HEADS, 2, 8, LANES))
    return _ret_scan(q, kt, v, dl, n_ctx), z


_PROJ_SPEC = {"mlstm": _mlstm_proj_spec, "gla": _gla_proj_spec, "retention": _ret_proj_spec}
_MIX = {"mlstm": _mlstm_mix, "gla": _gla_mix, "retention": _ret_mix}


def kernel(x, c, ctx, c_ctx, l0_norm_w, l0_ada_w, l0_ada_b, l0_in_w, l0_conv_w, l0_conv_b, l0_gate_b, l0_head_norm_w, l0_out_w, l1_norm_w, l1_ada_w, l1_ada_b, l1_in_w, l1_gk_w2, l1_gk_b, l1_head_norm_w, l1_out_w, l2_norm_w, l2_ada_w, l2_ada_b, l2_in_w, l2_decay_logit, l2_head_norm_w, l2_out_w, l3_norm_w, l3_ada_w, l3_ada_b, l3_in_w, l3_conv_w, l3_conv_b, l3_gate_b, l3_head_norm_w, l3_out_w, final_norm_w):
    layers = (
        ("mlstm", l0_norm_w, l0_ada_w, l0_ada_b, l0_out_w, l0_head_norm_w, (l0_in_w, l0_conv_w, l0_conv_b, l0_gate_b)),
        ("gla", l1_norm_w, l1_ada_w, l1_ada_b, l1_out_w, l1_head_norm_w, (l1_in_w, l1_gk_w2, l1_gk_b)),
        ("retention", l2_norm_w, l2_ada_w, l2_ada_b, l2_out_w, l2_head_norm_w, (l2_in_w, l2_decay_logit)),
        ("mlstm", l3_norm_w, l3_ada_w, l3_ada_b, l3_out_w, l3_head_norm_w, (l3_in_w, l3_conv_w, l3_conv_b, l3_gate_b)),
    )
    b, _, d = x.shape
    n_ctx = ctx.shape[1]
    assert n_ctx % TOKEN_TILE == 0 and n_ctx % SCAN_CHUNK == 0 and x.shape[1] % SCAN_CHUNK == 0
    assert TOKEN_TILE == SCAN_CHUNK and TOKEN_TILE % GLA_CHUNK == 0
    l = n_ctx + x.shape[1]
    h = (ctx, x)
    cc = jnp.concatenate([c, c_ctx[None, :], jnp.zeros((7, d), F32)], axis=0)
    modsels = []
    for _, _, ada_w, ada_b, _, _, _ in layers:
        mod = _ada(cc, ada_w, ada_b)
        mod_lat = mod[:b].reshape(b, 1, 3, d)
        mod_ctx = jnp.broadcast_to(mod[b].reshape(1, 1, 3, d), (b, 1, 3, d))
        modsels.append(jnp.concatenate([mod_ctx, mod_lat], axis=1))
    specs = [_PROJ_SPEC[kind](l, n_ctx, *params) for kind, _, _, _, _, _, params in layers]
    proj_outs = _proj_call(specs[0], h, n_ctx, modsels[0], layers[0][1])
    for li, (kind, _, _, _, out_w, head_norm_w, params) in enumerate(layers):
        o, z = _MIX[kind](proj_outs, n_ctx, *params)
        center = kind != "gla"
        if li == len(layers) - 1:
            return _out_final_call(o, z, head_norm_w, out_w.astype(BF16), h, modsels[li], final_norm_w,
                                   n_ctx, center)
        h, proj_outs = _out_proj_call(o, z, head_norm_w, out_w.astype(BF16), h, modsels[li], n_ctx, center,
                                      specs[li + 1], modsels[li + 1], layers[li + 1][1])
```

```python
import functools

import jax
import jax.numpy as jnp
from jax import lax
from jax.experimental import pallas as pl
from jax.experimental.pallas import tpu as pltpu

N_HEADS = 4
GRID_W = 64
GLA_RANK = 16
GLA_TAU = 16.0
ROPE_BASE = 10000.0
NORM_EPS = 1e-6
NEG_INF = -1e30

TOKEN_TILE = 256
SCAN_CHUNK = 256
GLA_CHUNK = 128
GLA_SUB = 16
GLA_EXP_CLAMP = 80.0
GLA_INTRA_CHUNKS = 8
LANES = 128
VMEM_LIMIT = 56 * 1024 * 1024

F32 = jnp.float32
BF16 = jnp.bfloat16


def _silu(x):
    half = 0.5 * x
    return half + half * jnp.tanh(half)


def _log_sigmoid(x):
    return jnp.minimum(x, 0.0) - jnp.log(1.0 + jnp.exp(-jnp.abs(x)))


def _dot(a, b):
    return jnp.dot(a, b, preferred_element_type=F32)


def _split3(a):
    hi = a.astype(BF16)
    r1 = a - hi.astype(F32)
    mid = r1.astype(BF16)
    lo = (r1 - mid.astype(F32)).astype(BF16)
    return hi, mid, lo


def _dot01_left(m01, a):
    hi, mid, lo = _split3(a)
    return _dot(m01, lo) + _dot(m01, mid) + _dot(m01, hi)


def _tri_masks(n):
    row = lax.broadcasted_iota(jnp.int32, (n, n), 0)
    col = lax.broadcasted_iota(jnp.int32, (n, n), 1)
    return col <= row, col >= row


_HEAD_SLAB = lambda i, h: (i, h, 0, 0)


def _prefix_scan(x, lane, backward, op, identity):
    ax = x.ndim - 1
    n = x.shape[ax]
    sh = 1
    while sh < n:
        if backward:
            x = op(x, jnp.where(lane < n - sh, pltpu.roll(x, n - sh, ax), identity))
        else:
            x = op(x, jnp.where(lane >= sh, pltpu.roll(x, sh, ax), identity))
        sh *= 2
    return x


def _bw_chunk(step, n_ctx_chunks, n_chunks):
    return jnp.where(step < n_ctx_chunks, n_ctx_chunks - 1 - step, n_chunks - 1 - (step - n_ctx_chunks))


def _first_touch_groups(n_ctx_chunks, n_chunks):
    def bw_idx(s):
        return n_ctx_chunks - 1 - s if s < n_ctx_chunks else n_chunks - 1 - (s - n_ctx_chunks)
    bw_step = {bw_idx(s): s for s in range(n_chunks)}
    groups = []
    for s in range(n_chunks):
        flags = (s <= bw_step[s], s < bw_idx(s))
        if groups and groups[-1][2] == flags:
            groups[-1][1] = s + 1
        else:
            groups.append([s, s + 1, flags])
    return [tuple(g) for g in groups]


def _scan_steps(step, n_ctx_chunks, n_chunks, unroll):
    for start, stop, first in _first_touch_groups(n_ctx_chunks, n_chunks):
        lax.fori_loop(start, stop, lambda s, carry, first=first: step(s, first) or carry, 0,
                      unroll=unroll if (stop - start) % unroll == 0 else 1)


def _emit(o_ref, acc_ref, start, n, value, first):
    if first:
        acc_ref[pl.ds(start, n), :] = value
    else:
        o_ref[0, pl.ds(start, n), :] = (acc_ref[pl.ds(start, n), :] + value).astype(o_ref.dtype)


def _ada_body(c_ref, w_ref, b_ref, o_ref):
    s = _silu(c_ref[...])
    o_ref[...] = jnp.dot(s, w_ref[...], preferred_element_type=F32,
                         precision=lax.Precision.HIGHEST) + b_ref[...]


def _ada(cc, w, b):
    rows, d = cc.shape
    n = w.shape[1]
    tn = 512
    return pl.pallas_call(
        _ada_body,
        grid=(n // tn,),
        in_specs=[pl.BlockSpec((rows, d), lambda j: (0, 0)),
                  pl.BlockSpec((d, tn), lambda j: (0, j)),
                  pl.BlockSpec((1, tn), lambda j: (0, j))],
        out_specs=pl.BlockSpec((rows, tn), lambda j: (0, j)),
        out_shape=jax.ShapeDtypeStruct((rows, n), F32),
        name="ada",
    )(cc, w, b.reshape(1, n))


def _hidden_operands(h, n_ctx, tile_offset=0):
    if not isinstance(h, tuple):
        d = h.shape[2]
        return [h], [pl.BlockSpec((1, TOKEN_TILE, d), lambda i, t: (i, t + tile_offset, 0))]
    ctx, lat = h
    d = lat.shape[2]
    nct = n_ctx // TOKEN_TILE
    return [ctx, lat], [
        pl.BlockSpec((1, TOKEN_TILE, d), lambda i, t: (i, jnp.minimum(t + tile_offset, nct - 1), 0)),
        pl.BlockSpec((1, TOKEN_TILE, d), lambda i, t: (i, jnp.maximum(t + tile_offset - nct, 0), 0))]


def _with_hidden_tile(body, n_hidden, n_ctx_tiles, tile_offset=0):
    def kernel_fn(*refs):
        if n_hidden == 1:
            x = refs[0][0]
        else:
            is_ctx = pl.program_id(1) + tile_offset < n_ctx_tiles
            x = jnp.where(is_ctx, refs[0][0], refs[1][0])
        body(x, *refs[n_hidden:])
    return kernel_fn


def _modulated(x, mod_ref, nw_ref):
    y = x * lax.rsqrt(jnp.mean(x * x, axis=-1, keepdims=True) + NORM_EPS) * nw_ref[...]
    m = mod_ref[0, 0]
    return (y * (1.0 + m[1:2]) + m[0:1]).astype(BF16)


def _store_heads(ref, x):
    w = ref.shape[3]
    for hh in range(ref.shape[1]):
        ref[0, hh] = x[:, hh * w:(hh + 1) * w].astype(ref.dtype)


def _store_chunked_t(ref, x, chunk):
    xt = x.T
    for j in range(x.shape[0] // chunk):
        ref[0, j] = xt[:, j * chunk:(j + 1) * chunk].astype(ref.dtype)


GATE_ROWS = 16


def _proj_mlstm_body(x, mod_ref, nw_ref, w_ref, wg_ref, gb_ref, qk_ref, v_ref, z_ref, g_ref):
    u = _modulated(x, mod_ref, nw_ref)
    qk_w, di = qk_ref.shape[2], z_ref.shape[2]
    g = _dot(u, wg_ref[...]) + gb_ref[...]
    lane = lax.broadcasted_iota(jnp.int32, g.shape, 1)
    is_forget = (lane // N_HEADS) % 2 == 1
    g = jnp.where(is_forget, _log_sigmoid(g), g)
    _store_gate_rows(g_ref, g)
    qk_ref[0] = _dot(u, w_ref[:, :qk_w])
    _store_heads(v_ref, _dot(u, w_ref[:, qk_w:qk_w + di]))
    z_ref[0] = _dot(u, w_ref[:, qk_w + di:qk_w + 2 * di]).astype(BF16)


def _store_gate_rows(g_ref, g):
    c = g.shape[0]
    gt = g.T[:4 * N_HEADS]
    lane_row = lax.broadcasted_iota(jnp.int32, (1, c), 1)
    per_dir = []
    for dirn in range(2):
        i_r = gt[2 * dirn * N_HEADS:(2 * dirn + 1) * N_HEADS]
        a_r = gt[(2 * dirn + 1) * N_HEADS:(2 * dirn + 2) * N_HEADS]
        b_r = _prefix_scan(a_r, lane_row, bool(dirn), jnp.add, 0.0)
        u = i_r - b_r
        b_end = jnp.sum(a_r, axis=1, keepdims=True)
        g_max = jnp.max(b_end + u, axis=1, keepdims=True)
        per_dir.append((u, b_r, _prefix_scan(u, lane_row, bool(dirn), jnp.maximum, NEG_INF),
                        jnp.broadcast_to(b_end, u.shape), jnp.broadcast_to(g_max, u.shape)))
    for hh in range(N_HEADS):
        rows = [arr[hh:hh + 1] for dirn in range(2) for arr in per_dir[dirn]]
        g_ref[0, 0, hh] = jnp.concatenate(rows + [jnp.zeros((GATE_ROWS - len(rows), c), F32)], axis=0)


def _proj_gla_body(x, mod_ref, nw_ref, w_ref, wlr_ref, q_ref, kt_ref, v_ref, z_ref, code_ref, *, q_scale):
    u = _modulated(x, mod_ref, nw_ref)
    key, di = kt_ref.shape[2], z_ref.shape[2]
    _store_heads(q_ref, _dot(u, w_ref[:, :key]) * q_scale)
    _store_chunked_t(kt_ref, _dot(u, w_ref[:, key:2 * key]), GLA_CHUNK)
    _store_heads(v_ref, _dot(u, w_ref[:, 2 * key:2 * key + di]))
    z_ref[0] = _dot(u, w_ref[:, 2 * key + di:2 * key + 2 * di]).astype(BF16)
    code_ref[0] = _dot(u, wlr_ref[...]).astype(BF16)


def _rope(t, cos, sin):
    outs = []
    for g in range(t.shape[1] // LANES):
        tg = t[:, g * LANES:(g + 1) * LANES]
        cg = cos[:, (g % 2) * LANES:(g % 2 + 1) * LANES]
        sg = sin[:, (g % 2) * LANES:(g % 2 + 1) * LANES]
        outs.append(tg * cg + pltpu.roll(tg, LANES // 2, 1) * sg)
    return jnp.concatenate(outs, axis=1)


def _proj_ret_body(x, mod_ref, nw_ref, w_ref, cos_ref, sin_ref, q_ref, kt_ref, v_ref, z_ref, *, k_scale):
    u = _modulated(x, mod_ref, nw_ref)
    key, di = kt_ref.shape[2], z_ref.shape[2]
    cos = cos_ref[...]
    sin = sin_ref[...]
    _store_heads(q_ref, _rope(_dot(u, w_ref[:, :key]), cos, sin))
    _store_chunked_t(kt_ref, _rope(_dot(u, w_ref[:, key:2 * key]), cos, sin) * k_scale, SCAN_CHUNK)
    _store_heads(v_ref, _dot(u, w_ref[:, 2 * key:2 * key + di]))
    z_ref[0] = _dot(u, w_ref[:, 2 * key + di:2 * key + 2 * di]).astype(BF16)


def _proj_specs(spec, b, l, d):
    in_specs = [pl.BlockSpec((1, 1, 3, d), lambda i, t: (i, jnp.minimum(t, 1), 0, 0)),
                pl.BlockSpec((1, d), lambda i, t: (0, 0))]
    for a in spec["consts"]:
        in_specs.append(pl.BlockSpec(a.shape, lambda i, t, nd=a.ndim: (0,) * nd, pipeline_mode=pl.Buffered(1)))
    for a in spec["tables"]:
        in_specs.append(pl.BlockSpec((TOKEN_TILE, a.shape[1]), lambda i, t: (t, 0)))
    out_specs, out_shape = [], []
    for n, dt, chunk in spec["outs"]:
        if chunk is None:
            out_specs.append(pl.BlockSpec((1, TOKEN_TILE, n), lambda i, t: (i, t, 0)))
            out_shape.append(jax.ShapeDtypeStruct((b, l, n), dt))
        elif isinstance(chunk, tuple):
            out_specs.append(pl.BlockSpec((1, 1) + chunk, lambda i, t, z=(0,) * len(chunk): (i, t) + z))
            out_shape.append(jax.ShapeDtypeStruct((b, l // TOKEN_TILE) + chunk, dt))
        elif chunk == "heads":
            out_specs.append(pl.BlockSpec((1, N_HEADS, TOKEN_TILE, n // N_HEADS), lambda i, t: (i, 0, t, 0)))
            out_shape.append(jax.ShapeDtypeStruct((b, N_HEADS, l, n // N_HEADS), dt))
        else:
            out_specs.append(pl.BlockSpec((1, TOKEN_TILE // chunk, n, chunk), lambda i, t: (i, t, 0, 0)))
            out_shape.append(jax.ShapeDtypeStruct((b, l // chunk, n, chunk), dt))
    return in_specs, out_specs, out_shape


def _proj_call(spec, h, n_ctx, modsel, norm_w):
    h_args, h_specs = _hidden_operands(h, n_ctx)
    b, _, d = h_args[-1].shape
    l = sum(a.shape[1] for a in h_args)
    in_specs, out_specs, out_shape = _proj_specs(spec, b, l, d)
    return pl.pallas_call(
        _with_hidden_tile(spec["body"], len(h_args), n_ctx // TOKEN_TILE),
        grid=(b, l // TOKEN_TILE),
        in_specs=h_specs + in_specs,
        out_specs=out_specs,
        out_shape=out_shape,
        compiler_params=pltpu.CompilerParams(
            dimension_semantics=("parallel", "parallel"), vmem_limit_bytes=VMEM_LIMIT),
        name="proj_" + spec["name"],
    )(*h_args, modsel, norm_w.reshape(1, d), *spec["consts"], *spec["tables"])


def _conv_body(x_ref, w_ref, b_ref, o_ref, *, n_ctx, n_blocks, scale, transposed):
    blk = SCAN_CHUNK
    l = x_ref.shape[1]
    w = w_ref[...]
    bias = b_ref[...]
    rid = lax.broadcasted_iota(jnp.int32, (blk, 1), 0)

    def emit(chunk, start, acc):
        y = _silu(acc + bias) * scale
        if transposed:
            o_ref[0, chunk] = y.T.astype(BF16)
        else:
            o_ref[0, pl.ds(start, blk), :] = y.astype(BF16)

    def combine(a, bm, cc, first, last):
        return (jnp.where(first, 0.0, pltpu.roll(a, 1, 0)) + bm
                + jnp.where(last, 0.0, pltpu.roll(cc, blk - 1, 0)))

    xc = x_ref[0, 0:blk, :]
    emit(0, 0, combine(xc * w[3:4], xc * w[4:5], xc * w[5:6], rid == 0, rid == blk - 1))

    col = rid % GRID_W

    def lat_block(it, carry):
        base = pl.multiple_of(n_ctx + it * blk, blk)
        mid = x_ref[0, pl.ds(base, blk), :]
        up_edge = x_ref[0, pl.ds(pl.multiple_of(base - GRID_W, GRID_W), GRID_W), :]
        dn_start = pl.multiple_of(jnp.minimum(base + blk, l - GRID_W), GRID_W)
        dn_edge = x_ref[0, pl.ds(dn_start, GRID_W), :]
        up_edge = jnp.where(it > 0, up_edge, 0.0)
        dn_edge = jnp.where(it < n_blocks - 1, dn_edge, 0.0)
        up = jnp.concatenate([up_edge, mid[:blk - GRID_W]], axis=0)
        down = jnp.concatenate([mid[GRID_W:], dn_edge], axis=0)
        a = up * w[0:1] + mid * w[3:4] + down * w[6:7]
        bm = up * w[1:2] + mid * w[4:5] + down * w[7:8]
        cc = up * w[2:3] + mid * w[5:6] + down * w[8:9]
        emit(n_ctx // blk + it, base, combine(a, bm, cc, col == 0, col == GRID_W - 1))
        return carry

    lax.fori_loop(0, n_blocks, lat_block, 0)


def _conv(qk_raw, conv_w, conv_b, n_ctx, col_offset, width, scale, transposed):
    b, l, c = qk_raw.shape
    tc = 256
    blk = SCAN_CHUNK
    assert n_ctx == blk and (l - n_ctx) % blk == 0 and blk % GRID_W == 0
    off = col_offset // tc
    if transposed:
        out_spec = pl.BlockSpec((1, l // blk, tc, blk), lambda i, j: (i, 0, j, 0))
        out_shape = jax.ShapeDtypeStruct((b, l // blk, width, blk), BF16)
    else:
        out_spec = pl.BlockSpec((1, None, l, tc), lambda i, j: (i, j, 0, 0))
        out_shape = jax.ShapeDtypeStruct((b, width // tc, l, tc), BF16)
    return pl.pallas_call(
        functools.partial(_conv_body, n_ctx=n_ctx, n_blocks=(l - n_ctx) // blk, scale=scale,
                          transposed=transposed),
        grid=(b, width // tc),
        in_specs=[pl.BlockSpec((1, l, tc), lambda i, j: (i, 0, j + off)),
                  pl.BlockSpec((9, tc), lambda i, j: (0, j + off)),
                  pl.BlockSpec((1, tc), lambda i, j: (0, j + off))],
        out_specs=out_spec,
        out_shape=out_shape,
        compiler_params=pltpu.CompilerParams(
            dimension_semantics=("parallel", "parallel"), vmem_limit_bytes=VMEM_LIMIT),
        name="mlstm_conv_kt" if transposed else "mlstm_conv_q",
    )(qk_raw, conv_w.reshape(9, c), conv_b.reshape(1, c))


def _mlstm_scan_body(q_ref, kt_ref, v_ref, g_ref, o_ref, acc_ref, st_ref, n_ref, row_ref,
                     *, n_ctx_chunks, n_chunks):
    c = SCAN_CHUNK
    dv = v_ref.shape[2]
    st_ref[...] = jnp.zeros(st_ref.shape, F32)
    n_ref[...] = jnp.zeros(n_ref.shape, F32)
    masks = _tri_masks(c)

    gates = g_ref[0]
    token_rows, chunk_rows = [], []
    for dirn in range(2):
        u, b_r, cm = (gates[:, 5 * dirn + r:5 * dirn + r + 1] for r in range(3))
        b_end = gates[:, 5 * dirn + 3:5 * dirn + 4, 0:1]
        g_max = gates[:, 5 * dirn + 4:5 * dirn + 5, 0:1]
        m = jnp.zeros((1, 1, 1), F32)
        m_prev_l, m_new_l = [None] * n_chunks, [None] * n_chunks
        for s in range(n_chunks):
            idx = s
            if dirn:
                idx = n_ctx_chunks - 1 - s if s < n_ctx_chunks else n_chunks - 1 - (s - n_ctx_chunks)
            m_prev_l[idx] = m
            m = jnp.maximum(b_end[idx:idx + 1] + m, g_max[idx:idx + 1])
            m_new_l[idx] = m
        m_prev = jnp.concatenate(m_prev_l, axis=0)
        m_new = jnp.concatenate(m_new_l, axis=0)
        mm = jnp.maximum(m_prev, cm)
        token_rows += [u, jnp.exp(b_end + u - m_new), mm, b_r + mm]
        chunk_rows += [jnp.broadcast_to(jnp.exp(b_end + m_prev - m_new), (n_chunks, 1, c)),
                       jnp.broadcast_to(m_prev, (n_chunks, 1, c))]
    row_ref[...] = jnp.concatenate(token_rows + chunk_rows + [jnp.zeros((n_chunks, 4, c), F32)], axis=1)

    def lane_bcast_col(row):
        return jnp.broadcast_to(row, (LANES, c)).T

    def step(s, first):
        dirs = (0, 1)
        idxs = (s, _bw_chunk(s, n_ctx_chunks, n_chunks))
        starts = [pl.multiple_of(idx * c, c) for idx in idxs]
        q = [q_ref[0, pl.ds(st, c), :] for st in starts]
        kt = [kt_ref[0, idx] for idx in idxs]
        v = [v_ref[0, pl.ds(st, c), :] for st in starts]
        rows = [row_ref[idx] for idx in idxs]
        state = [st_ref[d] for d in dirs]
        nrm = [n_ref[d] for d in dirs]
        qk = [_dot(q[d], kt[d]) for d in dirs]
        mm_b = [lane_bcast_col(rows[d][4 * d + 2:4 * d + 3]) for d in dirs]
        bmm_b = [lane_bcast_col(rows[d][4 * d + 3:4 * d + 4]) for d in dirs]
        m_prev = [rows[d][9 + 2 * d:10 + 2 * d, 0:1] for d in dirs]
        decay = [rows[d][8 + 2 * d:9 + 2 * d, 0:1] for d in dirs]
        sc = [qk[d] * jnp.exp(jnp.where(masks[d], rows[d][4 * d:4 * d + 1]
                                        - jnp.concatenate([mm_b[d]] * (c // LANES), axis=1), NEG_INF))
              for d in dirs]
        w_inter = [jnp.exp(m_prev[d] - mm_b[d]) for d in dirs]
        intra = [_dot(sc[d].astype(BF16), v[d]) for d in dirs]
        inter = [_dot(q[d], state[d].astype(BF16)) for d in dirs]
        qn = [_dot(q[d], nrm[d].astype(BF16)) for d in dirs]
        wkt = [kt[d].astype(F32) * rows[d][4 * d + 1:4 * d + 2] for d in dirs]
        upd = [_dot(wkt[d].astype(BF16), v[d]) for d in dirs]
        for d in dirs:
            num = intra[d] + inter[d] * jnp.concatenate([w_inter[d]] * (dv // LANES), axis=1)
            den = jnp.sum(sc[d], axis=1, keepdims=True) + qn[d] * w_inter[d]
            rcp = 1.0 / jnp.maximum(jnp.abs(den), jnp.exp(-bmm_b[d]))
            _emit(o_ref, acc_ref, starts[d], c, num * jnp.concatenate([rcp] * (dv // LANES), axis=1), first[d])
        for d in dirs:
            st_ref[d] = decay[d] * state[d] + upd[d]
            n_ref[d] = decay[d] * nrm[d] + jnp.sum(wkt[d], axis=1, keepdims=True)

    _scan_steps(step, n_ctx_chunks, n_chunks, 1)


def _mlstm_scan(q, kt, v, gates, n_ctx):
    b, _, l, dk = q.shape
    dv = v.shape[3]
    n_chunks = l // SCAN_CHUNK
    return pl.pallas_call(
        functools.partial(_mlstm_scan_body, n_ctx_chunks=n_ctx // SCAN_CHUNK, n_chunks=n_chunks),
        grid=(b, N_HEADS),
        in_specs=[pl.BlockSpec((1, None, l, dk), _HEAD_SLAB),
                  pl.BlockSpec((1, n_chunks, dk, SCAN_CHUNK), lambda i, h: (i, 0, h, 0)),
                  pl.BlockSpec((1, None, l, dv), _HEAD_SLAB),
                  pl.BlockSpec((1, n_chunks, None, GATE_ROWS, SCAN_CHUNK), lambda i, h: (i, 0, h, 0, 0))],
        out_specs=pl.BlockSpec((1, None, l, dv), _HEAD_SLAB),
        out_shape=jax.ShapeDtypeStruct((b, N_HEADS, l, dv), BF16),
        scratch_shapes=[pltpu.VMEM((l, dv), F32),
                        pltpu.VMEM((2, dk, dv), F32),
                        pltpu.VMEM((2, dk, LANES), F32),
                        pltpu.VMEM((n_chunks, 16, SCAN_CHUNK), F32)],
        compiler_params=pltpu.CompilerParams(
            dimension_semantics=("parallel", "parallel"), vmem_limit_bytes=VMEM_LIMIT),
        name="mlstm_scan",
    )(q, kt, v, gates)


def _ret_scan_body(q_ref, kt_ref, v_ref, dl_ref, o_ref, acc_ref, st_ref, dm_ref, *, n_ctx_chunks, n_chunks):
    c = SCAN_CHUNK
    st_ref[...] = jnp.zeros(st_ref.shape, F32)
    masks = _tri_masks(c)
    row = lax.broadcasted_iota(jnp.int32, (c, c), 0)
    col = lax.broadcasted_iota(jnp.int32, (c, c), 1)
    dist = jnp.abs(row - col).astype(F32)
    rid = lax.broadcasted_iota(jnp.int32, (c, 1), 0).astype(F32)
    lid = lax.broadcasted_iota(jnp.int32, (1, c), 1).astype(F32)
    log_gamma = []
    for dirn in range(2):
        lg = _log_sigmoid(dl_ref[0, dirn])[0:1, 0:1]
        log_gamma.append(lg)
        dm_ref[dirn] = jnp.where(masks[dirn], jnp.exp(lg * dist), 0.0)

    pos_col = (rid + 1.0, c - rid)
    pos_row = (lid + 1.0, c - lid)

    def step(s, first):
        dirs = (0, 1)
        idxs = (s, _bw_chunk(s, n_ctx_chunks, n_chunks))
        starts = [pl.multiple_of(idx * c, c) for idx in idxs]
        q = [q_ref[0, pl.ds(st, c), :] for st in starts]
        kt = [kt_ref[0, idx] for idx in idxs]
        v = [v_ref[0, pl.ds(st, c), :] for st in starts]
        state = [st_ref[d] for d in dirs]
        sc = [(_dot(q[d], kt[d]) * dm_ref[d]).astype(BF16) for d in dirs]
        intra = [_dot(sc[d], v[d]) for d in dirs]
        inter = [_dot(q[d], state[d].astype(BF16)) for d in dirs]
        kdt = [(kt[d].astype(F32) * jnp.exp(log_gamma[d] * (c - pos_row[d]))).astype(BF16) for d in dirs]
        upd = [_dot(kdt[d], v[d]) for d in dirs]
        for d in dirs:
            _emit(o_ref, acc_ref, starts[d], c, intra[d] + inter[d] * jnp.exp(log_gamma[d] * pos_col[d]), first[d])
        for d in dirs:
            st_ref[d] = jnp.exp(log_gamma[d] * c) * state[d] + upd[d]

    _scan_steps(step, n_ctx_chunks, n_chunks, 4)


def _ret_scan(q, kt, v, dl, n_ctx):
    b, _, l, dk = q.shape
    dv = v.shape[3]
    n_chunks = l // SCAN_CHUNK
    return pl.pallas_call(
        functools.partial(_ret_scan_body, n_ctx_chunks=n_ctx // SCAN_CHUNK, n_chunks=n_chunks),
        grid=(b, N_HEADS),
        in_specs=[pl.BlockSpec((1, None, l, dk), _HEAD_SLAB),
                  pl.BlockSpec((1, n_chunks, dk, SCAN_CHUNK), lambda i, h: (i, 0, h, 0)),
                  pl.BlockSpec((1, None, l, dv), _HEAD_SLAB),
                  pl.BlockSpec((1, 2, 8, LANES), lambda i, h: (h, 0, 0, 0))],
        out_specs=pl.BlockSpec((1, None, l, dv), _HEAD_SLAB),
        out_shape=jax.ShapeDtypeStruct((b, N_HEADS, l, dv), BF16),
        scratch_shapes=[pltpu.VMEM((l, dv), F32),
                        pltpu.VMEM((2, dk, dv), F32),
                        pltpu.VMEM((2, SCAN_CHUNK, SCAN_CHUNK), F32)],
        compiler_params=pltpu.CompilerParams(
            dimension_semantics=("parallel", "parallel"), vmem_limit_bytes=VMEM_LIMIT),
        name="ret_scan",
    )(q, kt, v, dl)


def _gla_scan_body(q_ref, kt_ref, v_ref, code_ref, w2_ref, b2_ref, o_ref,
                   acc_ref, st_ref, s_ref, qi_ref, kdt_ref, e_ref, m01_ref, *, n_ctx_chunks, n_chunks):
    c = GLA_CHUNK
    t = GLA_SUB
    n_sub = c // t
    dk = q_ref.shape[2]
    dv = v_ref.shape[2]
    st_ref[...] = jnp.zeros(st_ref.shape, F32)
    masks = _tri_masks(c)
    for d in range(2):
        m01_ref[d] = masks[d].astype(BF16)
    lid = lax.broadcasted_iota(jnp.int32, (1, c), 1)

    def intra(it, carry, first_chunk, group):
        chains = [(first_chunk + it * group + j, dirn) for j in range(group) for dirn in range(2)]
        starts = [pl.multiple_of(idx * c, c) for idx, _ in chains]
        qf = [q_ref[0, pl.ds(st, c), :].astype(F32) for st in starts]
        ktf = [kt_ref[0, idx].astype(F32) for idx, _ in chains]
        code = [code_ref[0, pl.ds(st, c), :] for st in starts]
        a = [_log_sigmoid(_dot(code[k], w2_ref[d]) + b2_ref[d]) * (1.0 / GLA_TAU) for k, (_, d) in enumerate(chains)]
        parts = [_split3(x) for x in a]
        b = [None] * len(chains)
        for p in (2, 1, 0):
            for k, (_, d) in enumerate(chains):
                term = _dot(m01_ref[d], parts[k][p])
                b[k] = term if b[k] is None else b[k] + term
        qp, kp, kdt, e_end = [], [], [], []
        for k, (_, dirn) in enumerate(chains):
            refs = []
            for sb in range(n_sub):
                r0 = sb * t + (t - 1 if dirn else 0)
                refs.append(b[k][r0:r0 + 1] - a[k][r0:r0 + 1])
            own = jnp.concatenate([jnp.broadcast_to(r, (t, dk)) for r in refs], axis=0)
            b_rel = b[k] - own
            q_own = qf[k] * jnp.exp(b_rel)
            k_rel_t = (ktf[k] * jnp.exp(jnp.minimum(-(b_rel.T), GLA_EXP_CLAMP))).astype(BF16)
            r_end = 0 if dirn else c - 1
            b_end = b[k][r_end:r_end + 1]
            kdt.append((ktf[k] * jnp.exp((b_end - b[k]).T)).astype(BF16))
            e_end.append(jnp.broadcast_to(jnp.exp(b_end), (8, dk)))
            col_blocks = []
            for sb in range(n_sub):
                pieces = []
                for j in range(n_sub):
                    if (j <= sb) if dirn else (j >= sb):
                        qj = q_own[j * t:(j + 1) * t]
                        pieces.append(qj if j == sb else qj * jnp.exp(refs[j] - refs[sb]))
                    else:
                        pieces.append(jnp.zeros((t, dk), F32))
                col_blocks.append(jnp.concatenate(pieces, axis=0).astype(BF16))
            qp.append(col_blocks)
            kp.append(jnp.concatenate(
                [jnp.where(jnp.logical_and(lid >= sb * t, lid < (sb + 1) * t), k_rel_t, 0.0)
                 for sb in range(n_sub)], axis=0))
        s = [_dot(jnp.concatenate(qp[k], axis=1), kp[k]) for k in range(len(chains))]
        for k, (idx, dirn) in enumerate(chains):
            s_ref[dirn, pl.ds(starts[k], c), :] = jnp.where(masks[dirn], s[k], 0.0).astype(BF16)
            qi_ref[dirn, pl.ds(starts[k], c), :] = qp[k][n_sub - 1 if dirn else 0]
            kdt_ref[dirn, idx] = kdt[k]
            e_ref[dirn, idx] = e_end[k]
        return carry

    n_full = n_chunks // GLA_INTRA_CHUNKS
    lax.fori_loop(0, n_full, functools.partial(intra, first_chunk=0, group=GLA_INTRA_CHUNKS), 0)
    rest = n_chunks - n_full * GLA_INTRA_CHUNKS
    if rest:
        intra(0, 0, n_full * GLA_INTRA_CHUNKS, rest)

    def step(s, first):
        dirs = (0, 1)
        idxs = (s, _bw_chunk(s, n_ctx_chunks, n_chunks))
        starts = [pl.multiple_of(idx * c, c) for idx in idxs]
        v = [v_ref[0, pl.ds(st, c), :] for st in starts]
        state = [st_ref[d] for d in dirs]
        intra = [_dot(s_ref[d, pl.ds(starts[d], c), :], v[d]) for d in dirs]
        inter = [_dot(qi_ref[d, pl.ds(starts[d], c), :], state[d].astype(BF16)) for d in dirs]
        upd = [_dot(kdt_ref[d, idxs[d]], v[d]) for d in dirs]
        e_cols = [jnp.concatenate([e_ref[d, idxs[d]]] * (dk // 8), axis=0).T for d in dirs]
        for d in dirs:
            _emit(o_ref, acc_ref, starts[d], c, intra[d] + inter[d], first[d])
        for d in dirs:
            st_ref[d] = jnp.concatenate([e_cols[d]] * (dv // dk), axis=1) * state[d] + upd[d]

    _scan_steps(step, n_ctx_chunks, n_chunks, 4)


def _gla_scan(q, kt, v, code, w2, b2, n_ctx):
    b, _, l, dk = q.shape
    dv = v.shape[3]
    n_chunks = l // GLA_CHUNK
    return pl.pallas_call(
        functools.partial(_gla_scan_body, n_ctx_chunks=n_ctx // GLA_CHUNK, n_chunks=n_chunks),
        grid=(b, N_HEADS),
        in_specs=[pl.BlockSpec((1, None, l, dk), _HEAD_SLAB),
                  pl.BlockSpec((1, n_chunks, dk, GLA_CHUNK), lambda i, h: (i, 0, h, 0)),
                  pl.BlockSpec((1, None, l, dv), _HEAD_SLAB),
                  pl.BlockSpec((1, l, LANES), lambda i, h: (i, 0, 0)),
                  pl.BlockSpec((2, LANES, dk), lambda i, h: (0, 0, h)),
                  pl.BlockSpec((2, 1, dk), lambda i, h: (0, 0, h))],
        out_specs=pl.BlockSpec((1, None, l, dv), _HEAD_SLAB),
        out_shape=jax.ShapeDtypeStruct((b, N_HEADS, l, dv), BF16),
        scratch_shapes=[pltpu.VMEM((l, dv), F32),
                        pltpu.VMEM((2, dk, dv), F32),
                        pltpu.VMEM((2, l, GLA_CHUNK), BF16),
                        pltpu.VMEM((2, l, dk), BF16),
                        pltpu.VMEM((2, n_chunks, dk, GLA_CHUNK), BF16),
                        pltpu.VMEM((2, n_chunks, 8, dk), F32),
                        pltpu.VMEM((2, GLA_CHUNK, GLA_CHUNK), BF16)],
        compiler_params=pltpu.CompilerParams(
            dimension_semantics=("parallel", "parallel"), vmem_limit_bytes=VMEM_LIMIT),
        name="gla_scan",
    )(q, kt, v, code, w2, b2)


def _out_core(x, o_ref, z_ref, hw_ref, ow_ref, mod_ref, center):
    dv = o_ref.shape[3]
    proj = None
    for hh in range(N_HEADS):
        cols = slice(hh * dv, (hh + 1) * dv)
        yh = o_ref[0, hh].astype(F32)
        if center:
            yh = yh - jnp.mean(yh, axis=-1, keepdims=True)
        yn = yh * lax.rsqrt(jnp.mean(yh * yh, axis=-1, keepdims=True) + NORM_EPS) * hw_ref[:, cols]
        a = (yn * _silu(z_ref[0, :, cols].astype(F32))).astype(BF16)
        term = _dot(a, ow_ref[cols, :])
        proj = term if proj is None else proj + term
    return x + mod_ref[0, 0][2:3] * proj


def _out_final_body(x, o_ref, z_ref, hw_ref, ow_ref, mod_ref, fw_ref, out_ref, *, center):
    hn = _out_core(x, o_ref, z_ref, hw_ref, ow_ref, mod_ref, center)
    out_ref[0] = hn * lax.rsqrt(jnp.mean(hn * hn, axis=-1, keepdims=True) + NORM_EPS) * fw_ref[...]


def _out_proj_body(x, o_ref, z_ref, hw_ref, ow_ref, mod_ref, *rest, center, proj_body, n_proj_in):
    proj_in, h_out_ref, proj_out = rest[:n_proj_in], rest[n_proj_in], rest[n_proj_in + 1:]
    hn = _out_core(x, o_ref, z_ref, hw_ref, ow_ref, mod_ref, center)
    h_out_ref[0] = hn
    proj_body(hn, *proj_in, *proj_out)


def _out_operands(o, z, head_norm_w, out_w, modsel, skip):
    _, n_heads, _, dv = o.shape
    di = n_heads * dv
    d = out_w.shape[1]
    tok = lambda i, t: (i, t + skip, 0)
    specs = [pl.BlockSpec((1, n_heads, TOKEN_TILE, dv), lambda i, t: (i, 0, t + skip, 0)),
             pl.BlockSpec((1, TOKEN_TILE, di), tok),
             pl.BlockSpec((1, di), lambda i, t: (0, 0)),
             pl.BlockSpec((di, d), lambda i, t: (0, 0), pipeline_mode=pl.Buffered(1)),
             pl.BlockSpec((1, 1, 3, d), lambda i, t: (i, jnp.minimum(t + skip, 1), 0, 0))]
    return [o, z, head_norm_w.reshape(1, di), out_w, modsel], specs


def _out_final_call(o, z, head_norm_w, out_w, h, modsel, final_w, n_ctx, center):
    b, _, l, _ = o.shape
    d = out_w.shape[1]
    skip = n_ctx // TOKEN_TILE
    nt = l // TOKEN_TILE - skip
    h_args, h_specs = _hidden_operands(h, n_ctx, skip)
    args, specs = _out_operands(o, z, head_norm_w, out_w, modsel, skip)
    return pl.pallas_call(
        _with_hidden_tile(functools.partial(_out_final_body, center=center),
                          len(h_args), n_ctx // TOKEN_TILE, skip),
        grid=(b, nt),
        in_specs=h_specs + specs + [pl.BlockSpec((1, d), lambda i, t: (0, 0))],
        out_specs=pl.BlockSpec((1, TOKEN_TILE, d), lambda i, t: (i, t, 0)),
        out_shape=jax.ShapeDtypeStruct((b, nt * TOKEN_TILE, d), F32),
        compiler_params=pltpu.CompilerParams(
            dimension_semantics=("parallel", "parallel"), vmem_limit_bytes=VMEM_LIMIT),
        name="out_final",
    )(*h_args, *args, final_w.reshape(1, d))


def _out_proj_call(o, z, head_norm_w, out_w, h, modsel, n_ctx, center, spec, next_modsel, next_norm_w):
    b, _, l, _ = o.shape
    d = out_w.shape[1]
    h_args, h_specs = _hidden_operands(h, n_ctx)
    args, specs = _out_operands(o, z, head_norm_w, out_w, modsel, 0)
    p_in, p_out_specs, p_out_shape = _proj_specs(spec, b, l, d)
    res = pl.pallas_call(
        _with_hidden_tile(functools.partial(_out_proj_body, center=center, proj_body=spec["body"],
                                            n_proj_in=len(p_in)),
                          len(h_args), n_ctx // TOKEN_TILE),
        grid=(b, l // TOKEN_TILE),
        in_specs=h_specs + specs + p_in,
        out_specs=[pl.BlockSpec((1, TOKEN_TILE, d), lambda i, t: (i, t, 0))] + p_out_specs,
        out_shape=[jax.ShapeDtypeStruct((b, l, d), F32)] + p_out_shape,
        compiler_params=pltpu.CompilerParams(
            dimension_semantics=("parallel", "parallel"), vmem_limit_bytes=VMEM_LIMIT),
        name="out_proj_" + spec["name"],
    )(*h_args, *args, next_modsel, next_norm_w.reshape(1, d), *spec["consts"], *spec["tables"])
    return res[0], res[1:]


def _pad_cols(w, n):
    return jnp.pad(w, ((0, 0), (0, n - w.shape[1])))


def _mlstm_proj_spec(l, n_ctx, in_w, conv_w, conv_b, gate_b):
    d = in_w.shape[0]
    di = 2 * d
    qk_w = 2 * d
    wg = _pad_cols(in_w[:, qk_w + 2 * di:], LANES).astype(BF16)
    gb = _pad_cols(gate_b.reshape(1, -1), LANES)
    return dict(body=_proj_mlstm_body, consts=[in_w.astype(BF16), wg, gb], tables=[], name="mlstm",
                outs=[(qk_w, F32, None), (di, BF16, "heads"), (di, BF16, None),
                      (0, F32, (N_HEADS, GATE_ROWS, TOKEN_TILE))])


def _mlstm_mix(outs, n_ctx, in_w, conv_w, conv_b, gate_b):
    qk_raw, v, z, g = outs
    b, l, qk_w = qk_raw.shape
    head_qk = qk_w // (2 * N_HEADS)
    q = _conv(qk_raw, conv_w, conv_b, n_ctx, 0, qk_w // 2, 1.0, False)
    kt = _conv(qk_raw, conv_w, conv_b, n_ctx, qk_w // 2, qk_w // 2, head_qk ** -0.5, True)
    return _mlstm_scan(q, kt, v, g, n_ctx), z


def _gla_proj_spec(l, n_ctx, in_w, gk_w2, gk_b):
    d = in_w.shape[0]
    di = 2 * d
    key = gk_w2.shape[2]
    assert key // N_HEADS == LANES
    wlr = _pad_cols(in_w[:, 2 * key + 2 * di:], LANES).astype(BF16)
    return dict(body=functools.partial(_proj_gla_body, q_scale=(key // N_HEADS) ** -0.5),
                consts=[in_w.astype(BF16), wlr], tables=[], name="gla",
                outs=[(key, BF16, "heads"), (key, BF16, GLA_CHUNK), (di, BF16, "heads"), (di, BF16, None),
                      (LANES, BF16, None)])


def _gla_mix(outs, n_ctx, in_w, gk_w2, gk_b):
    q, kt, v, z, code = outs
    key = gk_w2.shape[2]
    w2 = jnp.zeros((2, LANES, key), F32)
    w2 = w2.at[0, :GLA_RANK].set(gk_w2[0]).at[1, GLA_RANK:2 * GLA_RANK].set(gk_w2[1]).astype(BF16)
    return _gla_scan(q, kt, v, code, w2, gk_b.reshape(2, 1, key), n_ctx), z


def _rope_tables(n_ctx, n_lat, head_k):
    quarter = head_k // 4
    pos = jnp.arange(n_lat, dtype=jnp.int32)
    inv_freq = ROPE_BASE ** (-jnp.arange(quarter, dtype=F32) / quarter)
    cos_parts, sin_parts = [], []
    for p in ((pos // GRID_W).astype(F32), (pos % GRID_W).astype(F32)):
        ang = p[:, None] * inv_freq[None, :]
        cos_parts += [jnp.cos(ang), jnp.cos(ang)]
        sin_parts += [-jnp.sin(ang), jnp.sin(ang)]
    cos = jnp.concatenate(cos_parts, axis=1)
    sin = jnp.concatenate(sin_parts, axis=1)
    cos = jnp.concatenate([jnp.ones((n_ctx, head_k), F32), cos], axis=0)
    sin = jnp.concatenate([jnp.zeros((n_ctx, head_k), F32), sin], axis=0)
    return cos, sin


def _ret_proj_spec(l, n_ctx, in_w, decay_logit):
    d = in_w.shape[0]
    di = 2 * d
    key = (in_w.shape[1] - 2 * di) // 2
    head_k = key // N_HEADS
    cos, sin = _rope_tables(n_ctx, l - n_ctx, head_k)
    return dict(body=functools.partial(_proj_ret_body, k_scale=head_k ** -0.5),
                consts=[in_w.astype(BF16)], tables=[cos, sin], name="ret",
                outs=[(key, BF16, "heads"), (key, BF16, SCAN_CHUNK), (di, BF16, "heads"), (di, BF16, None)])


def _ret_mix(outs, n_ctx, in_w, decay_logit):
    q, kt, v, z = outs
    dl = jnp.broadcast_to(decay_logit.astype(F32).T[:, :, None, None], (N_HEADS, 2, 8, LANES))
    return _ret_scan(q, kt, v, dl, n_ctx), z


_PROJ_SPEC = {"mlstm": _mlstm_proj_spec, "gla": _gla_proj_spec, "retention": _ret_proj_spec}
_MIX = {"mlstm": _mlstm_mix, "gla": _gla_mix, "retention": _ret_mix}


def kernel(x, c, ctx, c_ctx, l0_norm_w, l0_ada_w, l0_ada_b, l0_in_w, l0_conv_w, l0_conv_b, l0_gate_b, l0_head_norm_w, l0_out_w, l1_norm_w, l1_ada_w, l1_ada_b, l1_in_w, l1_gk_w2, l1_gk_b, l1_head_norm_w, l1_out_w, l2_norm_w, l2_ada_w, l2_ada_b, l2_in_w, l2_decay_logit, l2_head_norm_w, l2_out_w, l3_norm_w, l3_ada_w, l3_ada_b, l3_in_w, l3_conv_w, l3_conv_b, l3_gate_b, l3_head_norm_w, l3_out_w, final_norm_w):
    layers = (
        ("mlstm", l0_norm_w, l0_ada_w, l0_ada_b, l0_out_w, l0_head_norm_w, (l0_in_w, l0_conv_w, l0_conv_b, l0_gate_b)),
        ("gla", l1_norm_w, l1_ada_w, l1_ada_b, l1_out_w, l1_head_norm_w, (l1_in_w, l1_gk_w2, l1_gk_b)),
        ("retention", l2_norm_w, l2_ada_w, l2_ada_b, l2_out_w, l2_head_norm_w, (l2_in_w, l2_decay_logit)),
        ("mlstm", l3_norm_w, l3_ada_w, l3_ada_b, l3_out_w, l3_head_norm_w, (l3_in_w, l3_conv_w, l3_conv_b, l3_gate_b)),
    )
    b, _, d = x.shape
    n_ctx = ctx.shape[1]
    assert n_ctx % TOKEN_TILE == 0 and n_ctx % SCAN_CHUNK == 0 and x.shape[1] % SCAN_CHUNK == 0
    assert TOKEN_TILE == SCAN_CHUNK and TOKEN_TILE % GLA_CHUNK == 0
    l = n_ctx + x.shape[1]
    h = (ctx, x)
    cc = jnp.concatenate([c, c_ctx[None, :], jnp.zeros((7, d), F32)], axis=0)
    modsels = []
    for _, _, ada_w, ada_b, _, _, _ in layers:
        mod = _ada(cc, ada_w, ada_b)
        mod_lat = mod[:b].reshape(b, 1, 3, d)
        mod_ctx = jnp.broadcast_to(mod[b].reshape(1, 1, 3, d), (b, 1, 3, d))
        modsels.append(jnp.concatenate([mod_ctx, mod_lat], axis=1))
    specs = [_PROJ_SPEC[kind](l, n_ctx, *params) for kind, _, _, _, _, _, params in layers]
    proj_outs = _proj_call(specs[0], h, n_ctx, modsels[0], layers[0][1])
    for li, (kind, _, _, _, out_w, head_norm_w, params) in enumerate(layers):
        o, z = _MIX[kind](proj_outs, n_ctx, *params)
        center = kind != "gla"
        if li == len(layers) - 1:
            return _out_final_call(o, z, head_norm_w, out_w.astype(BF16), h, modsels[li], final_norm_w,
                                   n_ctx, center)
        h, proj_outs = _out_proj_call(o, z, head_norm_w, out_w.astype(BF16), h, modsels[li], n_ctx, center,
                                      specs[li + 1], modsels[li + 1], layers[li + 1][1])
```

```python
import functools

import jax
import jax.numpy as jnp
from jax import lax
from jax.experimental import pallas as pl
from jax.experimental.pallas import tpu as pltpu

N_HEADS = 4
GRID_W = 64
GLA_RANK = 16
GLA_TAU = 16.0
ROPE_BASE = 10000.0
NORM_EPS = 1e-6
NEG_INF = -1e30

TOKEN_TILE = 256
SCAN_CHUNK = 256
GLA_CHUNK = 128
GLA_SUB = 16
GLA_EXP_CLAMP = 80.0
GLA_INTRA_CHUNKS = 8
LANES = 128
VMEM_LIMIT = 56 * 1024 * 1024

F32 = jnp.float32
BF16 = jnp.bfloat16


def _silu(x):
    half = 0.5 * x
    return half + half * jnp.tanh(half)


def _log_sigmoid(x):
    return jnp.minimum(x, 0.0) - jnp.log(1.0 + jnp.exp(-jnp.abs(x)))


def _dot(a, b):
    return jnp.dot(a, b, preferred_element_type=F32)


def _split3(a):
    hi = a.astype(BF16)
    r1 = a - hi.astype(F32)
    mid = r1.astype(BF16)
    lo = (r1 - mid.astype(F32)).astype(BF16)
    return hi, mid, lo


def _dot01_left(m01, a):
    hi, mid, lo = _split3(a)
    return _dot(m01, lo) + _dot(m01, mid) + _dot(m01, hi)


def _tri_masks(n):
    row = lax.broadcasted_iota(jnp.int32, (n, n), 0)
    col = lax.broadcasted_iota(jnp.int32, (n, n), 1)
    return col <= row, col >= row


_HEAD_SLAB = lambda i, h: (i, h, 0, 0)


def _prefix_scan(x, lane, backward, op, identity):
    ax = x.ndim - 1
    n = x.shape[ax]
    sh = 1
    while sh < n:
        if backward:
            x = op(x, jnp.where(lane < n - sh, pltpu.roll(x, n - sh, ax), identity))
        else:
            x = op(x, jnp.where(lane >= sh, pltpu.roll(x, sh, ax), identity))
        sh *= 2
    return x


def _bw_chunk(step, n_ctx_chunks, n_chunks):
    return jnp.where(step < n_ctx_chunks, n_ctx_chunks - 1 - step, n_chunks - 1 - (step - n_ctx_chunks))


def _first_touch_groups(n_ctx_chunks, n_chunks):
    def bw_idx(s):
        return n_ctx_chunks - 1 - s if s < n_ctx_chunks else n_chunks - 1 - (s - n_ctx_chunks)
    bw_step = {bw_idx(s): s for s in range(n_chunks)}
    groups = []
    for s in range(n_chunks):
        flags = (s <= bw_step[s], s < bw_idx(s))
        if groups and groups[-1][2] == flags:
            groups[-1][1] = s + 1
        else:
            groups.append([s, s + 1, flags])
    return [tuple(g) for g in groups]


def _scan_steps(step, n_ctx_chunks, n_chunks, unroll):
    for start, stop, first in _first_touch_groups(n_ctx_chunks, n_chunks):
        lax.fori_loop(start, stop, lambda s, carry, first=first: step(s, first) or carry, 0,
                      unroll=unroll if (stop - start) % unroll == 0 else 1)


def _emit(o_ref, acc_ref, start, n, value, first):
    if first:
        acc_ref[pl.ds(start, n), :] = value
    else:
        o_ref[0, pl.ds(start, n), :] = (acc_ref[pl.ds(start, n), :] + value).astype(o_ref.dtype)


def _ada_body(c_ref, w_ref, b_ref, o_ref):
    s = _silu(c_ref[...])
    o_ref[...] = jnp.dot(s, w_ref[...], preferred_element_type=F32,
                         precision=lax.Precision.HIGHEST) + b_ref[...]


def _ada(cc, w, b):
    rows, d = cc.shape
    n = w.shape[1]
    tn = 512
    return pl.pallas_call(
        _ada_body,
        grid=(n // tn,),
        in_specs=[pl.BlockSpec((rows, d), lambda j: (0, 0)),
                  pl.BlockSpec((d, tn), lambda j: (0, j)),
                  pl.BlockSpec((1, tn), lambda j: (0, j))],
        out_specs=pl.BlockSpec((rows, tn), lambda j: (0, j)),
        out_shape=jax.ShapeDtypeStruct((rows, n), F32),
        name="ada",
    )(cc, w, b.reshape(1, n))


def _hidden_operands(h, n_ctx, tile_offset=0):
    if not isinstance(h, tuple):
        d = h.shape[2]
        return [h], [pl.BlockSpec((1, TOKEN_TILE, d), lambda i, t: (i, t + tile_offset, 0))]
    ctx, lat = h
    d = lat.shape[2]
    nct = n_ctx // TOKEN_TILE
    return [ctx, lat], [
        pl.BlockSpec((1, TOKEN_TILE, d), lambda i, t: (i, jnp.minimum(t + tile_offset, nct - 1), 0)),
        pl.BlockSpec((1, TOKEN_TILE, d), lambda i, t: (i, jnp.maximum(t + tile_offset - nct, 0), 0))]


def _with_hidden_tile(body, n_hidden, n_ctx_tiles, tile_offset=0):
    def kernel_fn(*refs):
        if n_hidden == 1:
            x = refs[0][0]
        else:
            is_ctx = pl.program_id(1) + tile_offset < n_ctx_tiles
            x = jnp.where(is_ctx, refs[0][0], refs[1][0])
        body(x, *refs[n_hidden:])
    return kernel_fn


def _modulated(x, mod_ref, nw_ref):
    y = x * lax.rsqrt(jnp.mean(x * x, axis=-1, keepdims=True) + NORM_EPS) * nw_ref[...]
    m = mod_ref[0, 0]
    return (y * (1.0 + m[1:2]) + m[0:1]).astype(BF16)


def _store_heads(ref, x):
    w = ref.shape[3]
    for hh in range(ref.shape[1]):
        ref[0, hh] = x[:, hh * w:(hh + 1) * w].astype(ref.dtype)


def _store_chunked_t(ref, x, chunk):
    xt = x.T
    for j in range(x.shape[0] // chunk):
        ref[0, j] = xt[:, j * chunk:(j + 1) * chunk].astype(ref.dtype)


GATE_ROWS = 16


def _proj_mlstm_body(x, mod_ref, nw_ref, w_ref, wg_ref, gb_ref, qk_ref, v_ref, z_ref, g_ref):
    u = _modulated(x, mod_ref, nw_ref)
    qk_w, di = qk_ref.shape[2], z_ref.shape[2]
    g = _dot(u, wg_ref[...]) + gb_ref[...]
    lane = lax.broadcasted_iota(jnp.int32, g.shape, 1)
    is_forget = (lane // N_HEADS) % 2 == 1
    g = jnp.where(is_forget, _log_sigmoid(g), g)
    _store_gate_rows(g_ref, g)
    qk_ref[0] = _dot(u, w_ref[:, :qk_w])
    _store_heads(v_ref, _dot(u, w_ref[:, qk_w:qk_w + di]))
    z_ref[0] = _dot(u, w_ref[:, qk_w + di:qk_w + 2 * di]).astype(BF16)


def _store_gate_rows(g_ref, g):
    c = g.shape[0]
    gt = g.T[:4 * N_HEADS]
    lane_row = lax.broadcasted_iota(jnp.int32, (1, c), 1)
    per_dir = []
    for dirn in range(2):
        i_r = gt[2 * dirn * N_HEADS:(2 * dirn + 1) * N_HEADS]
        a_r = gt[(2 * dirn + 1) * N_HEADS:(2 * dirn + 2) * N_HEADS]
        b_r = _prefix_scan(a_r, lane_row, bool(dirn), jnp.add, 0.0)
        u = i_r - b_r
        b_end = jnp.sum(a_r, axis=1, keepdims=True)
        g_max = jnp.max(b_end + u, axis=1, keepdims=True)
        per_dir.append((u, b_r, _prefix_scan(u, lane_row, bool(dirn), jnp.maximum, NEG_INF),
                        jnp.broadcast_to(b_end, u.shape), jnp.broadcast_to(g_max, u.shape)))
    for hh in range(N_HEADS):
        rows = [arr[hh:hh + 1] for dirn in range(2) for arr in per_dir[dirn]]
        g_ref[0, 0, hh] = jnp.concatenate(rows + [jnp.zeros((GATE_ROWS - len(rows), c), F32)], axis=0)


def _proj_gla_body(x, mod_ref, nw_ref, w_ref, wlr_ref, q_ref, kt_ref, v_ref, z_ref, code_ref, *, q_scale):
    u = _modulated(x, mod_ref, nw_ref)
    key, di = kt_ref.shape[2], z_ref.shape[2]
    _store_heads(q_ref, _dot(u, w_ref[:, :key]) * q_scale)
    _store_chunked_t(kt_ref, _dot(u, w_ref[:, key:2 * key]), GLA_CHUNK)
    _store_heads(v_ref, _dot(u, w_ref[:, 2 * key:2 * key + di]))
    z_ref[0] = _dot(u, w_ref[:, 2 * key + di:2 * key + 2 * di]).astype(BF16)
    code_ref[0] = _dot(u, wlr_ref[...]).astype(BF16)


def _rope(t, cos, sin):
    outs = []
    for g in range(t.shape[1] // LANES):
        tg = t[:, g * LANES:(g + 1) * LANES]
        cg = cos[:, (g % 2) * LANES:(g % 2 + 1) * LANES]
        sg = sin[:, (g % 2) * LANES:(g % 2 + 1) * LANES]
        outs.append(tg * cg + pltpu.roll(tg, LANES // 2, 1) * sg)
    return jnp.concatenate(outs, axis=1)


def _proj_ret_body(x, mod_ref, nw_ref, w_ref, cos_ref, sin_ref, q_ref, kt_ref, v_ref, z_ref, *, k_scale):
    u = _modulated(x, mod_ref, nw_ref)
    key, di = kt_ref.shape[2], z_ref.shape[2]
    cos = cos_ref[...]
    sin = sin_ref[...]
    _store_heads(q_ref, _rope(_dot(u, w_ref[:, :key]), cos, sin))
    _store_chunked_t(kt_ref, _rope(_dot(u, w_ref[:, key:2 * key]), cos, sin) * k_scale, SCAN_CHUNK)
    _store_heads(v_ref, _dot(u, w_ref[:, 2 * key:2 * key + di]))
    z_ref[0] = _dot(u, w_ref[:, 2 * key + di:2 * key + 2 * di]).astype(BF16)


def _proj_specs(spec, b, l, d):
    in_specs = [pl.BlockSpec((1, 1, 3, d), lambda i, t: (i, jnp.minimum(t, 1), 0, 0)),
                pl.BlockSpec((1, d), lambda i, t: (0, 0))]
    for a in spec["consts"]:
        in_specs.append(pl.BlockSpec(a.shape, lambda i, t, nd=a.ndim: (0,) * nd, pipeline_mode=pl.Buffered(1)))
    for a in spec["tables"]:
        in_specs.append(pl.BlockSpec((TOKEN_TILE, a.shape[1]), lambda i, t: (t, 0)))
    out_specs, out_shape = [], []
    for n, dt, chunk in spec["outs"]:
        if chunk is None:
            out_specs.append(pl.BlockSpec((1, TOKEN_TILE, n), lambda i, t: (i, t, 0)))
            out_shape.append(jax.ShapeDtypeStruct((b, l, n), dt))
        elif isinstance(chunk, tuple):
            out_specs.append(pl.BlockSpec((1, 1) + chunk, lambda i, t, z=(0,) * len(chunk): (i, t) + z))
            out_shape.append(jax.ShapeDtypeStruct((b, l // TOKEN_TILE) + chunk, dt))
        elif chunk == "heads":
            out_specs.append(pl.BlockSpec((1, N_HEADS, TOKEN_TILE, n // N_HEADS), lambda i, t: (i, 0, t, 0)))
            out_shape.append(jax.ShapeDtypeStruct((b, N_HEADS, l, n // N_HEADS), dt))
        else:
            out_specs.append(pl.BlockSpec((1, TOKEN_TILE // chunk, n, chunk), lambda i, t: (i, t, 0, 0)))
            out_shape.append(jax.ShapeDtypeStruct((b, l // chunk, n, chunk), dt))
    return in_specs, out_specs, out_shape


def _proj_call(spec, h, n_ctx, modsel, norm_w):
    h_args, h_specs = _hidden_operands(h, n_ctx)
    b, _, d = h_args[-1].shape
    l = sum(a.shape[1] for a in h_args)
    in_specs, out_specs, out_shape = _proj_specs(spec, b, l, d)
    return pl.pallas_call(
        _with_hidden_tile(spec["body"], len(h_args), n_ctx // TOKEN_TILE),
        grid=(b, l // TOKEN_TILE),
        in_specs=h_specs + in_specs,
        out_specs=out_specs,
        out_shape=out_shape,
        compiler_params=pltpu.CompilerParams(
            dimension_semantics=("parallel", "parallel"), vmem_limit_bytes=VMEM_LIMIT),
        name="proj_" + spec["name"],
    )(*h_args, modsel, norm_w.reshape(1, d), *spec["consts"], *spec["tables"])


def _conv_body(x_ref, w_ref, b_ref, o_ref, *, n_ctx, n_blocks, scale, transposed):
    blk = SCAN_CHUNK
    l = x_ref.shape[1]
    w = w_ref[...]
    bias = b_ref[...]
    rid = lax.broadcasted_iota(jnp.int32, (blk, 1), 0)

    def emit(chunk, start, acc):
        y = _silu(acc + bias) * scale
        if transposed:
            o_ref[0, chunk] = y.T.astype(BF16)
        else:
            o_ref[0, pl.ds(start, blk), :] = y.astype(BF16)

    def combine(a, bm, cc, first, last):
        return (jnp.where(first, 0.0, pltpu.roll(a, 1, 0)) + bm
                + jnp.where(last, 0.0, pltpu.roll(cc, blk - 1, 0)))

    xc = x_ref[0, 0:blk, :]
    emit(0, 0, combine(xc * w[3:4], xc * w[4:5], xc * w[5:6], rid == 0, rid == blk - 1))

    col = rid % GRID_W

    def lat_block(it, carry):
        base = pl.multiple_of(n_ctx + it * blk, blk)
        mid = x_ref[0, pl.ds(base, blk), :]
        up_edge = x_ref[0, pl.ds(pl.multiple_of(base - GRID_W, GRID_W), GRID_W), :]
        dn_start = pl.multiple_of(jnp.minimum(base + blk, l - GRID_W), GRID_W)
        dn_edge = x_ref[0, pl.ds(dn_start, GRID_W), :]
        up_edge = jnp.where(it > 0, up_edge, 0.0)
        dn_edge = jnp.where(it < n_blocks - 1, dn_edge, 0.0)
        up = jnp.concatenate([up_edge, mid[:blk - GRID_W]], axis=0)
        down = jnp.concatenate([mid[GRID_W:], dn_edge], axis=0)
        a = up * w[0:1] + mid * w[3:4] + down * w[6:7]
        bm = up * w[1:2] + mid * w[4:5] + down * w[7:8]
        cc = up * w[2:3] + mid * w[5:6] + down * w[8:9]
        emit(n_ctx // blk + it, base, combine(a, bm, cc, col == 0, col == GRID_W - 1))
        return carry

    lax.fori_loop(0, n_blocks, lat_block, 0)


def _conv(qk_raw, conv_w, conv_b, n_ctx, col_offset, width, scale, transposed):
    b, l, c = qk_raw.shape
    tc = 256
    blk = SCAN_CHUNK
    assert n_ctx == blk and (l - n_ctx) % blk == 0 and blk % GRID_W == 0
    off = col_offset // tc
    if transposed:
        out_spec = pl.BlockSpec((1, l // blk, tc, blk), lambda i, j: (i, 0, j, 0))
        out_shape = jax.ShapeDtypeStruct((b, l // blk, width, blk), BF16)
    else:
        out_spec = pl.BlockSpec((1, None, l, tc), lambda i, j: (i, j, 0, 0))
        out_shape = jax.ShapeDtypeStruct((b, width // tc, l, tc), BF16)
    return pl.pallas_call(
        functools.partial(_conv_body, n_ctx=n_ctx, n_blocks=(l - n_ctx) // blk, scale=scale,
                          transposed=transposed),
        grid=(b, width // tc),
        in_specs=[pl.BlockSpec((1, l, tc), lambda i, j: (i, 0, j + off)),
                  pl.BlockSpec((9, tc), lambda i, j: (0, j + off)),
                  pl.BlockSpec((1, tc), lambda i, j: (0, j + off))],
        out_specs=out_spec,
        out_shape=out_shape,
        compiler_params=pltpu.CompilerParams(
            dimension_semantics=("parallel", "parallel"), vmem_limit_bytes=VMEM_LIMIT),
        name="mlstm_conv_kt" if transposed else "mlstm_conv_q",
    )(qk_raw, conv_w.reshape(9, c), conv_b.reshape(1, c))


def _mlstm_scan_body(q_ref, kt_ref, v_ref, g_ref, o_ref, acc_ref, st_ref, n_ref, row_ref,
                     *, n_ctx_chunks, n_chunks):
    c = SCAN_CHUNK
    dv = v_ref.shape[2]
    st_ref[...] = jnp.zeros(st_ref.shape, F32)
    n_ref[...] = jnp.zeros(n_ref.shape, F32)
    masks = _tri_masks(c)

    gates = g_ref[0]
    token_rows, chunk_rows = [], []
    for dirn in range(2):
        u, b_r, cm = (gates[:, 5 * dirn + r:5 * dirn + r + 1] for r in range(3))
        b_end = gates[:, 5 * dirn + 3:5 * dirn + 4, 0:1]
        g_max = gates[:, 5 * dirn + 4:5 * dirn + 5, 0:1]
        m = jnp.zeros((1, 1, 1), F32)
        m_prev_l, m_new_l = [None] * n_chunks, [None] * n_chunks
        for s in range(n_chunks):
            idx = s
            if dirn:
                idx = n_ctx_chunks - 1 - s if s < n_ctx_chunks else n_chunks - 1 - (s - n_ctx_chunks)
            m_prev_l[idx] = m
            m = jnp.maximum(b_end[idx:idx + 1] + m, g_max[idx:idx + 1])
            m_new_l[idx] = m
        m_prev = jnp.concatenate(m_prev_l, axis=0)
        m_new = jnp.concatenate(m_new_l, axis=0)
        mm = jnp.maximum(m_prev, cm)
        token_rows += [u, jnp.exp(b_end + u - m_new), mm, b_r + mm]
        chunk_rows += [jnp.broadcast_to(jnp.exp(b_end + m_prev - m_new), (n_chunks, 1, c)),
                       jnp.broadcast_to(m_prev, (n_chunks, 1, c))]
    row_ref[...] = jnp.concatenate(token_rows + chunk_rows + [jnp.zeros((n_chunks, 4, c), F32)], axis=1)

    def lane_bcast_col(row):
        return jnp.broadcast_to(row, (LANES, c)).T

    def step(s, first):
        dirs = (0, 1)
        idxs = (s, _bw_chunk(s, n_ctx_chunks, n_chunks))
        starts = [pl.multiple_of(idx * c, c) for idx in idxs]
        q = [q_ref[0, pl.ds(st, c), :] for st in starts]
        kt = [kt_ref[0, idx] for idx in idxs]
        v = [v_ref[0, pl.ds(st, c), :] for st in starts]
        rows = [row_ref[idx] for idx in idxs]
        state = [st_ref[d] for d in dirs]
        nrm = [n_ref[d] for d in dirs]
        qk = [_dot(q[d], kt[d]) for d in dirs]
        mm_b = [lane_bcast_col(rows[d][4 * d + 2:4 * d + 3]) for d in dirs]
        bmm_b = [lane_bcast_col(rows[d][4 * d + 3:4 * d + 4]) for d in dirs]
        m_prev = [rows[d][9 + 2 * d:10 + 2 * d, 0:1] for d in dirs]
        decay = [rows[d][8 + 2 * d:9 + 2 * d, 0:1] for d in dirs]
        sc = [qk[d] * jnp.exp(jnp.where(masks[d], rows[d][4 * d:4 * d + 1]
                                        - jnp.concatenate([mm_b[d]] * (c // LANES), axis=1), NEG_INF))
              for d in dirs]
        w_inter = [jnp.exp(m_prev[d] - mm_b[d]) for d in dirs]
        intra = [_dot(sc[d].astype(BF16), v[d]) for d in dirs]
        inter = [_dot(q[d], state[d].astype(BF16)) for d in dirs]
        qn = [_dot(q[d], nrm[d].astype(BF16)) for d in dirs]
        wkt = [kt[d].astype(F32) * rows[d][4 * d + 1:4 * d + 2] for d in dirs]
        upd = [_dot(wkt[d].astype(BF16), v[d]) for d in dirs]
        for d in dirs:
            num = intra[d] + inter[d] * jnp.concatenate([w_inter[d]] * (dv // LANES), axis=1)
            den = jnp.sum(sc[d], axis=1, keepdims=True) + qn[d] * w_inter[d]
            rcp = 1.0 / jnp.maximum(jnp.abs(den), jnp.exp(-bmm_b[d]))
            _emit(o_ref, acc_ref, starts[d], c, num * jnp.concatenate([rcp] * (dv // LANES), axis=1), first[d])
        for d in dirs:
            st_ref[d] = decay[d] * state[d] + upd[d]
            n_ref[d] = decay[d] * nrm[d] + jnp.sum(wkt[d], axis=1, keepdims=True)

    _scan_steps(step, n_ctx_chunks, n_chunks, 1)


def _mlstm_scan(q, kt, v, gates, n_ctx):
    b, _, l, dk = q.shape
    dv = v.shape[3]
    n_chunks = l // SCAN_CHUNK
    return pl.pallas_call(
        functools.partial(_mlstm_scan_body, n_ctx_chunks=n_ctx // SCAN_CHUNK, n_chunks=n_chunks),
        grid=(b, N_HEADS),
        in_specs=[pl.BlockSpec((1, None, l, dk), _HEAD_SLAB),
                  pl.BlockSpec((1, n_chunks, dk, SCAN_CHUNK), lambda i, h: (i, 0, h, 0)),
                  pl.BlockSpec((1, None, l, dv), _HEAD_SLAB),
                  pl.BlockSpec((1, n_chunks, None, GATE_ROWS, SCAN_CHUNK), lambda i, h: (i, 0, h, 0, 0))],
        out_specs=pl.BlockSpec((1, None, l, dv), _HEAD_SLAB),
        out_shape=jax.ShapeDtypeStruct((b, N_HEADS, l, dv), BF16),
        scratch_shapes=[pltpu.VMEM((l, dv), F32),
                        pltpu.VMEM((2, dk, dv), F32),
                        pltpu.VMEM((2, dk, LANES), F32),
                        pltpu.VMEM((n_chunks, 16, SCAN_CHUNK), F32)],
        compiler_params=pltpu.CompilerParams(
            dimension_semantics=("parallel", "parallel"), vmem_limit_bytes=VMEM_LIMIT),
        name="mlstm_scan",
    )(q, kt, v, gates)


def _ret_scan_body(q_ref, kt_ref, v_ref, dl_ref, o_ref, acc_ref, st_ref, dm_ref, *, n_ctx_chunks, n_chunks):
    c = SCAN_CHUNK
    st_ref[...] = jnp.zeros(st_ref.shape, F32)
    masks = _tri_masks(c)
    row = lax.broadcasted_iota(jnp.int32, (c, c), 0)
    col = lax.broadcasted_iota(jnp.int32, (c, c), 1)
    dist = jnp.abs(row - col).astype(F32)
    rid = lax.broadcasted_iota(jnp.int32, (c, 1), 0).astype(F32)
    lid = lax.broadcasted_iota(jnp.int32, (1, c), 1).astype(F32)
    log_gamma = []
    for dirn in range(2):
        lg = _log_sigmoid(dl_ref[0, dirn])[0:1, 0:1]
        log_gamma.append(lg)
        dm_ref[dirn] = jnp.where(masks[dirn], jnp.exp(lg * dist), 0.0)

    pos_col = (rid + 1.0, c - rid)
    pos_row = (lid + 1.0, c - lid)

    def step(s, first):
        dirs = (0, 1)
        idxs = (s, _bw_chunk(s, n_ctx_chunks, n_chunks))
        starts = [pl.multiple_of(idx * c, c) for idx in idxs]
        q = [q_ref[0, pl.ds(st, c), :] for st in starts]
        kt = [kt_ref[0, idx] for idx in idxs]
        v = [v_ref[0, pl.ds(st, c), :] for st in starts]
        state = [st_ref[d] for d in dirs]
        sc = [(_dot(q[d], kt[d]) * dm_ref[d]).astype(BF16) for d in dirs]
        intra = [_dot(sc[d], v[d]) for d in dirs]
        inter = [_dot(q[d], state[d].astype(BF16)) for d in dirs]
        kdt = [(kt[d].astype(F32) * jnp.exp(log_gamma[d] * (c - pos_row[d]))).astype(BF16) for d in dirs]
        upd = [_dot(kdt[d], v[d]) for d in dirs]
        for d in dirs:
            _emit(o_ref, acc_ref, starts[d], c, intra[d] + inter[d] * jnp.exp(log_gamma[d] * pos_col[d]), first[d])
        for d in dirs:
            st_ref[d] = jnp.exp(log_gamma[d] * c) * state[d] + upd[d]

    _scan_steps(step, n_ctx_chunks, n_chunks, 4)


def _ret_scan(q, kt, v, dl, n_ctx):
    b, _, l, dk = q.shape
    dv = v.shape[3]
    n_chunks = l // SCAN_CHUNK
    return pl.pallas_call(
        functools.partial(_ret_scan_body, n_ctx_chunks=n_ctx // SCAN_CHUNK, n_chunks=n_chunks),
        grid=(b, N_HEADS),
        in_specs=[pl.BlockSpec((1, None, l, dk), _HEAD_SLAB),
                  pl.BlockSpec((1, n_chunks, dk, SCAN_CHUNK), lambda i, h: (i, 0, h, 0)),
                  pl.BlockSpec((1, None, l, dv), _HEAD_SLAB),
                  pl.BlockSpec((1, 2, 8, LANES), lambda i, h: (h, 0, 0, 0))],
        out_specs=pl.BlockSpec((1, None, l, dv), _HEAD_SLAB),
        out_shape=jax.ShapeDtypeStruct((b, N_HEADS, l, dv), BF16),
        scratch_shapes=[pltpu.VMEM((l, dv), F32),
                        pltpu.VMEM((2, dk, dv), F32),
                        pltpu.VMEM((2, SCAN_CHUNK, SCAN_CHUNK), F32)],
        compiler_params=pltpu.CompilerParams(
            dimension_semantics=("parallel", "parallel"), vmem_limit_bytes=VMEM_LIMIT),
        name="ret_scan",
    )(q, kt, v, dl)


def _gla_scan_body(q_ref, kt_ref, v_ref, code_ref, w2_ref, b2_ref, o_ref,
                   acc_ref, st_ref, sq_ref, kdt_ref, e_ref, m01_ref, *, n_ctx_chunks, n_chunks):
    c = GLA_CHUNK
    t = GLA_SUB
    n_sub = c // t
    dk = q_ref.shape[2]
    dv = v_ref.shape[2]
    st_ref[...] = jnp.zeros(st_ref.shape, F32)
    masks = _tri_masks(c)
    for d in range(2):
        m01_ref[d] = masks[d].astype(BF16)
    lid = lax.broadcasted_iota(jnp.int32, (1, c), 1)

    def intra(it, carry, first_chunk, group):
        chains = [(first_chunk + it * group + j, dirn) for j in range(group) for dirn in range(2)]
        starts = [pl.multiple_of(idx * c, c) for idx, _ in chains]
        qf = [q_ref[0, pl.ds(st, c), :].astype(F32) for st in starts]
        ktf = [kt_ref[0, idx].astype(F32) for idx, _ in chains]
        code = [code_ref[0, pl.ds(st, c), :] for st in starts]
        a = [_log_sigmoid(_dot(code[k], w2_ref[d]) + b2_ref[d]) * (1.0 / GLA_TAU) for k, (_, d) in enumerate(chains)]
        parts = [_split3(x) for x in a]
        m01 = [m01_ref[d] for d in range(2)]
        m01_twice = [jnp.concatenate([m, m], axis=1) for m in m01]
        b = [_dot(m01_twice[d], jnp.concatenate([parts[k][2], parts[k][1]], axis=0)) for k, (_, d) in enumerate(chains)]
        b = [b[k] + _dot(m01[d], parts[k][0]) for k, (_, d) in enumerate(chains)]
        qp, kp, kdt, e_end = [], [], [], []
        for k, (_, dirn) in enumerate(chains):
            refs = []
            for sb in range(n_sub):
                r0 = sb * t + (t - 1 if dirn else 0)
                refs.append(b[k][r0:r0 + 1] - a[k][r0:r0 + 1])
            own = jnp.concatenate([jnp.broadcast_to(r, (t, dk)) for r in refs], axis=0)
            b_rel = b[k] - own
            q_own = qf[k] * jnp.exp(b_rel)
            k_rel_t = (ktf[k] * jnp.exp(jnp.minimum(-(b_rel.T), GLA_EXP_CLAMP))).astype(BF16)
            r_end = 0 if dirn else c - 1
            b_end = b[k][r_end:r_end + 1]
            kdt.append((ktf[k] * jnp.exp((b_end - b[k]).T)).astype(BF16))
            e_end.append(jnp.broadcast_to(jnp.exp(b_end), (8, dk)))
            col_blocks = []
            for sb in range(n_sub):
                pieces = []
                for j in range(n_sub):
                    if (j <= sb) if dirn else (j >= sb):
                        qj = q_own[j * t:(j + 1) * t]
                        pieces.append(qj if j == sb else qj * jnp.exp(refs[j] - refs[sb]))
                    else:
                        pieces.append(jnp.zeros((t, dk), F32))
                col_blocks.append(jnp.concatenate(pieces, axis=0).astype(BF16))
            qp.append(col_blocks)
            kp.append(jnp.concatenate(
                [jnp.where(jnp.logical_and(lid >= sb * t, lid < (sb + 1) * t), k_rel_t, 0.0)
                 for sb in range(n_sub)], axis=0))
        s = [_dot(jnp.concatenate(qp[k], axis=1), kp[k]) for k in range(len(chains))]
        for k, (idx, dirn) in enumerate(chains):
            sq_ref[dirn, pl.ds(starts[k], c), 0:c] = jnp.where(masks[dirn], s[k], 0.0).astype(BF16)
            sq_ref[dirn, pl.ds(starts[k], c), c:2 * c] = qp[k][n_sub - 1 if dirn else 0]
            kdt_ref[dirn, idx] = kdt[k]
            e_ref[dirn, idx] = e_end[k]
        return carry

    n_full = n_chunks // GLA_INTRA_CHUNKS
    lax.fori_loop(0, n_full, functools.partial(intra, first_chunk=0, group=GLA_INTRA_CHUNKS), 0)
    rest = n_chunks - n_full * GLA_INTRA_CHUNKS
    if rest:
        intra(0, 0, n_full * GLA_INTRA_CHUNKS, rest)

    def step(s, first):
        dirs = (0, 1)
        idxs = (s, _bw_chunk(s, n_ctx_chunks, n_chunks))
        starts = [pl.multiple_of(idx * c, c) for idx in idxs]
        v = [v_ref[0, pl.ds(st, c), :] for st in starts]
        state = [st_ref[d] for d in dirs]
        out = [_dot(sq_ref[d, pl.ds(starts[d], c), :],
                    jnp.concatenate([v[d], state[d].astype(BF16)], axis=0)) for d in dirs]
        upd = [_dot(kdt_ref[d, idxs[d]], v[d]) for d in dirs]
        e_cols = [jnp.concatenate([e_ref[d, idxs[d]]] * (dk // 8), axis=0).T for d in dirs]
        for d in dirs:
            _emit(o_ref, acc_ref, starts[d], c, out[d], first[d])
        for d in dirs:
            st_ref[d] = jnp.concatenate([e_cols[d]] * (dv // dk), axis=1) * state[d] + upd[d]

    _scan_steps(step, n_ctx_chunks, n_chunks, 4)


def _gla_scan(q, kt, v, code, w2, b2, n_ctx):
    b, _, l, dk = q.shape
    dv = v.shape[3]
    n_chunks = l // GLA_CHUNK
    return pl.pallas_call(
        functools.partial(_gla_scan_body, n_ctx_chunks=n_ctx // GLA_CHUNK, n_chunks=n_chunks),
        grid=(b, N_HEADS),
        in_specs=[pl.BlockSpec((1, None, l, dk), _HEAD_SLAB),
                  pl.BlockSpec((1, n_chunks, dk, GLA_CHUNK), lambda i, h: (i, 0, h, 0)),
                  pl.BlockSpec((1, None, l, dv), _HEAD_SLAB),
                  pl.BlockSpec((1, l, LANES), lambda i, h: (i, 0, 0)),
                  pl.BlockSpec((2, LANES, dk), lambda i, h: (0, 0, h)),
                  pl.BlockSpec((2, 1, dk), lambda i, h: (0, 0, h))],
        out_specs=pl.BlockSpec((1, None, l, dv), _HEAD_SLAB),
        out_shape=jax.ShapeDtypeStruct((b, N_HEADS, l, dv), BF16),
        scratch_shapes=[pltpu.VMEM((l, dv), F32),
                        pltpu.VMEM((2, dk, dv), F32),
                        pltpu.VMEM((2, l, GLA_CHUNK + dk), BF16),
                        pltpu.VMEM((2, n_chunks, dk, GLA_CHUNK), BF16),
                        pltpu.VMEM((2, n_chunks, 8, dk), F32),
                        pltpu.VMEM((2, GLA_CHUNK, GLA_CHUNK), BF16)],
        compiler_params=pltpu.CompilerParams(
            dimension_semantics=("parallel", "parallel"), vmem_limit_bytes=VMEM_LIMIT),
        name="gla_scan",
    )(q, kt, v, code, w2, b2)


def _out_core(x, o_ref, z_ref, hw_ref, ow_ref, mod_ref, center):
    dv = o_ref.shape[3]
    proj = None
    for hh in range(N_HEADS):
        cols = slice(hh * dv, (hh + 1) * dv)
        yh = o_ref[0, hh].astype(F32)
        if center:
            yh = yh - jnp.mean(yh, axis=-1, keepdims=True)
        yn = yh * lax.rsqrt(jnp.mean(yh * yh, axis=-1, keepdims=True) + NORM_EPS) * hw_ref[:, cols]
        a = (yn * _silu(z_ref[0, :, cols].astype(F32))).astype(BF16)
        term = _dot(a, ow_ref[cols, :])
        proj = term if proj is None else proj + term
    return x + mod_ref[0, 0][2:3] * proj


def _out_final_body(x, o_ref, z_ref, hw_ref, ow_ref, mod_ref, fw_ref, out_ref, *, center):
    hn = _out_core(x, o_ref, z_ref, hw_ref, ow_ref, mod_ref, center)
    out_ref[0] = hn * lax.rsqrt(jnp.mean(hn * hn, axis=-1, keepdims=True) + NORM_EPS) * fw_ref[...]


def _out_proj_body(x, o_ref, z_ref, hw_ref, ow_ref, mod_ref, *rest, center, proj_body, n_proj_in):
    proj_in, h_out_ref, proj_out = rest[:n_proj_in], rest[n_proj_in], rest[n_proj_in + 1:]
    hn = _out_core(x, o_ref, z_ref, hw_ref, ow_ref, mod_ref, center)
    h_out_ref[0] = hn
    proj_body(hn, *proj_in, *proj_out)


def _out_operands(o, z, head_norm_w, out_w, modsel, skip):
    _, n_heads, _, dv = o.shape
    di = n_heads * dv
    d = out_w.shape[1]
    tok = lambda i, t: (i, t + skip, 0)
    specs = [pl.BlockSpec((1, n_heads, TOKEN_TILE, dv), lambda i, t: (i, 0, t + skip, 0)),
             pl.BlockSpec((1, TOKEN_TILE, di), tok),
             pl.BlockSpec((1, di), lambda i, t: (0, 0)),
             pl.BlockSpec((di, d), lambda i, t: (0, 0), pipeline_mode=pl.Buffered(1)),
             pl.BlockSpec((1, 1, 3, d), lambda i, t: (i, jnp.minimum(t + skip, 1), 0, 0))]
    return [o, z, head_norm_w.reshape(1, di), out_w, modsel], specs


def _out_final_call(o, z, head_norm_w, out_w, h, modsel, final_w, n_ctx, center):
    b, _, l, _ = o.shape
    d = out_w.shape[1]
    skip = n_ctx // TOKEN_TILE
    nt = l // TOKEN_TILE - skip
    h_args, h_specs = _hidden_operands(h, n_ctx, skip)
    args, specs = _out_operands(o, z, head_norm_w, out_w, modsel, skip)
    return pl.pallas_call(
        _with_hidden_tile(functools.partial(_out_final_body, center=center),
                          len(h_args), n_ctx // TOKEN_TILE, skip),
        grid=(b, nt),
        in_specs=h_specs + specs + [pl.BlockSpec((1, d), lambda i, t: (0, 0))],
        out_specs=pl.BlockSpec((1, TOKEN_TILE, d), lambda i, t: (i, t, 0)),
        out_shape=jax.ShapeDtypeStruct((b, nt * TOKEN_TILE, d), F32),
        compiler_params=pltpu.CompilerParams(
            dimension_semantics=("parallel", "parallel"), vmem_limit_bytes=VMEM_LIMIT),
        name="out_final",
    )(*h_args, *args, final_w.reshape(1, d))


def _out_proj_call(o, z, head_norm_w, out_w, h, modsel, n_ctx, center, spec, next_modsel, next_norm_w):
    b, _, l, _ = o.shape
    d = out_w.shape[1]
    h_args, h_specs = _hidden_operands(h, n_ctx)
    args, specs = _out_operands(o, z, head_norm_w, out_w, modsel, 0)
    p_in, p_out_specs, p_out_shape = _proj_specs(spec, b, l, d)
    res = pl.pallas_call(
        _with_hidden_tile(functools.partial(_out_proj_body, center=center, proj_body=spec["body"],
                                            n_proj_in=len(p_in)),
                          len(h_args), n_ctx // TOKEN_TILE),
        grid=(b, l // TOKEN_TILE),
        in_specs=h_specs + specs + p_in,
        out_specs=[pl.BlockSpec((1, TOKEN_TILE, d), lambda i, t: (i, t, 0))] + p_out_specs,
        out_shape=[jax.ShapeDtypeStruct((b, l, d), F32)] + p_out_shape,
        compiler_params=pltpu.CompilerParams(
            dimension_semantics=("parallel", "parallel"), vmem_limit_bytes=VMEM_LIMIT),
        name="out_proj_" + spec["name"],
    )(*h_args, *args, next_modsel, next_norm_w.reshape(1, d), *spec["consts"], *spec["tables"])
    return res[0], res[1:]


def _pad_cols(w, n):
    return jnp.pad(w, ((0, 0), (0, n - w.shape[1])))


def _mlstm_proj_spec(l, n_ctx, in_w, conv_w, conv_b, gate_b):
    d = in_w.shape[0]
    di = 2 * d
    qk_w = 2 * d
    wg = _pad_cols(in_w[:, qk_w + 2 * di:], LANES).astype(BF16)
    gb = _pad_cols(gate_b.reshape(1, -1), LANES)
    return dict(body=_proj_mlstm_body, consts=[in_w.astype(BF16), wg, gb], tables=[], name="mlstm",
                outs=[(qk_w, F32, None), (di, BF16, "heads"), (di, BF16, None),
                      (0, F32, (N_HEADS, GATE_ROWS, TOKEN_TILE))])


def _mlstm_mix(outs, n_ctx, in_w, conv_w, conv_b, gate_b):
    qk_raw, v, z, g = outs
    b, l, qk_w = qk_raw.shape
    head_qk = qk_w // (2 * N_HEADS)
    q = _conv(qk_raw, conv_w, conv_b, n_ctx, 0, qk_w // 2, 1.0, False)
    kt = _conv(qk_raw, conv_w, conv_b, n_ctx, qk_w // 2, qk_w // 2, head_qk ** -0.5, True)
    return _mlstm_scan(q, kt, v, g, n_ctx), z


def _gla_proj_spec(l, n_ctx, in_w, gk_w2, gk_b):
    d = in_w.shape[0]
    di = 2 * d
    key = gk_w2.shape[2]
    assert key // N_HEADS == LANES
    wlr = _pad_cols(in_w[:, 2 * key + 2 * di:], LANES).astype(BF16)
    return dict(body=functools.partial(_proj_gla_body, q_scale=(key // N_HEADS) ** -0.5),
                consts=[in_w.astype(BF16), wlr], tables=[], name="gla",
                outs=[(key, BF16, "heads"), (key, BF16, GLA_CHUNK), (di, BF16, "heads"), (di, BF16, None),
                      (LANES, BF16, None)])


def _gla_mix(outs, n_ctx, in_w, gk_w2, gk_b):
    q, kt, v, z, code = outs
    key = gk_w2.shape[2]
    w2 = jnp.zeros((2, LANES, key), F32)
    w2 = w2.at[0, :GLA_RANK].set(gk_w2[0]).at[1, GLA_RANK:2 * GLA_RANK].set(gk_w2[1]).astype(BF16)
    return _gla_scan(q, kt, v, code, w2, gk_b.reshape(2, 1, key), n_ctx), z


def _rope_tables(n_ctx, n_lat, head_k):
    quarter = head_k // 4
    pos = jnp.arange(n_lat, dtype=jnp.int32)
    inv_freq = ROPE_BASE ** (-jnp.arange(quarter, dtype=F32) / quarter)
    cos_parts, sin_parts = [], []
    for p in ((pos // GRID_W).astype(F32), (pos % GRID_W).astype(F32)):
        ang = p[:, None] * inv_freq[None, :]
        cos_parts += [jnp.cos(ang), jnp.cos(ang)]
        sin_parts += [-jnp.sin(ang), jnp.sin(ang)]
    cos = jnp.concatenate(cos_parts, axis=1)
    sin = jnp.concatenate(sin_parts, axis=1)
    cos = jnp.concatenate([jnp.ones((n_ctx, head_k), F32), cos], axis=0)
    sin = jnp.concatenate([jnp.zeros((n_ctx, head_k), F32), sin], axis=0)
    return cos, sin


def _ret_proj_spec(l, n_ctx, in_w, decay_logit):
    d = in_w.shape[0]
    di = 2 * d
    key = (in_w.shape[1] - 2 * di) // 2
    head_k = key // N_HEADS
    cos, sin = _rope_tables(n_ctx, l - n_ctx, head_k)
    return dict(body=functools.partial(_proj_ret_body, k_scale=head_k ** -0.5),
                consts=[in_w.astype(BF16)], tables=[cos, sin], name="ret",
                outs=[(key, BF16, "heads"), (key, BF16, SCAN_CHUNK), (di, BF16, "heads"), (di, BF16, None)])


def _ret_mix(outs, n_ctx, in_w, decay_logit):
    q, kt, v, z = outs
    dl = jnp.broadcast_to(decay_logit.astype(F32).T[:, :, None, None], (N_HEADS, 2, 8, LANES))
    return _ret_scan(q, kt, v, dl, n_ctx), z


_PROJ_SPEC = {"mlstm": _mlstm_proj_spec, "gla": _gla_proj_spec, "retention": _ret_proj_spec}
_MIX = {"mlstm": _mlstm_mix, "gla": _gla_mix, "retention": _ret_mix}


def kernel(x, c, ctx, c_ctx, l0_norm_w, l0_ada_w, l0_ada_b, l0_in_w, l0_conv_w, l0_conv_b, l0_gate_b, l0_head_norm_w, l0_out_w, l1_norm_w, l1_ada_w, l1_ada_b, l1_in_w, l1_gk_w2, l1_gk_b, l1_head_norm_w, l1_out_w, l2_norm_w, l2_ada_w, l2_ada_b, l2_in_w, l2_decay_logit, l2_head_norm_w, l2_out_w, l3_norm_w, l3_ada_w, l3_ada_b, l3_in_w, l3_conv_w, l3_conv_b, l3_gate_b, l3_head_norm_w, l3_out_w, final_norm_w):
    layers = (
        ("mlstm", l0_norm_w, l0_ada_w, l0_ada_b, l0_out_w, l0_head_norm_w, (l0_in_w, l0_conv_w, l0_conv_b, l0_gate_b)),
        ("gla", l1_norm_w, l1_ada_w, l1_ada_b, l1_out_w, l1_head_norm_w, (l1_in_w, l1_gk_w2, l1_gk_b)),
        ("retention", l2_norm_w, l2_ada_w, l2_ada_b, l2_out_w, l2_head_norm_w, (l2_in_w, l2_decay_logit)),
        ("mlstm", l3_norm_w, l3_ada_w, l3_ada_b, l3_out_w, l3_head_norm_w, (l3_in_w, l3_conv_w, l3_conv_b, l3_gate_b)),
    )
    b, _, d = x.shape
    n_ctx = ctx.shape[1]
    assert n_ctx % TOKEN_TILE == 0 and n_ctx % SCAN_CHUNK == 0 and x.shape[1] % SCAN_CHUNK == 0
    assert TOKEN_TILE == SCAN_CHUNK and TOKEN_TILE % GLA_CHUNK == 0
    l = n_ctx + x.shape[1]
    h = (ctx, x)
    cc = jnp.concatenate([c, c_ctx[None, :], jnp.zeros((7, d), F32)], axis=0)
    modsels = []
    for _, _, ada_w, ada_b, _, _, _ in layers:
        mod = _ada(cc, ada_w, ada_b)
        mod_lat = mod[:b].reshape(b, 1, 3, d)
        mod_ctx = jnp.broadcast_to(mod[b].reshape(1, 1, 3, d), (b, 1, 3, d))
        modsels.append(jnp.concatenate([mod_ctx, mod_lat], axis=1))
    specs = [_PROJ_SPEC[kind](l, n_ctx, *params) for kind, _, _, _, _, _, params in layers]
    proj_outs = _proj_call(specs[0], h, n_ctx, modsels[0], layers[0][1])
    for li, (kind, _, _, _, out_w, head_norm_w, params) in enumerate(layers):
        o, z = _MIX[kind](proj_outs, n_ctx, *params)
        center = kind != "gla"
        if li == len(layers) - 1:
            return _out_final_call(o, z, head_norm_w, out_w.astype(BF16), h, modsels[li], final_norm_w,
                                   n_ctx, center)
        h, proj_outs = _out_proj_call(o, z, head_norm_w, out_w.astype(BF16), h, modsels[li], n_ctx, center,
                                      specs[li + 1], modsels[li + 1], layers[li + 1][1])
```

```python
import functools

import jax
import jax.numpy as jnp
from jax import lax
from jax.experimental import pallas as pl
from jax.experimental.pallas import tpu as pltpu

N_HEADS = 4
GRID_W = 64
GLA_RANK = 16
GLA_TAU = 16.0
ROPE_BASE = 10000.0
NORM_EPS = 1e-6
NEG_INF = -1e30

TOKEN_TILE = 256
SCAN_CHUNK = 256
GLA_CHUNK = 128
GLA_SUB = 16
GLA_EXP_CLAMP = 80.0
GLA_INTRA_CHUNKS = 8
LANES = 128
VMEM_LIMIT = 56 * 1024 * 1024

F32 = jnp.float32
BF16 = jnp.bfloat16


def _silu(x):
    half = 0.5 * x
    return half + half * jnp.tanh(half)


def _log_sigmoid(x):
    return jnp.minimum(x, 0.0) - jnp.log(1.0 + jnp.exp(-jnp.abs(x)))


def _dot(a, b):
    return jnp.dot(a, b, preferred_element_type=F32)


def _split3(a):
    hi = a.astype(BF16)
    r1 = a - hi.astype(F32)
    mid = r1.astype(BF16)
    lo = (r1 - mid.astype(F32)).astype(BF16)
    return hi, mid, lo


def _dot01_left(m01, a):
    hi, mid, lo = _split3(a)
    return _dot(m01, lo) + _dot(m01, mid) + _dot(m01, hi)


def _tri_masks(n):
    row = lax.broadcasted_iota(jnp.int32, (n, n), 0)
    col = lax.broadcasted_iota(jnp.int32, (n, n), 1)
    return col <= row, col >= row


_HEAD_SLAB = lambda i, h: (i, h, 0, 0)


def _prefix_scan(x, lane, backward, op, identity):
    ax = x.ndim - 1
    n = x.shape[ax]
    sh = 1
    while sh < n:
        if backward:
            x = op(x, jnp.where(lane < n - sh, pltpu.roll(x, n - sh, ax), identity))
        else:
            x = op(x, jnp.where(lane >= sh, pltpu.roll(x, sh, ax), identity))
        sh *= 2
    return x


def _bw_chunk(step, n_ctx_chunks, n_chunks):
    return jnp.where(step < n_ctx_chunks, n_ctx_chunks - 1 - step, n_chunks - 1 - (step - n_ctx_chunks))


def _first_touch_groups(n_ctx_chunks, n_chunks):
    def bw_idx(s):
        return n_ctx_chunks - 1 - s if s < n_ctx_chunks else n_chunks - 1 - (s - n_ctx_chunks)
    bw_step = {bw_idx(s): s for s in range(n_chunks)}
    groups = []
    for s in range(n_chunks):
        flags = (s <= bw_step[s], s < bw_idx(s))
        if groups and groups[-1][2] == flags:
            groups[-1][1] = s + 1
        else:
            groups.append([s, s + 1, flags])
    return [tuple(g) for g in groups]


def _scan_steps(step, n_ctx_chunks, n_chunks, unroll):
    for start, stop, first in _first_touch_groups(n_ctx_chunks, n_chunks):
        lax.fori_loop(start, stop, lambda s, carry, first=first: step(s, first) or carry, 0,
                      unroll=unroll if (stop - start) % unroll == 0 else 1)


def _emit(o_ref, acc_ref, start, n, value, first):
    if first:
        acc_ref[pl.ds(start, n), :] = value
    else:
        o_ref[0, pl.ds(start, n), :] = (acc_ref[pl.ds(start, n), :] + value).astype(o_ref.dtype)


def _ada_body(c_ref, w_ref, b_ref, o_ref):
    s = _silu(c_ref[...])
    o_ref[...] = jnp.dot(s, w_ref[...], preferred_element_type=F32,
                         precision=lax.Precision.HIGHEST) + b_ref[...]


def _ada(cc, w, b):
    rows, d = cc.shape
    n = w.shape[1]
    tn = 512
    return pl.pallas_call(
        _ada_body,
        grid=(n // tn,),
        in_specs=[pl.BlockSpec((rows, d), lambda j: (0, 0)),
                  pl.BlockSpec((d, tn), lambda j: (0, j)),
                  pl.BlockSpec((1, tn), lambda j: (0, j))],
        out_specs=pl.BlockSpec((rows, tn), lambda j: (0, j)),
        out_shape=jax.ShapeDtypeStruct((rows, n), F32),
        name="ada",
    )(cc, w, b.reshape(1, n))


def _hidden_operands(h, n_ctx, tile_offset=0):
    if not isinstance(h, tuple):
        d = h.shape[2]
        return [h], [pl.BlockSpec((1, TOKEN_TILE, d), lambda i, t: (i, t + tile_offset, 0))]
    ctx, lat = h
    d = lat.shape[2]
    nct = n_ctx // TOKEN_TILE
    return [ctx, lat], [
        pl.BlockSpec((1, TOKEN_TILE, d), lambda i, t: (i, jnp.minimum(t + tile_offset, nct - 1), 0)),
        pl.BlockSpec((1, TOKEN_TILE, d), lambda i, t: (i, jnp.maximum(t + tile_offset - nct, 0), 0))]


def _with_hidden_tile(body, n_hidden, n_ctx_tiles, tile_offset=0):
    def kernel_fn(*refs):
        if n_hidden == 1:
            x = refs[0][0]
        else:
            is_ctx = pl.program_id(1) + tile_offset < n_ctx_tiles
            x = jnp.where(is_ctx, refs[0][0], refs[1][0])
        body(x, *refs[n_hidden:])
    return kernel_fn


def _modulated(x, mod_ref, nw_ref):
    y = x * lax.rsqrt(jnp.mean(x * x, axis=-1, keepdims=True) + NORM_EPS) * nw_ref[...]
    m = mod_ref[0, 0]
    return (y * (1.0 + m[1:2]) + m[0:1]).astype(BF16)


def _store_heads(ref, x):
    w = ref.shape[3]
    for hh in range(ref.shape[1]):
        ref[0, hh] = x[:, hh * w:(hh + 1) * w].astype(ref.dtype)


def _store_chunked_t(ref, x, chunk):
    xt = x.T
    for j in range(x.shape[0] // chunk):
        ref[0, j] = xt[:, j * chunk:(j + 1) * chunk].astype(ref.dtype)


GATE_ROWS = 16


def _proj_mlstm_body(x, mod_ref, nw_ref, w_ref, wg_ref, gb_ref, qk_ref, v_ref, z_ref, g_ref):
    u = _modulated(x, mod_ref, nw_ref)
    qk_w, di = qk_ref.shape[2], z_ref.shape[2]
    g = _dot(u, wg_ref[...]) + gb_ref[...]
    lane = lax.broadcasted_iota(jnp.int32, g.shape, 1)
    is_forget = (lane // N_HEADS) % 2 == 1
    g = jnp.where(is_forget, _log_sigmoid(g), g)
    _store_gate_rows(g_ref, g)
    qk_ref[0] = _dot(u, w_ref[:, :qk_w])
    _store_heads(v_ref, _dot(u, w_ref[:, qk_w:qk_w + di]))
    z_ref[0] = _dot(u, w_ref[:, qk_w + di:qk_w + 2 * di]).astype(BF16)


def _store_gate_rows(g_ref, g):
    c = g.shape[0]
    gt = g.T[:4 * N_HEADS]
    lane_row = lax.broadcasted_iota(jnp.int32, (1, c), 1)
    per_dir = []
    for dirn in range(2):
        i_r = gt[2 * dirn * N_HEADS:(2 * dirn + 1) * N_HEADS]
        a_r = gt[(2 * dirn + 1) * N_HEADS:(2 * dirn + 2) * N_HEADS]
        b_r = _prefix_scan(a_r, lane_row, bool(dirn), jnp.add, 0.0)
        u = i_r - b_r
        b_end = jnp.sum(a_r, axis=1, keepdims=True)
        g_max = jnp.max(b_end + u, axis=1, keepdims=True)
        per_dir.append((u, b_r, _prefix_scan(u, lane_row, bool(dirn), jnp.maximum, NEG_INF),
                        jnp.broadcast_to(b_end, u.shape), jnp.broadcast_to(g_max, u.shape)))
    for hh in range(N_HEADS):
        rows = [arr[hh:hh + 1] for dirn in range(2) for arr in per_dir[dirn]]
        g_ref[0, 0, hh] = jnp.concatenate(rows + [jnp.zeros((GATE_ROWS - len(rows), c), F32)], axis=0)


def _proj_gla_body(x, mod_ref, nw_ref, w_ref, wlr_ref, q_ref, kt_ref, v_ref, z_ref, code_ref, *, q_scale):
    u = _modulated(x, mod_ref, nw_ref)
    key, di = kt_ref.shape[2], z_ref.shape[2]
    _store_heads(q_ref, _dot(u, w_ref[:, :key]) * q_scale)
    _store_chunked_t(kt_ref, _dot(u, w_ref[:, key:2 * key]), GLA_CHUNK)
    _store_heads(v_ref, _dot(u, w_ref[:, 2 * key:2 * key + di]))
    z_ref[0] = _dot(u, w_ref[:, 2 * key + di:2 * key + 2 * di]).astype(BF16)
    code_ref[0] = _dot(u, wlr_ref[...]).astype(BF16)


def _rope(t, cos, sin):
    outs = []
    for g in range(t.shape[1] // LANES):
        tg = t[:, g * LANES:(g + 1) * LANES]
        cg = cos[:, (g % 2) * LANES:(g % 2 + 1) * LANES]
        sg = sin[:, (g % 2) * LANES:(g % 2 + 1) * LANES]
        outs.append(tg * cg + pltpu.roll(tg, LANES // 2, 1) * sg)
    return jnp.concatenate(outs, axis=1)


def _proj_ret_body(x, mod_ref, nw_ref, w_ref, cos_ref, sin_ref, q_ref, kt_ref, v_ref, z_ref, *, k_scale):
    u = _modulated(x, mod_ref, nw_ref)
    key, di = kt_ref.shape[2], z_ref.shape[2]
    cos = cos_ref[...]
    sin = sin_ref[...]
    _store_heads(q_ref, _rope(_dot(u, w_ref[:, :key]), cos, sin))
    _store_chunked_t(kt_ref, _rope(_dot(u, w_ref[:, key:2 * key]), cos, sin) * k_scale, SCAN_CHUNK)
    _store_heads(v_ref, _dot(u, w_ref[:, 2 * key:2 * key + di]))
    z_ref[0] = _dot(u, w_ref[:, 2 * key + di:2 * key + 2 * di]).astype(BF16)


def _proj_specs(spec, b, l, d):
    in_specs = [pl.BlockSpec((1, 1, 3, d), lambda i, t: (i, jnp.minimum(t, 1), 0, 0)),
                pl.BlockSpec((1, d), lambda i, t: (0, 0))]
    for a in spec["consts"]:
        in_specs.append(pl.BlockSpec(a.shape, lambda i, t, nd=a.ndim: (0,) * nd, pipeline_mode=pl.Buffered(1)))
    for a in spec["tables"]:
        in_specs.append(pl.BlockSpec((TOKEN_TILE, a.shape[1]), lambda i, t: (t, 0)))
    out_specs, out_shape = [], []
    for n, dt, chunk in spec["outs"]:
        if chunk is None:
            out_specs.append(pl.BlockSpec((1, TOKEN_TILE, n), lambda i, t: (i, t, 0)))
            out_shape.append(jax.ShapeDtypeStruct((b, l, n), dt))
        elif isinstance(chunk, tuple):
            out_specs.append(pl.BlockSpec((1, 1) + chunk, lambda i, t, z=(0,) * len(chunk): (i, t) + z))
            out_shape.append(jax.ShapeDtypeStruct((b, l // TOKEN_TILE) + chunk, dt))
        elif chunk == "heads":
            out_specs.append(pl.BlockSpec((1, N_HEADS, TOKEN_TILE, n // N_HEADS), lambda i, t: (i, 0, t, 0)))
            out_shape.append(jax.ShapeDtypeStruct((b, N_HEADS, l, n // N_HEADS), dt))
        else:
            out_specs.append(pl.BlockSpec((1, TOKEN_TILE // chunk, n, chunk), lambda i, t: (i, t, 0, 0)))
            out_shape.append(jax.ShapeDtypeStruct((b, l // chunk, n, chunk), dt))
    return in_specs, out_specs, out_shape


def _proj_call(spec, h, n_ctx, modsel, norm_w):
    h_args, h_specs = _hidden_operands(h, n_ctx)
    b, _, d = h_args[-1].shape
    l = sum(a.shape[1] for a in h_args)
    in_specs, out_specs, out_shape = _proj_specs(spec, b, l, d)
    return pl.pallas_call(
        _with_hidden_tile(spec["body"], len(h_args), n_ctx // TOKEN_TILE),
        grid=(b, l // TOKEN_TILE),
        in_specs=h_specs + in_specs,
        out_specs=out_specs,
        out_shape=out_shape,
        compiler_params=pltpu.CompilerParams(
            dimension_semantics=("parallel", "parallel"), vmem_limit_bytes=VMEM_LIMIT),
        name="proj_" + spec["name"],
    )(*h_args, modsel, norm_w.reshape(1, d), *spec["consts"], *spec["tables"])


def _conv_body(x_ref, w_ref, b_ref, o_ref, *, n_ctx, n_blocks, scale, transposed):
    blk = SCAN_CHUNK
    l = x_ref.shape[1]
    w = w_ref[...]
    bias = b_ref[...]
    rid = lax.broadcasted_iota(jnp.int32, (blk, 1), 0)

    def emit(chunk, start, acc):
        y = _silu(acc + bias) * scale
        if transposed:
            o_ref[0, chunk] = y.T.astype(BF16)
        else:
            o_ref[0, pl.ds(start, blk), :] = y.astype(BF16)

    def combine(a, bm, cc, first, last):
        return (jnp.where(first, 0.0, pltpu.roll(a, 1, 0)) + bm
                + jnp.where(last, 0.0, pltpu.roll(cc, blk - 1, 0)))

    xc = x_ref[0, 0:blk, :]
    emit(0, 0, combine(xc * w[3:4], xc * w[4:5], xc * w[5:6], rid == 0, rid == blk - 1))

    col = rid % GRID_W

    def lat_block(it, carry):
        base = pl.multiple_of(n_ctx + it * blk, blk)
        mid = x_ref[0, pl.ds(base, blk), :]
        up_edge = x_ref[0, pl.ds(pl.multiple_of(base - GRID_W, GRID_W), GRID_W), :]
        dn_start = pl.multiple_of(jnp.minimum(base + blk, l - GRID_W), GRID_W)
        dn_edge = x_ref[0, pl.ds(dn_start, GRID_W), :]
        up_edge = jnp.where(it > 0, up_edge, 0.0)
        dn_edge = jnp.where(it < n_blocks - 1, dn_edge, 0.0)
        up = jnp.concatenate([up_edge, mid[:blk - GRID_W]], axis=0)
        down = jnp.concatenate([mid[GRID_W:], dn_edge], axis=0)
        a = up * w[0:1] + mid * w[3:4] + down * w[6:7]
        bm = up * w[1:2] + mid * w[4:5] + down * w[7:8]
        cc = up * w[2:3] + mid * w[5:6] + down * w[8:9]
        emit(n_ctx // blk + it, base, combine(a, bm, cc, col == 0, col == GRID_W - 1))
        return carry

    lax.fori_loop(0, n_blocks, lat_block, 0)


def _conv(qk_raw, conv_w, conv_b, n_ctx, col_offset, width, scale, transposed):
    b, l, c = qk_raw.shape
    tc = 256
    blk = SCAN_CHUNK
    assert n_ctx == blk and (l - n_ctx) % blk == 0 and blk % GRID_W == 0
    off = col_offset // tc
    if transposed:
        out_spec = pl.BlockSpec((1, l // blk, tc, blk), lambda i, j: (i, 0, j, 0))
        out_shape = jax.ShapeDtypeStruct((b, l // blk, width, blk), BF16)
    else:
        out_spec = pl.BlockSpec((1, None, l, tc), lambda i, j: (i, j, 0, 0))
        out_shape = jax.ShapeDtypeStruct((b, width // tc, l, tc), BF16)
    return pl.pallas_call(
        functools.partial(_conv_body, n_ctx=n_ctx, n_blocks=(l - n_ctx) // blk, scale=scale,
                          transposed=transposed),
        grid=(b, width // tc),
        in_specs=[pl.BlockSpec((1, l, tc), lambda i, j: (i, 0, j + off)),
                  pl.BlockSpec((9, tc), lambda i, j: (0, j + off)),
                  pl.BlockSpec((1, tc), lambda i, j: (0, j + off))],
        out_specs=out_spec,
        out_shape=out_shape,
        compiler_params=pltpu.CompilerParams(
            dimension_semantics=("parallel", "parallel"), vmem_limit_bytes=VMEM_LIMIT),
        name="mlstm_conv_kt" if transposed else "mlstm_conv_q",
    )(qk_raw, conv_w.reshape(9, c), conv_b.reshape(1, c))


def _mlstm_scan_body(q_ref, kt_ref, v_ref, g_ref, o_ref, acc_ref, st_ref, n_ref, row_ref,
                     *, n_ctx_chunks, n_chunks):
    c = SCAN_CHUNK
    dv = v_ref.shape[2]
    st_ref[...] = jnp.zeros(st_ref.shape, F32)
    n_ref[...] = jnp.zeros(n_ref.shape, F32)
    masks = _tri_masks(c)

    gates = g_ref[0]
    token_rows, chunk_rows = [], []
    for dirn in range(2):
        u, b_r, cm = (gates[:, 5 * dirn + r:5 * dirn + r + 1] for r in range(3))
        b_end = gates[:, 5 * dirn + 3:5 * dirn + 4, 0:1]
        g_max = gates[:, 5 * dirn + 4:5 * dirn + 5, 0:1]
        m = jnp.zeros((1, 1, 1), F32)
        m_prev_l, m_new_l = [None] * n_chunks, [None] * n_chunks
        for s in range(n_chunks):
            idx = s
            if dirn:
                idx = n_ctx_chunks - 1 - s if s < n_ctx_chunks else n_chunks - 1 - (s - n_ctx_chunks)
            m_prev_l[idx] = m
            m = jnp.maximum(b_end[idx:idx + 1] + m, g_max[idx:idx + 1])
            m_new_l[idx] = m
        m_prev = jnp.concatenate(m_prev_l, axis=0)
        m_new = jnp.concatenate(m_new_l, axis=0)
        mm = jnp.maximum(m_prev, cm)
        token_rows += [u, jnp.exp(b_end + u - m_new), mm, b_r + mm]
        chunk_rows += [jnp.broadcast_to(jnp.exp(b_end + m_prev - m_new), (n_chunks, 1, c)),
                       jnp.broadcast_to(m_prev, (n_chunks, 1, c))]
    row_ref[...] = jnp.concatenate(token_rows + chunk_rows + [jnp.zeros((n_chunks, 4, c), F32)], axis=1)

    def lane_bcast_col(row):
        return jnp.broadcast_to(row, (LANES, c)).T

    def step(s, first):
        dirs = (0, 1)
        idxs = (s, _bw_chunk(s, n_ctx_chunks, n_chunks))
        starts = [pl.multiple_of(idx * c, c) for idx in idxs]
        q = [q_ref[0, pl.ds(st, c), :] for st in starts]
        kt = [kt_ref[0, idx] for idx in idxs]
        v = [v_ref[0, pl.ds(st, c), :] for st in starts]
        rows = [row_ref[idx] for idx in idxs]
        state = [st_ref[d] for d in dirs]
        nrm = [n_ref[d] for d in dirs]
        qk = [_dot(q[d], kt[d]) for d in dirs]
        mm_b = [lane_bcast_col(rows[d][4 * d + 2:4 * d + 3]) for d in dirs]
        bmm_b = [lane_bcast_col(rows[d][4 * d + 3:4 * d + 4]) for d in dirs]
        m_prev = [rows[d][9 + 2 * d:10 + 2 * d, 0:1] for d in dirs]
        decay = [rows[d][8 + 2 * d:9 + 2 * d, 0:1] for d in dirs]
        sc = [qk[d] * jnp.exp(jnp.where(masks[d], rows[d][4 * d:4 * d + 1]
                                        - jnp.concatenate([mm_b[d]] * (c // LANES), axis=1), NEG_INF))
              for d in dirs]
        w_inter = [jnp.exp(m_prev[d] - mm_b[d]) for d in dirs]
        inter = [_dot(q[d], state[d].astype(BF16)) for d in dirs]
        qn = [_dot(q[d], nrm[d].astype(BF16)) for d in dirs]
        wkt = [kt[d].astype(F32) * rows[d][4 * d + 1:4 * d + 2] for d in dirs]
        both = [_dot(jnp.concatenate([sc[d].astype(BF16), wkt[d].astype(BF16)], axis=0), v[d]) for d in dirs]
        intra = [both[d][:c] for d in dirs]
        upd = [both[d][c:] for d in dirs]
        for d in dirs:
            num = intra[d] + inter[d] * jnp.concatenate([w_inter[d]] * (dv // LANES), axis=1)
            den = jnp.sum(sc[d], axis=1, keepdims=True) + qn[d] * w_inter[d]
            rcp = 1.0 / jnp.maximum(jnp.abs(den), jnp.exp(-bmm_b[d]))
            _emit(o_ref, acc_ref, starts[d], c, num * jnp.concatenate([rcp] * (dv // LANES), axis=1), first[d])
        for d in dirs:
            st_ref[d] = decay[d] * state[d] + upd[d]
            n_ref[d] = decay[d] * nrm[d] + jnp.sum(wkt[d], axis=1, keepdims=True)

    _scan_steps(step, n_ctx_chunks, n_chunks, 1)


def _mlstm_scan(q, kt, v, gates, n_ctx):
    b, _, l, dk = q.shape
    dv = v.shape[3]
    n_chunks = l // SCAN_CHUNK
    return pl.pallas_call(
        functools.partial(_mlstm_scan_body, n_ctx_chunks=n_ctx // SCAN_CHUNK, n_chunks=n_chunks),
        grid=(b, N_HEADS),
        in_specs=[pl.BlockSpec((1, None, l, dk), _HEAD_SLAB),
                  pl.BlockSpec((1, n_chunks, dk, SCAN_CHUNK), lambda i, h: (i, 0, h, 0)),
                  pl.BlockSpec((1, None, l, dv), _HEAD_SLAB),
                  pl.BlockSpec((1, n_chunks, None, GATE_ROWS, SCAN_CHUNK), lambda i, h: (i, 0, h, 0, 0))],
        out_specs=pl.BlockSpec((1, None, l, dv), _HEAD_SLAB),
        out_shape=jax.ShapeDtypeStruct((b, N_HEADS, l, dv), BF16),
        scratch_shapes=[pltpu.VMEM((l, dv), F32),
                        pltpu.VMEM((2, dk, dv), F32),
                        pltpu.VMEM((2, dk, LANES), F32),
                        pltpu.VMEM((n_chunks, 16, SCAN_CHUNK), F32)],
        compiler_params=pltpu.CompilerParams(
            dimension_semantics=("parallel", "parallel"), vmem_limit_bytes=VMEM_LIMIT),
        name="mlstm_scan",
    )(q, kt, v, gates)


def _ret_scan_body(q_ref, kt_ref, v_ref, dl_ref, o_ref, acc_ref, st_ref, dm_ref, *, n_ctx_chunks, n_chunks):
    c = SCAN_CHUNK
    st_ref[...] = jnp.zeros(st_ref.shape, F32)
    masks = _tri_masks(c)
    row = lax.broadcasted_iota(jnp.int32, (c, c), 0)
    col = lax.broadcasted_iota(jnp.int32, (c, c), 1)
    dist = jnp.abs(row - col).astype(F32)
    rid = lax.broadcasted_iota(jnp.int32, (c, 1), 0).astype(F32)
    lid = lax.broadcasted_iota(jnp.int32, (1, c), 1).astype(F32)
    log_gamma = []
    for dirn in range(2):
        lg = _log_sigmoid(dl_ref[0, dirn])[0:1, 0:1]
        log_gamma.append(lg)
        dm_ref[dirn] = jnp.where(masks[dirn], jnp.exp(lg * dist), 0.0)

    pos_col = (rid + 1.0, c - rid)
    pos_row = (lid + 1.0, c - lid)

    def step(s, first):
        dirs = (0, 1)
        idxs = (s, _bw_chunk(s, n_ctx_chunks, n_chunks))
        starts = [pl.multiple_of(idx * c, c) for idx in idxs]
        q = [q_ref[0, pl.ds(st, c), :] for st in starts]
        kt = [kt_ref[0, idx] for idx in idxs]
        v = [v_ref[0, pl.ds(st, c), :] for st in starts]
        state = [st_ref[d] for d in dirs]
        sc = [(_dot(q[d], kt[d]) * dm_ref[d]).astype(BF16) for d in dirs]
        inter = [_dot(q[d], state[d].astype(BF16)) for d in dirs]
        kdt = [(kt[d].astype(F32) * jnp.exp(log_gamma[d] * (c - pos_row[d]))).astype(BF16) for d in dirs]
        both = [_dot(jnp.concatenate([sc[d], kdt[d]], axis=0), v[d]) for d in dirs]
        intra = [both[d][:c] for d in dirs]
        upd = [both[d][c:] for d in dirs]
        for d in dirs:
            _emit(o_ref, acc_ref, starts[d], c, intra[d] + inter[d] * jnp.exp(log_gamma[d] * pos_col[d]), first[d])
        for d in dirs:
            st_ref[d] = jnp.exp(log_gamma[d] * c) * state[d] + upd[d]

    _scan_steps(step, n_ctx_chunks, n_chunks, 4)


def _ret_scan(q, kt, v, dl, n_ctx):
    b, _, l, dk = q.shape
    dv = v.shape[3]
    n_chunks = l // SCAN_CHUNK
    return pl.pallas_call(
        functools.partial(_ret_scan_body, n_ctx_chunks=n_ctx // SCAN_CHUNK, n_chunks=n_chunks),
        grid=(b, N_HEADS),
        in_specs=[pl.BlockSpec((1, None, l, dk), _HEAD_SLAB),
                  pl.BlockSpec((1, n_chunks, dk, SCAN_CHUNK), lambda i, h: (i, 0, h, 0)),
                  pl.BlockSpec((1, None, l, dv), _HEAD_SLAB),
                  pl.BlockSpec((1, 2, 8, LANES), lambda i, h: (h, 0, 0, 0))],
        out_specs=pl.BlockSpec((1, None, l, dv), _HEAD_SLAB),
        out_shape=jax.ShapeDtypeStruct((b, N_HEADS, l, dv), BF16),
        scratch_shapes=[pltpu.VMEM((l, dv), F32),
                        pltpu.VMEM((2, dk, dv), F32),
                        pltpu.VMEM((2, SCAN_CHUNK, SCAN_CHUNK), F32)],
        compiler_params=pltpu.CompilerParams(
            dimension_semantics=("parallel", "parallel"), vmem_limit_bytes=VMEM_LIMIT),
        name="ret_scan",
    )(q, kt, v, dl)


def _gla_scan_body(q_ref, kt_ref, v_ref, code_ref, w2_ref, b2_ref, o_ref,
                   acc_ref, st_ref, sq_ref, kdt_ref, e_ref, m01_ref, *, n_ctx_chunks, n_chunks):
    c = GLA_CHUNK
    t = GLA_SUB
    n_sub = c // t
    dk = q_ref.shape[2]
    dv = v_ref.shape[2]
    st_ref[...] = jnp.zeros(st_ref.shape, F32)
    masks = _tri_masks(c)
    for d in range(2):
        m01_ref[d] = masks[d].astype(BF16)
    lid = lax.broadcasted_iota(jnp.int32, (1, c), 1)

    def intra(it, carry, first_chunk, group):
        chains = [(first_chunk + it * group + j, dirn) for j in range(group) for dirn in range(2)]
        starts = [pl.multiple_of(idx * c, c) for idx, _ in chains]
        qf = [q_ref[0, pl.ds(st, c), :].astype(F32) for st in starts]
        ktf = [kt_ref[0, idx].astype(F32) for idx, _ in chains]
        code = [code_ref[0, pl.ds(st, c), :] for st in starts]
        a = [_log_sigmoid(_dot(code[k], w2_ref[d]) + b2_ref[d]) * (1.0 / GLA_TAU) for k, (_, d) in enumerate(chains)]
        parts = [_split3(x) for x in a]
        m01 = [m01_ref[d] for d in range(2)]
        m01_twice = [jnp.concatenate([m, m], axis=1) for m in m01]
        b = [_dot(m01_twice[d], jnp.concatenate([parts[k][2], parts[k][1]], axis=0)) for k, (_, d) in enumerate(chains)]
        b = [b[k] + _dot(m01[d], parts[k][0]) for k, (_, d) in enumerate(chains)]
        qp, kp, kdt, e_end = [], [], [], []
        for k, (_, dirn) in enumerate(chains):
            refs = []
            for sb in range(n_sub):
                r0 = sb * t + (t - 1 if dirn else 0)
                refs.append(b[k][r0:r0 + 1] - a[k][r0:r0 + 1])
            own = jnp.concatenate([jnp.broadcast_to(r, (t, dk)) for r in refs], axis=0)
            b_rel = b[k] - own
            q_own = qf[k] * jnp.exp(b_rel)
            k_rel_t = (ktf[k] * jnp.exp(jnp.minimum(-(b_rel.T), GLA_EXP_CLAMP))).astype(BF16)
            r_end = 0 if dirn else c - 1
            b_end = b[k][r_end:r_end + 1]
            kdt.append((ktf[k] * jnp.exp((b_end - b[k]).T)).astype(BF16))
            e_end.append(jnp.broadcast_to(jnp.exp(b_end), (8, dk)))
            col_blocks = []
            for sb in range(n_sub):
                pieces = []
                for j in range(n_sub):
                    if (j <= sb) if dirn else (j >= sb):
                        qj = q_own[j * t:(j + 1) * t]
                        pieces.append(qj if j == sb else qj * jnp.exp(refs[j] - refs[sb]))
                    else:
                        pieces.append(jnp.zeros((t, dk), F32))
                col_blocks.append(jnp.concatenate(pieces, axis=0).astype(BF16))
            qp.append(col_blocks)
            kp.append(jnp.concatenate(
                [jnp.where(jnp.logical_and(lid >= sb * t, lid < (sb + 1) * t), k_rel_t, 0.0)
                 for sb in range(n_sub)], axis=0))
        s = [_dot(jnp.concatenate(qp[k], axis=1), kp[k]) for k in range(len(chains))]
        for k, (idx, dirn) in enumerate(chains):
            sq_ref[dirn, pl.ds(starts[k], c), 0:c] = jnp.where(masks[dirn], s[k], 0.0).astype(BF16)
            sq_ref[dirn, pl.ds(starts[k], c), c:2 * c] = qp[k][n_sub - 1 if dirn else 0]
            kdt_ref[dirn, idx] = kdt[k]
            e_ref[dirn, idx] = e_end[k]
        return carry

    n_full = n_chunks // GLA_INTRA_CHUNKS
    lax.fori_loop(0, n_full, functools.partial(intra, first_chunk=0, group=GLA_INTRA_CHUNKS), 0)
    rest = n_chunks - n_full * GLA_INTRA_CHUNKS
    if rest:
        intra(0, 0, n_full * GLA_INTRA_CHUNKS, rest)

    def step(s, first):
        dirs = (0, 1)
        idxs = (s, _bw_chunk(s, n_ctx_chunks, n_chunks))
        starts = [pl.multiple_of(idx * c, c) for idx in idxs]
        v = [v_ref[0, pl.ds(st, c), :] for st in starts]
        state = [st_ref[d] for d in dirs]
        out = [_dot(sq_ref[d, pl.ds(starts[d], c), :],
                    jnp.concatenate([v[d], state[d].astype(BF16)], axis=0)) for d in dirs]
        upd = [_dot(kdt_ref[d, idxs[d]], v[d]) for d in dirs]
        e_cols = [jnp.concatenate([e_ref[d, idxs[d]]] * (dk // 8), axis=0).T for d in dirs]
        for d in dirs:
            _emit(o_ref, acc_ref, starts[d], c, out[d], first[d])
        for d in dirs:
            st_ref[d] = jnp.concatenate([e_cols[d]] * (dv // dk), axis=1) * state[d] + upd[d]

    _scan_steps(step, n_ctx_chunks, n_chunks, 4)


def _gla_scan(q, kt, v, code, w2, b2, n_ctx):
    b, _, l, dk = q.shape
    dv = v.shape[3]
    n_chunks = l // GLA_CHUNK
    return pl.pallas_call(
        functools.partial(_gla_scan_body, n_ctx_chunks=n_ctx // GLA_CHUNK, n_chunks=n_chunks),
        grid=(b, N_HEADS),
        in_specs=[pl.BlockSpec((1, None, l, dk), _HEAD_SLAB),
                  pl.BlockSpec((1, n_chunks, dk, GLA_CHUNK), lambda i, h: (i, 0, h, 0)),
                  pl.BlockSpec((1, None, l, dv), _HEAD_SLAB),
                  pl.BlockSpec((1, l, LANES), lambda i, h: (i, 0, 0)),
                  pl.BlockSpec((2, LANES, dk), lambda i, h: (0, 0, h)),
                  pl.BlockSpec((2, 1, dk), lambda i, h: (0, 0, h))],
        out_specs=pl.BlockSpec((1, None, l, dv), _HEAD_SLAB),
        out_shape=jax.ShapeDtypeStruct((b, N_HEADS, l, dv), BF16),
        scratch_shapes=[pltpu.VMEM((l, dv), F32),
                        pltpu.VMEM((2, dk, dv), F32),
                        pltpu.VMEM((2, l, GLA_CHUNK + dk), BF16),
                        pltpu.VMEM((2, n_chunks, dk, GLA_CHUNK), BF16),
                        pltpu.VMEM((2, n_chunks, 8, dk), F32),
                        pltpu.VMEM((2, GLA_CHUNK, GLA_CHUNK), BF16)],
        compiler_params=pltpu.CompilerParams(
            dimension_semantics=("parallel", "parallel"), vmem_limit_bytes=VMEM_LIMIT),
        name="gla_scan",
    )(q, kt, v, code, w2, b2)


def _out_core(x, o_ref, z_ref, hw_ref, ow_ref, mod_ref, center):
    dv = o_ref.shape[3]
    proj = None
    for hh in range(N_HEADS):
        cols = slice(hh * dv, (hh + 1) * dv)
        yh = o_ref[0, hh].astype(F32)
        if center:
            yh = yh - jnp.mean(yh, axis=-1, keepdims=True)
        yn = yh * lax.rsqrt(jnp.mean(yh * yh, axis=-1, keepdims=True) + NORM_EPS) * hw_ref[:, cols]
        a = (yn * _silu(z_ref[0, :, cols].astype(F32))).astype(BF16)
        term = _dot(a, ow_ref[cols, :])
        proj = term if proj is None else proj + term
    return x + mod_ref[0, 0][2:3] * proj


def _out_final_body(x, o_ref, z_ref, hw_ref, ow_ref, mod_ref, fw_ref, out_ref, *, center):
    hn = _out_core(x, o_ref, z_ref, hw_ref, ow_ref, mod_ref, center)
    out_ref[0] = hn * lax.rsqrt(jnp.mean(hn * hn, axis=-1, keepdims=True) + NORM_EPS) * fw_ref[...]


def _out_proj_body(x, o_ref, z_ref, hw_ref, ow_ref, mod_ref, *rest, center, proj_body, n_proj_in):
    proj_in, h_out_ref, proj_out = rest[:n_proj_in], rest[n_proj_in], rest[n_proj_in + 1:]
    hn = _out_core(x, o_ref, z_ref, hw_ref, ow_ref, mod_ref, center)
    h_out_ref[0] = hn
    proj_body(hn, *proj_in, *proj_out)


def _out_operands(o, z, head_norm_w, out_w, modsel, skip):
    _, n_heads, _, dv = o.shape
    di = n_heads * dv
    d = out_w.shape[1]
    tok = lambda i, t: (i, t + skip, 0)
    specs = [pl.BlockSpec((1, n_heads, TOKEN_TILE, dv), lambda i, t: (i, 0, t + skip, 0)),
             pl.BlockSpec((1, TOKEN_TILE, di), tok),
             pl.BlockSpec((1, di), lambda i, t: (0, 0)),
             pl.BlockSpec((di, d), lambda i, t: (0, 0), pipeline_mode=pl.Buffered(1)),
             pl.BlockSpec((1, 1, 3, d), lambda i, t: (i, jnp.minimum(t + skip, 1), 0, 0))]
    return [o, z, head_norm_w.reshape(1, di), out_w, modsel], specs


def _out_final_call(o, z, head_norm_w, out_w, h, modsel, final_w, n_ctx, center):
    b, _, l, _ = o.shape
    d = out_w.shape[1]
    skip = n_ctx // TOKEN_TILE
    nt = l // TOKEN_TILE - skip
    h_args, h_specs = _hidden_operands(h, n_ctx, skip)
    args, specs = _out_operands(o, z, head_norm_w, out_w, modsel, skip)
    return pl.pallas_call(
        _with_hidden_tile(functools.partial(_out_final_body, center=center),
                          len(h_args), n_ctx // TOKEN_TILE, skip),
        grid=(b, nt),
        in_specs=h_specs + specs + [pl.BlockSpec((1, d), lambda i, t: (0, 0))],
        out_specs=pl.BlockSpec((1, TOKEN_TILE, d), lambda i, t: (i, t, 0)),
        out_shape=jax.ShapeDtypeStruct((b, nt * TOKEN_TILE, d), F32),
        compiler_params=pltpu.CompilerParams(
            dimension_semantics=("parallel", "parallel"), vmem_limit_bytes=VMEM_LIMIT),
        name="out_final",
    )(*h_args, *args, final_w.reshape(1, d))


def _out_proj_call(o, z, head_norm_w, out_w, h, modsel, n_ctx, center, spec, next_modsel, next_norm_w):
    b, _, l, _ = o.shape
    d = out_w.shape[1]
    h_args, h_specs = _hidden_operands(h, n_ctx)
    args, specs = _out_operands(o, z, head_norm_w, out_w, modsel, 0)
    p_in, p_out_specs, p_out_shape = _proj_specs(spec, b, l, d)
    res = pl.pallas_call(
        _with_hidden_tile(functools.partial(_out_proj_body, center=center, proj_body=spec["body"],
                                            n_proj_in=len(p_in)),
                          len(h_args), n_ctx // TOKEN_TILE),
        grid=(b, l // TOKEN_TILE),
        in_specs=h_specs + specs + p_in,
        out_specs=[pl.BlockSpec((1, TOKEN_TILE, d), lambda i, t: (i, t, 0))] + p_out_specs,
        out_shape=[jax.ShapeDtypeStruct((b, l, d), F32)] + p_out_shape,
        compiler_params=pltpu.CompilerParams(
            dimension_semantics=("parallel", "parallel"), vmem_limit_bytes=VMEM_LIMIT),
        name="out_proj_" + spec["name"],
    )(*h_args, *args, next_modsel, next_norm_w.reshape(1, d), *spec["consts"], *spec["tables"])
    return res[0], res[1:]


def _pad_cols(w, n):
    return jnp.pad(w, ((0, 0), (0, n - w.shape[1])))


def _mlstm_proj_spec(l, n_ctx, in_w, conv_w, conv_b, gate_b):
    d = in_w.shape[0]
    di = 2 * d
    qk_w = 2 * d
    wg = _pad_cols(in_w[:, qk_w + 2 * di:], LANES).astype(BF16)
    gb = _pad_cols(gate_b.reshape(1, -1), LANES)
    return dict(body=_proj_mlstm_body, consts=[in_w.astype(BF16), wg, gb], tables=[], name="mlstm",
                outs=[(qk_w, F32, None), (di, BF16, "heads"), (di, BF16, None),
                      (0, F32, (N_HEADS, GATE_ROWS, TOKEN_TILE))])


def _mlstm_mix(outs, n_ctx, in_w, conv_w, conv_b, gate_b):
    qk_raw, v, z, g = outs
    b, l, qk_w = qk_raw.shape
    head_qk = qk_w // (2 * N_HEADS)
    q = _conv(qk_raw, conv_w, conv_b, n_ctx, 0, qk_w // 2, 1.0, False)
    kt = _conv(qk_raw, conv_w, conv_b, n_ctx, qk_w // 2, qk_w // 2, head_qk ** -0.5, True)
    return _mlstm_scan(q, kt, v, g, n_ctx), z


def _gla_proj_spec(l, n_ctx, in_w, gk_w2, gk_b):
    d = in_w.shape[0]
    di = 2 * d
    key = gk_w2.shape[2]
    assert key // N_HEADS == LANES
    wlr = _pad_cols(in_w[:, 2 * key + 2 * di:], LANES).astype(BF16)
    return dict(body=functools.partial(_proj_gla_body, q_scale=(key // N_HEADS) ** -0.5),
                consts=[in_w.astype(BF16), wlr], tables=[], name="gla",
                outs=[(key, BF16, "heads"), (key, BF16, GLA_CHUNK), (di, BF16, "heads"), (di, BF16, None),
                      (LANES, BF16, None)])


def _gla_mix(outs, n_ctx, in_w, gk_w2, gk_b):
    q, kt, v, z, code = outs
    key = gk_w2.shape[2]
    w2 = jnp.zeros((2, LANES, key), F32)
    w2 = w2.at[0, :GLA_RANK].set(gk_w2[0]).at[1, GLA_RANK:2 * GLA_RANK].set(gk_w2[1]).astype(BF16)
    return _gla_scan(q, kt, v, code, w2, gk_b.reshape(2, 1, key), n_ctx), z


def _rope_tables(n_ctx, n_lat, head_k):
    quarter = head_k // 4
    pos = jnp.arange(n_lat, dtype=jnp.int32)
    inv_freq = ROPE_BASE ** (-jnp.arange(quarter, dtype=F32) / quarter)
    cos_parts, sin_parts = [], []
    for p in ((pos // GRID_W).astype(F32), (pos % GRID_W).astype(F32)):
        ang = p[:, None] * inv_freq[None, :]
        cos_parts += [jnp.cos(ang), jnp.cos(ang)]
        sin_parts += [-jnp.sin(ang), jnp.sin(ang)]
    cos = jnp.concatenate(cos_parts, axis=1)
    sin = jnp.concatenate(sin_parts, axis=1)
    cos = jnp.concatenate([jnp.ones((n_ctx, head_k), F32), cos], axis=0)
    sin = jnp.concatenate([jnp.zeros((n_ctx, head_k), F32), sin], axis=0)
    return cos, sin


def _ret_proj_spec(l, n_ctx, in_w, decay_logit):
    d = in_w.shape[0]
    di = 2 * d
    key = (in_w.shape[1] - 2 * di) // 2
    head_k = key // N_HEADS
    cos, sin = _rope_tables(n_ctx, l - n_ctx, head_k)
    return dict(body=functools.partial(_proj_ret_body, k_scale=head_k ** -0.5),
                consts=[in_w.astype(BF16)], tables=[cos, sin], name="ret",
                outs=[(key, BF16, "heads"), (key, BF16, SCAN_CHUNK), (di, BF16, "heads"), (di, BF16, None)])


def _ret_mix(outs, n_ctx, in_w, decay_logit):
    q, kt, v, z = outs
    dl = jnp.broadcast_to(decay_logit.astype(F32).T[:, :, None, None], (N_HEADS, 2, 8, LANES))
    return _ret_scan(q, kt, v, dl, n_ctx), z


_PROJ_SPEC = {"mlstm": _mlstm_proj_spec, "gla": _gla_proj_spec, "retention": _ret_proj_spec}
_MIX = {"mlstm": _mlstm_mix, "gla": _gla_mix, "retention": _ret_mix}


def kernel(x, c, ctx, c_ctx, l0_norm_w, l0_ada_w, l0_ada_b, l0_in_w, l0_conv_w, l0_conv_b, l0_gate_b, l0_head_norm_w, l0_out_w, l1_norm_w, l1_ada_w, l1_ada_b, l1_in_w, l1_gk_w2, l1_gk_b, l1_head_norm_w, l1_out_w, l2_norm_w, l2_ada_w, l2_ada_b, l2_in_w, l2_decay_logit, l2_head_norm_w, l2_out_w, l3_norm_w, l3_ada_w, l3_ada_b, l3_in_w, l3_conv_w, l3_conv_b, l3_gate_b, l3_head_norm_w, l3_out_w, final_norm_w):
    layers = (
        ("mlstm", l0_norm_w, l0_ada_w, l0_ada_b, l0_out_w, l0_head_norm_w, (l0_in_w, l0_conv_w, l0_conv_b, l0_gate_b)),
        ("gla", l1_norm_w, l1_ada_w, l1_ada_b, l1_out_w, l1_head_norm_w, (l1_in_w, l1_gk_w2, l1_gk_b)),
        ("retention", l2_norm_w, l2_ada_w, l2_ada_b, l2_out_w, l2_head_norm_w, (l2_in_w, l2_decay_logit)),
        ("mlstm", l3_norm_w, l3_ada_w, l3_ada_b, l3_out_w, l3_head_norm_w, (l3_in_w, l3_conv_w, l3_conv_b, l3_gate_b)),
    )
    b, _, d = x.shape
    n_ctx = ctx.shape[1]
    assert n_ctx % TOKEN_TILE == 0 and n_ctx % SCAN_CHUNK == 0 and x.shape[1] % SCAN_CHUNK == 0
    assert TOKEN_TILE == SCAN_CHUNK and TOKEN_TILE % GLA_CHUNK == 0
    l = n_ctx + x.shape[1]
    h = (ctx, x)
    cc = jnp.concatenate([c, c_ctx[None, :], jnp.zeros((7, d), F32)], axis=0)
    modsels = []
    for _, _, ada_w, ada_b, _, _, _ in layers:
        mod = _ada(cc, ada_w, ada_b)
        mod_lat = mod[:b].reshape(b, 1, 3, d)
        mod_ctx = jnp.broadcast_to(mod[b].reshape(1, 1, 3, d), (b, 1, 3, d))
        modsels.append(jnp.concatenate([mod_ctx, mod_lat], axis=1))
    specs = [_PROJ_SPEC[kind](l, n_ctx, *params) for kind, _, _, _, _, _, params in layers]
    proj_outs = _proj_call(specs[0], h, n_ctx, modsels[0], layers[0][1])
    for li, (kind, _, _, _, out_w, head_norm_w, params) in enumerate(layers):
        o, z = _MIX[kind](proj_outs, n_ctx, *params)
        center = kind != "gla"
        if li == len(layers) - 1:
            return _out_final_call(o, z, head_norm_w, out_w.astype(BF16), h, modsels[li], final_norm_w,
                                   n_ctx, center)
        h, proj_outs = _out_proj_call(o, z, head_norm_w, out_w.astype(BF16), h, modsels[li], n_ctx, center,
                                      specs[li + 1], modsels[li + 1], layers[li + 1][1])
```

```python
import functools

import jax
import jax.numpy as jnp
from jax import lax
from jax.experimental import pallas as pl
from jax.experimental.pallas import tpu as pltpu

N_HEADS = 4
GRID_W = 64
GLA_RANK = 16
GLA_TAU = 16.0
ROPE_BASE = 10000.0
NORM_EPS = 1e-6
NEG_INF = -1e30

TOKEN_TILE = 256
SCAN_CHUNK = 256
GLA_CHUNK = 128
GLA_SUB = 16
GLA_EXP_CLAMP = 80.0
GLA_INTRA_CHUNKS = 8
LANES = 128
VMEM_LIMIT = 56 * 1024 * 1024

F32 = jnp.float32
BF16 = jnp.bfloat16


def _silu(x):
    half = 0.5 * x
    return half + half * jnp.tanh(half)


def _log_sigmoid(x):
    return jnp.minimum(x, 0.0) - jnp.log(1.0 + jnp.exp(-jnp.abs(x)))


def _dot(a, b):
    return jnp.dot(a, b, preferred_element_type=F32)


def _split3(a):
    hi = a.astype(BF16)
    r1 = a - hi.astype(F32)
    mid = r1.astype(BF16)
    lo = (r1 - mid.astype(F32)).astype(BF16)
    return hi, mid, lo


def _tri_masks(n):
    row = lax.broadcasted_iota(jnp.int32, (n, n), 0)
    col = lax.broadcasted_iota(jnp.int32, (n, n), 1)
    return col <= row, col >= row


_HEAD_SLAB = lambda i, h: (i, h, 0, 0)


def _prefix_scan(x, lane, backward, op, identity):
    ax = x.ndim - 1
    n = x.shape[ax]
    sh = 1
    while sh < n:
        if backward:
            x = op(x, jnp.where(lane < n - sh, pltpu.roll(x, n - sh, ax), identity))
        else:
            x = op(x, jnp.where(lane >= sh, pltpu.roll(x, sh, ax), identity))
        sh *= 2
    return x


def _bw_chunk(step, n_ctx_chunks, n_chunks):
    return jnp.where(step < n_ctx_chunks, n_ctx_chunks - 1 - step, n_chunks - 1 - (step - n_ctx_chunks))


def _first_touch_groups(n_ctx_chunks, n_chunks):
    def bw_idx(s):
        return n_ctx_chunks - 1 - s if s < n_ctx_chunks else n_chunks - 1 - (s - n_ctx_chunks)
    bw_step = {bw_idx(s): s for s in range(n_chunks)}
    groups = []
    for s in range(n_chunks):
        flags = (s <= bw_step[s], s < bw_idx(s))
        if groups and groups[-1][2] == flags:
            groups[-1][1] = s + 1
        else:
            groups.append([s, s + 1, flags])
    return [tuple(g) for g in groups]


def _scan_steps(step, n_ctx_chunks, n_chunks, unroll):
    for start, stop, first in _first_touch_groups(n_ctx_chunks, n_chunks):
        lax.fori_loop(start, stop, lambda s, carry, first=first: step(s, first) or carry, 0,
                      unroll=unroll if (stop - start) % unroll == 0 else 1)


def _emit(o_ref, acc_ref, start, n, value, first):
    if first:
        acc_ref[pl.ds(start, n), :] = value
    else:
        o_ref[0, pl.ds(start, n), :] = (acc_ref[pl.ds(start, n), :] + value).astype(o_ref.dtype)


def _ada_body(c_ref, w_ref, b_ref, o_ref):
    s = _silu(c_ref[...])
    o_ref[...] = jnp.dot(s, w_ref[...], preferred_element_type=F32,
                         precision=lax.Precision.HIGHEST) + b_ref[...]


def _ada(cc, w, b):
    rows, d = cc.shape
    n = w.shape[1]
    tn = 512
    return pl.pallas_call(
        _ada_body,
        grid=(n // tn,),
        in_specs=[pl.BlockSpec((rows, d), lambda j: (0, 0)),
                  pl.BlockSpec((d, tn), lambda j: (0, j)),
                  pl.BlockSpec((1, tn), lambda j: (0, j))],
        out_specs=pl.BlockSpec((rows, tn), lambda j: (0, j)),
        out_shape=jax.ShapeDtypeStruct((rows, n), F32),
        name="ada",
    )(cc, w, b.reshape(1, n))


def _hidden_operands(h, n_ctx, tile_offset=0):
    if not isinstance(h, tuple):
        d = h.shape[2]
        return [h], [pl.BlockSpec((1, TOKEN_TILE, d), lambda i, t: (i, t + tile_offset, 0))]
    ctx, lat = h
    d = lat.shape[2]
    nct = n_ctx // TOKEN_TILE
    return [ctx, lat], [
        pl.BlockSpec((1, TOKEN_TILE, d), lambda i, t: (i, jnp.minimum(t + tile_offset, nct - 1), 0)),
        pl.BlockSpec((1, TOKEN_TILE, d), lambda i, t: (i, jnp.maximum(t + tile_offset - nct, 0), 0))]


def _with_hidden_tile(body, n_hidden, n_ctx_tiles, tile_offset=0):
    def kernel_fn(*refs):
        if n_hidden == 1:
            x = refs[0][0]
        else:
            is_ctx = pl.program_id(1) + tile_offset < n_ctx_tiles
            x = jnp.where(is_ctx, refs[0][0], refs[1][0])
        body(x, *refs[n_hidden:])
    return kernel_fn


def _modulated(x, mod_ref, nw_ref):
    y = x * lax.rsqrt(jnp.mean(x * x, axis=-1, keepdims=True) + NORM_EPS) * nw_ref[...]
    m = mod_ref[0, 0]
    return (y * (1.0 + m[1:2]) + m[0:1]).astype(BF16)


def _store_heads(ref, x):
    w = ref.shape[3]
    for hh in range(ref.shape[1]):
        ref[0, hh] = x[:, hh * w:(hh + 1) * w].astype(ref.dtype)


def _store_chunked_t(ref, x, chunk):
    xt = x.T
    for j in range(x.shape[0] // chunk):
        ref[0, j] = xt[:, j * chunk:(j + 1) * chunk].astype(ref.dtype)


GATE_ROWS = 16


def _proj_mlstm_body(x, mod_ref, nw_ref, w_ref, wg_ref, gb_ref, qk_ref, v_ref, z_ref, g_ref):
    u = _modulated(x, mod_ref, nw_ref)
    qk_w, di = qk_ref.shape[2], z_ref.shape[2]
    g = _dot(u, wg_ref[...]) + gb_ref[...]
    lane = lax.broadcasted_iota(jnp.int32, g.shape, 1)
    is_forget = (lane // N_HEADS) % 2 == 1
    g = jnp.where(is_forget, _log_sigmoid(g), g)
    _store_gate_rows(g_ref, g)
    qk_ref[0] = _dot(u, w_ref[:, :qk_w])
    _store_heads(v_ref, _dot(u, w_ref[:, qk_w:qk_w + di]))
    z_ref[0] = _dot(u, w_ref[:, qk_w + di:qk_w + 2 * di]).astype(BF16)


def _store_gate_rows(g_ref, g):
    c = g.shape[0]
    gt = g.T[:4 * N_HEADS]
    lane_row = lax.broadcasted_iota(jnp.int32, (1, c), 1)
    per_dir = []
    for dirn in range(2):
        i_r = gt[2 * dirn * N_HEADS:(2 * dirn + 1) * N_HEADS]
        a_r = gt[(2 * dirn + 1) * N_HEADS:(2 * dirn + 2) * N_HEADS]
        b_r = _prefix_scan(a_r, lane_row, bool(dirn), jnp.add, 0.0)
        u = i_r - b_r
        b_end = jnp.sum(a_r, axis=1, keepdims=True)
        g_max = jnp.max(b_end + u, axis=1, keepdims=True)
        per_dir.append((u, b_r, _prefix_scan(u, lane_row, bool(dirn), jnp.maximum, NEG_INF),
                        jnp.broadcast_to(b_end, u.shape), jnp.broadcast_to(g_max, u.shape)))
    for hh in range(N_HEADS):
        rows = [arr[hh:hh + 1] for dirn in range(2) for arr in per_dir[dirn]]
        g_ref[0, 0, hh] = jnp.concatenate(rows + [jnp.zeros((GATE_ROWS - len(rows), c), F32)], axis=0)


def _proj_gla_body(x, mod_ref, nw_ref, w_ref, wlr_ref, q_ref, kt_ref, v_ref, z_ref, code_ref, *, q_scale):
    u = _modulated(x, mod_ref, nw_ref)
    key, di = kt_ref.shape[2], z_ref.shape[2]
    _store_heads(q_ref, _dot(u, w_ref[:, :key]) * q_scale)
    _store_chunked_t(kt_ref, _dot(u, w_ref[:, key:2 * key]), GLA_CHUNK)
    _store_heads(v_ref, _dot(u, w_ref[:, 2 * key:2 * key + di]))
    z_ref[0] = _dot(u, w_ref[:, 2 * key + di:2 * key + 2 * di]).astype(BF16)
    code_ref[0] = _dot(u, wlr_ref[...]).astype(BF16)


def _rope(t, cos, sin):
    outs = []
    for g in range(t.shape[1] // LANES):
        tg = t[:, g * LANES:(g + 1) * LANES]
        cg = cos[:, (g % 2) * LANES:(g % 2 + 1) * LANES]
        sg = sin[:, (g % 2) * LANES:(g % 2 + 1) * LANES]
        outs.append(tg * cg + pltpu.roll(tg, LANES // 2, 1) * sg)
    return jnp.concatenate(outs, axis=1)


def _proj_ret_body(x, mod_ref, nw_ref, w_ref, cos_ref, sin_ref, q_ref, kt_ref, v_ref, z_ref, *, k_scale):
    u = _modulated(x, mod_ref, nw_ref)
    key, di = kt_ref.shape[2], z_ref.shape[2]
    cos = cos_ref[...]
    sin = sin_ref[...]
    _store_heads(q_ref, _rope(_dot(u, w_ref[:, :key]), cos, sin))
    _store_chunked_t(kt_ref, _rope(_dot(u, w_ref[:, key:2 * key]), cos, sin) * k_scale, SCAN_CHUNK)
    _store_heads(v_ref, _dot(u, w_ref[:, 2 * key:2 * key + di]))
    z_ref[0] = _dot(u, w_ref[:, 2 * key + di:2 * key + 2 * di]).astype(BF16)


def _proj_specs(spec, b, l, d):
    in_specs = [pl.BlockSpec((1, 1, 3, d), lambda i, t: (i, jnp.minimum(t, 1), 0, 0)),
                pl.BlockSpec((1, d), lambda i, t: (0, 0))]
    for a in spec["consts"]:
        in_specs.append(pl.BlockSpec(a.shape, lambda i, t, nd=a.ndim: (0,) * nd, pipeline_mode=pl.Buffered(1)))
    for a in spec["tables"]:
        in_specs.append(pl.BlockSpec((TOKEN_TILE, a.shape[1]), lambda i, t: (t, 0)))
    out_specs, out_shape = [], []
    for n, dt, chunk in spec["outs"]:
        if chunk is None:
            out_specs.append(pl.BlockSpec((1, TOKEN_TILE, n), lambda i, t: (i, t, 0)))
            out_shape.append(jax.ShapeDtypeStruct((b, l, n), dt))
        elif isinstance(chunk, tuple):
            out_specs.append(pl.BlockSpec((1, 1) + chunk, lambda i, t, z=(0,) * len(chunk): (i, t) + z))
            out_shape.append(jax.ShapeDtypeStruct((b, l // TOKEN_TILE) + chunk, dt))
        elif chunk == "heads":
            out_specs.append(pl.BlockSpec((1, N_HEADS, TOKEN_TILE, n // N_HEADS), lambda i, t: (i, 0, t, 0)))
            out_shape.append(jax.ShapeDtypeStruct((b, N_HEADS, l, n // N_HEADS), dt))
        else:
            out_specs.append(pl.BlockSpec((1, TOKEN_TILE // chunk, n, chunk), lambda i, t: (i, t, 0, 0)))
            out_shape.append(jax.ShapeDtypeStruct((b, l // chunk, n, chunk), dt))
    return in_specs, out_specs, out_shape


def _proj_call(spec, h, n_ctx, modsel, norm_w):
    h_args, h_specs = _hidden_operands(h, n_ctx)
    b, _, d = h_args[-1].shape
    l = sum(a.shape[1] for a in h_args)
    in_specs, out_specs, out_shape = _proj_specs(spec, b, l, d)
    return pl.pallas_call(
        _with_hidden_tile(spec["body"], len(h_args), n_ctx // TOKEN_TILE),
        grid=(b, l // TOKEN_TILE),
        in_specs=h_specs + in_specs,
        out_specs=out_specs,
        out_shape=out_shape,
        compiler_params=pltpu.CompilerParams(
            dimension_semantics=("parallel", "parallel"), vmem_limit_bytes=VMEM_LIMIT),
        name="proj_" + spec["name"],
    )(*h_args, modsel, norm_w.reshape(1, d), *spec["consts"], *spec["tables"])


def _conv_body(x_ref, w_ref, b_ref, o_ref, *, n_ctx, n_blocks, scale, transposed):
    blk = SCAN_CHUNK
    l = x_ref.shape[1]
    w = w_ref[...]
    bias = b_ref[...]
    rid = lax.broadcasted_iota(jnp.int32, (blk, 1), 0)

    def emit(chunk, start, acc):
        y = _silu(acc + bias) * scale
        if transposed:
            o_ref[0, chunk] = y.T.astype(BF16)
        else:
            o_ref[0, pl.ds(start, blk), :] = y.astype(BF16)

    def combine(a, bm, cc, first, last):
        return (jnp.where(first, 0.0, pltpu.roll(a, 1, 0)) + bm
                + jnp.where(last, 0.0, pltpu.roll(cc, blk - 1, 0)))

    xc = x_ref[0, 0:blk, :]
    emit(0, 0, combine(xc * w[3:4], xc * w[4:5], xc * w[5:6], rid == 0, rid == blk - 1))

    col = rid % GRID_W

    def lat_block(it, carry):
        base = pl.multiple_of(n_ctx + it * blk, blk)
        mid = x_ref[0, pl.ds(base, blk), :]
        up_edge = x_ref[0, pl.ds(pl.multiple_of(base - GRID_W, GRID_W), GRID_W), :]
        dn_start = pl.multiple_of(jnp.minimum(base + blk, l - GRID_W), GRID_W)
        dn_edge = x_ref[0, pl.ds(dn_start, GRID_W), :]
        up_edge = jnp.where(it > 0, up_edge, 0.0)
        dn_edge = jnp.where(it < n_blocks - 1, dn_edge, 0.0)
        up = jnp.concatenate([up_edge, mid[:blk - GRID_W]], axis=0)
        down = jnp.concatenate([mid[GRID_W:], dn_edge], axis=0)
        a = up * w[0:1] + mid * w[3:4] + down * w[6:7]
        bm = up * w[1:2] + mid * w[4:5] + down * w[7:8]
        cc = up * w[2:3] + mid * w[5:6] + down * w[8:9]
        emit(n_ctx // blk + it, base, combine(a, bm, cc, col == 0, col == GRID_W - 1))
        return carry

    lax.fori_loop(0, n_blocks, lat_block, 0)


def _conv(qk_raw, conv_w, conv_b, n_ctx, col_offset, width, scale, transposed):
    b, l, c = qk_raw.shape
    tc = 256
    blk = SCAN_CHUNK
    assert n_ctx == blk and (l - n_ctx) % blk == 0 and blk % GRID_W == 0
    off = col_offset // tc
    if transposed:
        out_spec = pl.BlockSpec((1, l // blk, tc, blk), lambda i, j: (i, 0, j, 0))
        out_shape = jax.ShapeDtypeStruct((b, l // blk, width, blk), BF16)
    else:
        out_spec = pl.BlockSpec((1, None, l, tc), lambda i, j: (i, j, 0, 0))
        out_shape = jax.ShapeDtypeStruct((b, width // tc, l, tc), BF16)
    return pl.pallas_call(
        functools.partial(_conv_body, n_ctx=n_ctx, n_blocks=(l - n_ctx) // blk, scale=scale,
                          transposed=transposed),
        grid=(b, width // tc),
        in_specs=[pl.BlockSpec((1, l, tc), lambda i, j: (i, 0, j + off)),
                  pl.BlockSpec((9, tc), lambda i, j: (0, j + off)),
                  pl.BlockSpec((1, tc), lambda i, j: (0, j + off))],
        out_specs=out_spec,
        out_shape=out_shape,
        compiler_params=pltpu.CompilerParams(
            dimension_semantics=("parallel", "parallel"), vmem_limit_bytes=VMEM_LIMIT),
        name="mlstm_conv_kt" if transposed else "mlstm_conv_q",
    )(qk_raw, conv_w.reshape(9, c), conv_b.reshape(1, c))


def _mlstm_scan_body(q_ref, kt_ref, v_ref, g_ref, o_ref, acc_ref, st_ref, n_ref, row_ref,
                     *, n_ctx_chunks, n_chunks):
    c = SCAN_CHUNK
    dv = v_ref.shape[2]
    st_ref[...] = jnp.zeros(st_ref.shape, F32)
    n_ref[...] = jnp.zeros(n_ref.shape, F32)
    masks = _tri_masks(c)

    gates = g_ref[0]
    token_rows, chunk_rows = [], []
    for dirn in range(2):
        u, b_r, cm = (gates[:, 5 * dirn + r:5 * dirn + r + 1] for r in range(3))
        b_end = gates[:, 5 * dirn + 3:5 * dirn + 4, 0:1]
        g_max = gates[:, 5 * dirn + 4:5 * dirn + 5, 0:1]
        m = jnp.zeros((1, 1, 1), F32)
        m_prev_l, m_new_l = [None] * n_chunks, [None] * n_chunks
        for s in range(n_chunks):
            idx = s
            if dirn:
                idx = n_ctx_chunks - 1 - s if s < n_ctx_chunks else n_chunks - 1 - (s - n_ctx_chunks)
            m_prev_l[idx] = m
            m = jnp.maximum(b_end[idx:idx + 1] + m, g_max[idx:idx + 1])
            m_new_l[idx] = m
        m_prev = jnp.concatenate(m_prev_l, axis=0)
        m_new = jnp.concatenate(m_new_l, axis=0)
        mm = jnp.maximum(m_prev, cm)
        token_rows += [u, jnp.exp(b_end + u - m_new), mm, b_r + mm]
        chunk_rows += [jnp.broadcast_to(jnp.exp(b_end + m_prev - m_new), (n_chunks, 1, c)),
                       jnp.broadcast_to(m_prev, (n_chunks, 1, c))]
    row_ref[...] = jnp.concatenate(token_rows + chunk_rows + [jnp.zeros((n_chunks, 4, c), F32)], axis=1)

    def lane_bcast_col(row):
        return jnp.broadcast_to(row, (LANES, c)).T

    def step(s, first):
        dirs = (0, 1)
        idxs = (s, _bw_chunk(s, n_ctx_chunks, n_chunks))
        starts = [pl.multiple_of(idx * c, c) for idx in idxs]
        q = [q_ref[0, pl.ds(st, c), :] for st in starts]
        kt = [kt_ref[0, idx] for idx in idxs]
        v = [v_ref[0, pl.ds(st, c), :] for st in starts]
        rows = [row_ref[idx] for idx in idxs]
        state = [st_ref[d] for d in dirs]
        nrm = [n_ref[d] for d in dirs]
        qk = [_dot(q[d], kt[d]) for d in dirs]
        mm_b = [lane_bcast_col(rows[d][4 * d + 2:4 * d + 3]) for d in dirs]
        bmm_b = [lane_bcast_col(rows[d][4 * d + 3:4 * d + 4]) for d in dirs]
        m_prev = [rows[d][9 + 2 * d:10 + 2 * d, 0:1] for d in dirs]
        decay = [rows[d][8 + 2 * d:9 + 2 * d, 0:1] for d in dirs]
        sc = [qk[d] * jnp.exp(jnp.where(masks[d], rows[d][4 * d:4 * d + 1]
                                        - jnp.concatenate([mm_b[d]] * (c // LANES), axis=1), NEG_INF))
              for d in dirs]
        w_inter = [jnp.exp(m_prev[d] - mm_b[d]) for d in dirs]
        intra = [_dot(sc[d].astype(BF16), v[d]) for d in dirs]
        inter = [_dot(q[d], state[d].astype(BF16)) for d in dirs]
        qn = [_dot(q[d], nrm[d].astype(BF16)) for d in dirs]
        wkt = [kt[d].astype(F32) * rows[d][4 * d + 1:4 * d + 2] for d in dirs]
        upd = [_dot(wkt[d].astype(BF16), v[d]) for d in dirs]
        for d in dirs:
            num = intra[d] + inter[d] * jnp.concatenate([w_inter[d]] * (dv // LANES), axis=1)
            den = jnp.sum(sc[d], axis=1, keepdims=True) + qn[d] * w_inter[d]
            rcp = 1.0 / jnp.maximum(jnp.abs(den), jnp.exp(-bmm_b[d]))
            _emit(o_ref, acc_ref, starts[d], c, num * jnp.concatenate([rcp] * (dv // LANES), axis=1), first[d])
        for d in dirs:
            st_ref[d] = decay[d] * state[d] + upd[d]
            n_ref[d] = decay[d] * nrm[d] + jnp.sum(wkt[d], axis=1, keepdims=True)

    _scan_steps(step, n_ctx_chunks, n_chunks, 1)


def _mlstm_scan(q, kt, v, gates, n_ctx):
    b, _, l, dk = q.shape
    dv = v.shape[3]
    n_chunks = l // SCAN_CHUNK
    return pl.pallas_call(
        functools.partial(_mlstm_scan_body, n_ctx_chunks=n_ctx // SCAN_CHUNK, n_chunks=n_chunks),
        grid=(b, N_HEADS),
        in_specs=[pl.BlockSpec((1, None, l, dk), _HEAD_SLAB),
                  pl.BlockSpec((1, n_chunks, dk, SCAN_CHUNK), lambda i, h: (i, 0, h, 0)),
                  pl.BlockSpec((1, None, l, dv), _HEAD_SLAB),
                  pl.BlockSpec((1, n_chunks, None, GATE_ROWS, SCAN_CHUNK), lambda i, h: (i, 0, h, 0, 0))],
        out_specs=pl.BlockSpec((1, None, l, dv), _HEAD_SLAB),
        out_shape=jax.ShapeDtypeStruct((b, N_HEADS, l, dv), BF16),
        scratch_shapes=[pltpu.VMEM((l, dv), F32),
                        pltpu.VMEM((2, dk, dv), F32),
                        pltpu.VMEM((2, dk, LANES), F32),
                        pltpu.VMEM((n_chunks, 16, SCAN_CHUNK), F32)],
        compiler_params=pltpu.CompilerParams(
            dimension_semantics=("parallel", "parallel"), vmem_limit_bytes=VMEM_LIMIT),
        name="mlstm_scan",
    )(q, kt, v, gates)


def _ret_scan_body(q_ref, kt_ref, v_ref, dl_ref, o_ref, acc_ref, st_ref, dm_ref, *, n_ctx_chunks, n_chunks):
    c = SCAN_CHUNK
    st_ref[...] = jnp.zeros(st_ref.shape, F32)
    masks = _tri_masks(c)
    row = lax.broadcasted_iota(jnp.int32, (c, c), 0)
    col = lax.broadcasted_iota(jnp.int32, (c, c), 1)
    dist = jnp.abs(row - col).astype(F32)
    rid = lax.broadcasted_iota(jnp.int32, (c, 1), 0).astype(F32)
    lid = lax.broadcasted_iota(jnp.int32, (1, c), 1).astype(F32)
    log_gamma = []
    for dirn in range(2):
        lg = _log_sigmoid(dl_ref[0, dirn])[0:1, 0:1]
        log_gamma.append(lg)
        dm_ref[dirn] = jnp.where(masks[dirn], jnp.exp(lg * dist), 0.0)

    pos_col = (rid + 1.0, c - rid)
    pos_row = (lid + 1.0, c - lid)

    def step(s, first):
        dirs = (0, 1)
        idxs = (s, _bw_chunk(s, n_ctx_chunks, n_chunks))
        starts = [pl.multiple_of(idx * c, c) for idx in idxs]
        q = [q_ref[0, pl.ds(st, c), :] for st in starts]
        kt = [kt_ref[0, idx] for idx in idxs]
        v = [v_ref[0, pl.ds(st, c), :] for st in starts]
        state = [st_ref[d] for d in dirs]
        sc = [(_dot(q[d], kt[d]) * dm_ref[d]).astype(BF16) for d in dirs]
        intra = [_dot(sc[d], v[d]) for d in dirs]
        inter = [_dot(q[d], state[d].astype(BF16)) for d in dirs]
        kdt = [(kt[d].astype(F32) * jnp.exp(log_gamma[d] * (c - pos_row[d]))).astype(BF16) for d in dirs]
        upd = [_dot(kdt[d], v[d]) for d in dirs]
        for d in dirs:
            _emit(o_ref, acc_ref, starts[d], c, intra[d] + inter[d] * jnp.exp(log_gamma[d] * pos_col[d]), first[d])
        for d in dirs:
            st_ref[d] = jnp.exp(log_gamma[d] * c) * state[d] + upd[d]

    _scan_steps(step, n_ctx_chunks, n_chunks, 4)


def _ret_scan(q, kt, v, dl, n_ctx):
    b, _, l, dk = q.shape
    dv = v.shape[3]
    n_chunks = l // SCAN_CHUNK
    return pl.pallas_call(
        functools.partial(_ret_scan_body, n_ctx_chunks=n_ctx // SCAN_CHUNK, n_chunks=n_chunks),
        grid=(b, N_HEADS),
        in_specs=[pl.BlockSpec((1, None, l, dk), _HEAD_SLAB),
                  pl.BlockSpec((1, n_chunks, dk, SCAN_CHUNK), lambda i, h: (i, 0, h, 0)),
                  pl.BlockSpec((1, None, l, dv), _HEAD_SLAB),
                  pl.BlockSpec((1, 2, 8, LANES), lambda i, h: (h, 0, 0, 0))],
        out_specs=pl.BlockSpec((1, None, l, dv), _HEAD_SLAB),
        out_shape=jax.ShapeDtypeStruct((b, N_HEADS, l, dv), BF16),
        scratch_shapes=[pltpu.VMEM((l, dv), F32),
                        pltpu.VMEM((2, dk, dv), F32),
                        pltpu.VMEM((2, SCAN_CHUNK, SCAN_CHUNK), F32)],
        compiler_params=pltpu.CompilerParams(
            dimension_semantics=("parallel", "parallel"), vmem_limit_bytes=VMEM_LIMIT),
        name="ret_scan",
    )(q, kt, v, dl)


def _gla_scan_body(q_ref, kt_ref, v_ref, code_ref, w2_ref, b2_ref, o_ref,
                   acc_ref, st_ref, sq_ref, kdt_ref, e_ref, m01_ref, *, n_ctx_chunks, n_chunks):
    c = GLA_CHUNK
    t = GLA_SUB
    n_sub = c // t
    dk = q_ref.shape[2]
    dv = v_ref.shape[2]
    st_ref[...] = jnp.zeros(st_ref.shape, F32)
    masks = _tri_masks(c)
    for d in range(2):
        m01_ref[d] = masks[d].astype(BF16)
    lid = lax.broadcasted_iota(jnp.int32, (1, c), 1)

    def intra(it, carry, first_chunk, group):
        chains = [(first_chunk + it * group + j, dirn) for j in range(group) for dirn in range(2)]
        starts = [pl.multiple_of(idx * c, c) for idx, _ in chains]
        qf = [q_ref[0, pl.ds(st, c), :].astype(F32) for st in starts]
        ktf = [kt_ref[0, idx].astype(F32) for idx, _ in chains]
        code = [code_ref[0, pl.ds(st, c), :] for st in starts]
        a = [_log_sigmoid(_dot(code[k], w2_ref[d]) + b2_ref[d]) * (1.0 / GLA_TAU) for k, (_, d) in enumerate(chains)]
        parts = [_split3(x) for x in a]
        m01 = [m01_ref[d] for d in range(2)]
        m01_twice = [jnp.concatenate([m, m], axis=1) for m in m01]
        b = [_dot(m01_twice[d], jnp.concatenate([parts[k][2], parts[k][1]], axis=0)) for k, (_, d) in enumerate(chains)]
        b = [b[k] + _dot(m01[d], parts[k][0]) for k, (_, d) in enumerate(chains)]
        qp, kp, kdt, e_end = [], [], [], []
        for k, (_, dirn) in enumerate(chains):
            refs = []
            for sb in range(n_sub):
                r0 = sb * t + (t - 1 if dirn else 0)
                refs.append(b[k][r0:r0 + 1] - a[k][r0:r0 + 1])
            own = jnp.concatenate([jnp.broadcast_to(r, (t, dk)) for r in refs], axis=0)
            b_rel = b[k] - own
            q_own = qf[k] * jnp.exp(b_rel)
            k_rel_t = (ktf[k] * jnp.exp(jnp.minimum(-(b_rel.T), GLA_EXP_CLAMP))).astype(BF16)
            r_end = 0 if dirn else c - 1
            b_end = b[k][r_end:r_end + 1]
            kdt.append((ktf[k] * jnp.exp((b_end - b[k]).T)).astype(BF16))
            e_end.append(jnp.broadcast_to(jnp.exp(b_end), (8, dk)))
            col_blocks = []
            for sb in range(n_sub):
                pieces = []
                for j in range(n_sub):
                    if (j <= sb) if dirn else (j >= sb):
                        qj = q_own[j * t:(j + 1) * t]
                        pieces.append(qj if j == sb else qj * jnp.exp(refs[j] - refs[sb]))
                    else:
                        pieces.append(jnp.zeros((t, dk), F32))
                col_blocks.append(jnp.concatenate(pieces, axis=0).astype(BF16))
            qp.append(col_blocks)
            kp.append(jnp.concatenate(
                [jnp.where(jnp.logical_and(lid >= sb * t, lid < (sb + 1) * t), k_rel_t, 0.0)
                 for sb in range(n_sub)], axis=0))
        s = [_dot(jnp.concatenate(qp[k], axis=1), kp[k]) for k in range(len(chains))]
        for k, (idx, dirn) in enumerate(chains):
            sq_ref[dirn, pl.ds(starts[k], c), 0:c] = jnp.where(masks[dirn], s[k], 0.0).astype(BF16)
            sq_ref[dirn, pl.ds(starts[k], c), c:2 * c] = qp[k][n_sub - 1 if dirn else 0]
            kdt_ref[dirn, idx] = kdt[k]
            e_ref[dirn, idx] = e_end[k]
        return carry

    n_full = n_chunks // GLA_INTRA_CHUNKS
    lax.fori_loop(0, n_full, functools.partial(intra, first_chunk=0, group=GLA_INTRA_CHUNKS), 0)
    rest = n_chunks - n_full * GLA_INTRA_CHUNKS
    if rest:
        intra(0, 0, n_full * GLA_INTRA_CHUNKS, rest)

    def step(s, first):
        dirs = (0, 1)
        idxs = (s, _bw_chunk(s, n_ctx_chunks, n_chunks))
        starts = [pl.multiple_of(idx * c, c) for idx in idxs]
        v = [v_ref[0, pl.ds(st, c), :] for st in starts]
        state = [st_ref[d] for d in dirs]
        out = [_dot(sq_ref[d, pl.ds(starts[d], c), :],
                    jnp.concatenate([v[d], state[d].astype(BF16)], axis=0)) for d in dirs]
        upd = [_dot(kdt_ref[d, idxs[d]], v[d]) for d in dirs]
        e_cols = [jnp.concatenate([e_ref[d, idxs[d]]] * (dk // 8), axis=0).T for d in dirs]
        for d in dirs:
            _emit(o_ref, acc_ref, starts[d], c, out[d], first[d])
        for d in dirs:
            st_ref[d] = jnp.concatenate([e_cols[d]] * (dv // dk), axis=1) * state[d] + upd[d]

    _scan_steps(step, n_ctx_chunks, n_chunks, 8)


def _gla_scan(q, kt, v, code, w2, b2, n_ctx):
    b, _, l, dk = q.shape
    dv = v.shape[3]
    n_chunks = l // GLA_CHUNK
    return pl.pallas_call(
        functools.partial(_gla_scan_body, n_ctx_chunks=n_ctx // GLA_CHUNK, n_chunks=n_chunks),
        grid=(b, N_HEADS),
        in_specs=[pl.BlockSpec((1, None, l, dk), _HEAD_SLAB),
                  pl.BlockSpec((1, n_chunks, dk, GLA_CHUNK), lambda i, h: (i, 0, h, 0)),
                  pl.BlockSpec((1, None, l, dv), _HEAD_SLAB),
                  pl.BlockSpec((1, l, LANES), lambda i, h: (i, 0, 0)),
                  pl.BlockSpec((2, LANES, dk), lambda i, h: (0, 0, h)),
                  pl.BlockSpec((2, 1, dk), lambda i, h: (0, 0, h))],
        out_specs=pl.BlockSpec((1, None, l, dv), _HEAD_SLAB),
        out_shape=jax.ShapeDtypeStruct((b, N_HEADS, l, dv), BF16),
        scratch_shapes=[pltpu.VMEM((l, dv), F32),
                        pltpu.VMEM((2, dk, dv), F32),
                        pltpu.VMEM((2, l, GLA_CHUNK + dk), BF16),
                        pltpu.VMEM((2, n_chunks, dk, GLA_CHUNK), BF16),
                        pltpu.VMEM((2, n_chunks, 8, dk), F32),
                        pltpu.VMEM((2, GLA_CHUNK, GLA_CHUNK), BF16)],
        compiler_params=pltpu.CompilerParams(
            dimension_semantics=("parallel", "parallel"), vmem_limit_bytes=VMEM_LIMIT),
        name="gla_scan",
    )(q, kt, v, code, w2, b2)


def _out_core(x, o_ref, z_ref, hw_ref, ow_ref, mod_ref, center):
    dv = o_ref.shape[3]
    proj = None
    for hh in range(N_HEADS):
        cols = slice(hh * dv, (hh + 1) * dv)
        yh = o_ref[0, hh].astype(F32)
        if center:
            yh = yh - jnp.mean(yh, axis=-1, keepdims=True)
        yn = yh * lax.rsqrt(jnp.mean(yh * yh, axis=-1, keepdims=True) + NORM_EPS) * hw_ref[:, cols]
        a = (yn * _silu(z_ref[0, :, cols].astype(F32))).astype(BF16)
        term = _dot(a, ow_ref[cols, :])
        proj = term if proj is None else proj + term
    return x + mod_ref[0, 0][2:3] * proj


def _out_final_body(x, o_ref, z_ref, hw_ref, ow_ref, mod_ref, fw_ref, out_ref, *, center):
    hn = _out_core(x, o_ref, z_ref, hw_ref, ow_ref, mod_ref, center)
    out_ref[0] = hn * lax.rsqrt(jnp.mean(hn * hn, axis=-1, keepdims=True) + NORM_EPS) * fw_ref[...]


def _out_proj_body(x, o_ref, z_ref, hw_ref, ow_ref, mod_ref, *rest, center, proj_body, n_proj_in):
    proj_in, h_out_ref, proj_out = rest[:n_proj_in], rest[n_proj_in], rest[n_proj_in + 1:]
    hn = _out_core(x, o_ref, z_ref, hw_ref, ow_ref, mod_ref, center)
    h_out_ref[0] = hn
    proj_body(hn, *proj_in, *proj_out)


def _out_operands(o, z, head_norm_w, out_w, modsel, skip):
    _, n_heads, _, dv = o.shape
    di = n_heads * dv
    d = out_w.shape[1]
    tok = lambda i, t: (i, t + skip, 0)
    specs = [pl.BlockSpec((1, n_heads, TOKEN_TILE, dv), lambda i, t: (i, 0, t + skip, 0)),
             pl.BlockSpec((1, TOKEN_TILE, di), tok),
             pl.BlockSpec((1, di), lambda i, t: (0, 0)),
             pl.BlockSpec((di, d), lambda i, t: (0, 0), pipeline_mode=pl.Buffered(1)),
             pl.BlockSpec((1, 1, 3, d), lambda i, t: (i, jnp.minimum(t + skip, 1), 0, 0))]
    return [o, z, head_norm_w.reshape(1, di), out_w, modsel], specs


def _out_final_call(o, z, head_norm_w, out_w, h, modsel, final_w, n_ctx, center):
    b, _, l, _ = o.shape
    d = out_w.shape[1]
    skip = n_ctx // TOKEN_TILE
    nt = l // TOKEN_TILE - skip
    h_args, h_specs = _hidden_operands(h, n_ctx, skip)
    args, specs = _out_operands(o, z, head_norm_w, out_w, modsel, skip)
    return pl.pallas_call(
        _with_hidden_tile(functools.partial(_out_final_body, center=center),
                          len(h_args), n_ctx // TOKEN_TILE, skip),
        grid=(b, nt),
        in_specs=h_specs + specs + [pl.BlockSpec((1, d), lambda i, t: (0, 0))],
        out_specs=pl.BlockSpec((1, TOKEN_TILE, d), lambda i, t: (i, t, 0)),
        out_shape=jax.ShapeDtypeStruct((b, nt * TOKEN_TILE, d), F32),
        compiler_params=pltpu.CompilerParams(
            dimension_semantics=("parallel", "parallel"), vmem_limit_bytes=VMEM_LIMIT),
        name="out_final",
    )(*h_args, *args, final_w.reshape(1, d))


def _out_proj_call(o, z, head_norm_w, out_w, h, modsel, n_ctx, center, spec, next_modsel, next_norm_w):
    b, _, l, _ = o.shape
    d = out_w.shape[1]
    h_args, h_specs = _hidden_operands(h, n_ctx)
    args, specs = _out_operands(o, z, head_norm_w, out_w, modsel, 0)
    p_in, p_out_specs, p_out_shape = _proj_specs(spec, b, l, d)
    res = pl.pallas_call(
        _with_hidden_tile(functools.partial(_out_proj_body, center=center, proj_body=spec["body"],
                                            n_proj_in=len(p_in)),
                          len(h_args), n_ctx // TOKEN_TILE),
        grid=(b, l // TOKEN_TILE),
        in_specs=h_specs + specs + p_in,
        out_specs=[pl.BlockSpec((1, TOKEN_TILE, d), lambda i, t: (i, t, 0))] + p_out_specs,
        out_shape=[jax.ShapeDtypeStruct((b, l, d), F32)] + p_out_shape,
        compiler_params=pltpu.CompilerParams(
            dimension_semantics=("parallel", "parallel"), vmem_limit_bytes=VMEM_LIMIT),
        name="out_proj_" + spec["name"],
    )(*h_args, *args, next_modsel, next_norm_w.reshape(1, d), *spec["consts"], *spec["tables"])
    return res[0], res[1:]


def _pad_cols(w, n):
    return jnp.pad(w, ((0, 0), (0, n - w.shape[1])))


def _mlstm_proj_spec(l, n_ctx, in_w, conv_w, conv_b, gate_b):
    d = in_w.shape[0]
    di = 2 * d
    qk_w = 2 * d
    wg = _pad_cols(in_w[:, qk_w + 2 * di:], LANES).astype(BF16)
    gb = _pad_cols(gate_b.reshape(1, -1), LANES)
    return dict(body=_proj_mlstm_body, consts=[in_w.astype(BF16), wg, gb], tables=[], name="mlstm",
                outs=[(qk_w, F32, None), (di, BF16, "heads"), (di, BF16, None),
                      (0, F32, (N_HEADS, GATE_ROWS, TOKEN_TILE))])


def _mlstm_mix(outs, n_ctx, in_w, conv_w, conv_b, gate_b):
    qk_raw, v, z, g = outs
    b, l, qk_w = qk_raw.shape
    head_qk = qk_w // (2 * N_HEADS)
    q = _conv(qk_raw, conv_w, conv_b, n_ctx, 0, qk_w // 2, 1.0, False)
    kt = _conv(qk_raw, conv_w, conv_b, n_ctx, qk_w // 2, qk_w // 2, head_qk ** -0.5, True)
    return _mlstm_scan(q, kt, v, g, n_ctx), z


def _gla_proj_spec(l, n_ctx, in_w, gk_w2, gk_b):
    d = in_w.shape[0]
    di = 2 * d
    key = gk_w2.shape[2]
    assert key // N_HEADS == LANES
    wlr = _pad_cols(in_w[:, 2 * key + 2 * di:], LANES).astype(BF16)
    return dict(body=functools.partial(_proj_gla_body, q_scale=(key // N_HEADS) ** -0.5),
                consts=[in_w.astype(BF16), wlr], tables=[], name="gla",
                outs=[(key, BF16, "heads"), (key, BF16, GLA_CHUNK), (di, BF16, "heads"), (di, BF16, None),
                      (LANES, BF16, None)])


def _gla_mix(outs, n_ctx, in_w, gk_w2, gk_b):
    q, kt, v, z, code = outs
    key = gk_w2.shape[2]
    w2 = jnp.zeros((2, LANES, key), F32)
    w2 = w2.at[0, :GLA_RANK].set(gk_w2[0]).at[1, GLA_RANK:2 * GLA_RANK].set(gk_w2[1]).astype(BF16)
    return _gla_scan(q, kt, v, code, w2, gk_b.reshape(2, 1, key), n_ctx), z


def _rope_tables(n_ctx, n_lat, head_k):
    quarter = head_k // 4
    pos = jnp.arange(n_lat, dtype=jnp.int32)
    inv_freq = ROPE_BASE ** (-jnp.arange(quarter, dtype=F32) / quarter)
    cos_parts, sin_parts = [], []
    for p in ((pos // GRID_W).astype(F32), (pos % GRID_W).astype(F32)):
        ang = p[:, None] * inv_freq[None, :]
        cos_parts += [jnp.cos(ang), jnp.cos(ang)]
        sin_parts += [-jnp.sin(ang), jnp.sin(ang)]
    cos = jnp.concatenate(cos_parts, axis=1)
    sin = jnp.concatenate(sin_parts, axis=1)
    cos = jnp.concatenate([jnp.ones((n_ctx, head_k), F32), cos], axis=0)
    sin = jnp.concatenate([jnp.zeros((n_ctx, head_k), F32), sin], axis=0)
    return cos, sin


def _ret_proj_spec(l, n_ctx, in_w, decay_logit):
    d = in_w.shape[0]
    di = 2 * d
    key = (in_w.shape[1] - 2 * di) // 2
    head_k = key // N_HEADS
    cos, sin = _rope_tables(n_ctx, l - n_ctx, head_k)
    return dict(body=functools.partial(_proj_ret_body, k_scale=head_k ** -0.5),
                consts=[in_w.astype(BF16)], tables=[cos, sin], name="ret",
                outs=[(key, BF16, "heads"), (key, BF16, SCAN_CHUNK), (di, BF16, "heads"), (di, BF16, None)])


def _ret_mix(outs, n_ctx, in_w, decay_logit):
    q, kt, v, z = outs
    dl = jnp.broadcast_to(decay_logit.astype(F32).T[:, :, None, None], (N_HEADS, 2, 8, LANES))
    return _ret_scan(q, kt, v, dl, n_ctx), z


_PROJ_SPEC = {"mlstm": _mlstm_proj_spec, "gla": _gla_proj_spec, "retention": _ret_proj_spec}
_MIX = {"mlstm": _mlstm_mix, "gla": _gla_mix, "retention": _ret_mix}


def kernel(x, c, ctx, c_ctx, l0_norm_w, l0_ada_w, l0_ada_b, l0_in_w, l0_conv_w, l0_conv_b, l0_gate_b, l0_head_norm_w, l0_out_w, l1_norm_w, l1_ada_w, l1_ada_b, l1_in_w, l1_gk_w2, l1_gk_b, l1_head_norm_w, l1_out_w, l2_norm_w, l2_ada_w, l2_ada_b, l2_in_w, l2_decay_logit, l2_head_norm_w, l2_out_w, l3_norm_w, l3_ada_w, l3_ada_b, l3_in_w, l3_conv_w, l3_conv_b, l3_gate_b, l3_head_norm_w, l3_out_w, final_norm_w):
    layers = (
        ("mlstm", l0_norm_w, l0_ada_w, l0_ada_b, l0_out_w, l0_head_norm_w, (l0_in_w, l0_conv_w, l0_conv_b, l0_gate_b)),
        ("gla", l1_norm_w, l1_ada_w, l1_ada_b, l1_out_w, l1_head_norm_w, (l1_in_w, l1_gk_w2, l1_gk_b)),
        ("retention", l2_norm_w, l2_ada_w, l2_ada_b, l2_out_w, l2_head_norm_w, (l2_in_w, l2_decay_logit)),
        ("mlstm", l3_norm_w, l3_ada_w, l3_ada_b, l3_out_w, l3_head_norm_w, (l3_in_w, l3_conv_w, l3_conv_b, l3_gate_b)),
    )
    b, _, d = x.shape
    n_ctx = ctx.shape[1]
    assert n_ctx % TOKEN_TILE == 0 and n_ctx % SCAN_CHUNK == 0 and x.shape[1] % SCAN_CHUNK == 0
    assert TOKEN_TILE == SCAN_CHUNK and TOKEN_TILE % GLA_CHUNK == 0
    l = n_ctx + x.shape[1]
    h = (ctx, x)
    cc = jnp.concatenate([c, c_ctx[None, :], jnp.zeros((7, d), F32)], axis=0)
    modsels = []
    for _, _, ada_w, ada_b, _, _, _ in layers:
        mod = _ada(cc, ada_w, ada_b)
        mod_lat = mod[:b].reshape(b, 1, 3, d)
        mod_ctx = jnp.broadcast_to(mod[b].reshape(1, 1, 3, d), (b, 1, 3, d))
        modsels.append(jnp.concatenate([mod_ctx, mod_lat], axis=1))
    specs = [_PROJ_SPEC[kind](l, n_ctx, *params) for kind, _, _, _, _, _, params in layers]
    proj_outs = _proj_call(specs[0], h, n_ctx, modsels[0], layers[0][1])
    for li, (kind, _, _, _, out_w, head_norm_w, params) in enumerate(layers):
        o, z = _MIX[kind](proj_outs, n_ctx, *params)
        center = kind != "gla"
        if li == len(layers) - 1:
            return _out_final_call(o, z, head_norm_w, out_w.astype(BF16), h, modsels[li], final_norm_w,
                                   n_ctx, center)
        h, proj_outs = _out_proj_call(o, z, head_norm_w, out_w.astype(BF16), h, modsels[li], n_ctx, center,
                                      specs[li + 1], modsels[li + 1], layers[li + 1][1])
```

```python
import functools

import jax
import jax.numpy as jnp
from jax import lax
from jax.experimental import pallas as pl
from jax.experimental.pallas import tpu as pltpu

N_HEADS = 4
GRID_W = 64
GLA_RANK = 16
GLA_TAU = 16.0
ROPE_BASE = 10000.0
NORM_EPS = 1e-6
NEG_INF = -1e30

TOKEN_TILE = 256
SCAN_CHUNK = 256
GLA_CHUNK = 128
GLA_SUB = 16
GLA_EXP_CLAMP = 80.0
GLA_INTRA_CHUNKS = 8
LANES = 128
VMEM_LIMIT = 56 * 1024 * 1024

F32 = jnp.float32
BF16 = jnp.bfloat16


def _silu(x):
    half = 0.5 * x
    return half + half * jnp.tanh(half)


def _log_sigmoid(x):
    return jnp.minimum(x, 0.0) - jnp.log(1.0 + jnp.exp(-jnp.abs(x)))


def _dot(a, b):
    return jnp.dot(a, b, preferred_element_type=F32)


def _split3(a):
    hi = a.astype(BF16)
    r1 = a - hi.astype(F32)
    mid = r1.astype(BF16)
    lo = (r1 - mid.astype(F32)).astype(BF16)
    return hi, mid, lo


def _dot01_left(m01, a):
    hi, mid, lo = _split3(a)
    return _dot(m01, lo) + _dot(m01, mid) + _dot(m01, hi)


def _tri_masks(n):
    row = lax.broadcasted_iota(jnp.int32, (n, n), 0)
    col = lax.broadcasted_iota(jnp.int32, (n, n), 1)
    return col <= row, col >= row


_HEAD_SLAB = lambda i, h: (i, h, 0, 0)


def _prefix_scan(x, lane, backward, op, identity):
    ax = x.ndim - 1
    n = x.shape[ax]
    sh = 1
    while sh < n:
        if backward:
            x = op(x, jnp.where(lane < n - sh, pltpu.roll(x, n - sh, ax), identity))
        else:
            x = op(x, jnp.where(lane >= sh, pltpu.roll(x, sh, ax), identity))
        sh *= 2
    return x


def _bw_chunk(step, n_ctx_chunks, n_chunks):
    return jnp.where(step < n_ctx_chunks, n_ctx_chunks - 1 - step, n_chunks - 1 - (step - n_ctx_chunks))


def _scan_steps(step, n_ctx_chunks, n_chunks, unroll):
    for start, stop in ((0, n_ctx_chunks), (n_ctx_chunks, n_chunks)):
        lax.fori_loop(start, stop, lambda s, carry: step(s) or carry, 0,
                      unroll=unroll if (stop - start) % unroll == 0 else 1)


def _emit(o_ref, dirn, start, n, value):
    o_ref[0, dirn, pl.ds(start, n), :] = value.astype(o_ref.dtype)


def _ada_body(c_ref, w_ref, b_ref, o_ref):
    s = _silu(c_ref[...])
    o_ref[...] = jnp.dot(s, w_ref[...], preferred_element_type=F32,
                         precision=lax.Precision.HIGHEST) + b_ref[...]


def _ada(cc, w, b):
    rows, d = cc.shape
    n = w.shape[1]
    tn = 512
    return pl.pallas_call(
        _ada_body,
        grid=(n // tn,),
        in_specs=[pl.BlockSpec((rows, d), lambda j: (0, 0)),
                  pl.BlockSpec((d, tn), lambda j: (0, j)),
                  pl.BlockSpec((1, tn), lambda j: (0, j))],
        out_specs=pl.BlockSpec((rows, tn), lambda j: (0, j)),
        out_shape=jax.ShapeDtypeStruct((rows, n), F32),
        name="ada",
    )(cc, w, b.reshape(1, n))


def _hidden_operands(h, n_ctx, tile_offset=0):
    if not isinstance(h, tuple):
        d = h.shape[2]
        return [h], [pl.BlockSpec((1, TOKEN_TILE, d), lambda i, t: (i, t + tile_offset, 0))]
    ctx, lat = h
    d = lat.shape[2]
    nct = n_ctx // TOKEN_TILE
    return [ctx, lat], [
        pl.BlockSpec((1, TOKEN_TILE, d), lambda i, t: (i, jnp.minimum(t + tile_offset, nct - 1), 0)),
        pl.BlockSpec((1, TOKEN_TILE, d), lambda i, t: (i, jnp.maximum(t + tile_offset - nct, 0), 0))]


def _with_hidden_tile(body, n_hidden, n_ctx_tiles, tile_offset=0):
    def kernel_fn(*refs):
        if n_hidden == 1:
            x = refs[0][0]
        else:
            is_ctx = pl.program_id(1) + tile_offset < n_ctx_tiles
            x = jnp.where(is_ctx, refs[0][0], refs[1][0])
        body(x, *refs[n_hidden:])
    return kernel_fn


def _modulated(x, mod_ref, nw_ref):
    y = x * lax.rsqrt(jnp.mean(x * x, axis=-1, keepdims=True) + NORM_EPS) * nw_ref[...]
    m = mod_ref[0, 0]
    return (y * (1.0 + m[1:2]) + m[0:1]).astype(BF16)


def _store_heads(ref, x):
    w = ref.shape[3]
    for hh in range(ref.shape[1]):
        ref[0, hh] = x[:, hh * w:(hh + 1) * w].astype(ref.dtype)


def _store_chunked_t(ref, x, chunk):
    xt = x.T
    for j in range(x.shape[0] // chunk):
        ref[0, j] = xt[:, j * chunk:(j + 1) * chunk].astype(ref.dtype)


GATE_ROWS = 16


def _proj_mlstm_body(x, mod_ref, nw_ref, w_ref, wg_ref, gb_ref, qk_ref, v_ref, z_ref, g_ref):
    u = _modulated(x, mod_ref, nw_ref)
    qk_w, di = qk_ref.shape[2], z_ref.shape[2]
    g = _dot(u, wg_ref[...]) + gb_ref[...]
    lane = lax.broadcasted_iota(jnp.int32, g.shape, 1)
    is_forget = (lane // N_HEADS) % 2 == 1
    g = jnp.where(is_forget, _log_sigmoid(g), g)
    _store_gate_rows(g_ref, g)
    qk_ref[0] = _dot(u, w_ref[:, :qk_w])
    _store_heads(v_ref, _dot(u, w_ref[:, qk_w:qk_w + di]))
    z_ref[0] = _dot(u, w_ref[:, qk_w + di:qk_w + 2 * di]).astype(BF16)


def _store_gate_rows(g_ref, g):
    c = g.shape[0]
    gt = g.T[:4 * N_HEADS]
    lane_row = lax.broadcasted_iota(jnp.int32, (1, c), 1)
    per_dir = []
    for dirn in range(2):
        i_r = gt[2 * dirn * N_HEADS:(2 * dirn + 1) * N_HEADS]
        a_r = gt[(2 * dirn + 1) * N_HEADS:(2 * dirn + 2) * N_HEADS]
        b_r = _prefix_scan(a_r, lane_row, bool(dirn), jnp.add, 0.0)
        u = i_r - b_r
        b_end = jnp.sum(a_r, axis=1, keepdims=True)
        g_max = jnp.max(b_end + u, axis=1, keepdims=True)
        per_dir.append((u, b_r, _prefix_scan(u, lane_row, bool(dirn), jnp.maximum, NEG_INF),
                        jnp.broadcast_to(b_end, u.shape), jnp.broadcast_to(g_max, u.shape)))
    for hh in range(N_HEADS):
        rows = [arr[hh:hh + 1] for dirn in range(2) for arr in per_dir[dirn]]
        g_ref[0, 0, hh] = jnp.concatenate(rows + [jnp.zeros((GATE_ROWS - len(rows), c), F32)], axis=0)


def _proj_gla_body(x, mod_ref, nw_ref, w_ref, wlr_ref, q_ref, kt_ref, v_ref, z_ref, code_ref, *, q_scale):
    u = _modulated(x, mod_ref, nw_ref)
    key, di = kt_ref.shape[2], z_ref.shape[2]
    _store_heads(q_ref, _dot(u, w_ref[:, :key]) * q_scale)
    _store_chunked_t(kt_ref, _dot(u, w_ref[:, key:2 * key]), GLA_CHUNK)
    _store_heads(v_ref, _dot(u, w_ref[:, 2 * key:2 * key + di]))
    z_ref[0] = _dot(u, w_ref[:, 2 * key + di:2 * key + 2 * di]).astype(BF16)
    code_ref[0] = _dot(u, wlr_ref[...]).astype(BF16)


def _rope(t, cos, sin):
    outs = []
    for g in range(t.shape[1] // LANES):
        tg = t[:, g * LANES:(g + 1) * LANES]
        cg = cos[:, (g % 2) * LANES:(g % 2 + 1) * LANES]
        sg = sin[:, (g % 2) * LANES:(g % 2 + 1) * LANES]
        outs.append(tg * cg + pltpu.roll(tg, LANES // 2, 1) * sg)
    return jnp.concatenate(outs, axis=1)


def _proj_ret_body(x, mod_ref, nw_ref, w_ref, cos_ref, sin_ref, q_ref, kt_ref, v_ref, z_ref, *, k_scale):
    u = _modulated(x, mod_ref, nw_ref)
    key, di = kt_ref.shape[2], z_ref.shape[2]
    cos = cos_ref[...]
    sin = sin_ref[...]
    _store_heads(q_ref, _rope(_dot(u, w_ref[:, :key]), cos, sin))
    _store_chunked_t(kt_ref, _rope(_dot(u, w_ref[:, key:2 * key]), cos, sin) * k_scale, SCAN_CHUNK)
    _store_heads(v_ref, _dot(u, w_ref[:, 2 * key:2 * key + di]))
    z_ref[0] = _dot(u, w_ref[:, 2 * key + di:2 * key + 2 * di]).astype(BF16)


def _proj_specs(spec, b, l, d):
    in_specs = [pl.BlockSpec((1, 1, 3, d), lambda i, t: (i, jnp.minimum(t, 1), 0, 0)),
                pl.BlockSpec((1, d), lambda i, t: (0, 0))]
    for a in spec["consts"]:
        in_specs.append(pl.BlockSpec(a.shape, lambda i, t, nd=a.ndim: (0,) * nd, pipeline_mode=pl.Buffered(1)))
    for a in spec["tables"]:
        in_specs.append(pl.BlockSpec((TOKEN_TILE, a.shape[1]), lambda i, t: (t, 0)))
    out_specs, out_shape = [], []
    for n, dt, chunk in spec["outs"]:
        if chunk is None:
            out_specs.append(pl.BlockSpec((1, TOKEN_TILE, n), lambda i, t: (i, t, 0)))
            out_shape.append(jax.ShapeDtypeStruct((b, l, n), dt))
        elif isinstance(chunk, tuple):
            out_specs.append(pl.BlockSpec((1, 1) + chunk, lambda i, t, z=(0,) * len(chunk): (i, t) + z))
            out_shape.append(jax.ShapeDtypeStruct((b, l // TOKEN_TILE) + chunk, dt))
        elif chunk == "heads":
            out_specs.append(pl.BlockSpec((1, N_HEADS, TOKEN_TILE, n // N_HEADS), lambda i, t: (i, 0, t, 0)))
            out_shape.append(jax.ShapeDtypeStruct((b, N_HEADS, l, n // N_HEADS), dt))
        else:
            out_specs.append(pl.BlockSpec((1, TOKEN_TILE // chunk, n, chunk), lambda i, t: (i, t, 0, 0)))
            out_shape.append(jax.ShapeDtypeStruct((b, l // chunk, n, chunk), dt))
    return in_specs, out_specs, out_shape


def _proj_call(spec, h, n_ctx, modsel, norm_w):
    h_args, h_specs = _hidden_operands(h, n_ctx)
    b, _, d = h_args[-1].shape
    l = sum(a.shape[1] for a in h_args)
    in_specs, out_specs, out_shape = _proj_specs(spec, b, l, d)
    return pl.pallas_call(
        _with_hidden_tile(spec["body"], len(h_args), n_ctx // TOKEN_TILE),
        grid=(b, l // TOKEN_TILE),
        in_specs=h_specs + in_specs,
        out_specs=out_specs,
        out_shape=out_shape,
        compiler_params=pltpu.CompilerParams(
            dimension_semantics=("parallel", "parallel"), vmem_limit_bytes=VMEM_LIMIT),
        name="proj_" + spec["name"],
    )(*h_args, modsel, norm_w.reshape(1, d), *spec["consts"], *spec["tables"])


def _conv_body(x_ref, w_ref, b_ref, o_ref, *, n_ctx, n_blocks, scale, transposed):
    blk = SCAN_CHUNK
    l = x_ref.shape[1]
    w = w_ref[...]
    bias = b_ref[...]
    rid = lax.broadcasted_iota(jnp.int32, (blk, 1), 0)

    def emit(chunk, start, acc):
        y = _silu(acc + bias) * scale
        if transposed:
            o_ref[0, chunk] = y.T.astype(BF16)
        else:
            o_ref[0, pl.ds(start, blk), :] = y.astype(BF16)

    def combine(a, bm, cc, first, last):
        return (jnp.where(first, 0.0, pltpu.roll(a, 1, 0)) + bm
                + jnp.where(last, 0.0, pltpu.roll(cc, blk - 1, 0)))

    xc = x_ref[0, 0:blk, :]
    emit(0, 0, combine(xc * w[3:4], xc * w[4:5], xc * w[5:6], rid == 0, rid == blk - 1))

    col = rid % GRID_W

    def lat_block(it, carry):
        base = pl.multiple_of(n_ctx + it * blk, blk)
        mid = x_ref[0, pl.ds(base, blk), :]
        up_edge = x_ref[0, pl.ds(pl.multiple_of(base - GRID_W, GRID_W), GRID_W), :]
        dn_start = pl.multiple_of(jnp.minimum(base + blk, l - GRID_W), GRID_W)
        dn_edge = x_ref[0, pl.ds(dn_start, GRID_W), :]
        up_edge = jnp.where(it > 0, up_edge, 0.0)
        dn_edge = jnp.where(it < n_blocks - 1, dn_edge, 0.0)
        up = jnp.concatenate([up_edge, mid[:blk - GRID_W]], axis=0)
        down = jnp.concatenate([mid[GRID_W:], dn_edge], axis=0)
        a = up * w[0:1] + mid * w[3:4] + down * w[6:7]
        bm = up * w[1:2] + mid * w[4:5] + down * w[7:8]
        cc = up * w[2:3] + mid * w[5:6] + down * w[8:9]
        emit(n_ctx // blk + it, base, combine(a, bm, cc, col == 0, col == GRID_W - 1))
        return carry

    lax.fori_loop(0, n_blocks, lat_block, 0)


def _conv(qk_raw, conv_w, conv_b, n_ctx, col_offset, width, scale, transposed):
    b, l, c = qk_raw.shape
    tc = 256
    blk = SCAN_CHUNK
    assert n_ctx == blk and (l - n_ctx) % blk == 0 and blk % GRID_W == 0
    off = col_offset // tc
    if transposed:
        out_spec = pl.BlockSpec((1, l // blk, tc, blk), lambda i, j: (i, 0, j, 0))
        out_shape = jax.ShapeDtypeStruct((b, l // blk, width, blk), BF16)
    else:
        out_spec = pl.BlockSpec((1, None, l, tc), lambda i, j: (i, j, 0, 0))
        out_shape = jax.ShapeDtypeStruct((b, width // tc, l, tc), BF16)
    return pl.pallas_call(
        functools.partial(_conv_body, n_ctx=n_ctx, n_blocks=(l - n_ctx) // blk, scale=scale,
                          transposed=transposed),
        grid=(b, width // tc),
        in_specs=[pl.BlockSpec((1, l, tc), lambda i, j: (i, 0, j + off)),
                  pl.BlockSpec((9, tc), lambda i, j: (0, j + off)),
                  pl.BlockSpec((1, tc), lambda i, j: (0, j + off))],
        out_specs=out_spec,
        out_shape=out_shape,
        compiler_params=pltpu.CompilerParams(
            dimension_semantics=("parallel", "parallel"), vmem_limit_bytes=VMEM_LIMIT),
        name="mlstm_conv_kt" if transposed else "mlstm_conv_q",
    )(qk_raw, conv_w.reshape(9, c), conv_b.reshape(1, c))


def _mlstm_scan_body(q_ref, kt_ref, v_ref, g_ref, o_ref, st_ref, n_ref, row_ref,
                     *, n_ctx_chunks, n_chunks):
    c = SCAN_CHUNK
    dv = v_ref.shape[2]
    st_ref[...] = jnp.zeros(st_ref.shape, F32)
    n_ref[...] = jnp.zeros(n_ref.shape, F32)
    masks = _tri_masks(c)

    gates = g_ref[0]
    token_rows, chunk_rows = [], []
    for dirn in range(2):
        u, b_r, cm = (gates[:, 5 * dirn + r:5 * dirn + r + 1] for r in range(3))
        b_end = gates[:, 5 * dirn + 3:5 * dirn + 4, 0:1]
        g_max = gates[:, 5 * dirn + 4:5 * dirn + 5, 0:1]
        m = jnp.zeros((1, 1, 1), F32)
        m_prev_l, m_new_l = [None] * n_chunks, [None] * n_chunks
        for s in range(n_chunks):
            idx = s
            if dirn:
                idx = n_ctx_chunks - 1 - s if s < n_ctx_chunks else n_chunks - 1 - (s - n_ctx_chunks)
            m_prev_l[idx] = m
            m = jnp.maximum(b_end[idx:idx + 1] + m, g_max[idx:idx + 1])
            m_new_l[idx] = m
        m_prev = jnp.concatenate(m_prev_l, axis=0)
        m_new = jnp.concatenate(m_new_l, axis=0)
        mm = jnp.maximum(m_prev, cm)
        token_rows += [u, jnp.exp(b_end + u - m_new), mm, b_r + mm]
        chunk_rows += [jnp.broadcast_to(jnp.exp(b_end + m_prev - m_new), (n_chunks, 1, c)),
                       jnp.broadcast_to(m_prev, (n_chunks, 1, c))]
    row_ref[...] = jnp.concatenate(token_rows + chunk_rows + [jnp.zeros((n_chunks, 4, c), F32)], axis=1)

    def lane_bcast_col(row):
        return jnp.broadcast_to(row, (LANES, c)).T

    def step(s):
        dirs = (0, 1)
        idxs = (s, _bw_chunk(s, n_ctx_chunks, n_chunks))
        starts = [pl.multiple_of(idx * c, c) for idx in idxs]
        q = [q_ref[0, pl.ds(st, c), :] for st in starts]
        kt = [kt_ref[0, idx] for idx in idxs]
        v = [v_ref[0, pl.ds(st, c), :] for st in starts]
        rows = [row_ref[idx] for idx in idxs]
        state = [st_ref[d] for d in dirs]
        nrm = [n_ref[d] for d in dirs]
        qk = [_dot(q[d], kt[d]) for d in dirs]
        mm_b = [lane_bcast_col(rows[d][4 * d + 2:4 * d + 3]) for d in dirs]
        bmm_b = [lane_bcast_col(rows[d][4 * d + 3:4 * d + 4]) for d in dirs]
        m_prev = [rows[d][9 + 2 * d:10 + 2 * d, 0:1] for d in dirs]
        decay = [rows[d][8 + 2 * d:9 + 2 * d, 0:1] for d in dirs]
        sc = [qk[d] * jnp.exp(jnp.where(masks[d], rows[d][4 * d:4 * d + 1]
                                        - jnp.concatenate([mm_b[d]] * (c // LANES), axis=1), NEG_INF))
              for d in dirs]
        w_inter = [jnp.exp(m_prev[d] - mm_b[d]) for d in dirs]
        intra = [_dot(sc[d].astype(BF16), v[d]) for d in dirs]
        inter = [_dot(q[d], state[d].astype(BF16)) for d in dirs]
        qn = [_dot(q[d], nrm[d].astype(BF16)) for d in dirs]
        wkt = [kt[d].astype(F32) * rows[d][4 * d + 1:4 * d + 2] for d in dirs]
        upd = [_dot(wkt[d].astype(BF16), v[d]) for d in dirs]
        for d in dirs:
            num = intra[d] + inter[d] * jnp.concatenate([w_inter[d]] * (dv // LANES), axis=1)
            den = jnp.sum(sc[d], axis=1, keepdims=True) + qn[d] * w_inter[d]
            rcp = 1.0 / jnp.maximum(jnp.abs(den), jnp.exp(-bmm_b[d]))
            _emit(o_ref, d, starts[d], c, num * jnp.concatenate([rcp] * (dv // LANES), axis=1))
        for d in dirs:
            st_ref[d] = decay[d] * state[d] + upd[d]
            n_ref[d] = decay[d] * nrm[d] + jnp.sum(wkt[d], axis=1, keepdims=True)

    _scan_steps(step, n_ctx_chunks, n_chunks, 1)


def _mlstm_scan(q, kt, v, gates, n_ctx):
    b, _, l, dk = q.shape
    dv = v.shape[3]
    n_chunks = l // SCAN_CHUNK
    return pl.pallas_call(
        functools.partial(_mlstm_scan_body, n_ctx_chunks=n_ctx // SCAN_CHUNK, n_chunks=n_chunks),
        grid=(b, N_HEADS),
        in_specs=[pl.BlockSpec((1, None, l, dk), _HEAD_SLAB),
                  pl.BlockSpec((1, n_chunks, dk, SCAN_CHUNK), lambda i, h: (i, 0, h, 0)),
                  pl.BlockSpec((1, None, l, dv), _HEAD_SLAB),
                  pl.BlockSpec((1, n_chunks, None, GATE_ROWS, SCAN_CHUNK), lambda i, h: (i, 0, h, 0, 0))],
        out_specs=pl.BlockSpec((1, 2, None, l, dv), lambda i, h: (i, 0, h, 0, 0)),
        out_shape=jax.ShapeDtypeStruct((b, 2, N_HEADS, l, dv), BF16),
        scratch_shapes=[pltpu.VMEM((2, dk, dv), F32),
                        pltpu.VMEM((2, dk, LANES), F32),
                        pltpu.VMEM((n_chunks, 16, SCAN_CHUNK), F32)],
        compiler_params=pltpu.CompilerParams(
            dimension_semantics=("parallel", "parallel"), vmem_limit_bytes=VMEM_LIMIT),
        name="mlstm_scan",
    )(q, kt, v, gates)


def _ret_scan_body(q_ref, kt_ref, v_ref, dl_ref, o_ref, st_ref, dm_ref, *, n_ctx_chunks, n_chunks):
    c = SCAN_CHUNK
    st_ref[...] = jnp.zeros(st_ref.shape, F32)
    masks = _tri_masks(c)
    row = lax.broadcasted_iota(jnp.int32, (c, c), 0)
    col = lax.broadcasted_iota(jnp.int32, (c, c), 1)
    dist = jnp.abs(row - col).astype(F32)
    rid = lax.broadcasted_iota(jnp.int32, (c, 1), 0).astype(F32)
    lid = lax.broadcasted_iota(jnp.int32, (1, c), 1).astype(F32)
    log_gamma = []
    for dirn in range(2):
        lg = _log_sigmoid(dl_ref[0, dirn])[0:1, 0:1]
        log_gamma.append(lg)
        dm_ref[dirn] = jnp.where(masks[dirn], jnp.exp(lg * dist), 0.0)

    pos_col = (rid + 1.0, c - rid)
    pos_row = (lid + 1.0, c - lid)

    def step(s):
        dirs = (0, 1)
        idxs = (s, _bw_chunk(s, n_ctx_chunks, n_chunks))
        starts = [pl.multiple_of(idx * c, c) for idx in idxs]
        q = [q_ref[0, pl.ds(st, c), :] for st in starts]
        kt = [kt_ref[0, idx] for idx in idxs]
        v = [v_ref[0, pl.ds(st, c), :] for st in starts]
        state = [st_ref[d] for d in dirs]
        sc = [(_dot(q[d], kt[d]) * dm_ref[d]).astype(BF16) for d in dirs]
        intra = [_dot(sc[d], v[d]) for d in dirs]
        inter = [_dot(q[d], state[d].astype(BF16)) for d in dirs]
        kdt = [(kt[d].astype(F32) * jnp.exp(log_gamma[d] * (c - pos_row[d]))).astype(BF16) for d in dirs]
        upd = [_dot(kdt[d], v[d]) for d in dirs]
        for d in dirs:
            _emit(o_ref, d, starts[d], c, intra[d] + inter[d] * jnp.exp(log_gamma[d] * pos_col[d]))
        for d in dirs:
            st_ref[d] = jnp.exp(log_gamma[d] * c) * state[d] + upd[d]

    _scan_steps(step, n_ctx_chunks, n_chunks, 4)


def _ret_scan(q, kt, v, dl, n_ctx):
    b, _, l, dk = q.shape
    dv = v.shape[3]
    n_chunks = l // SCAN_CHUNK
    return pl.pallas_call(
        functools.partial(_ret_scan_body, n_ctx_chunks=n_ctx // SCAN_CHUNK, n_chunks=n_chunks),
        grid=(b, N_HEADS),
        in_specs=[pl.BlockSpec((1, None, l, dk), _HEAD_SLAB),
                  pl.BlockSpec((1, n_chunks, dk, SCAN_CHUNK), lambda i, h: (i, 0, h, 0)),
                  pl.BlockSpec((1, None, l, dv), _HEAD_SLAB),
                  pl.BlockSpec((1, 2, 8, LANES), lambda i, h: (h, 0, 0, 0))],
        out_specs=pl.BlockSpec((1, 2, None, l, dv), lambda i, h: (i, 0, h, 0, 0)),
        out_shape=jax.ShapeDtypeStruct((b, 2, N_HEADS, l, dv), BF16),
        scratch_shapes=[pltpu.VMEM((2, dk, dv), F32),
                        pltpu.VMEM((2, SCAN_CHUNK, SCAN_CHUNK), F32)],
        compiler_params=pltpu.CompilerParams(
            dimension_semantics=("parallel", "parallel"), vmem_limit_bytes=VMEM_LIMIT),
        name="ret_scan",
    )(q, kt, v, dl)


def _gla_scan_body(q_ref, kt_ref, v_ref, code_ref, w2_ref, b2_ref, o_ref,
                   st_ref, sq_ref, kdt_ref, e_ref, m01_ref, *, n_ctx_chunks, n_chunks):
    c = GLA_CHUNK
    t = GLA_SUB
    n_sub = c // t
    dk = q_ref.shape[2]
    dv = v_ref.shape[2]
    st_ref[...] = jnp.zeros(st_ref.shape, F32)
    masks = _tri_masks(c)
    for d in range(2):
        m01_ref[d] = masks[d].astype(BF16)
    lid = lax.broadcasted_iota(jnp.int32, (1, c), 1)

    def intra(it, carry, first_chunk, group):
        chains = [(first_chunk + it * group + j, dirn) for j in range(group) for dirn in range(2)]
        starts = [pl.multiple_of(idx * c, c) for idx, _ in chains]
        qf = [q_ref[0, pl.ds(st, c), :].astype(F32) for st in starts]
        ktf = [kt_ref[0, idx].astype(F32) for idx, _ in chains]
        code = [code_ref[0, pl.ds(st, c), :] for st in starts]
        a = [_log_sigmoid(_dot(code[k], w2_ref[d]) + b2_ref[d]) * (1.0 / GLA_TAU) for k, (_, d) in enumerate(chains)]
        parts = [_split3(x) for x in a]
        m01 = [m01_ref[d] for d in range(2)]
        m01_twice = [jnp.concatenate([m, m], axis=1) for m in m01]
        b = [_dot(m01_twice[d], jnp.concatenate([parts[k][2], parts[k][1]], axis=0)) for k, (_, d) in enumerate(chains)]
        b = [b[k] + _dot(m01[d], parts[k][0]) for k, (_, d) in enumerate(chains)]
        qp, kp, kdt, e_end = [], [], [], []
        for k, (_, dirn) in enumerate(chains):
            refs = []
            for sb in range(n_sub):
                r0 = sb * t + (t - 1 if dirn else 0)
                refs.append(b[k][r0:r0 + 1] - a[k][r0:r0 + 1])
            own = jnp.concatenate([jnp.broadcast_to(r, (t, dk)) for r in refs], axis=0)
            b_rel = b[k] - own
            q_own = qf[k] * jnp.exp(b_rel)
            k_rel_t = (ktf[k] * jnp.exp(jnp.minimum(-(b_rel.T), GLA_EXP_CLAMP))).astype(BF16)
            r_end = 0 if dirn else c - 1
            b_end = b[k][r_end:r_end + 1]
            kdt.append((ktf[k] * jnp.exp((b_end - b[k]).T)).astype(BF16))
            e_end.append(jnp.broadcast_to(jnp.exp(b_end), (8, dk)))
            col_blocks = []
            for sb in range(n_sub):
                pieces = []
                for j in range(n_sub):
                    if (j <= sb) if dirn else (j >= sb):
                        qj = q_own[j * t:(j + 1) * t]
                        pieces.append(qj if j == sb else qj * jnp.exp(refs[j] - refs[sb]))
                    else:
                        pieces.append(jnp.zeros((t, dk), F32))
                col_blocks.append(jnp.concatenate(pieces, axis=0).astype(BF16))
            qp.append(col_blocks)
            kp.append(jnp.concatenate(
                [jnp.where(jnp.logical_and(lid >= sb * t, lid < (sb + 1) * t), k_rel_t, 0.0)
                 for sb in range(n_sub)], axis=0))
        s = [_dot(jnp.concatenate(qp[k], axis=1), kp[k]) for k in range(len(chains))]
        for k, (idx, dirn) in enumerate(chains):
            sq_ref[dirn, pl.ds(starts[k], c), 0:c] = jnp.where(masks[dirn], s[k], 0.0).astype(BF16)
            sq_ref[dirn, pl.ds(starts[k], c), c:2 * c] = qp[k][n_sub - 1 if dirn else 0]
            kdt_ref[dirn, idx] = kdt[k]
            e_ref[dirn, idx] = e_end[k]
        return carry

    n_full = n_chunks // GLA_INTRA_CHUNKS
    lax.fori_loop(0, n_full, functools.partial(intra, first_chunk=0, group=GLA_INTRA_CHUNKS), 0)
    rest = n_chunks - n_full * GLA_INTRA_CHUNKS
    if rest:
        intra(0, 0, n_full * GLA_INTRA_CHUNKS, rest)

    def step(s):
        dirs = (0, 1)
        idxs = (s, _bw_chunk(s, n_ctx_chunks, n_chunks))
        starts = [pl.multiple_of(idx * c, c) for idx in idxs]
        v = [v_ref[0, pl.ds(st, c), :] for st in starts]
        state = [st_ref[d] for d in dirs]
        out = [_dot(sq_ref[d, pl.ds(starts[d], c), :],
                    jnp.concatenate([v[d], state[d].astype(BF16)], axis=0)) for d in dirs]
        upd = [_dot(kdt_ref[d, idxs[d]], v[d]) for d in dirs]
        e_cols = [jnp.concatenate([e_ref[d, idxs[d]]] * (dk // 8), axis=0).T for d in dirs]
        for d in dirs:
            _emit(o_ref, d, starts[d], c, out[d])
        for d in dirs:
            st_ref[d] = jnp.concatenate([e_cols[d]] * (dv // dk), axis=1) * state[d] + upd[d]

    _scan_steps(step, n_ctx_chunks, n_chunks, 4)


def _gla_scan(q, kt, v, code, w2, b2, n_ctx):
    b, _, l, dk = q.shape
    dv = v.shape[3]
    n_chunks = l // GLA_CHUNK
    return pl.pallas_call(
        functools.partial(_gla_scan_body, n_ctx_chunks=n_ctx // GLA_CHUNK, n_chunks=n_chunks),
        grid=(b, N_HEADS),
        in_specs=[pl.BlockSpec((1, None, l, dk), _HEAD_SLAB),
                  pl.BlockSpec((1, n_chunks, dk, GLA_CHUNK), lambda i, h: (i, 0, h, 0)),
                  pl.BlockSpec((1, None, l, dv), _HEAD_SLAB),
                  pl.BlockSpec((1, l, LANES), lambda i, h: (i, 0, 0)),
                  pl.BlockSpec((2, LANES, dk), lambda i, h: (0, 0, h)),
                  pl.BlockSpec((2, 1, dk), lambda i, h: (0, 0, h))],
        out_specs=pl.BlockSpec((1, 2, None, l, dv), lambda i, h: (i, 0, h, 0, 0)),
        out_shape=jax.ShapeDtypeStruct((b, 2, N_HEADS, l, dv), BF16),
        scratch_shapes=[pltpu.VMEM((2, dk, dv), F32),
                        pltpu.VMEM((2, l, GLA_CHUNK + dk), BF16),
                        pltpu.VMEM((2, n_chunks, dk, GLA_CHUNK), BF16),
                        pltpu.VMEM((2, n_chunks, 8, dk), F32),
                        pltpu.VMEM((2, GLA_CHUNK, GLA_CHUNK), BF16)],
        compiler_params=pltpu.CompilerParams(
            dimension_semantics=("parallel", "parallel"), vmem_limit_bytes=VMEM_LIMIT),
        name="gla_scan",
    )(q, kt, v, code, w2, b2)


def _out_core(x, o_ref, z_ref, hw_ref, ow_ref, mod_ref, center):
    dv = o_ref.shape[4]
    proj = None
    for hh in range(N_HEADS):
        cols = slice(hh * dv, (hh + 1) * dv)
        yh = o_ref[0, 0, hh].astype(F32) + o_ref[0, 1, hh].astype(F32)
        if center:
            yh = yh - jnp.mean(yh, axis=-1, keepdims=True)
        yn = yh * lax.rsqrt(jnp.mean(yh * yh, axis=-1, keepdims=True) + NORM_EPS) * hw_ref[:, cols]
        a = (yn * _silu(z_ref[0, :, cols].astype(F32))).astype(BF16)
        term = _dot(a, ow_ref[cols, :])
        proj = term if proj is None else proj + term
    return x + mod_ref[0, 0][2:3] * proj


def _out_final_body(x, o_ref, z_ref, hw_ref, ow_ref, mod_ref, fw_ref, out_ref, *, center):
    hn = _out_core(x, o_ref, z_ref, hw_ref, ow_ref, mod_ref, center)
    out_ref[0] = hn * lax.rsqrt(jnp.mean(hn * hn, axis=-1, keepdims=True) + NORM_EPS) * fw_ref[...]


def _out_proj_body(x, o_ref, z_ref, hw_ref, ow_ref, mod_ref, *rest, center, proj_body, n_proj_in):
    proj_in, h_out_ref, proj_out = rest[:n_proj_in], rest[n_proj_in], rest[n_proj_in + 1:]
    hn = _out_core(x, o_ref, z_ref, hw_ref, ow_ref, mod_ref, center)
    h_out_ref[0] = hn
    proj_body(hn, *proj_in, *proj_out)


def _out_operands(o, z, head_norm_w, out_w, modsel, skip):
    _, _, n_heads, _, dv = o.shape
    di = n_heads * dv
    d = out_w.shape[1]
    tok = lambda i, t: (i, t + skip, 0)
    specs = [pl.BlockSpec((1, 2, n_heads, TOKEN_TILE, dv), lambda i, t: (i, 0, 0, t + skip, 0)),
             pl.BlockSpec((1, TOKEN_TILE, di), tok),
             pl.BlockSpec((1, di), lambda i, t: (0, 0)),
             pl.BlockSpec((di, d), lambda i, t: (0, 0), pipeline_mode=pl.Buffered(1)),
             pl.BlockSpec((1, 1, 3, d), lambda i, t: (i, jnp.minimum(t + skip, 1), 0, 0))]
    return [o, z, head_norm_w.reshape(1, di), out_w, modsel], specs


def _out_final_call(o, z, head_norm_w, out_w, h, modsel, final_w, n_ctx, center):
    b, _, _, l, _ = o.shape
    d = out_w.shape[1]
    skip = n_ctx // TOKEN_TILE
    nt = l // TOKEN_TILE - skip
    h_args, h_specs = _hidden_operands(h, n_ctx, skip)
    args, specs = _out_operands(o, z, head_norm_w, out_w, modsel, skip)
    return pl.pallas_call(
        _with_hidden_tile(functools.partial(_out_final_body, center=center),
                          len(h_args), n_ctx // TOKEN_TILE, skip),
        grid=(b, nt),
        in_specs=h_specs + specs + [pl.BlockSpec((1, d), lambda i, t: (0, 0))],
        out_specs=pl.BlockSpec((1, TOKEN_TILE, d), lambda i, t: (i, t, 0)),
        out_shape=jax.ShapeDtypeStruct((b, nt * TOKEN_TILE, d), F32),
        compiler_params=pltpu.CompilerParams(
            dimension_semantics=("parallel", "parallel"), vmem_limit_bytes=VMEM_LIMIT),
        name="out_final",
    )(*h_args, *args, final_w.reshape(1, d))


def _out_proj_call(o, z, head_norm_w, out_w, h, modsel, n_ctx, center, spec, next_modsel, next_norm_w):
    b, _, _, l, _ = o.shape
    d = out_w.shape[1]
    h_args, h_specs = _hidden_operands(h, n_ctx)
    args, specs = _out_operands(o, z, head_norm_w, out_w, modsel, 0)
    p_in, p_out_specs, p_out_shape = _proj_specs(spec, b, l, d)
    res = pl.pallas_call(
        _with_hidden_tile(functools.partial(_out_proj_body, center=center, proj_body=spec["body"],
                                            n_proj_in=len(p_in)),
                          len(h_args), n_ctx // TOKEN_TILE),
        grid=(b, l // TOKEN_TILE),
        in_specs=h_specs + specs + p_in,
        out_specs=[pl.BlockSpec((1, TOKEN_TILE, d), lambda i, t: (i, t, 0))] + p_out_specs,
        out_shape=[jax.ShapeDtypeStruct((b, l, d), F32)] + p_out_shape,
        compiler_params=pltpu.CompilerParams(
            dimension_semantics=("parallel", "parallel"), vmem_limit_bytes=VMEM_LIMIT),
        name="out_proj_" + spec["name"],
    )(*h_args, *args, next_modsel, next_norm_w.reshape(1, d), *spec["consts"], *spec["tables"])
    return res[0], res[1:]


def _pad_cols(w, n):
    return jnp.pad(w, ((0, 0), (0, n - w.shape[1])))


def _mlstm_proj_spec(l, n_ctx, in_w, conv_w, conv_b, gate_b):
    d = in_w.shape[0]
    di = 2 * d
    qk_w = 2 * d
    wg = _pad_cols(in_w[:, qk_w + 2 * di:], LANES).astype(BF16)
    gb = _pad_cols(gate_b.reshape(1, -1), LANES)
    return dict(body=_proj_mlstm_body, consts=[in_w.astype(BF16), wg, gb], tables=[], name="mlstm",
                outs=[(qk_w, F32, None), (di, BF16, "heads"), (di, BF16, None),
                      (0, F32, (N_HEADS, GATE_ROWS, TOKEN_TILE))])


def _mlstm_mix(outs, n_ctx, in_w, conv_w, conv_b, gate_b):
    qk_raw, v, z, g = outs
    b, l, qk_w = qk_raw.shape
    head_qk = qk_w // (2 * N_HEADS)
    q = _conv(qk_raw, conv_w, conv_b, n_ctx, 0, qk_w // 2, 1.0, False)
    kt = _conv(qk_raw, conv_w, conv_b, n_ctx, qk_w // 2, qk_w // 2, head_qk ** -0.5, True)
    return _mlstm_scan(q, kt, v, g, n_ctx), z


def _gla_proj_spec(l, n_ctx, in_w, gk_w2, gk_b):
    d = in_w.shape[0]
    di = 2 * d
    key = gk_w2.shape[2]
    assert key // N_HEADS == LANES
    wlr = _pad_cols(in_w[:, 2 * key + 2 * di:], LANES).astype(BF16)
    return dict(body=functools.partial(_proj_gla_body, q_scale=(key // N_HEADS) ** -0.5),
                consts=[in_w.astype(BF16), wlr], tables=[], name="gla",
                outs=[(key, BF16, "heads"), (key, BF16, GLA_CHUNK), (di, BF16, "heads"), (di, BF16, None),
                      (LANES, BF16, None)])


def _gla_mix(outs, n_ctx, in_w, gk_w2, gk_b):
    q, kt, v, z, code = outs
    key = gk_w2.shape[2]
    w2 = jnp.zeros((2, LANES, key), F32)
    w2 = w2.at[0, :GLA_RANK].set(gk_w2[0]).at[1, GLA_RANK:2 * GLA_RANK].set(gk_w2[1]).astype(BF16)
    return _gla_scan(q, kt, v, code, w2, gk_b.reshape(2, 1, key), n_ctx), z


def _rope_tables(n_ctx, n_lat, head_k):
    quarter = head_k // 4
    pos = jnp.arange(n_lat, dtype=jnp.int32)
    inv_freq = ROPE_BASE ** (-jnp.arange(quarter, dtype=F32) / quarter)
    cos_parts, sin_parts = [], []
    for p in ((pos // GRID_W).astype(F32), (pos % GRID_W).astype(F32)):
        ang = p[:, None] * inv_freq[None, :]
        cos_parts += [jnp.cos(ang), jnp.cos(ang)]
        sin_parts += [-jnp.sin(ang), jnp.sin(ang)]
    cos = jnp.concatenate(cos_parts, axis=1)
    sin = jnp.concatenate(sin_parts, axis=1)
    cos = jnp.concatenate([jnp.ones((n_ctx, head_k), F32), cos], axis=0)
    sin = jnp.concatenate([jnp.zeros((n_ctx, head_k), F32), sin], axis=0)
    return cos, sin


def _ret_proj_spec(l, n_ctx, in_w, decay_logit):
    d = in_w.shape[0]
    di = 2 * d
    key = (in_w.shape[1] - 2 * di) // 2
    head_k = key // N_HEADS
    cos, sin = _rope_tables(n_ctx, l - n_ctx, head_k)
    return dict(body=functools.partial(_proj_ret_body, k_scale=head_k ** -0.5),
                consts=[in_w.astype(BF16)], tables=[cos, sin], name="ret",
                outs=[(key, BF16, "heads"), (key, BF16, SCAN_CHUNK), (di, BF16, "heads"), (di, BF16, None)])


def _ret_mix(outs, n_ctx, in_w, decay_logit):
    q, kt, v, z = outs
    dl = jnp.broadcast_to(decay_logit.astype(F32).T[:, :, None, None], (N_HEADS, 2, 8, LANES))
    return _ret_scan(q, kt, v, dl, n_ctx), z


_PROJ_SPEC = {"mlstm": _mlstm_proj_spec, "gla": _gla_proj_spec, "retention": _ret_proj_spec}
_MIX = {"mlstm": _mlstm_mix, "gla": _gla_mix, "retention": _ret_mix}


def kernel(x, c, ctx, c_ctx, l0_norm_w, l0_ada_w, l0_ada_b, l0_in_w, l0_conv_w, l0_conv_b, l0_gate_b, l0_head_norm_w, l0_out_w, l1_norm_w, l1_ada_w, l1_ada_b, l1_in_w, l1_gk_w2, l1_gk_b, l1_head_norm_w, l1_out_w, l2_norm_w, l2_ada_w, l2_ada_b, l2_in_w, l2_decay_logit, l2_head_norm_w, l2_out_w, l3_norm_w, l3_ada_w, l3_ada_b, l3_in_w, l3_conv_w, l3_conv_b, l3_gate_b, l3_head_norm_w, l3_out_w, final_norm_w):
    layers = (
        ("mlstm", l0_norm_w, l0_ada_w, l0_ada_b, l0_out_w, l0_head_norm_w, (l0_in_w, l0_conv_w, l0_conv_b, l0_gate_b)),
        ("gla", l1_norm_w, l1_ada_w, l1_ada_b, l1_out_w, l1_head_norm_w, (l1_in_w, l1_gk_w2, l1_gk_b)),
        ("retention", l2_norm_w, l2_ada_w, l2_ada_b, l2_out_w, l2_head_norm_w, (l2_in_w, l2_decay_logit)),
        ("mlstm", l3_norm_w, l3_ada_w, l3_ada_b, l3_out_w, l3_head_norm_w, (l3_in_w, l3_conv_w, l3_conv_b, l3_gate_b)),
    )
    b, _, d = x.shape
    n_ctx = ctx.shape[1]
    assert n_ctx % TOKEN_TILE == 0 and n_ctx % SCAN_CHUNK == 0 and x.shape[1] % SCAN_CHUNK == 0
    assert TOKEN_TILE == SCAN_CHUNK and TOKEN_TILE % GLA_CHUNK == 0
    l = n_ctx + x.shape[1]
    h = (ctx, x)
    cc = jnp.concatenate([c, c_ctx[None, :], jnp.zeros((7, d), F32)], axis=0)
    modsels = []
    for _, _, ada_w, ada_b, _, _, _ in layers:
        mod = _ada(cc, ada_w, ada_b)
        mod_lat = mod[:b].reshape(b, 1, 3, d)
        mod_ctx = jnp.broadcast_to(mod[b].reshape(1, 1, 3, d), (b, 1, 3, d))
        modsels.append(jnp.concatenate([mod_ctx, mod_lat], axis=1))
    specs = [_PROJ_SPEC[kind](l, n_ctx, *params) for kind, _, _, _, _, _, params in layers]
    proj_outs = _proj_call(specs[0], h, n_ctx, modsels[0], layers[0][1])
    for li, (kind, _, _, _, out_w, head_norm_w, params) in enumerate(layers):
        o, z = _MIX[kind](proj_outs, n_ctx, *params)
        center = kind != "gla"
        if li == len(layers) - 1:
            return _out_final_call(o, z, head_norm_w, out_w.astype(BF16), h, modsels[li], final_norm_w,
                                   n_ctx, center)
        h, proj_outs = _out_proj_call(o, z, head_norm_w, out_w.astype(BF16), h, modsels[li], n_ctx, center,
                                      specs[li + 1], modsels[li + 1], layers[li + 1][1])
```

```python
import functools

import jax
import jax.numpy as jnp
from jax import lax
from jax.experimental import pallas as pl
from jax.experimental.pallas import tpu as pltpu

N_HEADS = 4
GRID_W = 64
GLA_RANK = 16
GLA_TAU = 16.0
ROPE_BASE = 10000.0
NORM_EPS = 1e-6
NEG_INF = -1e30

TOKEN_TILE = 256
SCAN_CHUNK = 256
GLA_CHUNK = 128
GLA_SUB = 16
GLA_EXP_CLAMP = 80.0
GLA_INTRA_CHUNKS = 8
LANES = 128
VMEM_LIMIT = 56 * 1024 * 1024

F32 = jnp.float32
BF16 = jnp.bfloat16


def _silu(x):
    half = 0.5 * x
    return half + half * jnp.tanh(half)


def _log_sigmoid(x):
    return jnp.minimum(x, 0.0) - jnp.log(1.0 + jnp.exp(-jnp.abs(x)))


def _dot(a, b):
    return jnp.dot(a, b, preferred_element_type=F32)


def _split3(a):
    hi = a.astype(BF16)
    r1 = a - hi.astype(F32)
    mid = r1.astype(BF16)
    lo = (r1 - mid.astype(F32)).astype(BF16)
    return hi, mid, lo


def _dot01_left(m01, a):
    hi, mid, lo = _split3(a)
    return _dot(m01, lo) + _dot(m01, mid) + _dot(m01, hi)


def _tri_masks(n):
    row = lax.broadcasted_iota(jnp.int32, (n, n), 0)
    col = lax.broadcasted_iota(jnp.int32, (n, n), 1)
    return col <= row, col >= row


_HEAD_SLAB = lambda i, h: (i, h, 0, 0)


def _prefix_scan(x, lane, backward, op, identity):
    ax = x.ndim - 1
    n = x.shape[ax]
    sh = 1
    while sh < n:
        if backward:
            x = op(x, jnp.where(lane < n - sh, pltpu.roll(x, n - sh, ax), identity))
        else:
            x = op(x, jnp.where(lane >= sh, pltpu.roll(x, sh, ax), identity))
        sh *= 2
    return x


def _bw_chunk(step, n_ctx_chunks, n_chunks):
    return jnp.where(step < n_ctx_chunks, n_ctx_chunks - 1 - step, n_chunks - 1 - (step - n_ctx_chunks))


def _first_touch_groups(n_ctx_chunks, n_chunks):
    def bw_idx(s):
        return n_ctx_chunks - 1 - s if s < n_ctx_chunks else n_chunks - 1 - (s - n_ctx_chunks)
    bw_step = {bw_idx(s): s for s in range(n_chunks)}
    groups = []
    for s in range(n_chunks):
        flags = (s <= bw_step[s], s < bw_idx(s))
        if groups and groups[-1][2] == flags:
            groups[-1][1] = s + 1
        else:
            groups.append([s, s + 1, flags])
    return [tuple(g) for g in groups]


def _scan_steps(step, n_ctx_chunks, n_chunks, unroll):
    for start, stop, first in _first_touch_groups(n_ctx_chunks, n_chunks):
        lax.fori_loop(start, stop, lambda s, carry, first=first: step(s, first) or carry, 0,
                      unroll=unroll if (stop - start) % unroll == 0 else 1)


def _emit(o_ref, acc_ref, start, n, value, first):
    if first:
        acc_ref[pl.ds(start, n), :] = value
    else:
        o_ref[0, pl.ds(start, n), :] = (acc_ref[pl.ds(start, n), :] + value).astype(o_ref.dtype)


def _ada_body(c_ref, *refs):
    n = len(refs) // 3
    s = _silu(c_ref[...])
    for w_ref, b_ref, o_ref in zip(refs[:n], refs[n:2 * n], refs[2 * n:]):
        o_ref[...] = jnp.dot(s, w_ref[...], preferred_element_type=F32,
                             precision=lax.Precision.HIGHEST) + b_ref[...]


def _ada(cc, ws, bs):
    rows, d = cc.shape
    n = ws[0].shape[1]
    tn = 512
    col = lambda j: (0, j)
    return pl.pallas_call(
        _ada_body,
        grid=(n // tn,),
        in_specs=([pl.BlockSpec((rows, d), lambda j: (0, 0))]
                  + [pl.BlockSpec((d, tn), col) for _ in ws] + [pl.BlockSpec((1, tn), col) for _ in bs]),
        out_specs=[pl.BlockSpec((rows, tn), col) for _ in ws],
        out_shape=[jax.ShapeDtypeStruct((rows, n), F32) for _ in ws],
        name="ada",
    )(cc, *ws, *[b.reshape(1, n) for b in bs])


def _hidden_operands(h, n_ctx, tile_offset=0):
    if not isinstance(h, tuple):
        d = h.shape[2]
        return [h], [pl.BlockSpec((1, TOKEN_TILE, d), lambda i, t: (i, t + tile_offset, 0))]
    ctx, lat = h
    d = lat.shape[2]
    nct = n_ctx // TOKEN_TILE
    return [ctx, lat], [
        pl.BlockSpec((1, TOKEN_TILE, d), lambda i, t: (i, jnp.minimum(t + tile_offset, nct - 1), 0)),
        pl.BlockSpec((1, TOKEN_TILE, d), lambda i, t: (i, jnp.maximum(t + tile_offset - nct, 0), 0))]


def _with_hidden_tile(body, n_hidden, n_ctx_tiles, tile_offset=0):
    def kernel_fn(*refs):
        if n_hidden == 1:
            x = refs[0][0]
        else:
            is_ctx = pl.program_id(1) + tile_offset < n_ctx_tiles
            x = jnp.where(is_ctx, refs[0][0], refs[1][0])
        body(x, *refs[n_hidden:])
    return kernel_fn


def _modulated(x, mod_ref, nw_ref):
    y = x * lax.rsqrt(jnp.mean(x * x, axis=-1, keepdims=True) + NORM_EPS) * nw_ref[...]
    m = mod_ref[0, 0]
    return (y * (1.0 + m[1:2]) + m[0:1]).astype(BF16)


def _store_heads(ref, x):
    w = ref.shape[3]
    for hh in range(ref.shape[1]):
        ref[0, hh] = x[:, hh * w:(hh + 1) * w].astype(ref.dtype)


def _store_chunked_t(ref, x, chunk):
    xt = x.T
    for j in range(x.shape[0] // chunk):
        ref[0, j] = xt[:, j * chunk:(j + 1) * chunk].astype(ref.dtype)


GATE_ROWS = 16


def _proj_mlstm_body(x, mod_ref, nw_ref, w_ref, wg_ref, gb_ref, qk_ref, v_ref, z_ref, g_ref):
    u = _modulated(x, mod_ref, nw_ref)
    qk_w, di = qk_ref.shape[2], z_ref.shape[2]
    g = _dot(u, wg_ref[...]) + gb_ref[...]
    lane = lax.broadcasted_iota(jnp.int32, g.shape, 1)
    is_forget = (lane // N_HEADS) % 2 == 1
    g = jnp.where(is_forget, _log_sigmoid(g), g)
    _store_gate_rows(g_ref, g)
    qk_ref[0] = _dot(u, w_ref[:, :qk_w])
    _store_heads(v_ref, _dot(u, w_ref[:, qk_w:qk_w + di]))
    z_ref[0] = _dot(u, w_ref[:, qk_w + di:qk_w + 2 * di]).astype(BF16)


def _store_gate_rows(g_ref, g):
    c = g.shape[0]
    gt = g.T[:4 * N_HEADS]
    lane_row = lax.broadcasted_iota(jnp.int32, (1, c), 1)
    per_dir = []
    for dirn in range(2):
        i_r = gt[2 * dirn * N_HEADS:(2 * dirn + 1) * N_HEADS]
        a_r = gt[(2 * dirn + 1) * N_HEADS:(2 * dirn + 2) * N_HEADS]
        b_r = _prefix_scan(a_r, lane_row, bool(dirn), jnp.add, 0.0)
        u = i_r - b_r
        b_end = jnp.sum(a_r, axis=1, keepdims=True)
        g_max = jnp.max(b_end + u, axis=1, keepdims=True)
        per_dir.append((u, b_r, _prefix_scan(u, lane_row, bool(dirn), jnp.maximum, NEG_INF),
                        jnp.broadcast_to(b_end, u.shape), jnp.broadcast_to(g_max, u.shape)))
    for hh in range(N_HEADS):
        rows = [arr[hh:hh + 1] for dirn in range(2) for arr in per_dir[dirn]]
        g_ref[0, 0, hh] = jnp.concatenate(rows + [jnp.zeros((GATE_ROWS - len(rows), c), F32)], axis=0)


def _proj_gla_body(x, mod_ref, nw_ref, w_ref, wlr_ref, q_ref, kt_ref, v_ref, z_ref, code_ref, *, q_scale):
    u = _modulated(x, mod_ref, nw_ref)
    key, di = kt_ref.shape[2], z_ref.shape[2]
    _store_heads(q_ref, _dot(u, w_ref[:, :key]) * q_scale)
    _store_chunked_t(kt_ref, _dot(u, w_ref[:, key:2 * key]), GLA_CHUNK)
    _store_heads(v_ref, _dot(u, w_ref[:, 2 * key:2 * key + di]))
    z_ref[0] = _dot(u, w_ref[:, 2 * key + di:2 * key + 2 * di]).astype(BF16)
    code_ref[0] = _dot(u, wlr_ref[...]).astype(BF16)


def _rope(t, cos, sin):
    outs = []
    for g in range(t.shape[1] // LANES):
        tg = t[:, g * LANES:(g + 1) * LANES]
        cg = cos[:, (g % 2) * LANES:(g % 2 + 1) * LANES]
        sg = sin[:, (g % 2) * LANES:(g % 2 + 1) * LANES]
        outs.append(tg * cg + pltpu.roll(tg, LANES // 2, 1) * sg)
    return jnp.concatenate(outs, axis=1)


def _proj_ret_body(x, mod_ref, nw_ref, w_ref, cos_ref, sin_ref, q_ref, kt_ref, v_ref, z_ref, *, k_scale):
    u = _modulated(x, mod_ref, nw_ref)
    key, di = kt_ref.shape[2], z_ref.shape[2]
    cos = cos_ref[...]
    sin = sin_ref[...]
    _store_heads(q_ref, _rope(_dot(u, w_ref[:, :key]), cos, sin))
    _store_chunked_t(kt_ref, _rope(_dot(u, w_ref[:, key:2 * key]), cos, sin) * k_scale, SCAN_CHUNK)
    _store_heads(v_ref, _dot(u, w_ref[:, 2 * key:2 * key + di]))
    z_ref[0] = _dot(u, w_ref[:, 2 * key + di:2 * key + 2 * di]).astype(BF16)


def _proj_specs(spec, b, l, d):
    in_specs = [pl.BlockSpec((1, 1, 3, d), lambda i, t: (i, jnp.minimum(t, 1), 0, 0)),
                pl.BlockSpec((1, d), lambda i, t: (0, 0))]
    for a in spec["consts"]:
        in_specs.append(pl.BlockSpec(a.shape, lambda i, t, nd=a.ndim: (0,) * nd, pipeline_mode=pl.Buffered(1)))
    for a in spec["tables"]:
        in_specs.append(pl.BlockSpec((TOKEN_TILE, a.shape[1]), lambda i, t: (t, 0)))
    out_specs, out_shape = [], []
    for n, dt, chunk in spec["outs"]:
        if chunk is None:
            out_specs.append(pl.BlockSpec((1, TOKEN_TILE, n), lambda i, t: (i, t, 0)))
            out_shape.append(jax.ShapeDtypeStruct((b, l, n), dt))
        elif isinstance(chunk, tuple):
            out_specs.append(pl.BlockSpec((1, 1) + chunk, lambda i, t, z=(0,) * len(chunk): (i, t) + z))
            out_shape.append(jax.ShapeDtypeStruct((b, l // TOKEN_TILE) + chunk, dt))
        elif chunk == "heads":
            out_specs.append(pl.BlockSpec((1, N_HEADS, TOKEN_TILE, n // N_HEADS), lambda i, t: (i, 0, t, 0)))
            out_shape.append(jax.ShapeDtypeStruct((b, N_HEADS, l, n // N_HEADS), dt))
        else:
            out_specs.append(pl.BlockSpec((1, TOKEN_TILE // chunk, n, chunk), lambda i, t: (i, t, 0, 0)))
            out_shape.append(jax.ShapeDtypeStruct((b, l // chunk, n, chunk), dt))
    return in_specs, out_specs, out_shape


def _proj_call(spec, h, n_ctx, modsel, norm_w):
    h_args, h_specs = _hidden_operands(h, n_ctx)
    b, _, d = h_args[-1].shape
    l = sum(a.shape[1] for a in h_args)
    in_specs, out_specs, out_shape = _proj_specs(spec, b, l, d)
    return pl.pallas_call(
        _with_hidden_tile(spec["body"], len(h_args), n_ctx // TOKEN_TILE),
        grid=(b, l // TOKEN_TILE),
        in_specs=h_specs + in_specs,
        out_specs=out_specs,
        out_shape=out_shape,
        compiler_params=pltpu.CompilerParams(
            dimension_semantics=("parallel", "parallel"), vmem_limit_bytes=VMEM_LIMIT),
        name="proj_" + spec["name"],
    )(*h_args, modsel, norm_w.reshape(1, d), *spec["consts"], *spec["tables"])


def _conv_body(x_ref, w_ref, b_ref, o_ref, *, n_ctx, n_blocks, scale, transposed):
    blk = SCAN_CHUNK
    l = x_ref.shape[1]
    w = w_ref[...]
    bias = b_ref[...]
    rid = lax.broadcasted_iota(jnp.int32, (blk, 1), 0)

    def emit(chunk, start, acc):
        y = _silu(acc + bias) * scale
        if transposed:
            o_ref[0, chunk] = y.T.astype(BF16)
        else:
            o_ref[0, pl.ds(start, blk), :] = y.astype(BF16)

    def combine(a, bm, cc, first, last):
        return (jnp.where(first, 0.0, pltpu.roll(a, 1, 0)) + bm
                + jnp.where(last, 0.0, pltpu.roll(cc, blk - 1, 0)))

    xc = x_ref[0, 0:blk, :]
    emit(0, 0, combine(xc * w[3:4], xc * w[4:5], xc * w[5:6], rid == 0, rid == blk - 1))

    col = rid % GRID_W

    def lat_block(it, carry):
        base = pl.multiple_of(n_ctx + it * blk, blk)
        mid = x_ref[0, pl.ds(base, blk), :]
        up_edge = x_ref[0, pl.ds(pl.multiple_of(base - GRID_W, GRID_W), GRID_W), :]
        dn_start = pl.multiple_of(jnp.minimum(base + blk, l - GRID_W), GRID_W)
        dn_edge = x_ref[0, pl.ds(dn_start, GRID_W), :]
        up_edge = jnp.where(it > 0, up_edge, 0.0)
        dn_edge = jnp.where(it < n_blocks - 1, dn_edge, 0.0)
        up = jnp.concatenate([up_edge, mid[:blk - GRID_W]], axis=0)
        down = jnp.concatenate([mid[GRID_W:], dn_edge], axis=0)
        a = up * w[0:1] + mid * w[3:4] + down * w[6:7]
        bm = up * w[1:2] + mid * w[4:5] + down * w[7:8]
        cc = up * w[2:3] + mid * w[5:6] + down * w[8:9]
        emit(n_ctx // blk + it, base, combine(a, bm, cc, col == 0, col == GRID_W - 1))
        return carry

    lax.fori_loop(0, n_blocks, lat_block, 0)


def _conv(qk_raw, conv_w, conv_b, n_ctx, col_offset, width, scale, transposed):
    b, l, c = qk_raw.shape
    tc = 256
    blk = SCAN_CHUNK
    assert n_ctx == blk and (l - n_ctx) % blk == 0 and blk % GRID_W == 0
    off = col_offset // tc
    if transposed:
        out_spec = pl.BlockSpec((1, l // blk, tc, blk), lambda i, j: (i, 0, j, 0))
        out_shape = jax.ShapeDtypeStruct((b, l // blk, width, blk), BF16)
    else:
        out_spec = pl.BlockSpec((1, None, l, tc), lambda i, j: (i, j, 0, 0))
        out_shape = jax.ShapeDtypeStruct((b, width // tc, l, tc), BF16)
    return pl.pallas_call(
        functools.partial(_conv_body, n_ctx=n_ctx, n_blocks=(l - n_ctx) // blk, scale=scale,
                          transposed=transposed),
        grid=(b, width // tc),
        in_specs=[pl.BlockSpec((1, l, tc), lambda i, j: (i, 0, j + off)),
                  pl.BlockSpec((9, tc), lambda i, j: (0, j + off)),
                  pl.BlockSpec((1, tc), lambda i, j: (0, j + off))],
        out_specs=out_spec,
        out_shape=out_shape,
        compiler_params=pltpu.CompilerParams(
            dimension_semantics=("parallel", "parallel"), vmem_limit_bytes=VMEM_LIMIT),
        name="mlstm_conv_kt" if transposed else "mlstm_conv_q",
    )(qk_raw, conv_w.reshape(9, c), conv_b.reshape(1, c))


def _mlstm_scan_body(q_ref, kt_ref, v_ref, g_ref, o_ref, acc_ref, st_ref, n_ref, row_ref,
                     *, n_ctx_chunks, n_chunks):
    c = SCAN_CHUNK
    dv = v_ref.shape[2]
    st_ref[...] = jnp.zeros(st_ref.shape, F32)
    n_ref[...] = jnp.zeros(n_ref.shape, F32)
    masks = _tri_masks(c)

    gates = g_ref[0]
    token_rows, chunk_rows = [], []
    for dirn in range(2):
        u, b_r, cm = (gates[:, 5 * dirn + r:5 * dirn + r + 1] for r in range(3))
        b_end = gates[:, 5 * dirn + 3:5 * dirn + 4, 0:1]
        g_max = gates[:, 5 * dirn + 4:5 * dirn + 5, 0:1]
        m = jnp.zeros((1, 1, 1), F32)
        m_prev_l, m_new_l = [None] * n_chunks, [None] * n_chunks
        for s in range(n_chunks):
            idx = s
            if dirn:
                idx = n_ctx_chunks - 1 - s if s < n_ctx_chunks else n_chunks - 1 - (s - n_ctx_chunks)
            m_prev_l[idx] = m
            m = jnp.maximum(b_end[idx:idx + 1] + m, g_max[idx:idx + 1])
            m_new_l[idx] = m
        m_prev = jnp.concatenate(m_prev_l, axis=0)
        m_new = jnp.concatenate(m_new_l, axis=0)
        mm = jnp.maximum(m_prev, cm)
        token_rows += [u, jnp.exp(b_end + u - m_new), mm, b_r + mm]
        chunk_rows += [jnp.broadcast_to(jnp.exp(b_end + m_prev - m_new), (n_chunks, 1, c)),
                       jnp.broadcast_to(m_prev, (n_chunks, 1, c))]
    row_ref[...] = jnp.concatenate(token_rows + chunk_rows + [jnp.zeros((n_chunks, 4, c), F32)], axis=1)

    def lane_bcast_col(row):
        return jnp.broadcast_to(row, (LANES, c)).T

    def step(s, first):
        dirs = (0, 1)
        idxs = (s, _bw_chunk(s, n_ctx_chunks, n_chunks))
        starts = [pl.multiple_of(idx * c, c) for idx in idxs]
        q = [q_ref[0, pl.ds(st, c), :] for st in starts]
        kt = [kt_ref[0, idx] for idx in idxs]
        v = [v_ref[0, pl.ds(st, c), :] for st in starts]
        rows = [row_ref[idx] for idx in idxs]
        state = [st_ref[d] for d in dirs]
        nrm = [n_ref[d] for d in dirs]
        qk = [_dot(q[d], kt[d]) for d in dirs]
        mm_b = [lane_bcast_col(rows[d][4 * d + 2:4 * d + 3]) for d in dirs]
        bmm_b = [lane_bcast_col(rows[d][4 * d + 3:4 * d + 4]) for d in dirs]
        m_prev = [rows[d][9 + 2 * d:10 + 2 * d, 0:1] for d in dirs]
        decay = [rows[d][8 + 2 * d:9 + 2 * d, 0:1] for d in dirs]
        sc = [qk[d] * jnp.exp(jnp.where(masks[d], rows[d][4 * d:4 * d + 1]
                                        - jnp.concatenate([mm_b[d]] * (c // LANES), axis=1), NEG_INF))
              for d in dirs]
        w_inter = [jnp.exp(m_prev[d] - mm_b[d]) for d in dirs]
        intra = [_dot(sc[d].astype(BF16), v[d]) for d in dirs]
        inter = [_dot(q[d], state[d].astype(BF16)) for d in dirs]
        qn = [_dot(q[d], nrm[d].astype(BF16)) for d in dirs]
        wkt = [kt[d].astype(F32) * rows[d][4 * d + 1:4 * d + 2] for d in dirs]
        upd = [_dot(wkt[d].astype(BF16), v[d]) for d in dirs]
        for d in dirs:
            num = intra[d] + inter[d] * jnp.concatenate([w_inter[d]] * (dv // LANES), axis=1)
            den = jnp.sum(sc[d], axis=1, keepdims=True) + qn[d] * w_inter[d]
            rcp = 1.0 / jnp.maximum(jnp.abs(den), jnp.exp(-bmm_b[d]))
            _emit(o_ref, acc_ref, starts[d], c, num * jnp.concatenate([rcp] * (dv // LANES), axis=1), first[d])
        for d in dirs:
            st_ref[d] = decay[d] * state[d] + upd[d]
            n_ref[d] = decay[d] * nrm[d] + jnp.sum(wkt[d], axis=1, keepdims=True)

    _scan_steps(step, n_ctx_chunks, n_chunks, 1)


def _mlstm_scan(q, kt, v, gates, n_ctx):
    b, _, l, dk = q.shape
    dv = v.shape[3]
    n_chunks = l // SCAN_CHUNK
    return pl.pallas_call(
        functools.partial(_mlstm_scan_body, n_ctx_chunks=n_ctx // SCAN_CHUNK, n_chunks=n_chunks),
        grid=(b, N_HEADS),
        in_specs=[pl.BlockSpec((1, None, l, dk), _HEAD_SLAB),
                  pl.BlockSpec((1, n_chunks, dk, SCAN_CHUNK), lambda i, h: (i, 0, h, 0)),
                  pl.BlockSpec((1, None, l, dv), _HEAD_SLAB),
                  pl.BlockSpec((1, n_chunks, None, GATE_ROWS, SCAN_CHUNK), lambda i, h: (i, 0, h, 0, 0))],
        out_specs=pl.BlockSpec((1, None, l, dv), _HEAD_SLAB),
        out_shape=jax.ShapeDtypeStruct((b, N_HEADS, l, dv), BF16),
        scratch_shapes=[pltpu.VMEM((l, dv), F32),
                        pltpu.VMEM((2, dk, dv), F32),
                        pltpu.VMEM((2, dk, LANES), F32),
                        pltpu.VMEM((n_chunks, 16, SCAN_CHUNK), F32)],
        compiler_params=pltpu.CompilerParams(
            dimension_semantics=("parallel", "parallel"), vmem_limit_bytes=VMEM_LIMIT),
        name="mlstm_scan",
    )(q, kt, v, gates)


def _ret_scan_body(q_ref, kt_ref, v_ref, dl_ref, o_ref, acc_ref, st_ref, dm_ref, *, n_ctx_chunks, n_chunks):
    c = SCAN_CHUNK
    st_ref[...] = jnp.zeros(st_ref.shape, F32)
    masks = _tri_masks(c)
    row = lax.broadcasted_iota(jnp.int32, (c, c), 0)
    col = lax.broadcasted_iota(jnp.int32, (c, c), 1)
    dist = jnp.abs(row - col).astype(F32)
    rid = lax.broadcasted_iota(jnp.int32, (c, 1), 0).astype(F32)
    lid = lax.broadcasted_iota(jnp.int32, (1, c), 1).astype(F32)
    log_gamma = []
    for dirn in range(2):
        lg = _log_sigmoid(dl_ref[0, dirn])[0:1, 0:1]
        log_gamma.append(lg)
        dm_ref[dirn] = jnp.where(masks[dirn], jnp.exp(lg * dist), 0.0)

    pos_col = (rid + 1.0, c - rid)
    pos_row = (lid + 1.0, c - lid)

    def step(s, first):
        dirs = (0, 1)
        idxs = (s, _bw_chunk(s, n_ctx_chunks, n_chunks))
        starts = [pl.multiple_of(idx * c, c) for idx in idxs]
        q = [q_ref[0, pl.ds(st, c), :] for st in starts]
        kt = [kt_ref[0, idx] for idx in idxs]
        v = [v_ref[0, pl.ds(st, c), :] for st in starts]
        state = [st_ref[d] for d in dirs]
        sc = [(_dot(q[d], kt[d]) * dm_ref[d]).astype(BF16) for d in dirs]
        intra = [_dot(sc[d], v[d]) for d in dirs]
        inter = [_dot(q[d], state[d].astype(BF16)) for d in dirs]
        kdt = [(kt[d].astype(F32) * jnp.exp(log_gamma[d] * (c - pos_row[d]))).astype(BF16) for d in dirs]
        upd = [_dot(kdt[d], v[d]) for d in dirs]
        for d in dirs:
            _emit(o_ref, acc_ref, starts[d], c, intra[d] + inter[d] * jnp.exp(log_gamma[d] * pos_col[d]), first[d])
        for d in dirs:
            st_ref[d] = jnp.exp(log_gamma[d] * c) * state[d] + upd[d]

    _scan_steps(step, n_ctx_chunks, n_chunks, 4)


def _ret_scan(q, kt, v, dl, n_ctx):
    b, _, l, dk = q.shape
    dv = v.shape[3]
    n_chunks = l // SCAN_CHUNK
    return pl.pallas_call(
        functools.partial(_ret_scan_body, n_ctx_chunks=n_ctx // SCAN_CHUNK, n_chunks=n_chunks),
        grid=(b, N_HEADS),
        in_specs=[pl.BlockSpec((1, None, l, dk), _HEAD_SLAB),
                  pl.BlockSpec((1, n_chunks, dk, SCAN_CHUNK), lambda i, h: (i, 0, h, 0)),
                  pl.BlockSpec((1, None, l, dv), _HEAD_SLAB),
                  pl.BlockSpec((1, 2, 8, LANES), lambda i, h: (h, 0, 0, 0))],
        out_specs=pl.BlockSpec((1, None, l, dv), _HEAD_SLAB),
        out_shape=jax.ShapeDtypeStruct((b, N_HEADS, l, dv), BF16),
        scratch_shapes=[pltpu.VMEM((l, dv), F32),
                        pltpu.VMEM((2, dk, dv), F32),
                        pltpu.VMEM((2, SCAN_CHUNK, SCAN_CHUNK), F32)],
        compiler_params=pltpu.CompilerParams(
            dimension_semantics=("parallel", "parallel"), vmem_limit_bytes=VMEM_LIMIT),
        name="ret_scan",
    )(q, kt, v, dl)


def _gla_scan_body(q_ref, kt_ref, v_ref, code_ref, w2_ref, b2_ref, o_ref,
                   acc_ref, st_ref, sq_ref, kdt_ref, e_ref, m01_ref, *, n_ctx_chunks, n_chunks):
    c = GLA_CHUNK
    t = GLA_SUB
    n_sub = c // t
    dk = q_ref.shape[2]
    dv = v_ref.shape[2]
    st_ref[...] = jnp.zeros(st_ref.shape, F32)
    masks = _tri_masks(c)
    for d in range(2):
        m01_ref[d] = masks[d].astype(BF16)
    lid = lax.broadcasted_iota(jnp.int32, (1, c), 1)

    def intra(it, carry, first_chunk, group):
        chains = [(first_chunk + it * group + j, dirn) for j in range(group) for dirn in range(2)]
        starts = [pl.multiple_of(idx * c, c) for idx, _ in chains]
        qf = [q_ref[0, pl.ds(st, c), :].astype(F32) for st in starts]
        ktf = [kt_ref[0, idx].astype(F32) for idx, _ in chains]
        code = [code_ref[0, pl.ds(st, c), :] for st in starts]
        a = [_log_sigmoid(_dot(code[k], w2_ref[d]) + b2_ref[d]) * (1.0 / GLA_TAU) for k, (_, d) in enumerate(chains)]
        parts = [_split3(x) for x in a]
        m01 = [m01_ref[d] for d in range(2)]
        m01_twice = [jnp.concatenate([m, m], axis=1) for m in m01]
        b = [_dot(m01_twice[d], jnp.concatenate([parts[k][2], parts[k][1]], axis=0)) for k, (_, d) in enumerate(chains)]
        b = [b[k] + _dot(m01[d], parts[k][0]) for k, (_, d) in enumerate(chains)]
        qp, kp, kdt, e_end = [], [], [], []
        for k, (_, dirn) in enumerate(chains):
            refs = []
            for sb in range(n_sub):
                r0 = sb * t + (t - 1 if dirn else 0)
                refs.append(b[k][r0:r0 + 1] - a[k][r0:r0 + 1])
            own = jnp.concatenate([jnp.broadcast_to(r, (t, dk)) for r in refs], axis=0)
            b_rel = b[k] - own
            q_own = qf[k] * jnp.exp(b_rel)
            k_rel_t = (ktf[k] * jnp.exp(jnp.minimum(-(b_rel.T), GLA_EXP_CLAMP))).astype(BF16)
            r_end = 0 if dirn else c - 1
            b_end = b[k][r_end:r_end + 1]
            kdt.append((ktf[k] * jnp.exp((b_end - b[k]).T)).astype(BF16))
            e_end.append(jnp.broadcast_to(jnp.exp(b_end), (8, dk)))
            col_blocks = []
            for sb in range(n_sub):
                pieces = []
                for j in range(n_sub):
                    if (j <= sb) if dirn else (j >= sb):
                        qj = q_own[j * t:(j + 1) * t]
                        pieces.append(qj if j == sb else qj * jnp.exp(refs[j] - refs[sb]))
                    else:
                        pieces.append(jnp.zeros((t, dk), F32))
                col_blocks.append(jnp.concatenate(pieces, axis=0).astype(BF16))
            qp.append(col_blocks)
            kp.append(jnp.concatenate(
                [jnp.where(jnp.logical_and(lid >= sb * t, lid < (sb + 1) * t), k_rel_t, 0.0)
                 for sb in range(n_sub)], axis=0))
        s = [_dot(jnp.concatenate(qp[k], axis=1), kp[k]) for k in range(len(chains))]
        for k, (idx, dirn) in enumerate(chains):
            sq_ref[dirn, pl.ds(starts[k], c), 0:c] = jnp.where(masks[dirn], s[k], 0.0).astype(BF16)
            sq_ref[dirn, pl.ds(starts[k], c), c:2 * c] = qp[k][n_sub - 1 if dirn else 0]
            kdt_ref[dirn, idx] = kdt[k]
            e_ref[dirn, idx] = e_end[k]
        return carry

    n_full = n_chunks // GLA_INTRA_CHUNKS
    lax.fori_loop(0, n_full, functools.partial(intra, first_chunk=0, group=GLA_INTRA_CHUNKS), 0)
    rest = n_chunks - n_full * GLA_INTRA_CHUNKS
    if rest:
        intra(0, 0, n_full * GLA_INTRA_CHUNKS, rest)

    def step(s, first):
        dirs = (0, 1)
        idxs = (s, _bw_chunk(s, n_ctx_chunks, n_chunks))
        starts = [pl.multiple_of(idx * c, c) for idx in idxs]
        v = [v_ref[0, pl.ds(st, c), :] for st in starts]
        state = [st_ref[d] for d in dirs]
        out = [_dot(sq_ref[d, pl.ds(starts[d], c), :],
                    jnp.concatenate([v[d], state[d].astype(BF16)], axis=0)) for d in dirs]
        upd = [_dot(kdt_ref[d, idxs[d]], v[d]) for d in dirs]
        e_cols = [jnp.concatenate([e_ref[d, idxs[d]]] * (dk // 8), axis=0).T for d in dirs]
        for d in dirs:
            _emit(o_ref, acc_ref, starts[d], c, out[d], first[d])
        for d in dirs:
            st_ref[d] = jnp.concatenate([e_cols[d]] * (dv // dk), axis=1) * state[d] + upd[d]

    _scan_steps(step, n_ctx_chunks, n_chunks, 4)


def _gla_scan(q, kt, v, code, w2, b2, n_ctx):
    b, _, l, dk = q.shape
    dv = v.shape[3]
    n_chunks = l // GLA_CHUNK
    return pl.pallas_call(
        functools.partial(_gla_scan_body, n_ctx_chunks=n_ctx // GLA_CHUNK, n_chunks=n_chunks),
        grid=(b, N_HEADS),
        in_specs=[pl.BlockSpec((1, None, l, dk), _HEAD_SLAB),
                  pl.BlockSpec((1, n_chunks, dk, GLA_CHUNK), lambda i, h: (i, 0, h, 0)),
                  pl.BlockSpec((1, None, l, dv), _HEAD_SLAB),
                  pl.BlockSpec((1, l, LANES), lambda i, h: (i, 0, 0)),
                  pl.BlockSpec((2, LANES, dk), lambda i, h: (0, 0, h)),
                  pl.BlockSpec((2, 1, dk), lambda i, h: (0, 0, h))],
        out_specs=pl.BlockSpec((1, None, l, dv), _HEAD_SLAB),
        out_shape=jax.ShapeDtypeStruct((b, N_HEADS, l, dv), BF16),
        scratch_shapes=[pltpu.VMEM((l, dv), F32),
                        pltpu.VMEM((2, dk, dv), F32),
                        pltpu.VMEM((2, l, GLA_CHUNK + dk), BF16),
                        pltpu.VMEM((2, n_chunks, dk, GLA_CHUNK), BF16),
                        pltpu.VMEM((2, n_chunks, 8, dk), F32),
                        pltpu.VMEM((2, GLA_CHUNK, GLA_CHUNK), BF16)],
        compiler_params=pltpu.CompilerParams(
            dimension_semantics=("parallel", "parallel"), vmem_limit_bytes=VMEM_LIMIT),
        name="gla_scan",
    )(q, kt, v, code, w2, b2)


def _out_core(x, o_ref, z_ref, hw_ref, ow_ref, mod_ref, center):
    dv = o_ref.shape[3]
    proj = None
    for hh in range(N_HEADS):
        cols = slice(hh * dv, (hh + 1) * dv)
        yh = o_ref[0, hh].astype(F32)
        if center:
            yh = yh - jnp.mean(yh, axis=-1, keepdims=True)
        yn = yh * lax.rsqrt(jnp.mean(yh * yh, axis=-1, keepdims=True) + NORM_EPS) * hw_ref[:, cols]
        a = (yn * _silu(z_ref[0, :, cols].astype(F32))).astype(BF16)
        term = _dot(a, ow_ref[cols, :])
        proj = term if proj is None else proj + term
    return x + mod_ref[0, 0][2:3] * proj


def _out_final_body(x, o_ref, z_ref, hw_ref, ow_ref, mod_ref, fw_ref, out_ref, *, center):
    hn = _out_core(x, o_ref, z_ref, hw_ref, ow_ref, mod_ref, center)
    out_ref[0] = hn * lax.rsqrt(jnp.mean(hn * hn, axis=-1, keepdims=True) + NORM_EPS) * fw_ref[...]


def _out_proj_body(x, o_ref, z_ref, hw_ref, ow_ref, mod_ref, *rest, center, proj_body, n_proj_in):
    proj_in, h_out_ref, proj_out = rest[:n_proj_in], rest[n_proj_in], rest[n_proj_in + 1:]
    hn = _out_core(x, o_ref, z_ref, hw_ref, ow_ref, mod_ref, center)
    h_out_ref[0] = hn
    proj_body(hn, *proj_in, *proj_out)


def _out_operands(o, z, head_norm_w, out_w, modsel, skip):
    _, n_heads, _, dv = o.shape
    di = n_heads * dv
    d = out_w.shape[1]
    tok = lambda i, t: (i, t + skip, 0)
    specs = [pl.BlockSpec((1, n_heads, TOKEN_TILE, dv), lambda i, t: (i, 0, t + skip, 0)),
             pl.BlockSpec((1, TOKEN_TILE, di), tok),
             pl.BlockSpec((1, di), lambda i, t: (0, 0)),
             pl.BlockSpec((di, d), lambda i, t: (0, 0), pipeline_mode=pl.Buffered(1)),
             pl.BlockSpec((1, 1, 3, d), lambda i, t: (i, jnp.minimum(t + skip, 1), 0, 0))]
    return [o, z, head_norm_w.reshape(1, di), out_w, modsel], specs


def _out_final_call(o, z, head_norm_w, out_w, h, modsel, final_w, n_ctx, center):
    b, _, l, _ = o.shape
    d = out_w.shape[1]
    skip = n_ctx // TOKEN_TILE
    nt = l // TOKEN_TILE - skip
    h_args, h_specs = _hidden_operands(h, n_ctx, skip)
    args, specs = _out_operands(o, z, head_norm_w, out_w, modsel, skip)
    return pl.pallas_call(
        _with_hidden_tile(functools.partial(_out_final_body, center=center),
                          len(h_args), n_ctx // TOKEN_TILE, skip),
        grid=(b, nt),
        in_specs=h_specs + specs + [pl.BlockSpec((1, d), lambda i, t: (0, 0))],
        out_specs=pl.BlockSpec((1, TOKEN_TILE, d), lambda i, t: (i, t, 0)),
        out_shape=jax.ShapeDtypeStruct((b, nt * TOKEN_TILE, d), F32),
        compiler_params=pltpu.CompilerParams(
            dimension_semantics=("parallel", "parallel"), vmem_limit_bytes=VMEM_LIMIT),
        name="out_final",
    )(*h_args, *args, final_w.reshape(1, d))


def _out_proj_call(o, z, head_norm_w, out_w, h, modsel, n_ctx, center, spec, next_modsel, next_norm_w):
    b, _, l, _ = o.shape
    d = out_w.shape[1]
    h_args, h_specs = _hidden_operands(h, n_ctx)
    args, specs = _out_operands(o, z, head_norm_w, out_w, modsel, 0)
    p_in, p_out_specs, p_out_shape = _proj_specs(spec, b, l, d)
    res = pl.pallas_call(
        _with_hidden_tile(functools.partial(_out_proj_body, center=center, proj_body=spec["body"],
                                            n_proj_in=len(p_in)),
                          len(h_args), n_ctx // TOKEN_TILE),
        grid=(b, l // TOKEN_TILE),
        in_specs=h_specs + specs + p_in,
        out_specs=[pl.BlockSpec((1, TOKEN_TILE, d), lambda i, t: (i, t, 0))] + p_out_specs,
        out_shape=[jax.ShapeDtypeStruct((b, l, d), F32)] + p_out_shape,
        compiler_params=pltpu.CompilerParams(
            dimension_semantics=("parallel", "parallel"), vmem_limit_bytes=VMEM_LIMIT),
        name="out_proj_" + spec["name"],
    )(*h_args, *args, next_modsel, next_norm_w.reshape(1, d), *spec["consts"], *spec["tables"])
    return res[0], res[1:]


def _pad_cols(w, n):
    return jnp.pad(w, ((0, 0), (0, n - w.shape[1])))


def _mlstm_proj_spec(l, n_ctx, in_w, conv_w, conv_b, gate_b):
    d = in_w.shape[0]
    di = 2 * d
    qk_w = 2 * d
    wg = _pad_cols(in_w[:, qk_w + 2 * di:], LANES).astype(BF16)
    gb = _pad_cols(gate_b.reshape(1, -1), LANES)
    return dict(body=_proj_mlstm_body, consts=[in_w.astype(BF16), wg, gb], tables=[], name="mlstm",
                outs=[(qk_w, F32, None), (di, BF16, "heads"), (di, BF16, None),
                      (0, F32, (N_HEADS, GATE_ROWS, TOKEN_TILE))])


def _mlstm_mix(outs, n_ctx, in_w, conv_w, conv_b, gate_b):
    qk_raw, v, z, g = outs
    b, l, qk_w = qk_raw.shape
    head_qk = qk_w // (2 * N_HEADS)
    q = _conv(qk_raw, conv_w, conv_b, n_ctx, 0, qk_w // 2, 1.0, False)
    kt = _conv(qk_raw, conv_w, conv_b, n_ctx, qk_w // 2, qk_w // 2, head_qk ** -0.5, True)
    return _mlstm_scan(q, kt, v, g, n_ctx), z


def _gla_proj_spec(l, n_ctx, in_w, gk_w2, gk_b):
    d = in_w.shape[0]
    di = 2 * d
    key = gk_w2.shape[2]
    assert key // N_HEADS == LANES
    wlr = _pad_cols(in_w[:, 2 * key + 2 * di:], LANES).astype(BF16)
    return dict(body=functools.partial(_proj_gla_body, q_scale=(key // N_HEADS) ** -0.5),
                consts=[in_w.astype(BF16), wlr], tables=[], name="gla",
                outs=[(key, BF16, "heads"), (key, BF16, GLA_CHUNK), (di, BF16, "heads"), (di, BF16, None),
                      (LANES, BF16, None)])


def _gla_mix(outs, n_ctx, in_w, gk_w2, gk_b):
    q, kt, v, z, code = outs
    key = gk_w2.shape[2]
    w2 = jnp.zeros((2, LANES, key), F32)
    w2 = w2.at[0, :GLA_RANK].set(gk_w2[0]).at[1, GLA_RANK:2 * GLA_RANK].set(gk_w2[1]).astype(BF16)
    return _gla_scan(q, kt, v, code, w2, gk_b.reshape(2, 1, key), n_ctx), z


def _rope_tables(n_ctx, n_lat, head_k):
    quarter = head_k // 4
    pos = jnp.arange(n_lat, dtype=jnp.int32)
    inv_freq = ROPE_BASE ** (-jnp.arange(quarter, dtype=F32) / quarter)
    cos_parts, sin_parts = [], []
    for p in ((pos // GRID_W).astype(F32), (pos % GRID_W).astype(F32)):
        ang = p[:, None] * inv_freq[None, :]
        cos_parts += [jnp.cos(ang), jnp.cos(ang)]
        sin_parts += [-jnp.sin(ang), jnp.sin(ang)]
    cos = jnp.concatenate(cos_parts, axis=1)
    sin = jnp.concatenate(sin_parts, axis=1)
    cos = jnp.concatenate([jnp.ones((n_ctx, head_k), F32), cos], axis=0)
    sin = jnp.concatenate([jnp.zeros((n_ctx, head_k), F32), sin], axis=0)
    return cos, sin


def _ret_proj_spec(l, n_ctx, in_w, decay_logit):
    d = in_w.shape[0]
    di = 2 * d
    key = (in_w.shape[1] - 2 * di) // 2
    head_k = key // N_HEADS
    cos, sin = _rope_tables(n_ctx, l - n_ctx, head_k)
    return dict(body=functools.partial(_proj_ret_body, k_scale=head_k ** -0.5),
                consts=[in_w.astype(BF16)], tables=[cos, sin], name="ret",
                outs=[(key, BF16, "heads"), (key, BF16, SCAN_CHUNK), (di, BF16, "heads"), (di, BF16, None)])


def _ret_mix(outs, n_ctx, in_w, decay_logit):
    q, kt, v, z = outs
    dl = jnp.broadcast_to(decay_logit.astype(F32).T[:, :, None, None], (N_HEADS, 2, 8, LANES))
    return _ret_scan(q, kt, v, dl, n_ctx), z


_PROJ_SPEC = {"mlstm": _mlstm_proj_spec, "gla": _gla_proj_spec, "retention": _ret_proj_spec}
_MIX = {"mlstm": _mlstm_mix, "gla": _gla_mix, "retention": _ret_mix}


def kernel(x, c, ctx, c_ctx, l0_norm_w, l0_ada_w, l0_ada_b, l0_in_w, l0_conv_w, l0_conv_b, l0_gate_b, l0_head_norm_w, l0_out_w, l1_norm_w, l1_ada_w, l1_ada_b, l1_in_w, l1_gk_w2, l1_gk_b, l1_head_norm_w, l1_out_w, l2_norm_w, l2_ada_w, l2_ada_b, l2_in_w, l2_decay_logit, l2_head_norm_w, l2_out_w, l3_norm_w, l3_ada_w, l3_ada_b, l3_in_w, l3_conv_w, l3_conv_b, l3_gate_b, l3_head_norm_w, l3_out_w, final_norm_w):
    layers = (
        ("mlstm", l0_norm_w, l0_ada_w, l0_ada_b, l0_out_w, l0_head_norm_w, (l0_in_w, l0_conv_w, l0_conv_b, l0_gate_b)),
        ("gla", l1_norm_w, l1_ada_w, l1_ada_b, l1_out_w, l1_head_norm_w, (l1_in_w, l1_gk_w2, l1_gk_b)),
        ("retention", l2_norm_w, l2_ada_w, l2_ada_b, l2_out_w, l2_head_norm_w, (l2_in_w, l2_decay_logit)),
        ("mlstm", l3_norm_w, l3_ada_w, l3_ada_b, l3_out_w, l3_head_norm_w, (l3_in_w, l3_conv_w, l3_conv_b, l3_gate_b)),
    )
    b, _, d = x.shape
    n_ctx = ctx.shape[1]
    assert n_ctx % TOKEN_TILE == 0 and n_ctx % SCAN_CHUNK == 0 and x.shape[1] % SCAN_CHUNK == 0
    assert TOKEN_TILE == SCAN_CHUNK and TOKEN_TILE % GLA_CHUNK == 0
    l = n_ctx + x.shape[1]
    h = (ctx, x)
    cc = jnp.concatenate([c, c_ctx[None, :], jnp.zeros((7, d), F32)], axis=0)
    modsels = []
    for mod in _ada(cc, [layer[2] for layer in layers], [layer[3] for layer in layers]):
        mod_lat = mod[:b].reshape(b, 1, 3, d)
        mod_ctx = jnp.broadcast_to(mod[b].reshape(1, 1, 3, d), (b, 1, 3, d))
        modsels.append(jnp.concatenate([mod_ctx, mod_lat], axis=1))
    specs = [_PROJ_SPEC[kind](l, n_ctx, *params) for kind, _, _, _, _, _, params in layers]
    proj_outs = _proj_call(specs[0], h, n_ctx, modsels[0], layers[0][1])
    for li, (kind, _, _, _, out_w, head_norm_w, params) in enumerate(layers):
        o, z = _MIX[kind](proj_outs, n_ctx, *params)
        center = kind != "gla"
        if li == len(layers) - 1:
            return _out_final_call(o, z, head_norm_w, out_w.astype(BF16), h, modsels[li], final_norm_w,
                                   n_ctx, center)
        h, proj_outs = _out_proj_call(o, z, head_norm_w, out_w.astype(BF16), h, modsels[li], n_ctx, center,
                                      specs[li + 1], modsels[li + 1], layers[li + 1][1])
```
